```python
import jax, jax.numpy as jnp
from jax import lax
import numpy as np

D_MODEL = 1024
BATCH = 16
SEQ = 2048
DEPTH = 1
DEC_BATCH = 128
DEC_SEQ = 8
PAST_LEN = 16384
PAGE_SIZE = 128

PLE_DIM = 256
D_FF = 2816
NORM_EPS = 1e-6
GDN_HEADS = 4
GDN_DK = 128
GDN_DV = 128
GDN_CONV = 4
GDN_CHUNK = 64
SWA_HEADS = 8
SWA_KV_HEADS = 2
SWA_HD = 64
WINDOW = 128
ROT_DIM = SWA_HD // 4
ROPE_THETA = 500000.0

GDN_QK_W = GDN_HEADS * GDN_DK
GDN_V_W = GDN_HEADS * GDN_DV
GDN_CONV_W = 2 * GDN_QK_W + GDN_V_W
SWA_Q_W = SWA_HEADS * SWA_HD
SWA_KV_W = SWA_KV_HEADS * SWA_HD
MIX_W = GDN_V_W + SWA_Q_W
IN_SIZES = (GDN_CONV_W, GDN_V_W, GDN_HEADS, GDN_HEADS, SWA_Q_W, SWA_KV_W, SWA_KV_W)
IN_W = sum(IN_SIZES)

kernel_name = 'hymba_gdn_swa_macaron_decode_step'


def rmsnorm(x, g):
    xf = x.astype(jnp.float32)
    xf = xf * lax.rsqrt(jnp.mean(xf * xf, axis=-1, keepdims=True) + NORM_EPS)
    return (xf * g.astype(jnp.float32)).astype(x.dtype)


def swiglu(x, w_gu, w_down):
    gate, up = jnp.split(x @ w_gu, 2, axis=-1)
    return (jax.nn.silu(gate) * up) @ w_down


def l2norm(x):
    return x * lax.rsqrt(jnp.sum(x * x, axis=-1, keepdims=True) + 1e-6)


def causal_conv(x, buf, w):
    T = x.shape[1]
    full = jnp.concatenate([buf.astype(x.dtype), x], axis=1)
    out = full[:, 0:T] * w[0]
    for j in range(1, GDN_CONV):
        out = out + full[:, j:j + T] * w[j]
    return jax.nn.silu(out), full[:, -(GDN_CONV - 1):]


def rope_partial(x, pos):
    half = ROT_DIM // 2
    inv = ROPE_THETA ** (-jnp.arange(half, dtype=jnp.float32) * 2.0 / ROT_DIM)
    ang = pos.astype(jnp.float32)[:, None] * inv[None, :]
    cos = jnp.cos(ang)[None, :, None, :]
    sin = jnp.sin(ang)[None, :, None, :]
    xf = x.astype(jnp.float32)
    x1 = xf[..., :half]
    x2 = xf[..., half:ROT_DIM]
    out = jnp.concatenate([x1 * cos - x2 * sin, x2 * cos + x1 * sin, xf[..., ROT_DIM:]], axis=-1)
    return out.astype(x.dtype)


def sink_softmax(s, mask, sinks):
    s = jnp.where(mask, s, -jnp.inf)
    sk = sinks.astype(jnp.float32).reshape(SWA_KV_HEADS, SWA_HEADS // SWA_KV_HEADS)[:, :, None, None]
    m = jnp.maximum(jnp.max(s, axis=-1, keepdims=True), sk)
    p = jnp.exp(s - m)
    return p / (jnp.sum(p, axis=-1, keepdims=True) + jnp.exp(sk - m))


def swa_prompt(q, k, v, sinks):
    B, T = q.shape[:2]
    G = SWA_HEADS // SWA_KV_HEADS
    NB = T // WINDOW
    qb = q.reshape(B, NB, WINDOW, SWA_KV_HEADS, G, SWA_HD)
    pad = ((0, 0), (WINDOW, 0), (0, 0), (0, 0))
    kp = jnp.pad(k, pad).reshape(B, NB + 1, WINDOW, SWA_KV_HEADS, SWA_HD)
    vp = jnp.pad(v, pad).reshape(B, NB + 1, WINDOW, SWA_KV_HEADS, SWA_HD)
    kk = jnp.concatenate([kp[:, :-1], kp[:, 1:]], axis=2)
    vv = jnp.concatenate([vp[:, :-1], vp[:, 1:]], axis=2)
    s = jnp.einsum('bnqkgd,bnskd->bnkgqs', qb, kk, preferred_element_type=jnp.float32) * (SWA_HD ** -0.5)
    blk = jnp.arange(NB)[:, None] * WINDOW
    qpos = blk + jnp.arange(WINDOW)[None, :]
    kpos = blk - WINDOW + jnp.arange(2 * WINDOW)[None, :]
    diff = qpos[:, :, None] - kpos[:, None, :]
    mask = (diff >= 0) & (diff < WINDOW) & (kpos[:, None, :] >= 0)
    p = sink_softmax(s, mask[:, None, None], sinks)
    o = jnp.einsum('bnkgqs,bnskd->bnqkgd', p.astype(vv.dtype), vv)
    return o.reshape(B, T, SWA_Q_W)


def swa_sample(q, k, v, k_buf, v_buf, sinks):
    B, T = q.shape[:2]
    G = SWA_HEADS // SWA_KV_HEADS
    wb = k_buf.shape[1]
    kk = jnp.concatenate([k_buf.astype(k.dtype), k], axis=1)
    vv = jnp.concatenate([v_buf.astype(v.dtype), v], axis=1)
    qg = q.reshape(B, T, SWA_KV_HEADS, G, SWA_HD)
    s = jnp.einsum('bqkgd,bskd->bkgqs', qg, kk, preferred_element_type=jnp.float32) * (SWA_HD ** -0.5)
    qpos = PAST_LEN + jnp.arange(T)
    kpos = jnp.concatenate([PAST_LEN - wb + jnp.arange(wb), qpos])
    diff = qpos[:, None] - kpos[None, :]
    mask = (diff >= 0) & (diff < WINDOW)
    p = sink_softmax(s, mask, sinks)
    o = jnp.einsum('bkgqs,bskd->bqkgd', p.astype(vv.dtype), vv)
    return o.reshape(B, T, SWA_Q_W), kk[:, -wb:], vv[:, -wb:]


def gated_delta_chunked(q, k, v, g, beta, S0):
    B, T, H, _ = q.shape
    dv = v.shape[-1]
    C = min(GDN_CHUNK, T)
    N = -(-T // C)
    pad = N * C - T

    def prep(x):
        x = jnp.pad(x, [(0, 0), (0, pad)] + [(0, 0)] * (x.ndim - 2))
        x = x.reshape((B, N, C) + x.shape[2:])
        return jnp.moveaxis(x, (1, 3), (0, 2))

    qc, kc, vc, gcs, bc = prep(q), prep(k), prep(v), prep(g), prep(beta)
    gc = jnp.cumsum(gcs, axis=-1)
    tri = jnp.tril(jnp.ones((C, C), bool))
    strict = jnp.tril(jnp.ones((C, C), bool), -1)
    decay = jnp.exp(jnp.where(tri, gc[..., :, None] - gc[..., None, :], -jnp.inf))
    kb = kc * bc[..., None]
    vb = vc * bc[..., None]
    A = jnp.where(strict, jnp.einsum('nbhid,nbhjd->nbhij', kb, kc) * decay, 0.0)
    Tm = A + jnp.eye(C, dtype=A.dtype)
    u = lax.linalg.triangular_solve(Tm, vb, left_side=True, lower=True)
    w = lax.linalg.triangular_solve(Tm, kb * jnp.exp(gc)[..., None], left_side=True, lower=True)
    qk = jnp.einsum('nbhid,nbhjd->nbhij', qc, kc) * decay
    glast = gc[..., -1]
    kdec = kc * jnp.exp(glast[..., None] - gc)[..., None]
    qdec = qc * jnp.exp(gc)[..., None]

    def step(S, xs):
        qd, kd, u_, w_, qk_, gl = xs
        v_new = u_ - jnp.einsum('bhcd,bhde->bhce', w_, S)
        o = jnp.einsum('bhcd,bhde->bhce', qd, S) + jnp.einsum('bhij,bhje->bhie', qk_, v_new)
        S = S * jnp.exp(gl)[..., None, None] + jnp.einsum('bhcd,bhce->bhde', kd, v_new)
        return S, o

    S, o = lax.scan(step, S0, (qdec, kdec, u, w, qk, glast))
    o = jnp.moveaxis(o, (0, 2), (1, 3)).reshape(B, N * C, H, dv)[:, :T]
    return o, S


def layer(x, p, lp, S0, conv_buf, kv_buf, pos0):
    B, T, _ = x.shape
    f32 = jnp.float32
    h = x + 0.5 * swiglu(rmsnorm(x, lp['norm_ffn1']), lp['ffn1_gu'], lp['ffn1_down'])
    n = rmsnorm(h, lp['norm_mix'])
    proj = n @ lp['w_in']
    qkv_pre, z, a, b, q_s, k_s, v_s = jnp.split(proj, np.cumsum(IN_SIZES)[:-1].tolist(), axis=-1)
    qkv, new_conv = causal_conv(qkv_pre, conv_buf, lp['conv_w'])
    qg, kg, vg = jnp.split(qkv.astype(f32), [GDN_QK_W, 2 * GDN_QK_W], axis=-1)
    qg = l2norm(qg.reshape(B, T, GDN_HEADS, GDN_DK)) * (GDN_DK ** -0.5)
    kg = l2norm(kg.reshape(B, T, GDN_HEADS, GDN_DK))
    vg = vg.reshape(B, T, GDN_HEADS, GDN_DV)
    gdec = -jnp.exp(lp['a_log'].astype(f32)) * jax.nn.softplus(a.astype(f32) + lp['dt_bias'].astype(f32))
    beta = jax.nn.sigmoid(b.astype(f32))
    o_g, S_new = gated_delta_chunked(qg, kg, vg, gdec, beta, S0.astype(f32))
    o_g = rmsnorm(o_g, lp['gdn_norm']) * jax.nn.silu(z.reshape(B, T, GDN_HEADS, GDN_DV).astype(f32))
    o_g = o_g.reshape(B, T, GDN_V_W).astype(x.dtype)
    pos = pos0 + jnp.arange(T)
    q_s = rope_partial(q_s.reshape(B, T, SWA_HEADS, SWA_HD), pos)
    k_s = rope_partial(k_s.reshape(B, T, SWA_KV_HEADS, SWA_HD), pos)
    v_s = v_s.reshape(B, T, SWA_KV_HEADS, SWA_HD)
    if kv_buf is None:
        o_s = swa_prompt(q_s, k_s, v_s, lp['sinks'])
        new_k, new_v = k_s[:, -WINDOW:], v_s[:, -WINDOW:]
    else:
        o_s, new_k, new_v = swa_sample(q_s, k_s, v_s, kv_buf[0], kv_buf[1], lp['sinks'])
    h = h + jnp.concatenate([o_g, o_s], axis=-1) @ lp['w_out']
    h = h + 0.5 * swiglu(rmsnorm(h, lp['norm_ffn2']), lp['ffn2_gu'], lp['ffn2_down'])
    h = h + jax.nn.sigmoid(rmsnorm(h, lp['norm_ple']) @ lp['ple_gate']) * (p @ lp['ple_proj'])
    return h, S_new.astype(S0.dtype), new_conv, new_k, new_v


def setup_inputs(seed: int = 0) -> dict:
    key = jax.random.key(seed)
    ks = jax.random.split(key, 32)
    f32 = jnp.float32

    def nrm(k, shape, scale):
        return jax.random.normal(k, shape, f32) * scale

    def gain(k, dim):
        return 1.0 + 0.02 * jax.random.normal(k, (DEPTH, dim), f32)

    wb = min(WINDOW, PAST_LEN)
    return {
        'x_prompt': nrm(ks[0], (BATCH, SEQ, D_MODEL), 1.0),
        'x_sample': nrm(ks[1], (DEC_BATCH, DEC_SEQ, D_MODEL), 1.0),
        'state_gdn': nrm(ks[2], (DEPTH, DEC_BATCH, GDN_HEADS, GDN_DK, GDN_DV), GDN_DK ** -0.5),
        'state_conv': nrm(ks[3], (DEPTH, DEC_BATCH, GDN_CONV - 1, GDN_CONV_W), 1.0),
        'cache_swa_k': nrm(ks[4], (DEPTH, DEC_BATCH, wb, SWA_KV_HEADS, SWA_HD), 1.0),
        'cache_swa_v': nrm(ks[5], (DEPTH, DEC_BATCH, wb, SWA_KV_HEADS, SWA_HD), 1.0),
        'p_prompt': nrm(ks[6], (DEPTH, BATCH, SEQ, PLE_DIM), 1.0),
        'p_sample': nrm(ks[7], (DEPTH, DEC_BATCH, DEC_SEQ, PLE_DIM), 1.0),
        'norm_ffn1': gain(ks[8], D_MODEL),
        'ffn1_gu': nrm(ks[9], (DEPTH, D_MODEL, 2 * D_FF), D_MODEL ** -0.5),
        'ffn1_down': nrm(ks[10], (DEPTH, D_FF, D_MODEL), D_FF ** -0.5),
        'norm_mix': gain(ks[11], D_MODEL),
        'w_in': nrm(ks[12], (DEPTH, D_MODEL, IN_W), D_MODEL ** -0.5),
        'conv_w': nrm(ks[13], (DEPTH, GDN_CONV, GDN_CONV_W), GDN_CONV ** -0.5),
        'a_log': jnp.log(jax.random.uniform(ks[14], (DEPTH, GDN_HEADS), f32, 1.0, 16.0)),
        'dt_bias': nrm(ks[15], (DEPTH, GDN_HEADS), 0.1),
        'gdn_norm': gain(ks[16], GDN_DV),
        'sinks': nrm(ks[17], (DEPTH, SWA_HEADS), 1.0),
        'w_out': nrm(ks[18], (DEPTH, MIX_W, D_MODEL), MIX_W ** -0.5),
        'norm_ffn2': gain(ks[19], D_MODEL),
        'ffn2_gu': nrm(ks[20], (DEPTH, D_MODEL, 2 * D_FF), D_MODEL ** -0.5),
        'ffn2_down': nrm(ks[21], (DEPTH, D_FF, D_MODEL), D_FF ** -0.5),
        'norm_ple': gain(ks[22], D_MODEL),
        'ple_proj': nrm(ks[23], (DEPTH, PLE_DIM, D_MODEL), PLE_DIM ** -0.5),
        'ple_gate': nrm(ks[24], (DEPTH, D_MODEL, D_MODEL), D_MODEL ** -0.5),
        'norm_final': 1.0 + 0.02 * jax.random.normal(ks[25], (D_MODEL,), f32),
    }


def reference(x_prompt, x_sample, state_gdn, state_conv, cache_swa_k, cache_swa_v, p_prompt, p_sample,
              norm_ffn1, ffn1_gu, ffn1_down, norm_mix, w_in, conv_w, a_log, dt_bias, gdn_norm, sinks,
              w_out, norm_ffn2, ffn2_gu, ffn2_down, norm_ple, ple_proj, ple_gate, norm_final):
    hp, hs = x_prompt, x_sample
    sg_p, sc_p, kk_p, vv_p = [], [], [], []
    sg_s, sc_s, kk_s, vv_s = [], [], [], []
    for i in range(DEPTH):
        lp = {'norm_ffn1': norm_ffn1[i], 'ffn1_gu': ffn1_gu[i], 'ffn1_down': ffn1_down[i],
              'norm_mix': norm_mix[i], 'w_in': w_in[i], 'conv_w': conv_w[i], 'a_log': a_log[i],
              'dt_bias': dt_bias[i], 'gdn_norm': gdn_norm[i], 'sinks': sinks[i], 'w_out': w_out[i],
              'norm_ffn2': norm_ffn2[i], 'ffn2_gu': ffn2_gu[i], 'ffn2_down': ffn2_down[i],
              'norm_ple': norm_ple[i], 'ple_proj': ple_proj[i], 'ple_gate': ple_gate[i]}
        B = hp.shape[0]
        S0 = jnp.zeros((B, GDN_HEADS, GDN_DK, GDN_DV), state_gdn.dtype)
        cb0 = jnp.zeros((B, GDN_CONV - 1, GDN_CONV_W), hp.dtype)
        hp, s1, c1, k1, v1 = layer(hp, p_prompt[i], lp, S0, cb0, None, 0)
        sg_p.append(s1); sc_p.append(c1); kk_p.append(k1); vv_p.append(v1)
        hs, s2, c2, k2, v2 = layer(hs, p_sample[i], lp, state_gdn[i], state_conv[i],
                                   (cache_swa_k[i], cache_swa_v[i]), PAST_LEN)
        sg_s.append(s2); sc_s.append(c2); kk_s.append(k2); vv_s.append(v2)
    y_prompt = rmsnorm(hp, norm_final)
    y_sample = rmsnorm(hs, norm_final)
    return (y_prompt, y_sample,
            jnp.stack(sg_p), jnp.stack(sc_p), jnp.stack(kk_p), jnp.stack(vv_p),
            jnp.stack(sg_s), jnp.stack(sc_s), jnp.stack(kk_s), jnp.stack(vv_s))
```

```python
import functools

import jax
import jax.numpy as jnp
import numpy as np
from jax.experimental import pallas as pl
from jax.experimental.pallas import tpu as pltpu

F32 = jnp.float32
BF16 = jnp.bfloat16

D_MODEL = 1024
D_FF = 2816
PLE_DIM = 256
NORM_EPS = 1e-6
GDN_HEADS = 4
GDN_DK = 128
GDN_DV = 128
GDN_CONV = 4
GDN_QK_W = GDN_HEADS * GDN_DK
GDN_V_W = GDN_HEADS * GDN_DV
GDN_CONV_W = 2 * GDN_QK_W + GDN_V_W
SWA_HEADS = 8
SWA_KV_HEADS = 2
SWA_HD = 64
SWA_Q_W = SWA_HEADS * SWA_HD
SWA_KV_W = SWA_KV_HEADS * SWA_HD
WINDOW = 128
ROT_DIM = SWA_HD // 4
ROPE_THETA = 500000.0
PAST_LEN = 16384

LANES = 128
SUBLANES = 8
VMEM_LIMIT_BYTES = 56 * 1024 * 1024

P_QKV = 0
P_Z = P_QKV + GDN_CONV_W
P_QS = P_Z + GDN_V_W
P_KS = P_QS + SWA_Q_W
P_VS = P_KS + SWA_KV_W
P_AB = P_VS + SWA_KV_W
P_W = P_AB + LANES

FF_CHUNK = 256
TOKEN_TILE = 512


def _rms(x, g):
    return x * jax.lax.rsqrt(jnp.mean(x * x, axis=-1, keepdims=True) + NORM_EPS) * g


def _sigmoid(x):
    return 1.0 / (1.0 + jnp.exp(-x))


def _silu(x):
    return x * _sigmoid(x)


def _mm(a, b):
    return jnp.dot(a.astype(BF16), b.astype(BF16), preferred_element_type=F32)


def _mm_nt(a, b):
    return jax.lax.dot_general(a.astype(BF16), b.astype(BF16), (((1,), (1,)), ((), ())),
                               preferred_element_type=F32)


def _swiglu_acc(n_bf16, wgu_ref, wd_ref):
    rows = n_bf16.shape[0]
    acc = jnp.zeros((rows, D_MODEL), F32)
    for c in range(D_FF // FF_CHUNK):
        lo = c * FF_CHUNK
        gate = jnp.dot(n_bf16, wgu_ref[:, lo:lo + FF_CHUNK], preferred_element_type=F32)
        up = jnp.dot(n_bf16, wgu_ref[:, D_FF + lo:D_FF + lo + FF_CHUNK], preferred_element_type=F32)
        act = (_silu(gate) * up).astype(BF16)
        acc = acc + jnp.dot(act, wd_ref[lo:lo + FF_CHUNK, :], preferred_element_type=F32)
    return acc


def _ffn_in_kernel(x_ref, g1_ref, wgu_ref, wd_ref, gm_ref, win_ref, h_ref, proj_ref):
    x = x_ref[...]
    n1 = _rms(x, g1_ref[...]).astype(BF16)
    h = x + 0.5 * _swiglu_acc(n1, wgu_ref, wd_ref)
    h_ref[...] = h
    n = _rms(h, gm_ref[...]).astype(BF16)
    proj_ref[...] = jnp.dot(n, win_ref[...], preferred_element_type=F32)


def _const_spec(shape):
    return pl.BlockSpec(shape, lambda i: (0,) * len(shape), pipeline_mode=pl.Buffered(1))


def _ffn_in(x2d, g1, wgu, wd, gm, win):
    n_tok = x2d.shape[0]
    tm = min(TOKEN_TILE, n_tok)
    assert n_tok % tm == 0
    return pl.pallas_call(
        _ffn_in_kernel,
        grid=(n_tok // tm,),
        in_specs=[
            pl.BlockSpec((tm, D_MODEL), lambda i: (i, 0)),
            _const_spec((1, D_MODEL)),
            _const_spec((D_MODEL, 2 * D_FF)),
            _const_spec((D_FF, D_MODEL)),
            _const_spec((1, D_MODEL)),
            _const_spec((D_MODEL, P_W)),
        ],
        out_specs=[
            pl.BlockSpec((tm, D_MODEL), lambda i: (i, 0)),
            pl.BlockSpec((tm, P_W), lambda i: (i, 0)),
        ],
        out_shape=[
            jax.ShapeDtypeStruct((n_tok, D_MODEL), F32),
            jax.ShapeDtypeStruct((n_tok, P_W), F32),
        ],
        compiler_params=pltpu.CompilerParams(
            dimension_semantics=("arbitrary",), vmem_limit_bytes=VMEM_LIMIT_BYTES),
        name="ffn_in",
    )(x2d, g1, wgu, wd, gm, win)


def _ffn_out_kernel(o_ref, h_ref, p_ref, wout_ref, g2_ref, wgu_ref, wd_ref, gp_ref, wpg_ref, wpp_ref,
                    gf_ref, y_ref):
    h = h_ref[...] + jnp.dot(o_ref[...], wout_ref[...], preferred_element_type=F32)
    n2 = _rms(h, g2_ref[...]).astype(BF16)
    h = h + 0.5 * _swiglu_acc(n2, wgu_ref, wd_ref)
    npl = _rms(h, gp_ref[...]).astype(BF16)
    gate = _sigmoid(jnp.dot(npl, wpg_ref[...], preferred_element_type=F32))
    pe = jnp.dot(p_ref[...].astype(BF16), wpp_ref[...], preferred_element_type=F32)
    h = h + gate * pe
    y_ref[...] = _rms(h, gf_ref[...])


def _ffn_out(o2d, h2d, p2d, wout, g2, wgu, wd, gp, wpg, wpp, gf):
    n_tok = h2d.shape[0]
    tm = min(TOKEN_TILE, n_tok)
    assert n_tok % tm == 0
    return pl.pallas_call(
        _ffn_out_kernel,
        grid=(n_tok // tm,),
        in_specs=[
            pl.BlockSpec((tm, D_MODEL), lambda i: (i, 0)),
            pl.BlockSpec((tm, D_MODEL), lambda i: (i, 0)),
            pl.BlockSpec((tm, PLE_DIM), lambda i: (i, 0)),
            _const_spec((D_MODEL, D_MODEL)),
            _const_spec((1, D_MODEL)),
            _const_spec((D_MODEL, 2 * D_FF)),
            _const_spec((D_FF, D_MODEL)),
            _const_spec((1, D_MODEL)),
            _const_spec((D_MODEL, D_MODEL)),
            _const_spec((PLE_DIM, D_MODEL)),
            _const_spec((1, D_MODEL)),
        ],
        out_specs=pl.BlockSpec((tm, D_MODEL), lambda i: (i, 0)),
        out_shape=jax.ShapeDtypeStruct((n_tok, D_MODEL), F32),
        compiler_params=pltpu.CompilerParams(
            dimension_semantics=("arbitrary",), vmem_limit_bytes=VMEM_LIMIT_BYTES),
        name="ffn_out",
    )(o2d, h2d, p2d, wout, g2, wgu, wd, gp, wpg, wpp, gf)


def _pad_rows(x):
    rows = x.shape[0]
    if rows == LANES:
        return x
    return jnp.concatenate([x, jnp.zeros((LANES - rows, x.shape[1]), x.dtype)], axis=0)


def _split3(x):
    hi = x.astype(BF16)
    r = x - hi.astype(F32)
    mid = r.astype(BF16)
    lo = (r - mid.astype(F32)).astype(BF16)
    return hi, mid, lo


def _unit_lower_inverse(a, rows):
    ri = jax.lax.broadcasted_iota(jnp.int32, (rows, LANES), 0)
    ci = jax.lax.broadcasted_iota(jnp.int32, (rows, LANES), 1)
    eye = (ri == ci).astype(F32)
    blk = ri ^ ci
    a0 = jnp.where(blk < SUBLANES, a, 0.0)
    x = eye - a0
    p = _mm(a0, _pad_rows(a0))
    x = x + _mm(x, _pad_rows(p))
    p = _mm(p, _pad_rows(p))
    x = x + _mm(x, _pad_rows(p))
    s = SUBLANES
    while s < rows:
        off = jnp.where((blk < 2 * s) & (blk >= s), a, 0.0)
        y = _mm(off, _pad_rows(x))
        x = x - _mm(x, _pad_rows(y))
        s *= 2
    return x


def _mixer_kernel(sinks_ref, proj_ref, s0_ref, conv0_ref, k0_ref, v0_ref, convw_ref, alog_ref, dt_ref,
                  gnorm_ref, rc_ref, rs1_ref, rs2_ref,
                  o_ref, sout_ref, convout_ref, kout_ref, vout_ref,
                  xext_ref, s_ref, kprev_ref, vprev_ref, *, tb, prev0_valid):
    t = pl.program_id(1)
    nt = pl.num_programs(1)

    @pl.when(t == 0)
    def _init():
        xext_ref[0:SUBLANES, :] = jnp.zeros((SUBLANES, GDN_CONV_W), F32)
        xext_ref[SUBLANES - (GDN_CONV - 1):SUBLANES, :] = conv0_ref[0]
        s_ref[...] = s0_ref[0]
        kprev_ref[...] = k0_ref[0]
        vprev_ref[...] = v0_ref[0]

    xext_ref[SUBLANES:SUBLANES + tb, :] = proj_ref[0, :, P_QKV:P_QKV + GDN_CONV_W]
    base = SUBLANES - (GDN_CONV - 1)
    conv = xext_ref[base:base + tb, :] * convw_ref[0:1, :]
    for j in range(1, GDN_CONV):
        conv = conv + xext_ref[base + j:base + j + tb, :] * convw_ref[j:j + 1, :]
    qkv = _silu(conv)

    ab = proj_ref[0, :, P_AB:P_AB + LANES]
    zab = ab + dt_ref[...]
    softplus = jnp.maximum(zab, 0.0) + jnp.log(1.0 + jnp.exp(-jnp.abs(zab)))
    g_all = -jnp.exp(alog_ref[...]) * softplus
    beta_all = _sigmoid(ab)

    ri = jax.lax.broadcasted_iota(jnp.int32, (tb, LANES), 0)
    ci = jax.lax.broadcasted_iota(jnp.int32, (tb, LANES), 1)
    tri = ci <= ri
    strict = ci < ri
    r128 = jax.lax.broadcasted_iota(jnp.int32, (LANES, LANES), 0)
    c128 = jax.lax.broadcasted_iota(jnp.int32, (LANES, LANES), 1)
    ltri = tri.astype(BF16)
    utri = (r128 <= c128).astype(BF16)

    g_pad = _pad_rows(g_all)
    g_hi, g_mid, g_lo = _split3(g_pad)
    gc_cols = (jnp.dot(ltri, g_hi, preferred_element_type=F32)
               + jnp.dot(ltri, g_mid, preferred_element_type=F32)
               + jnp.dot(ltri, g_lo, preferred_element_type=F32))
    gt = jnp.transpose(g_pad)[0:SUBLANES, :]
    t_hi, t_mid, t_lo = _split3(gt)
    gc_rows = (jnp.dot(t_hi, utri, preferred_element_type=F32)
               + jnp.dot(t_mid, utri, preferred_element_type=F32)
               + jnp.dot(t_lo, utri, preferred_element_type=F32))

    og = []
    for hh in range(GDN_HEADS):
        lo = hh * GDN_DK
        q = qkv[:, lo:lo + GDN_DK]
        k = qkv[:, GDN_QK_W + lo:GDN_QK_W + lo + GDN_DK]
        v = qkv[:, 2 * GDN_QK_W + lo:2 * GDN_QK_W + lo + GDN_DV]
        q = q * jax.lax.rsqrt(jnp.sum(q * q, axis=-1, keepdims=True) + 1e-6) * (GDN_DK ** -0.5)
        k = k * jax.lax.rsqrt(jnp.sum(k * k, axis=-1, keepdims=True) + 1e-6)
        beta = beta_all[:, GDN_HEADS + hh:GDN_HEADS + hh + 1]
        gcc = gc_cols[:, hh:hh + 1]
        gcr = gc_rows[hh:hh + 1, :]
        glast = gc_cols[tb - 1:tb, hh:hh + 1]
        decay = jnp.exp(jnp.where(tri, gcc - gcr, -jnp.inf))
        kb = k * beta
        vb = v * beta
        k_pad = _pad_rows(k)
        a_mat = jnp.where(strict, _mm_nt(kb, k_pad) * decay, 0.0)
        qk = _mm_nt(q, k_pad) * decay
        tinv = _unit_lower_inverse(a_mat, tb)
        egc = jnp.exp(gcc)
        u = _mm(tinv, _pad_rows(vb))
        w = _mm(tinv, _pad_rows(kb * egc))
        s_old = s_ref[hh]
        v_new = u - _mm(w, s_old)
        o = _mm(q * egc, s_old) + _mm(qk, _pad_rows(v_new))
        kdec_t = jnp.transpose(_pad_rows(k * jnp.exp(glast - gcc)))
        s_ref[hh] = s_old * jnp.exp(glast) + _mm(kdec_t, _pad_rows(v_new))
        o = _rms(o, gnorm_ref[...])
        z = proj_ref[0, :, P_Z + hh * GDN_DV:P_Z + (hh + 1) * GDN_DV]
        og.append(o * _silu(z))

    rc = rc_ref[...]
    rs1 = rs1_ref[...]
    rs2 = rs2_ref[...]

    def rope(x):
        return x * rc + pltpu.roll(x, LANES - ROT_DIM // 2, 1) * rs1 + pltpu.roll(x, ROT_DIM // 2, 1) * rs2

    k_own = rope(proj_ref[0, :, P_KS:P_KS + SWA_KV_W])
    v_own = proj_ref[0, :, P_VS:P_VS + SWA_KV_W]
    k_prev = kprev_ref[...]
    v_prev = vprev_ref[...]
    first_half = ci < SWA_HD
    q_rows = [None] * SWA_HEADS
    for p in range(SWA_HEADS // 2):
        qp = rope(proj_ref[0, :, P_QS + p * LANES:P_QS + (p + 1) * LANES])
        q_rows[p] = jnp.where(first_half, qp, 0.0)
        q_rows[SWA_HEADS // 2 + p] = jnp.where(first_half, 0.0, qp)
    q_all = jnp.concatenate(q_rows, axis=0)
    scale = SWA_HD ** -0.5
    s_prev = _mm_nt(q_all, k_prev) * scale
    s_own = _mm_nt(q_all, _pad_rows(k_own)) * scale
    if prev0_valid:
        prev_mask = ci > ri
    else:
        prev_mask = ci > ri + jnp.where(t > 0, 0, LANES)
    v_own_pad = _pad_rows(v_own)
    pv = []
    for hh in range(SWA_HEADS):
        sp = jnp.where(prev_mask, s_prev[hh * tb:(hh + 1) * tb], -jnp.inf)
        so = jnp.where(tri, s_own[hh * tb:(hh + 1) * tb], -jnp.inf)
        sk = sinks_ref[hh]
        m = jnp.maximum(jnp.maximum(jnp.max(sp, axis=-1, keepdims=True),
                                    jnp.max(so, axis=-1, keepdims=True)), sk)
        pp = jnp.exp(sp - m)
        po = jnp.exp(so - m)
        den = jnp.sum(pp, axis=-1, keepdims=True) + jnp.sum(po, axis=-1, keepdims=True) + jnp.exp(sk - m)
        pv.append((_mm(pp, v_prev) + _mm(po, v_own_pad)) / den)
    os_ = [jnp.where(first_half, pv[p], pv[SWA_HEADS // 2 + p]) for p in range(SWA_HEADS // 2)]

    o_ref[0] = jnp.concatenate(og + os_, axis=-1).astype(o_ref.dtype)

    if tb == WINDOW:
        k_new, v_new_cache = k_own, v_own
    else:
        k_new = jnp.concatenate([k_prev[tb:], k_own], axis=0)
        v_new_cache = jnp.concatenate([v_prev[tb:], v_own], axis=0)
    kprev_ref[...] = k_new
    vprev_ref[...] = v_new_cache
    xext_ref[0:SUBLANES, :] = xext_ref[tb:tb + SUBLANES, :]

    @pl.when(t == nt - 1)
    def _final():
        sout_ref[0] = s_ref[...]
        convout_ref[0] = xext_ref[tb + SUBLANES - (GDN_CONV - 1):tb + SUBLANES, :]
        kout_ref[0] = k_new
        vout_ref[0] = v_new_cache


def _mixer(proj3d, s0, conv0, k0, v0, convw, alog_row, dt_row, gnorm, sinks, rc, rs1, rs2, *, tb,
           prev0_valid):
    bsz, seq, _ = proj3d.shape
    assert seq % tb == 0
    nt = seq // tb
    kern = functools.partial(_mixer_kernel, tb=tb, prev0_valid=prev0_valid)
    batch_blk = lambda shape: pl.BlockSpec((1,) + shape, lambda b, t: (b,) + (0,) * len(shape))
    const_blk = lambda shape: pl.BlockSpec(shape, lambda b, t: (0,) * len(shape))
    time_blk = lambda shape: pl.BlockSpec(shape, lambda b, t: (t, 0))
    return pl.pallas_call(
        kern,
        grid=(bsz, nt),
        in_specs=[
            pl.BlockSpec(memory_space=pltpu.SMEM),
            pl.BlockSpec((1, tb, P_W), lambda b, t: (b, t, 0)),
            batch_blk((GDN_HEADS, GDN_DK, GDN_DV)),
            batch_blk((GDN_CONV - 1, GDN_CONV_W)),
            batch_blk((WINDOW, SWA_KV_W)),
            batch_blk((WINDOW, SWA_KV_W)),
            const_blk((GDN_CONV, GDN_CONV_W)),
            const_blk((1, LANES)),
            const_blk((1, LANES)),
            const_blk((1, GDN_DV)),
            time_blk((tb, LANES)),
            time_blk((tb, LANES)),
            time_blk((tb, LANES)),
        ],
        out_specs=[
            pl.BlockSpec((1, tb, D_MODEL), lambda b, t: (b, t, 0)),
            batch_blk((GDN_HEADS, GDN_DK, GDN_DV)),
            batch_blk((GDN_CONV - 1, GDN_CONV_W)),
            batch_blk((WINDOW, SWA_KV_W)),
            batch_blk((WINDOW, SWA_KV_W)),
        ],
        out_shape=[
            jax.ShapeDtypeStruct((bsz, seq, D_MODEL), BF16),
            jax.ShapeDtypeStruct((bsz, GDN_HEADS, GDN_DK, GDN_DV), F32),
            jax.ShapeDtypeStruct((bsz, GDN_CONV - 1, GDN_CONV_W), F32),
            jax.ShapeDtypeStruct((bsz, WINDOW, SWA_KV_W), F32),
            jax.ShapeDtypeStruct((bsz, WINDOW, SWA_KV_W), F32),
        ],
        scratch_shapes=[
            pltpu.VMEM((tb + SUBLANES, GDN_CONV_W), F32),
            pltpu.VMEM((GDN_HEADS, GDN_DK, GDN_DV), F32),
            pltpu.VMEM((WINDOW, SWA_KV_W), F32),
            pltpu.VMEM((WINDOW, SWA_KV_W), F32),
        ],
        compiler_params=pltpu.CompilerParams(
            dimension_semantics=("arbitrary", "arbitrary"), vmem_limit_bytes=VMEM_LIMIT_BYTES),
        name="mixer",
    )(sinks, proj3d, s0, conv0, k0, v0, convw, alog_row, dt_row, gnorm, rc, rs1, rs2)


def _rope_tables(pos):
    half = ROT_DIM // 2
    inv = ROPE_THETA ** (-jnp.arange(half, dtype=F32) * 2.0 / ROT_DIM)
    ang = pos.astype(F32)[:, None] * inv[None, :]
    cos, sin = jnp.cos(ang), jnp.sin(ang)
    n = pos.shape[0]
    rest = SWA_HD - ROT_DIM
    c = jnp.concatenate([cos, cos, jnp.ones((n, rest), F32)], axis=1)
    s1 = jnp.concatenate([-sin, jnp.zeros((n, half + rest), F32)], axis=1)
    s2 = jnp.concatenate([jnp.zeros((n, half), F32), sin, jnp.zeros((n, rest), F32)], axis=1)
    reps = LANES // SWA_HD
    return jnp.tile(c, (1, reps)), jnp.tile(s1, (1, reps)), jnp.tile(s2, (1, reps))


def _pair_order():
    order = []
    for p in range(SWA_HEADS // 2):
        order += [p, SWA_HEADS // 2 + p]
    return np.asarray(order)


def _layer_weights(norm_ffn1, ffn1_gu, ffn1_down, norm_mix, w_in, conv_w, a_log, dt_bias, gdn_norm, sinks,
                   w_out, norm_ffn2, ffn2_gu, ffn2_down, norm_ple, ple_proj, ple_gate, norm_final):
    order = _pair_order()
    head_cols = (order[:, None] * SWA_HD + np.arange(SWA_HD)[None, :]).reshape(-1)
    o_qkv, o_z = 0, GDN_CONV_W
    o_a = o_z + GDN_V_W
    o_b = o_a + GDN_HEADS
    o_qs = o_b + GDN_HEADS
    o_ks = o_qs + SWA_Q_W
    o_vs = o_ks + SWA_KV_W
    win_p = jnp.concatenate([
        w_in[:, o_qkv:o_z], w_in[:, o_z:o_a], w_in[:, o_qs:o_ks][:, head_cols], w_in[:, o_ks:o_vs],
        w_in[:, o_vs:o_vs + SWA_KV_W], w_in[:, o_a:o_qs],
        jnp.zeros((D_MODEL, LANES - 2 * GDN_HEADS), w_in.dtype)], axis=1).astype(BF16)
    wout_p = jnp.concatenate([w_out[:GDN_V_W], w_out[GDN_V_W:][head_cols]], axis=0).astype(BF16)
    pad_row = lambda v: jnp.concatenate([v.astype(F32), jnp.zeros((LANES - v.shape[0],), F32)])[None, :]
    return dict(
        g1=norm_ffn1[None, :], wgu1=ffn1_gu.astype(BF16), wd1=ffn1_down.astype(BF16),
        gm=norm_mix[None, :], win=win_p, convw=conv_w, alog=pad_row(a_log), dt=pad_row(dt_bias),
        gnorm=gdn_norm[None, :], sinks=sinks.astype(F32), wout=wout_p,
        g2=norm_ffn2[None, :], wgu2=ffn2_gu.astype(BF16), wd2=ffn2_down.astype(BF16),
        gp=norm_ple[None, :], wpp=ple_proj.astype(BF16), wpg=ple_gate.astype(BF16), gf=norm_final[None, :])


def _group(x, p, w, s0, conv0, k0, v0, pos0, tb, prev0_valid):
    bsz, seq, _ = x.shape
    h2d, proj2d = _ffn_in(x.reshape(bsz * seq, D_MODEL), w["g1"], w["wgu1"], w["wd1"], w["gm"], w["win"])
    rc, rs1, rs2 = _rope_tables(pos0 + jnp.arange(seq))
    o, s_new, conv_new, k_new, v_new = _mixer(
        proj2d.reshape(bsz, seq, P_W), s0, conv0, k0, v0, w["convw"], w["alog"], w["dt"], w["gnorm"],
        w["sinks"], rc, rs1, rs2, tb=tb, prev0_valid=prev0_valid)
    y = _ffn_out(o.reshape(bsz * seq, D_MODEL), h2d, p.reshape(bsz * seq, PLE_DIM), w["wout"], w["g2"],
                 w["wgu2"], w["wd2"], w["gp"], w["wpg"], w["wpp"], w["gf"])
    kv_shape = (1, bsz, WINDOW, SWA_KV_HEADS, SWA_HD)
    return (y.reshape(bsz, seq, D_MODEL), s_new[None], conv_new[None], k_new.reshape(kv_shape),
            v_new.reshape(kv_shape))


def kernel(x_prompt, x_sample, state_gdn, state_conv, cache_swa_k, cache_swa_v, p_prompt, p_sample, norm_ffn1, ffn1_gu, ffn1_down, norm_mix, w_in, conv_w, a_log, dt_bias, gdn_norm, sinks, w_out, norm_ffn2, ffn2_gu, ffn2_down, norm_ple, ple_proj, ple_gate, norm_final):
    assert state_gdn.shape[0] == 1, "one layer"
    w = _layer_weights(norm_ffn1[0], ffn1_gu[0], ffn1_down[0], norm_mix[0], w_in[0], conv_w[0], a_log[0],
                       dt_bias[0], gdn_norm[0], sinks[0], w_out[0], norm_ffn2[0], ffn2_gu[0], ffn2_down[0],
                       norm_ple[0], ple_proj[0], ple_gate[0], norm_final)
    bp = x_prompt.shape[0]
    bs = x_sample.shape[0]
    zeros = lambda shape: jnp.zeros(shape, F32)
    yp, sg_p, sc_p, kk_p, vv_p = _group(
        x_prompt, p_prompt[0], w, zeros((bp, GDN_HEADS, GDN_DK, GDN_DV)),
        zeros((bp, GDN_CONV - 1, GDN_CONV_W)), zeros((bp, WINDOW, SWA_KV_W)), zeros((bp, WINDOW, SWA_KV_W)),
        0, WINDOW, False)
    ys, sg_s, sc_s, kk_s, vv_s = _group(
        x_sample, p_sample[0], w, state_gdn[0], state_conv[0],
        cache_swa_k[0].reshape(bs, WINDOW, SWA_KV_W), cache_swa_v[0].reshape(bs, WINDOW, SWA_KV_W),
        PAST_LEN, x_sample.shape[1], True)
    return (yp, ys, sg_p, sc_p, kk_p, vv_p, sg_s, sc_s, kk_s, vv_s)
```

```python
import functools

import jax
import jax.numpy as jnp
import numpy as np
from jax.experimental import pallas as pl
from jax.experimental.pallas import tpu as pltpu

F32 = jnp.float32
BF16 = jnp.bfloat16

D_MODEL = 1024
D_FF = 2816
PLE_DIM = 256
NORM_EPS = 1e-6
GDN_HEADS = 4
GDN_DK = 128
GDN_DV = 128
GDN_CONV = 4
GDN_QK_W = GDN_HEADS * GDN_DK
GDN_V_W = GDN_HEADS * GDN_DV
GDN_CONV_W = 2 * GDN_QK_W + GDN_V_W
SWA_HEADS = 8
SWA_KV_HEADS = 2
SWA_HD = 64
SWA_Q_W = SWA_HEADS * SWA_HD
SWA_KV_W = SWA_KV_HEADS * SWA_HD
WINDOW = 128
ROT_DIM = SWA_HD // 4
ROPE_THETA = 500000.0
PAST_LEN = 16384

LANES = 128
SUBLANES = 8
VMEM_LIMIT_BYTES = 56 * 1024 * 1024

P_QKV = 0
P_Z = P_QKV + GDN_CONV_W
P_QS = P_Z + GDN_V_W
P_KS = P_QS + SWA_Q_W
P_VS = P_KS + SWA_KV_W
P_AB = P_VS + SWA_KV_W
P_W = P_AB + LANES

FF_CHUNK = 256
TOKEN_TILE = 512


def _rms(x, g):
    return x * jax.lax.rsqrt(jnp.mean(x * x, axis=-1, keepdims=True) + NORM_EPS) * g


def _sigmoid(x):
    return 1.0 / (1.0 + jnp.exp(-x))


def _silu(x):
    return x * _sigmoid(x)


def _mm(a, b):
    return jnp.dot(a.astype(BF16), b.astype(BF16), preferred_element_type=F32)


def _mm_nt(a, b):
    return jax.lax.dot_general(a.astype(BF16), b.astype(BF16), (((1,), (1,)), ((), ())),
                               preferred_element_type=F32)


def _swiglu_acc(n_bf16, wgu_ref, wd_ref):
    rows = n_bf16.shape[0]
    acc = jnp.zeros((rows, D_MODEL), F32)
    for c in range(D_FF // FF_CHUNK):
        lo = c * FF_CHUNK
        gate = jnp.dot(n_bf16, wgu_ref[:, lo:lo + FF_CHUNK], preferred_element_type=F32)
        up = jnp.dot(n_bf16, wgu_ref[:, D_FF + lo:D_FF + lo + FF_CHUNK], preferred_element_type=F32)
        act = (_silu(gate) * up).astype(BF16)
        acc = acc + jnp.dot(act, wd_ref[lo:lo + FF_CHUNK, :], preferred_element_type=F32)
    return acc


def _ffn_in_kernel(x_ref, g1_ref, wgu_ref, wd_ref, gm_ref, win_ref, h_ref, proj_ref):
    x = x_ref[...]
    n1 = _rms(x, g1_ref[...]).astype(BF16)
    h = x + 0.5 * _swiglu_acc(n1, wgu_ref, wd_ref)
    h_ref[...] = h
    n = _rms(h, gm_ref[...]).astype(BF16)
    proj_ref[...] = jnp.dot(n, win_ref[...], preferred_element_type=F32)


def _const_spec(shape):
    return pl.BlockSpec(shape, lambda i: (0,) * len(shape), pipeline_mode=pl.Buffered(1))


def _ffn_in(x2d, g1, wgu, wd, gm, win):
    n_tok = x2d.shape[0]
    tm = min(TOKEN_TILE, n_tok)
    assert n_tok % tm == 0
    return pl.pallas_call(
        _ffn_in_kernel,
        grid=(n_tok // tm,),
        in_specs=[
            pl.BlockSpec((tm, D_MODEL), lambda i: (i, 0)),
            _const_spec((1, D_MODEL)),
            _const_spec((D_MODEL, 2 * D_FF)),
            _const_spec((D_FF, D_MODEL)),
            _const_spec((1, D_MODEL)),
            _const_spec((D_MODEL, P_W)),
        ],
        out_specs=[
            pl.BlockSpec((tm, D_MODEL), lambda i: (i, 0)),
            pl.BlockSpec((tm, P_W), lambda i: (i, 0)),
        ],
        out_shape=[
            jax.ShapeDtypeStruct((n_tok, D_MODEL), F32),
            jax.ShapeDtypeStruct((n_tok, P_W), F32),
        ],
        compiler_params=pltpu.CompilerParams(
            dimension_semantics=("arbitrary",), vmem_limit_bytes=VMEM_LIMIT_BYTES),
        name="ffn_in",
    )(x2d, g1, wgu, wd, gm, win)


def _ffn_out_kernel(o_ref, h_ref, p_ref, wout_ref, g2_ref, wgu_ref, wd_ref, gp_ref, wpg_ref, wpp_ref,
                    gf_ref, y_ref):
    h = h_ref[...] + jnp.dot(o_ref[...], wout_ref[...], preferred_element_type=F32)
    n2 = _rms(h, g2_ref[...]).astype(BF16)
    h = h + 0.5 * _swiglu_acc(n2, wgu_ref, wd_ref)
    npl = _rms(h, gp_ref[...]).astype(BF16)
    gate = _sigmoid(jnp.dot(npl, wpg_ref[...], preferred_element_type=F32))
    pe = jnp.dot(p_ref[...].astype(BF16), wpp_ref[...], preferred_element_type=F32)
    h = h + gate * pe
    y_ref[...] = _rms(h, gf_ref[...])


def _ffn_out(o2d, h2d, p2d, wout, g2, wgu, wd, gp, wpg, wpp, gf):
    n_tok = h2d.shape[0]
    tm = min(TOKEN_TILE, n_tok)
    assert n_tok % tm == 0
    return pl.pallas_call(
        _ffn_out_kernel,
        grid=(n_tok // tm,),
        in_specs=[
            pl.BlockSpec((tm, D_MODEL), lambda i: (i, 0)),
            pl.BlockSpec((tm, D_MODEL), lambda i: (i, 0)),
            pl.BlockSpec((tm, PLE_DIM), lambda i: (i, 0)),
            _const_spec((D_MODEL, D_MODEL)),
            _const_spec((1, D_MODEL)),
            _const_spec((D_MODEL, 2 * D_FF)),
            _const_spec((D_FF, D_MODEL)),
            _const_spec((1, D_MODEL)),
            _const_spec((D_MODEL, D_MODEL)),
            _const_spec((PLE_DIM, D_MODEL)),
            _const_spec((1, D_MODEL)),
        ],
        out_specs=pl.BlockSpec((tm, D_MODEL), lambda i: (i, 0)),
        out_shape=jax.ShapeDtypeStruct((n_tok, D_MODEL), F32),
        compiler_params=pltpu.CompilerParams(
            dimension_semantics=("arbitrary",), vmem_limit_bytes=VMEM_LIMIT_BYTES),
        name="ffn_out",
    )(o2d, h2d, p2d, wout, g2, wgu, wd, gp, wpg, wpp, gf)


def _pad_rows(x):
    rows = x.shape[0]
    if rows == LANES:
        return x
    return jnp.concatenate([x, jnp.zeros((LANES - rows, x.shape[1]), x.dtype)], axis=0)


def _split3(x):
    hi = x.astype(BF16)
    r = x - hi.astype(F32)
    mid = r.astype(BF16)
    lo = (r - mid.astype(F32)).astype(BF16)
    return hi, mid, lo


def _unit_lower_inverse(a, rows):
    ri = jax.lax.broadcasted_iota(jnp.int32, (rows, LANES), 0)
    ci = jax.lax.broadcasted_iota(jnp.int32, (rows, LANES), 1)
    eye = (ri == ci).astype(F32)
    blk = ri ^ ci
    a0 = jnp.where(blk < SUBLANES, a, 0.0)
    x = eye - a0
    p = _mm(a0, _pad_rows(a0))
    x = x + _mm(x, _pad_rows(p))
    p = _mm(p, _pad_rows(p))
    x = x + _mm(x, _pad_rows(p))
    s = SUBLANES
    while s < rows:
        off = jnp.where((blk < 2 * s) & (blk >= s), a, 0.0)
        y = _mm(off, _pad_rows(x))
        x = x - _mm(x, _pad_rows(y))
        s *= 2
    return x


def _mixer_kernel(sinks_ref, proj_ref, s0_ref, conv0_ref, k0_ref, v0_ref, convw_ref, alog_ref, dt_ref,
                  gnorm_ref, rc_ref, rs1_ref, rs2_ref,
                  o_ref, sout_ref, convout_ref, kout_ref, vout_ref,
                  xext_ref, s_ref, kprev_ref, vprev_ref, *, tb, prev0_valid):
    t = pl.program_id(1)
    nt = pl.num_programs(1)

    @pl.when(t == 0)
    def _init():
        xext_ref[0:SUBLANES, :] = jnp.zeros((SUBLANES, GDN_CONV_W), F32)
        xext_ref[SUBLANES - (GDN_CONV - 1):SUBLANES, :] = conv0_ref[0]
        s_ref[...] = s0_ref[0]
        kprev_ref[...] = k0_ref[0]
        vprev_ref[...] = v0_ref[0]

    xext_ref[SUBLANES:SUBLANES + tb, :] = proj_ref[0, :, P_QKV:P_QKV + GDN_CONV_W]
    base = SUBLANES - (GDN_CONV - 1)
    conv = xext_ref[base:base + tb, :] * convw_ref[0:1, :]
    for j in range(1, GDN_CONV):
        conv = conv + xext_ref[base + j:base + j + tb, :] * convw_ref[j:j + 1, :]
    qkv = _silu(conv)

    ab = proj_ref[0, :, P_AB:P_AB + LANES]
    zab = ab + dt_ref[...]
    softplus = jnp.maximum(zab, 0.0) + jnp.log(1.0 + jnp.exp(-jnp.abs(zab)))
    g_all = -jnp.exp(alog_ref[...]) * softplus
    beta_all = _sigmoid(ab)

    ri = jax.lax.broadcasted_iota(jnp.int32, (tb, LANES), 0)
    ci = jax.lax.broadcasted_iota(jnp.int32, (tb, LANES), 1)
    tri = ci <= ri
    strict = ci < ri
    r128 = jax.lax.broadcasted_iota(jnp.int32, (LANES, LANES), 0)
    c128 = jax.lax.broadcasted_iota(jnp.int32, (LANES, LANES), 1)
    ltri = tri.astype(BF16)
    utri = (r128 <= c128).astype(BF16)

    g_pad = _pad_rows(g_all)
    g_hi, g_mid, g_lo = _split3(g_pad)
    gc_cols = (jnp.dot(ltri, g_hi, preferred_element_type=F32)
               + jnp.dot(ltri, g_mid, preferred_element_type=F32)
               + jnp.dot(ltri, g_lo, preferred_element_type=F32))
    gt = jnp.transpose(g_pad)[0:SUBLANES, :]
    t_hi, t_mid, t_lo = _split3(gt)
    gc_rows = (jnp.dot(t_hi, utri, preferred_element_type=F32)
               + jnp.dot(t_mid, utri, preferred_element_type=F32)
               + jnp.dot(t_lo, utri, preferred_element_type=F32))

    og = []
    for hh in range(GDN_HEADS):
        lo = hh * GDN_DK
        q = qkv[:, lo:lo + GDN_DK]
        k = qkv[:, GDN_QK_W + lo:GDN_QK_W + lo + GDN_DK]
        v = qkv[:, 2 * GDN_QK_W + lo:2 * GDN_QK_W + lo + GDN_DV]
        q = q * jax.lax.rsqrt(jnp.sum(q * q, axis=-1, keepdims=True) + 1e-6) * (GDN_DK ** -0.5)
        k = k * jax.lax.rsqrt(jnp.sum(k * k, axis=-1, keepdims=True) + 1e-6)
        beta = beta_all[:, GDN_HEADS + hh:GDN_HEADS + hh + 1]
        gcc = gc_cols[:, hh:hh + 1]
        gcr = gc_rows[hh:hh + 1, :]
        glast = gc_cols[tb - 1:tb, hh:hh + 1]
        decay = jnp.exp(jnp.where(tri, gcc - gcr, -jnp.inf))
        kb = k * beta
        vb = v * beta
        k_pad = _pad_rows(k)
        a_mat = jnp.where(strict, _mm_nt(kb, k_pad) * decay, 0.0)
        qk = _mm_nt(q, k_pad) * decay
        tinv = _unit_lower_inverse(a_mat, tb)
        egc = jnp.exp(gcc)
        u = _mm(tinv, _pad_rows(vb))
        w = _mm(tinv, _pad_rows(kb * egc))
        s_old = s_ref[hh]
        v_new = u - _mm(w, s_old)
        o = _mm(q * egc, s_old) + _mm(qk, _pad_rows(v_new))
        kdec_t = jnp.transpose(_pad_rows(k * jnp.exp(glast - gcc)))
        s_ref[hh] = s_old * jnp.exp(glast) + _mm(kdec_t, _pad_rows(v_new))
        o = _rms(o, gnorm_ref[...])
        z = proj_ref[0, :, P_Z + hh * GDN_DV:P_Z + (hh + 1) * GDN_DV]
        og.append(o * _silu(z))

    rc = rc_ref[...]
    rs1 = rs1_ref[...]
    rs2 = rs2_ref[...]

    def rope(x):
        return x * rc + pltpu.roll(x, LANES - ROT_DIM // 2, 1) * rs1 + pltpu.roll(x, ROT_DIM // 2, 1) * rs2

    k_own = rope(proj_ref[0, :, P_KS:P_KS + SWA_KV_W])
    v_own = proj_ref[0, :, P_VS:P_VS + SWA_KV_W]
    k_prev = kprev_ref[...]
    v_prev = vprev_ref[...]
    first_half = ci < SWA_HD
    q_rows = [None] * SWA_HEADS
    for p in range(SWA_HEADS // 2):
        qp = rope(proj_ref[0, :, P_QS + p * LANES:P_QS + (p + 1) * LANES])
        q_rows[p] = jnp.where(first_half, qp, 0.0)
        q_rows[SWA_HEADS // 2 + p] = jnp.where(first_half, 0.0, qp)
    q_all = jnp.concatenate(q_rows, axis=0)
    scale = SWA_HD ** -0.5
    s_prev = _mm_nt(q_all, k_prev) * scale
    s_own = _mm_nt(q_all, _pad_rows(k_own)) * scale
    if prev0_valid:
        prev_mask = ci > ri
    else:
        prev_mask = ci > ri + jnp.where(t > 0, 0, LANES)
    v_own_pad = _pad_rows(v_own)
    pv = []
    for hh in range(SWA_HEADS):
        sp = jnp.where(prev_mask, s_prev[hh * tb:(hh + 1) * tb], -jnp.inf)
        so = jnp.where(tri, s_own[hh * tb:(hh + 1) * tb], -jnp.inf)
        sk = sinks_ref[hh]
        m = jnp.maximum(jnp.maximum(jnp.max(sp, axis=-1, keepdims=True),
                                    jnp.max(so, axis=-1, keepdims=True)), sk)
        pp = jnp.exp(sp - m)
        po = jnp.exp(so - m)
        den = jnp.sum(pp, axis=-1, keepdims=True) + jnp.sum(po, axis=-1, keepdims=True) + jnp.exp(sk - m)
        pv.append((_mm(pp, v_prev) + _mm(po, v_own_pad)) / den)
    os_ = [jnp.where(first_half, pv[p], pv[SWA_HEADS // 2 + p]) for p in range(SWA_HEADS // 2)]

    o_ref[0] = jnp.concatenate(og + os_, axis=-1).astype(o_ref.dtype)

    if tb == WINDOW:
        k_new, v_new_cache = k_own, v_own
    else:
        k_new = jnp.concatenate([k_prev[tb:], k_own], axis=0)
        v_new_cache = jnp.concatenate([v_prev[tb:], v_own], axis=0)
    kprev_ref[...] = k_new
    vprev_ref[...] = v_new_cache
    xext_ref[0:SUBLANES, :] = xext_ref[tb:tb + SUBLANES, :]

    @pl.when(t == nt - 1)
    def _final():
        sout_ref[0] = s_ref[...]
        convout_ref[0] = xext_ref[tb + SUBLANES - (GDN_CONV - 1):tb + SUBLANES, :]
        kout_ref[0] = k_new
        vout_ref[0] = v_new_cache


def _mixer(proj3d, s0, conv0, k0, v0, convw, alog_row, dt_row, gnorm, sinks, rc, rs1, rs2, *, tb,
           prev0_valid):
    bsz, seq, _ = proj3d.shape
    assert seq % tb == 0
    nt = seq // tb
    kern = functools.partial(_mixer_kernel, tb=tb, prev0_valid=prev0_valid)
    batch_blk = lambda shape: pl.BlockSpec((1,) + shape, lambda b, t: (b,) + (0,) * len(shape))
    const_blk = lambda shape: pl.BlockSpec(shape, lambda b, t: (0,) * len(shape))
    time_blk = lambda shape: pl.BlockSpec(shape, lambda b, t: (t, 0))
    return pl.pallas_call(
        kern,
        grid=(bsz, nt),
        in_specs=[
            pl.BlockSpec(memory_space=pltpu.SMEM),
            pl.BlockSpec((1, tb, P_W), lambda b, t: (b, t, 0)),
            batch_blk((GDN_HEADS, GDN_DK, GDN_DV)),
            batch_blk((GDN_CONV - 1, GDN_CONV_W)),
            batch_blk((WINDOW, SWA_KV_W)),
            batch_blk((WINDOW, SWA_KV_W)),
            const_blk((GDN_CONV, GDN_CONV_W)),
            const_blk((1, LANES)),
            const_blk((1, LANES)),
            const_blk((1, GDN_DV)),
            time_blk((tb, LANES)),
            time_blk((tb, LANES)),
            time_blk((tb, LANES)),
        ],
        out_specs=[
            pl.BlockSpec((1, tb, D_MODEL), lambda b, t: (b, t, 0)),
            batch_blk((GDN_HEADS, GDN_DK, GDN_DV)),
            batch_blk((GDN_CONV - 1, GDN_CONV_W)),
            batch_blk((WINDOW, SWA_KV_W)),
            batch_blk((WINDOW, SWA_KV_W)),
        ],
        out_shape=[
            jax.ShapeDtypeStruct((bsz, seq, D_MODEL), BF16),
            jax.ShapeDtypeStruct((bsz, GDN_HEADS, GDN_DK, GDN_DV), F32),
            jax.ShapeDtypeStruct((bsz, GDN_CONV - 1, GDN_CONV_W), F32),
            jax.ShapeDtypeStruct((bsz, WINDOW, SWA_KV_W), F32),
            jax.ShapeDtypeStruct((bsz, WINDOW, SWA_KV_W), F32),
        ],
        scratch_shapes=[
            pltpu.VMEM((tb + SUBLANES, GDN_CONV_W), F32),
            pltpu.VMEM((GDN_HEADS, GDN_DK, GDN_DV), F32),
            pltpu.VMEM((WINDOW, SWA_KV_W), F32),
            pltpu.VMEM((WINDOW, SWA_KV_W), F32),
        ],
        compiler_params=pltpu.CompilerParams(
            dimension_semantics=("arbitrary", "arbitrary"), vmem_limit_bytes=VMEM_LIMIT_BYTES),
        name="mixer",
    )(sinks, proj3d, s0, conv0, k0, v0, convw, alog_row, dt_row, gnorm, rc, rs1, rs2)


CHUNK = 64
CAT_W = GDN_HEADS * CHUNK


def _tile_rows(y, n):
    return jnp.concatenate([y] * n, axis=0)


def _mixer_prompt_kernel(sinks_ref, proj_ref, convw_ref, alog_ref, dt_ref, gnorm_ref, rc_ref, rs1_ref,
                         rs2_ref, o_ref, sout_ref, convout_ref, kout_ref, vout_ref,
                         xext_ref, s_ref, kprev_ref, vprev_ref, *, tb):
    t = pl.program_id(1)
    nt = pl.num_programs(1)
    nch = tb // CHUNK

    @pl.when(t == 0)
    def _init():
        xext_ref[0:SUBLANES, :] = jnp.zeros((SUBLANES, GDN_CONV_W), F32)
        s_ref[...] = jnp.zeros(s_ref.shape, F32)
        kprev_ref[...] = jnp.zeros(kprev_ref.shape, F32)
        vprev_ref[...] = jnp.zeros(vprev_ref.shape, F32)

    s_cat = s_ref[...]
    k_prev = kprev_ref[...]
    v_prev = vprev_ref[...]

    xext_ref[SUBLANES:SUBLANES + tb, :] = proj_ref[0, :, P_QKV:P_QKV + GDN_CONV_W]
    base = SUBLANES - (GDN_CONV - 1)
    conv = xext_ref[base:base + tb, :] * convw_ref[0:1, :]
    for j in range(1, GDN_CONV):
        conv = conv + xext_ref[base + j:base + j + tb, :] * convw_ref[j:j + 1, :]
    qkv = _silu(conv)
    conv_tail = xext_ref[tb:tb + SUBLANES, :]

    r_t = jax.lax.broadcasted_iota(jnp.int32, (tb, LANES), 0)
    c_t = jax.lax.broadcasted_iota(jnp.int32, (tb, LANES), 1)
    lane_lo = c_t < CHUNK

    rc = rc_ref[...]
    rs1 = rs1_ref[...]
    rs2 = rs2_ref[...]

    def rope(x):
        return x * rc + pltpu.roll(x, LANES - ROT_DIM // 2, 1) * rs1 + pltpu.roll(x, ROT_DIM // 2, 1) * rs2

    k_own = rope(proj_ref[0, :, P_KS:P_KS + SWA_KV_W])
    v_own = proj_ref[0, :, P_VS:P_VS + SWA_KV_W]
    k_cat = jnp.concatenate([k_prev, k_own], axis=0)
    v_cat = jnp.concatenate([v_prev, v_own], axis=0).astype(BF16)
    q_rows = [None] * SWA_HEADS
    for p in range(SWA_HEADS // 2):
        qp = rope(proj_ref[0, :, P_QS + p * LANES:P_QS + (p + 1) * LANES])
        q_rows[p] = jnp.where(lane_lo, qp, 0.0)
        q_rows[SWA_HEADS // 2 + p] = jnp.where(lane_lo, 0.0, qp)
    s_all = _mm_nt(jnp.concatenate(q_rows, axis=0), k_cat) * (SWA_HD ** -0.5)
    rk = jax.lax.broadcasted_iota(jnp.int32, (tb, 2 * WINDOW), 0)
    ck = jax.lax.broadcasted_iota(jnp.int32, (tb, 2 * WINDOW), 1)
    first_off = jnp.where(t > 0, 0, WINDOW)
    dist = ck - rk
    kmask = (dist > jnp.where(ck < WINDOW, first_off, -WINDOW)) & (dist <= jnp.where(ck < WINDOW, 2 * WINDOW, WINDOW))
    pv = [None] * SWA_HEADS

    def swa_head(hh):
        sh = jnp.where(kmask, s_all[hh * tb:(hh + 1) * tb], -jnp.inf)
        sk = sinks_ref[hh]
        m = jnp.maximum(jnp.max(sh, axis=-1, keepdims=True), sk)
        ph = jnp.exp(sh - m)
        den = jnp.sum(ph, axis=-1, keepdims=True) + jnp.exp(sk - m)
        pv[hh] = jnp.dot(ph.astype(BF16), v_cat, preferred_element_type=F32) / den

    pending_heads = list(range(SWA_HEADS))

    def fill():
        if pending_heads:
            swa_head(pending_heads.pop(0))

    def head_l2(x):
        parts = []
        for hh in range(GDN_HEADS):
            xh = x[:, hh * GDN_DK:(hh + 1) * GDN_DK]
            parts.append(xh * jax.lax.rsqrt(jnp.sum(xh * xh, axis=-1, keepdims=True) + 1e-6))
        return jnp.concatenate(parts, axis=1)

    q_all = head_l2(qkv[:, 0:GDN_QK_W]) * (GDN_DK ** -0.5)
    k_all = head_l2(qkv[:, GDN_QK_W:2 * GDN_QK_W])
    v_all = qkv[:, 2 * GDN_QK_W:]

    ab = proj_ref[0, :, P_AB:P_AB + LANES]
    zab = ab + dt_ref[...]
    softplus = jnp.maximum(zab, 0.0) + jnp.log(1.0 + jnp.exp(-jnp.abs(zab)))
    g_all = -jnp.exp(alog_ref[...]) * softplus
    beta_all = _sigmoid(ab)

    same_chunk = (r_t ^ c_t) < CHUNK
    lblk = ((c_t <= r_t) & same_chunk).astype(BF16)
    ublk = ((r_t <= c_t) & same_chunk).astype(BF16)
    g_hi, g_mid, g_lo = _split3(g_all)
    gc_cols = (jnp.dot(lblk, g_hi, preferred_element_type=F32)
               + jnp.dot(lblk, g_mid, preferred_element_type=F32)
               + jnp.dot(lblk, g_lo, preferred_element_type=F32))
    gt = jnp.transpose(g_all)[0:SUBLANES, :]
    t_hi, t_mid, t_lo = _split3(gt)
    gc_rows = (jnp.dot(t_hi, ublk, preferred_element_type=F32)
               + jnp.dot(t_mid, ublk, preferred_element_type=F32)
               + jnp.dot(t_lo, ublk, preferred_element_type=F32))
    gc_rows_sw = pltpu.roll(gc_rows, CHUNK, 1)

    lane_lo_chunk = jax.lax.broadcasted_iota(jnp.int32, (CHUNK, LANES), 1) < CHUNK
    lane_lo_row = jax.lax.broadcasted_iota(jnp.int32, (1, LANES), 1) < CHUNK
    gc_full = jnp.concatenate(
        [jnp.broadcast_to(gc_cols[:, hh:hh + 1], (tb, LANES)) for hh in range(GDN_HEADS)], axis=1)
    beta_full = jnp.concatenate(
        [jnp.broadcast_to(beta_all[:, GDN_HEADS + hh:GDN_HEADS + hh + 1], (tb, LANES))
         for hh in range(GDN_HEADS)], axis=1)
    egc_full = jnp.exp(gc_full)
    kb_all = k_all * beta_full
    vb_all = v_all * beta_full
    kbg_all = kb_all * egc_full
    qd_all = q_all * egc_full

    ri = jax.lax.broadcasted_iota(jnp.int32, (CHUNK, CAT_W), 0)
    ci = jax.lax.broadcasted_iota(jnp.int32, (CHUNK, CAT_W), 1)
    cj = ci & (CHUNK - 1)
    tri = cj <= ri
    strict = cj < ri
    eye = (cj == ri).astype(F32)
    blk = ri ^ cj
    zero_tile = jnp.zeros((CHUNK, LANES), F32)

    def bd(y):
        blocks = []
        for hh in range(GDN_HEADS):
            y_tile = y[:, (hh // 2) * LANES:(hh // 2 + 1) * LANES]
            keep = jnp.where(lane_lo_chunk, y_tile, 0.0) if hh % 2 == 0 else jnp.where(lane_lo_chunk, 0.0, y_tile)
            blocks.append(jnp.concatenate([keep, zero_tile] if hh < 2 else [zero_tile, keep], axis=1))
        return jnp.concatenate(blocks, axis=0).astype(BF16)

    def bd_w(y):
        blocks = []
        for hh in range(GDN_HEADS):
            tiles = [zero_tile] * GDN_HEADS
            tiles[hh] = y[:, hh * LANES:(hh + 1) * LANES]
            blocks.append(jnp.concatenate(tiles, axis=1))
        return jnp.concatenate(blocks, axis=0).astype(BF16)

    def mmbd(x, y_bd):
        return jnp.dot(x.astype(BF16), y_bd, preferred_element_type=F32)

    chunks = range(nch)
    row_sl = [slice(c * CHUNK, (c + 1) * CHUNK) for c in chunks]
    a_mat, qk, xinv, pw = [None] * nch, [None] * nch, [None] * nch, [None] * nch
    for c in chunks:
        rows = row_sl[c]
        gcc_pairs, gcr_pairs = [], []
        for p in range(GDN_HEADS // 2):
            gcc_pairs.append(jnp.where(lane_lo_chunk, gc_full[rows, (2 * p) * LANES:(2 * p + 1) * LANES],
                                       gc_full[rows, (2 * p + 1) * LANES:(2 * p + 2) * LANES]))
            ra = (gc_rows if c == 0 else gc_rows_sw)[2 * p:2 * p + 1, :]
            rb_ = (gc_rows_sw if c == 0 else gc_rows)[2 * p + 1:2 * p + 2, :]
            gcr_pairs.append(jnp.where(lane_lo_row, ra, rb_))
        gcc = jnp.concatenate(gcc_pairs, axis=1)
        gcr = jnp.concatenate(gcr_pairs, axis=1)
        decay = jnp.exp(jnp.where(tri, gcc - gcr, -jnp.inf))
        kq = jax.lax.dot_general(
            jnp.concatenate([kb_all[rows], q_all[rows]], axis=0).astype(BF16), bd_w(k_all[rows]),
            (((1,), (1,)), ((), ())), preferred_element_type=F32)
        a_mat[c] = jnp.where(strict, kq[0:CHUNK] * decay, 0.0)
        qk[c] = kq[CHUNK:] * decay
    fill()
    a0 = [jnp.where(blk < SUBLANES, a_mat[c], 0.0) for c in chunks]
    for c in chunks:
        xinv[c] = eye - a0[c]
        pw[c] = mmbd(a0[c], bd(a0[c]))
    fill()
    for c in chunks:
        pw_bd = bd(pw[c])
        xinv[c] = xinv[c] + mmbd(xinv[c], pw_bd)
        pw[c] = mmbd(pw[c], pw_bd)
    fill()
    for c in chunks:
        xinv[c] = xinv[c] + mmbd(xinv[c], bd(pw[c]))
    fill()
    s = SUBLANES
    while s < CHUNK:
        y = [mmbd(jnp.where((blk < 2 * s) & (blk >= s), a_mat[c], 0.0), bd(xinv[c])) for c in chunks]
        fill()
        for c in chunks:
            xinv[c] = xinv[c] - mmbd(xinv[c], bd(y[c]))
        fill()
        s *= 2
    prep = []
    for c in chunks:
        rows = row_sl[c]
        rhs = jnp.concatenate(
            [jnp.concatenate([vb_all[rows, hh * GDN_DV:(hh + 1) * GDN_DV],
                              kbg_all[rows, hh * GDN_DK:(hh + 1) * GDN_DK]], axis=1)
             for hh in range(GDN_HEADS)], axis=0)
        uw = _mm(bd(xinv[c]), rhs)
        u_all = jnp.concatenate([uw[hh * CHUNK:(hh + 1) * CHUNK, 0:GDN_DV] for hh in range(GDN_HEADS)], axis=1)
        w_all = jnp.concatenate([uw[hh * CHUNK:(hh + 1) * CHUNK, GDN_DV:] for hh in range(GDN_HEADS)], axis=1)
        glast = gc_full[c * CHUNK + CHUNK - 1:c * CHUNK + CHUNK, :]
        kdec = k_all[rows] * jnp.exp(glast - gc_full[rows])
        kdec_t = jnp.transpose(jnp.concatenate(
            [kdec[:, hh * GDN_DK:(hh + 1) * GDN_DK] for hh in range(GDN_HEADS)], axis=0))
        prep.append((u_all, w_all, qk[c], kdec_t, jnp.exp(glast)))
    fill()

    zeros_s = jnp.zeros((GDN_DK, GDN_DV), F32)
    o_chunks = []
    for c in range(nch):
        rows = slice(c * CHUNK, (c + 1) * CHUNK)
        u_all, w_all, qk, kdec_t, eg_last = prep[c]
        ws, qs = [], []
        for p in range(GDN_HEADS // 2):
            cols = slice(2 * p * LANES, (2 * p + 2) * LANES)
            sa = s_cat[:, (2 * p) * LANES:(2 * p + 1) * LANES]
            sb = s_cat[:, (2 * p + 1) * LANES:(2 * p + 2) * LANES]
            s_bd = jnp.concatenate([jnp.concatenate([sa, zeros_s], axis=1),
                                    jnp.concatenate([zeros_s, sb], axis=1)], axis=0)
            r = _mm(jnp.concatenate([w_all[:, cols], qd_all[rows, cols]], axis=0), s_bd)
            ws.append(r[0:CHUNK])
            qs.append(r[CHUNK:])
        v_new = u_all - jnp.concatenate(ws, axis=1)
        r2 = mmbd(jnp.concatenate([qk, kdec_t], axis=0), bd_w(v_new))
        o_chunks.append(jnp.concatenate(qs, axis=1) + r2[0:CHUNK])
        s_cat = s_cat * eg_last + r2[CHUNK:]
    while pending_heads:
        fill()
    os_ = [jnp.where(lane_lo, pv[p], pv[SWA_HEADS // 2 + p]) for p in range(SWA_HEADS // 2)]
    o_gdn = jnp.concatenate(o_chunks, axis=0)
    og = []
    for hh in range(GDN_HEADS):
        oh = _rms(o_gdn[:, hh * GDN_DV:(hh + 1) * GDN_DV], gnorm_ref[...])
        og.append(oh * _silu(proj_ref[0, :, P_Z + hh * GDN_DV:P_Z + (hh + 1) * GDN_DV]))

    o_ref[0] = jnp.concatenate(og + os_, axis=-1).astype(o_ref.dtype)

    s_ref[...] = s_cat
    kprev_ref[...] = k_own
    vprev_ref[...] = v_own
    xext_ref[0:SUBLANES, :] = conv_tail

    @pl.when(t == nt - 1)
    def _final():
        for hh in range(GDN_HEADS):
            sout_ref[0, hh] = s_ref[:, hh * GDN_DV:(hh + 1) * GDN_DV]
        convout_ref[0] = xext_ref[tb + SUBLANES - (GDN_CONV - 1):tb + SUBLANES, :]
        kout_ref[0] = k_own
        vout_ref[0] = v_own


def _mixer_prompt(proj3d, convw, alog_row, dt_row, gnorm, sinks, rc, rs1, rs2):
    bsz, seq, _ = proj3d.shape
    tb = WINDOW
    assert seq % tb == 0
    kern = functools.partial(_mixer_prompt_kernel, tb=tb)
    batch_blk = lambda shape: pl.BlockSpec((1,) + shape, lambda b, t: (b,) + (0,) * len(shape))
    const_blk = lambda shape: pl.BlockSpec(shape, lambda b, t: (0,) * len(shape))
    time_blk = lambda shape: pl.BlockSpec(shape, lambda b, t: (t, 0))
    return pl.pallas_call(
        kern,
        grid=(bsz, seq // tb),
        in_specs=[
            pl.BlockSpec(memory_space=pltpu.SMEM),
            pl.BlockSpec((1, tb, P_W), lambda b, t: (b, t, 0)),
            const_blk((GDN_CONV, GDN_CONV_W)),
            const_blk((1, LANES)),
            const_blk((1, LANES)),
            const_blk((1, GDN_DV)),
            time_blk((tb, LANES)),
            time_blk((tb, LANES)),
            time_blk((tb, LANES)),
        ],
        out_specs=[
            pl.BlockSpec((1, tb, D_MODEL), lambda b, t: (b, t, 0)),
            batch_blk((GDN_HEADS, GDN_DK, GDN_DV)),
            batch_blk((GDN_CONV - 1, GDN_CONV_W)),
            batch_blk((WINDOW, SWA_KV_W)),
            batch_blk((WINDOW, SWA_KV_W)),
        ],
        out_shape=[
            jax.ShapeDtypeStruct((bsz, seq, D_MODEL), BF16),
            jax.ShapeDtypeStruct((bsz, GDN_HEADS, GDN_DK, GDN_DV), F32),
            jax.ShapeDtypeStruct((bsz, GDN_CONV - 1, GDN_CONV_W), F32),
            jax.ShapeDtypeStruct((bsz, WINDOW, SWA_KV_W), F32),
            jax.ShapeDtypeStruct((bsz, WINDOW, SWA_KV_W), F32),
        ],
        scratch_shapes=[
            pltpu.VMEM((tb + SUBLANES, GDN_CONV_W), F32),
            pltpu.VMEM((GDN_DK, GDN_HEADS * GDN_DV), F32),
            pltpu.VMEM((WINDOW, SWA_KV_W), F32),
            pltpu.VMEM((WINDOW, SWA_KV_W), F32),
        ],
        compiler_params=pltpu.CompilerParams(
            dimension_semantics=("arbitrary", "arbitrary"), vmem_limit_bytes=VMEM_LIMIT_BYTES),
        name="mixer_prompt",
    )(sinks, proj3d, convw, alog_row, dt_row, gnorm, rc, rs1, rs2)


def _rope_tables(pos):
    half = ROT_DIM // 2
    inv = ROPE_THETA ** (-jnp.arange(half, dtype=F32) * 2.0 / ROT_DIM)
    ang = pos.astype(F32)[:, None] * inv[None, :]
    cos, sin = jnp.cos(ang), jnp.sin(ang)
    n = pos.shape[0]
    rest = SWA_HD - ROT_DIM
    c = jnp.concatenate([cos, cos, jnp.ones((n, rest), F32)], axis=1)
    s1 = jnp.concatenate([-sin, jnp.zeros((n, half + rest), F32)], axis=1)
    s2 = jnp.concatenate([jnp.zeros((n, half), F32), sin, jnp.zeros((n, rest), F32)], axis=1)
    reps = LANES // SWA_HD
    return jnp.tile(c, (1, reps)), jnp.tile(s1, (1, reps)), jnp.tile(s2, (1, reps))


def _pair_order():
    order = []
    for p in range(SWA_HEADS // 2):
        order += [p, SWA_HEADS // 2 + p]
    return np.asarray(order)


def _layer_weights(norm_ffn1, ffn1_gu, ffn1_down, norm_mix, w_in, conv_w, a_log, dt_bias, gdn_norm, sinks,
                   w_out, norm_ffn2, ffn2_gu, ffn2_down, norm_ple, ple_proj, ple_gate, norm_final):
    order = _pair_order()
    head_cols = (order[:, None] * SWA_HD + np.arange(SWA_HD)[None, :]).reshape(-1)
    o_qkv, o_z = 0, GDN_CONV_W
    o_a = o_z + GDN_V_W
    o_b = o_a + GDN_HEADS
    o_qs = o_b + GDN_HEADS
    o_ks = o_qs + SWA_Q_W
    o_vs = o_ks + SWA_KV_W
    win_p = jnp.concatenate([
        w_in[:, o_qkv:o_z], w_in[:, o_z:o_a], w_in[:, o_qs:o_ks][:, head_cols], w_in[:, o_ks:o_vs],
        w_in[:, o_vs:o_vs + SWA_KV_W], w_in[:, o_a:o_qs],
        jnp.zeros((D_MODEL, LANES - 2 * GDN_HEADS), w_in.dtype)], axis=1).astype(BF16)
    wout_p = jnp.concatenate([w_out[:GDN_V_W], w_out[GDN_V_W:][head_cols]], axis=0).astype(BF16)
    pad_row = lambda v: jnp.concatenate([v.astype(F32), jnp.zeros((LANES - v.shape[0],), F32)])[None, :]
    return dict(
        g1=norm_ffn1[None, :], wgu1=ffn1_gu.astype(BF16), wd1=ffn1_down.astype(BF16),
        gm=norm_mix[None, :], win=win_p, convw=conv_w, alog=pad_row(a_log), dt=pad_row(dt_bias),
        gnorm=gdn_norm[None, :], sinks=sinks.astype(F32), wout=wout_p,
        g2=norm_ffn2[None, :], wgu2=ffn2_gu.astype(BF16), wd2=ffn2_down.astype(BF16),
        gp=norm_ple[None, :], wpp=ple_proj.astype(BF16), wpg=ple_gate.astype(BF16), gf=norm_final[None, :])


def _group(x, p, w, state, pos0):
    bsz, seq, _ = x.shape
    h2d, proj2d = _ffn_in(x.reshape(bsz * seq, D_MODEL), w["g1"], w["wgu1"], w["wd1"], w["gm"], w["win"])
    rc, rs1, rs2 = _rope_tables(pos0 + jnp.arange(seq))
    proj3d = proj2d.reshape(bsz, seq, P_W)
    if state is None:
        o, s_new, conv_new, k_new, v_new = _mixer_prompt(
            proj3d, w["convw"], w["alog"], w["dt"], w["gnorm"], w["sinks"], rc, rs1, rs2)
    else:
        s0, conv0, k0, v0 = state
        o, s_new, conv_new, k_new, v_new = _mixer(
            proj3d, s0, conv0, k0, v0, w["convw"], w["alog"], w["dt"], w["gnorm"],
            w["sinks"], rc, rs1, rs2, tb=seq, prev0_valid=True)
    y = _ffn_out(o.reshape(bsz * seq, D_MODEL), h2d, p.reshape(bsz * seq, PLE_DIM), w["wout"], w["g2"],
                 w["wgu2"], w["wd2"], w["gp"], w["wpg"], w["wpp"], w["gf"])
    kv_shape = (1, bsz, WINDOW, SWA_KV_HEADS, SWA_HD)
    return (y.reshape(bsz, seq, D_MODEL), s_new[None], conv_new[None], k_new.reshape(kv_shape),
            v_new.reshape(kv_shape))


def kernel(x_prompt, x_sample, state_gdn, state_conv, cache_swa_k, cache_swa_v, p_prompt, p_sample, norm_ffn1, ffn1_gu, ffn1_down, norm_mix, w_in, conv_w, a_log, dt_bias, gdn_norm, sinks, w_out, norm_ffn2, ffn2_gu, ffn2_down, norm_ple, ple_proj, ple_gate, norm_final):
    assert state_gdn.shape[0] == 1, "one layer"
    w = _layer_weights(norm_ffn1[0], ffn1_gu[0], ffn1_down[0], norm_mix[0], w_in[0], conv_w[0], a_log[0],
                       dt_bias[0], gdn_norm[0], sinks[0], w_out[0], norm_ffn2[0], ffn2_gu[0], ffn2_down[0],
                       norm_ple[0], ple_proj[0], ple_gate[0], norm_final)
    bs = x_sample.shape[0]
    yp, sg_p, sc_p, kk_p, vv_p = _group(x_prompt, p_prompt[0], w, None, 0)
    ys, sg_s, sc_s, kk_s, vv_s = _group(
        x_sample, p_sample[0], w,
        (state_gdn[0], state_conv[0], cache_swa_k[0].reshape(bs, WINDOW, SWA_KV_W),
         cache_swa_v[0].reshape(bs, WINDOW, SWA_KV_W)), PAST_LEN)
    return (yp, ys, sg_p, sc_p, kk_p, vv_p, sg_s, sc_s, kk_s, vv_s)
```

```python
import functools

import jax
import jax.numpy as jnp
import numpy as np
from jax.experimental import pallas as pl
from jax.experimental.pallas import tpu as pltpu

F32 = jnp.float32
BF16 = jnp.bfloat16

D_MODEL = 1024
D_FF = 2816
PLE_DIM = 256
NORM_EPS = 1e-6
GDN_HEADS = 4
GDN_DK = 128
GDN_DV = 128
GDN_CONV = 4
GDN_QK_W = GDN_HEADS * GDN_DK
GDN_V_W = GDN_HEADS * GDN_DV
GDN_CONV_W = 2 * GDN_QK_W + GDN_V_W
SWA_HEADS = 8
SWA_KV_HEADS = 2
SWA_HD = 64
SWA_Q_W = SWA_HEADS * SWA_HD
SWA_KV_W = SWA_KV_HEADS * SWA_HD
WINDOW = 128
ROT_DIM = SWA_HD // 4
ROPE_THETA = 500000.0
PAST_LEN = 16384

LANES = 128
SUBLANES = 8
VMEM_LIMIT_BYTES = 56 * 1024 * 1024

P_QKV = 0
P_Z = P_QKV + GDN_CONV_W
P_QS = P_Z + GDN_V_W
P_KS = P_QS + SWA_Q_W
P_VS = P_KS + SWA_KV_W
P_AB = P_VS + SWA_KV_W
P_W = P_AB + LANES

FF_CHUNK = 256
TOKEN_TILE = 512


def _rms(x, g):
    return x * jax.lax.rsqrt(jnp.mean(x * x, axis=-1, keepdims=True) + NORM_EPS) * g


def _sigmoid(x):
    return 1.0 / (1.0 + jnp.exp(-x))


def _silu(x):
    return x * _sigmoid(x)


def _mm(a, b):
    return jnp.dot(a.astype(BF16), b.astype(BF16), preferred_element_type=F32)


def _mm_nt(a, b):
    return jax.lax.dot_general(a.astype(BF16), b.astype(BF16), (((1,), (1,)), ((), ())),
                               preferred_element_type=F32)


def _swiglu_acc(n_bf16, wgu_ref, wd_ref):
    rows = n_bf16.shape[0]
    acc = jnp.zeros((rows, D_MODEL), F32)
    for c in range(D_FF // FF_CHUNK):
        lo = c * FF_CHUNK
        gate = jnp.dot(n_bf16, wgu_ref[:, lo:lo + FF_CHUNK], preferred_element_type=F32)
        up = jnp.dot(n_bf16, wgu_ref[:, D_FF + lo:D_FF + lo + FF_CHUNK], preferred_element_type=F32)
        act = (_silu(gate) * up).astype(BF16)
        acc = acc + jnp.dot(act, wd_ref[lo:lo + FF_CHUNK, :], preferred_element_type=F32)
    return acc


def _ffn_in_kernel(x_ref, g1_ref, wgu_ref, wd_ref, gm_ref, win_ref, h_ref, proj_ref):
    x = x_ref[...]
    n1 = _rms(x, g1_ref[...]).astype(BF16)
    h = x + 0.5 * _swiglu_acc(n1, wgu_ref, wd_ref)
    h_ref[...] = h
    n = _rms(h, gm_ref[...]).astype(BF16)
    proj_ref[...] = jnp.dot(n, win_ref[...], preferred_element_type=F32)


def _const_spec(shape):
    return pl.BlockSpec(shape, lambda i: (0,) * len(shape), pipeline_mode=pl.Buffered(1))


def _ffn_in(x2d, g1, wgu, wd, gm, win):
    n_tok = x2d.shape[0]
    tm = min(TOKEN_TILE, n_tok)
    assert n_tok % tm == 0
    return pl.pallas_call(
        _ffn_in_kernel,
        grid=(n_tok // tm,),
        in_specs=[
            pl.BlockSpec((tm, D_MODEL), lambda i: (i, 0)),
            _const_spec((1, D_MODEL)),
            _const_spec((D_MODEL, 2 * D_FF)),
            _const_spec((D_FF, D_MODEL)),
            _const_spec((1, D_MODEL)),
            _const_spec((D_MODEL, P_W)),
        ],
        out_specs=[
            pl.BlockSpec((tm, D_MODEL), lambda i: (i, 0)),
            pl.BlockSpec((tm, P_W), lambda i: (i, 0)),
        ],
        out_shape=[
            jax.ShapeDtypeStruct((n_tok, D_MODEL), F32),
            jax.ShapeDtypeStruct((n_tok, P_W), F32),
        ],
        compiler_params=pltpu.CompilerParams(
            dimension_semantics=("arbitrary",), vmem_limit_bytes=VMEM_LIMIT_BYTES),
        name="ffn_in",
    )(x2d, g1, wgu, wd, gm, win)


def _ffn_out_kernel(o_ref, h_ref, p_ref, wout_ref, g2_ref, wgu_ref, wd_ref, gp_ref, wpg_ref, wpp_ref,
                    gf_ref, y_ref):
    h = h_ref[...] + jnp.dot(o_ref[...], wout_ref[...], preferred_element_type=F32)
    n2 = _rms(h, g2_ref[...]).astype(BF16)
    h = h + 0.5 * _swiglu_acc(n2, wgu_ref, wd_ref)
    npl = _rms(h, gp_ref[...]).astype(BF16)
    gate = _sigmoid(jnp.dot(npl, wpg_ref[...], preferred_element_type=F32))
    pe = jnp.dot(p_ref[...].astype(BF16), wpp_ref[...], preferred_element_type=F32)
    h = h + gate * pe
    y_ref[...] = _rms(h, gf_ref[...])


def _ffn_out(o2d, h2d, p2d, wout, g2, wgu, wd, gp, wpg, wpp, gf):
    n_tok = h2d.shape[0]
    tm = min(TOKEN_TILE, n_tok)
    assert n_tok % tm == 0
    return pl.pallas_call(
        _ffn_out_kernel,
        grid=(n_tok // tm,),
        in_specs=[
            pl.BlockSpec((tm, D_MODEL), lambda i: (i, 0)),
            pl.BlockSpec((tm, D_MODEL), lambda i: (i, 0)),
            pl.BlockSpec((tm, PLE_DIM), lambda i: (i, 0)),
            _const_spec((D_MODEL, D_MODEL)),
            _const_spec((1, D_MODEL)),
            _const_spec((D_MODEL, 2 * D_FF)),
            _const_spec((D_FF, D_MODEL)),
            _const_spec((1, D_MODEL)),
            _const_spec((D_MODEL, D_MODEL)),
            _const_spec((PLE_DIM, D_MODEL)),
            _const_spec((1, D_MODEL)),
        ],
        out_specs=pl.BlockSpec((tm, D_MODEL), lambda i: (i, 0)),
        out_shape=jax.ShapeDtypeStruct((n_tok, D_MODEL), F32),
        compiler_params=pltpu.CompilerParams(
            dimension_semantics=("arbitrary",), vmem_limit_bytes=VMEM_LIMIT_BYTES),
        name="ffn_out",
    )(o2d, h2d, p2d, wout, g2, wgu, wd, gp, wpg, wpp, gf)


def _pad_rows(x):
    rows = x.shape[0]
    if rows == LANES:
        return x
    return jnp.concatenate([x, jnp.zeros((LANES - rows, x.shape[1]), x.dtype)], axis=0)


def _split3(x):
    hi = x.astype(BF16)
    r = x - hi.astype(F32)
    mid = r.astype(BF16)
    lo = (r - mid.astype(F32)).astype(BF16)
    return hi, mid, lo


def _unit_lower_inverse(a, rows):
    ri = jax.lax.broadcasted_iota(jnp.int32, (rows, LANES), 0)
    ci = jax.lax.broadcasted_iota(jnp.int32, (rows, LANES), 1)
    eye = (ri == ci).astype(F32)
    blk = ri ^ ci
    a0 = jnp.where(blk < SUBLANES, a, 0.0)
    x = eye - a0
    p = _mm(a0, _pad_rows(a0))
    x = x + _mm(x, _pad_rows(p))
    p = _mm(p, _pad_rows(p))
    x = x + _mm(x, _pad_rows(p))
    s = SUBLANES
    while s < rows:
        off = jnp.where((blk < 2 * s) & (blk >= s), a, 0.0)
        y = _mm(off, _pad_rows(x))
        x = x - _mm(x, _pad_rows(y))
        s *= 2
    return x


def _mixer_kernel(sinks_ref, proj_ref, s0_ref, conv0_ref, k0_ref, v0_ref, convw_ref, alog_ref, dt_ref,
                  gnorm_ref, rc_ref, rs1_ref, rs2_ref,
                  o_ref, sout_ref, convout_ref, kout_ref, vout_ref,
                  xext_ref, s_ref, kprev_ref, vprev_ref, *, tb, prev0_valid):
    t = pl.program_id(1)
    nt = pl.num_programs(1)

    @pl.when(t == 0)
    def _init():
        xext_ref[0:SUBLANES, :] = jnp.zeros((SUBLANES, GDN_CONV_W), F32)
        xext_ref[SUBLANES - (GDN_CONV - 1):SUBLANES, :] = conv0_ref[0]
        s_ref[...] = s0_ref[0]
        kprev_ref[...] = k0_ref[0]
        vprev_ref[...] = v0_ref[0]

    xext_ref[SUBLANES:SUBLANES + tb, :] = proj_ref[0, :, P_QKV:P_QKV + GDN_CONV_W]
    base = SUBLANES - (GDN_CONV - 1)
    conv = xext_ref[base:base + tb, :] * convw_ref[0:1, :]
    for j in range(1, GDN_CONV):
        conv = conv + xext_ref[base + j:base + j + tb, :] * convw_ref[j:j + 1, :]
    qkv = _silu(conv)

    ab = proj_ref[0, :, P_AB:P_AB + LANES]
    zab = ab + dt_ref[...]
    softplus = jnp.maximum(zab, 0.0) + jnp.log(1.0 + jnp.exp(-jnp.abs(zab)))
    g_all = -jnp.exp(alog_ref[...]) * softplus
    beta_all = _sigmoid(ab)

    ri = jax.lax.broadcasted_iota(jnp.int32, (tb, LANES), 0)
    ci = jax.lax.broadcasted_iota(jnp.int32, (tb, LANES), 1)
    tri = ci <= ri
    strict = ci < ri
    r128 = jax.lax.broadcasted_iota(jnp.int32, (LANES, LANES), 0)
    c128 = jax.lax.broadcasted_iota(jnp.int32, (LANES, LANES), 1)
    ltri = tri.astype(BF16)
    utri = (r128 <= c128).astype(BF16)

    g_pad = _pad_rows(g_all)
    g_hi, g_mid, g_lo = _split3(g_pad)
    gc_cols = (jnp.dot(ltri, g_hi, preferred_element_type=F32)
               + jnp.dot(ltri, g_mid, preferred_element_type=F32)
               + jnp.dot(ltri, g_lo, preferred_element_type=F32))
    gt = jnp.transpose(g_pad)[0:SUBLANES, :]
    t_hi, t_mid, t_lo = _split3(gt)
    gc_rows = (jnp.dot(t_hi, utri, preferred_element_type=F32)
               + jnp.dot(t_mid, utri, preferred_element_type=F32)
               + jnp.dot(t_lo, utri, preferred_element_type=F32))

    og = []
    for hh in range(GDN_HEADS):
        lo = hh * GDN_DK
        q = qkv[:, lo:lo + GDN_DK]
        k = qkv[:, GDN_QK_W + lo:GDN_QK_W + lo + GDN_DK]
        v = qkv[:, 2 * GDN_QK_W + lo:2 * GDN_QK_W + lo + GDN_DV]
        q = q * jax.lax.rsqrt(jnp.sum(q * q, axis=-1, keepdims=True) + 1e-6) * (GDN_DK ** -0.5)
        k = k * jax.lax.rsqrt(jnp.sum(k * k, axis=-1, keepdims=True) + 1e-6)
        beta = beta_all[:, GDN_HEADS + hh:GDN_HEADS + hh + 1]
        gcc = gc_cols[:, hh:hh + 1]
        gcr = gc_rows[hh:hh + 1, :]
        glast = gc_cols[tb - 1:tb, hh:hh + 1]
        decay = jnp.exp(jnp.where(tri, gcc - gcr, -jnp.inf))
        kb = k * beta
        vb = v * beta
        k_pad = _pad_rows(k)
        a_mat = jnp.where(strict, _mm_nt(kb, k_pad) * decay, 0.0)
        qk = _mm_nt(q, k_pad) * decay
        tinv = _unit_lower_inverse(a_mat, tb)
        egc = jnp.exp(gcc)
        u = _mm(tinv, _pad_rows(vb))
        w = _mm(tinv, _pad_rows(kb * egc))
        s_old = s_ref[hh]
        v_new = u - _mm(w, s_old)
        o = _mm(q * egc, s_old) + _mm(qk, _pad_rows(v_new))
        kdec_t = jnp.transpose(_pad_rows(k * jnp.exp(glast - gcc)))
        s_ref[hh] = s_old * jnp.exp(glast) + _mm(kdec_t, _pad_rows(v_new))
        o = _rms(o, gnorm_ref[...])
        z = proj_ref[0, :, P_Z + hh * GDN_DV:P_Z + (hh + 1) * GDN_DV]
        og.append(o * _silu(z))

    rc = rc_ref[...]
    rs1 = rs1_ref[...]
    rs2 = rs2_ref[...]

    def rope(x):
        return x * rc + pltpu.roll(x, LANES - ROT_DIM // 2, 1) * rs1 + pltpu.roll(x, ROT_DIM // 2, 1) * rs2

    k_own = rope(proj_ref[0, :, P_KS:P_KS + SWA_KV_W])
    v_own = proj_ref[0, :, P_VS:P_VS + SWA_KV_W]
    k_prev = kprev_ref[...]
    v_prev = vprev_ref[...]
    first_half = ci < SWA_HD
    q_rows = [None] * SWA_HEADS
    for p in range(SWA_HEADS // 2):
        qp = rope(proj_ref[0, :, P_QS + p * LANES:P_QS + (p + 1) * LANES])
        q_rows[p] = jnp.where(first_half, qp, 0.0)
        q_rows[SWA_HEADS // 2 + p] = jnp.where(first_half, 0.0, qp)
    q_all = jnp.concatenate(q_rows, axis=0)
    scale = SWA_HD ** -0.5
    s_prev = _mm_nt(q_all, k_prev) * scale
    s_own = _mm_nt(q_all, _pad_rows(k_own)) * scale
    if prev0_valid:
        prev_mask = ci > ri
    else:
        prev_mask = ci > ri + jnp.where(t > 0, 0, LANES)
    v_own_pad = _pad_rows(v_own)
    pv = []
    for hh in range(SWA_HEADS):
        sp = jnp.where(prev_mask, s_prev[hh * tb:(hh + 1) * tb], -jnp.inf)
        so = jnp.where(tri, s_own[hh * tb:(hh + 1) * tb], -jnp.inf)
        sk = sinks_ref[hh]
        m = jnp.maximum(jnp.maximum(jnp.max(sp, axis=-1, keepdims=True),
                                    jnp.max(so, axis=-1, keepdims=True)), sk)
        pp = jnp.exp(sp - m)
        po = jnp.exp(so - m)
        den = jnp.sum(pp, axis=-1, keepdims=True) + jnp.sum(po, axis=-1, keepdims=True) + jnp.exp(sk - m)
        pv.append((_mm(pp, v_prev) + _mm(po, v_own_pad)) / den)
    os_ = [jnp.where(first_half, pv[p], pv[SWA_HEADS // 2 + p]) for p in range(SWA_HEADS // 2)]

    o_ref[0] = jnp.concatenate(og + os_, axis=-1).astype(o_ref.dtype)

    if tb == WINDOW:
        k_new, v_new_cache = k_own, v_own
    else:
        k_new = jnp.concatenate([k_prev[tb:], k_own], axis=0)
        v_new_cache = jnp.concatenate([v_prev[tb:], v_own], axis=0)
    kprev_ref[...] = k_new
    vprev_ref[...] = v_new_cache
    xext_ref[0:SUBLANES, :] = xext_ref[tb:tb + SUBLANES, :]

    @pl.when(t == nt - 1)
    def _final():
        sout_ref[0] = s_ref[...]
        convout_ref[0] = xext_ref[tb + SUBLANES - (GDN_CONV - 1):tb + SUBLANES, :]
        kout_ref[0] = k_new
        vout_ref[0] = v_new_cache


def _mixer(proj3d, s0, conv0, k0, v0, convw, alog_row, dt_row, gnorm, sinks, rc, rs1, rs2, *, tb,
           prev0_valid):
    bsz, seq, _ = proj3d.shape
    assert seq % tb == 0
    nt = seq // tb
    kern = functools.partial(_mixer_kernel, tb=tb, prev0_valid=prev0_valid)
    batch_blk = lambda shape: pl.BlockSpec((1,) + shape, lambda b, t: (b,) + (0,) * len(shape))
    const_blk = lambda shape: pl.BlockSpec(shape, lambda b, t: (0,) * len(shape))
    time_blk = lambda shape: pl.BlockSpec(shape, lambda b, t: (t, 0))
    return pl.pallas_call(
        kern,
        grid=(bsz, nt),
        in_specs=[
            pl.BlockSpec(memory_space=pltpu.SMEM),
            pl.BlockSpec((1, tb, P_W), lambda b, t: (b, t, 0)),
            batch_blk((GDN_HEADS, GDN_DK, GDN_DV)),
            batch_blk((GDN_CONV - 1, GDN_CONV_W)),
            batch_blk((WINDOW, SWA_KV_W)),
            batch_blk((WINDOW, SWA_KV_W)),
            const_blk((GDN_CONV, GDN_CONV_W)),
            const_blk((1, LANES)),
            const_blk((1, LANES)),
            const_blk((1, GDN_DV)),
            time_blk((tb, LANES)),
            time_blk((tb, LANES)),
            time_blk((tb, LANES)),
        ],
        out_specs=[
            pl.BlockSpec((1, tb, D_MODEL), lambda b, t: (b, t, 0)),
            batch_blk((GDN_HEADS, GDN_DK, GDN_DV)),
            batch_blk((GDN_CONV - 1, GDN_CONV_W)),
            batch_blk((WINDOW, SWA_KV_W)),
            batch_blk((WINDOW, SWA_KV_W)),
        ],
        out_shape=[
            jax.ShapeDtypeStruct((bsz, seq, D_MODEL), BF16),
            jax.ShapeDtypeStruct((bsz, GDN_HEADS, GDN_DK, GDN_DV), F32),
            jax.ShapeDtypeStruct((bsz, GDN_CONV - 1, GDN_CONV_W), F32),
            jax.ShapeDtypeStruct((bsz, WINDOW, SWA_KV_W), F32),
            jax.ShapeDtypeStruct((bsz, WINDOW, SWA_KV_W), F32),
        ],
        scratch_shapes=[
            pltpu.VMEM((tb + SUBLANES, GDN_CONV_W), F32),
            pltpu.VMEM((GDN_HEADS, GDN_DK, GDN_DV), F32),
            pltpu.VMEM((WINDOW, SWA_KV_W), F32),
            pltpu.VMEM((WINDOW, SWA_KV_W), F32),
        ],
        compiler_params=pltpu.CompilerParams(
            dimension_semantics=("arbitrary", "arbitrary"), vmem_limit_bytes=VMEM_LIMIT_BYTES),
        name="mixer",
    )(sinks, proj3d, s0, conv0, k0, v0, convw, alog_row, dt_row, gnorm, rc, rs1, rs2)


CHUNK = 64
CAT_W = GDN_HEADS * CHUNK


def _bd_cat(y, lane_lo_chunk):
    zero_tile = jnp.zeros((CHUNK, LANES), F32)
    blocks = []
    for hh in range(GDN_HEADS):
        y_tile = y[:, (hh // 2) * LANES:(hh // 2 + 1) * LANES]
        keep = jnp.where(lane_lo_chunk, y_tile, 0.0) if hh % 2 == 0 else jnp.where(lane_lo_chunk, 0.0, y_tile)
        blocks.append(jnp.concatenate([keep, zero_tile] if hh < 2 else [zero_tile, keep], axis=1))
    return jnp.concatenate(blocks, axis=0).astype(BF16)


def _bd_wide(y):
    zero_tile = jnp.zeros((CHUNK, LANES), F32)
    blocks = []
    for hh in range(GDN_HEADS):
        tiles = [zero_tile] * GDN_HEADS
        tiles[hh] = y[:, hh * LANES:(hh + 1) * LANES]
        blocks.append(jnp.concatenate(tiles, axis=1))
    return jnp.concatenate(blocks, axis=0).astype(BF16)


def _bd_pair(sa, sb):
    zeros = jnp.zeros(sa.shape, sa.dtype)
    return jnp.concatenate([jnp.concatenate([sa, zeros], axis=1), jnp.concatenate([zeros, sb], axis=1)], axis=0)


def _gates(ab, alog_row, dt_row):
    zab = ab + dt_row
    softplus = jnp.maximum(zab, 0.0) + jnp.log(1.0 + jnp.exp(-jnp.abs(zab)))
    return -jnp.exp(alog_row) * softplus, _sigmoid(ab)


def _head_l2(x):
    parts = []
    for hh in range(GDN_HEADS):
        xh = x[:, hh * GDN_DK:(hh + 1) * GDN_DK]
        parts.append(xh * jax.lax.rsqrt(jnp.sum(xh * xh, axis=-1, keepdims=True) + 1e-6))
    return jnp.concatenate(parts, axis=1)


def _lane_expand(cols, first_lane, rows):
    return jnp.concatenate(
        [jnp.broadcast_to(cols[:, first_lane + hh:first_lane + hh + 1], (rows, LANES))
         for hh in range(GDN_HEADS)], axis=1)


def _rope(x, rc, rs1, rs2):
    return x * rc + pltpu.roll(x, LANES - ROT_DIM // 2, 1) * rs1 + pltpu.roll(x, ROT_DIM // 2, 1) * rs2


def _mixer_prompt_kernel(sinks_ref, proj_ref, convw_ref, alog_ref, dt_ref, gnorm_ref, rc_ref, rs1_ref,
                         rs2_ref, o_ref, sout_ref, convout_ref, kout_ref, vout_ref,
                         xext_ref, s_ref, kprev_ref, vprev_ref, *, tb):
    t = pl.program_id(1)
    nt = pl.num_programs(1)
    nch = tb // CHUNK

    @pl.when(t == 0)
    def _init():
        xext_ref[0:SUBLANES, :] = jnp.zeros((SUBLANES, GDN_CONV_W), F32)
        s_ref[...] = jnp.zeros(s_ref.shape, F32)
        kprev_ref[...] = jnp.zeros(kprev_ref.shape, F32)
        vprev_ref[...] = jnp.zeros(vprev_ref.shape, F32)

    s_cat = s_ref[...]
    k_prev = kprev_ref[...]
    v_prev = vprev_ref[...]

    xext_ref[SUBLANES:SUBLANES + tb, :] = proj_ref[0, :, P_QKV:P_QKV + GDN_CONV_W]
    base = SUBLANES - (GDN_CONV - 1)
    conv = xext_ref[base:base + tb, :] * convw_ref[0:1, :]
    for j in range(1, GDN_CONV):
        conv = conv + xext_ref[base + j:base + j + tb, :] * convw_ref[j:j + 1, :]
    qkv = _silu(conv)
    conv_tail = xext_ref[tb:tb + SUBLANES, :]

    r_t = jax.lax.broadcasted_iota(jnp.int32, (tb, LANES), 0)
    c_t = jax.lax.broadcasted_iota(jnp.int32, (tb, LANES), 1)
    lane_lo = c_t < CHUNK

    rc = rc_ref[...]
    rs1 = rs1_ref[...]
    rs2 = rs2_ref[...]

    def rope(x):
        return x * rc + pltpu.roll(x, LANES - ROT_DIM // 2, 1) * rs1 + pltpu.roll(x, ROT_DIM // 2, 1) * rs2

    k_own = rope(proj_ref[0, :, P_KS:P_KS + SWA_KV_W])
    v_own = proj_ref[0, :, P_VS:P_VS + SWA_KV_W]
    k_cat = jnp.concatenate([k_prev, k_own], axis=0)
    v_cat = jnp.concatenate([v_prev, v_own], axis=0).astype(BF16)
    q_rows = [None] * SWA_HEADS
    for p in range(SWA_HEADS // 2):
        qp = rope(proj_ref[0, :, P_QS + p * LANES:P_QS + (p + 1) * LANES])
        q_rows[p] = jnp.where(lane_lo, qp, 0.0)
        q_rows[SWA_HEADS // 2 + p] = jnp.where(lane_lo, 0.0, qp)
    s_all = _mm_nt(jnp.concatenate(q_rows, axis=0), k_cat) * (SWA_HD ** -0.5)
    rk = jax.lax.broadcasted_iota(jnp.int32, (tb, 2 * WINDOW), 0)
    ck = jax.lax.broadcasted_iota(jnp.int32, (tb, 2 * WINDOW), 1)
    first_off = jnp.where(t > 0, 0, WINDOW)
    dist = ck - rk
    kmask = (dist > jnp.where(ck < WINDOW, first_off, -WINDOW)) & (dist <= jnp.where(ck < WINDOW, 2 * WINDOW, WINDOW))
    pv = [None] * SWA_HEADS

    def swa_head(hh):
        sh = jnp.where(kmask, s_all[hh * tb:(hh + 1) * tb], -jnp.inf)
        sk = sinks_ref[hh]
        m = jnp.maximum(jnp.max(sh, axis=-1, keepdims=True), sk)
        ph = jnp.exp(sh - m)
        den = jnp.sum(ph, axis=-1, keepdims=True) + jnp.exp(sk - m)
        pv[hh] = jnp.dot(ph.astype(BF16), v_cat, preferred_element_type=F32) / den

    pending_heads = list(range(SWA_HEADS))

    def fill():
        if pending_heads:
            swa_head(pending_heads.pop(0))

    def head_l2(x):
        parts = []
        for hh in range(GDN_HEADS):
            xh = x[:, hh * GDN_DK:(hh + 1) * GDN_DK]
            parts.append(xh * jax.lax.rsqrt(jnp.sum(xh * xh, axis=-1, keepdims=True) + 1e-6))
        return jnp.concatenate(parts, axis=1)

    q_all = head_l2(qkv[:, 0:GDN_QK_W]) * (GDN_DK ** -0.5)
    k_all = head_l2(qkv[:, GDN_QK_W:2 * GDN_QK_W])
    v_all = qkv[:, 2 * GDN_QK_W:]

    ab = proj_ref[0, :, P_AB:P_AB + LANES]
    zab = ab + dt_ref[...]
    softplus = jnp.maximum(zab, 0.0) + jnp.log(1.0 + jnp.exp(-jnp.abs(zab)))
    g_all = -jnp.exp(alog_ref[...]) * softplus
    beta_all = _sigmoid(ab)

    same_chunk = (r_t ^ c_t) < CHUNK
    lblk = ((c_t <= r_t) & same_chunk).astype(BF16)
    ublk = ((r_t <= c_t) & same_chunk).astype(BF16)
    g_hi, g_mid, g_lo = _split3(g_all)
    gc_cols = (jnp.dot(lblk, g_hi, preferred_element_type=F32)
               + jnp.dot(lblk, g_mid, preferred_element_type=F32)
               + jnp.dot(lblk, g_lo, preferred_element_type=F32))
    gt = jnp.transpose(g_all)[0:SUBLANES, :]
    t_hi, t_mid, t_lo = _split3(gt)
    gc_rows = (jnp.dot(t_hi, ublk, preferred_element_type=F32)
               + jnp.dot(t_mid, ublk, preferred_element_type=F32)
               + jnp.dot(t_lo, ublk, preferred_element_type=F32))
    gc_rows_sw = pltpu.roll(gc_rows, CHUNK, 1)

    lane_lo_chunk = jax.lax.broadcasted_iota(jnp.int32, (CHUNK, LANES), 1) < CHUNK
    lane_lo_row = jax.lax.broadcasted_iota(jnp.int32, (1, LANES), 1) < CHUNK
    gc_full = jnp.concatenate(
        [jnp.broadcast_to(gc_cols[:, hh:hh + 1], (tb, LANES)) for hh in range(GDN_HEADS)], axis=1)
    beta_full = jnp.concatenate(
        [jnp.broadcast_to(beta_all[:, GDN_HEADS + hh:GDN_HEADS + hh + 1], (tb, LANES))
         for hh in range(GDN_HEADS)], axis=1)
    egc_full = jnp.exp(gc_full)
    kb_all = k_all * beta_full
    vb_all = v_all * beta_full
    kbg_all = kb_all * egc_full
    qd_all = q_all * egc_full

    ri = jax.lax.broadcasted_iota(jnp.int32, (CHUNK, CAT_W), 0)
    ci = jax.lax.broadcasted_iota(jnp.int32, (CHUNK, CAT_W), 1)
    cj = ci & (CHUNK - 1)
    tri = cj <= ri
    strict = cj < ri
    eye = (cj == ri).astype(F32)
    blk = ri ^ cj
    bd = functools.partial(_bd_cat, lane_lo_chunk=lane_lo_chunk)
    bd_w = _bd_wide
    mmbd = _mm

    chunks = range(nch)
    row_sl = [slice(c * CHUNK, (c + 1) * CHUNK) for c in chunks]
    a_mat, qk, xinv, pw = [None] * nch, [None] * nch, [None] * nch, [None] * nch
    for c in chunks:
        rows = row_sl[c]
        gcc_pairs, gcr_pairs = [], []
        for p in range(GDN_HEADS // 2):
            gcc_pairs.append(jnp.where(lane_lo_chunk, gc_full[rows, (2 * p) * LANES:(2 * p + 1) * LANES],
                                       gc_full[rows, (2 * p + 1) * LANES:(2 * p + 2) * LANES]))
            ra = (gc_rows if c == 0 else gc_rows_sw)[2 * p:2 * p + 1, :]
            rb_ = (gc_rows_sw if c == 0 else gc_rows)[2 * p + 1:2 * p + 2, :]
            gcr_pairs.append(jnp.where(lane_lo_row, ra, rb_))
        gcc = jnp.concatenate(gcc_pairs, axis=1)
        gcr = jnp.concatenate(gcr_pairs, axis=1)
        decay = jnp.exp(jnp.where(tri, gcc - gcr, -jnp.inf))
        kq = jax.lax.dot_general(
            jnp.concatenate([kb_all[rows], q_all[rows]], axis=0).astype(BF16), bd_w(k_all[rows]),
            (((1,), (1,)), ((), ())), preferred_element_type=F32)
        a_mat[c] = jnp.where(strict, kq[0:CHUNK] * decay, 0.0)
        qk[c] = kq[CHUNK:] * decay
    fill()
    a0 = [jnp.where(blk < SUBLANES, a_mat[c], 0.0) for c in chunks]
    for c in chunks:
        xinv[c] = eye - a0[c]
        pw[c] = mmbd(a0[c], bd(a0[c]))
    fill()
    for c in chunks:
        pw_bd = bd(pw[c])
        xinv[c] = xinv[c] + mmbd(xinv[c], pw_bd)
        pw[c] = mmbd(pw[c], pw_bd)
    fill()
    for c in chunks:
        xinv[c] = xinv[c] + mmbd(xinv[c], bd(pw[c]))
    fill()
    s = SUBLANES
    while s < CHUNK:
        y = [mmbd(jnp.where((blk < 2 * s) & (blk >= s), a_mat[c], 0.0), bd(xinv[c])) for c in chunks]
        fill()
        for c in chunks:
            xinv[c] = xinv[c] - mmbd(xinv[c], bd(y[c]))
        fill()
        s *= 2
    prep = []
    for c in chunks:
        rows = row_sl[c]
        rhs = jnp.concatenate(
            [jnp.concatenate([vb_all[rows, hh * GDN_DV:(hh + 1) * GDN_DV],
                              kbg_all[rows, hh * GDN_DK:(hh + 1) * GDN_DK]], axis=1)
             for hh in range(GDN_HEADS)], axis=0)
        uw = _mm(bd(xinv[c]), rhs)
        u_all = jnp.concatenate([uw[hh * CHUNK:(hh + 1) * CHUNK, 0:GDN_DV] for hh in range(GDN_HEADS)], axis=1)
        w_all = jnp.concatenate([uw[hh * CHUNK:(hh + 1) * CHUNK, GDN_DV:] for hh in range(GDN_HEADS)], axis=1)
        glast = gc_full[c * CHUNK + CHUNK - 1:c * CHUNK + CHUNK, :]
        kdec = k_all[rows] * jnp.exp(glast - gc_full[rows])
        kdec_t = jnp.transpose(jnp.concatenate(
            [kdec[:, hh * GDN_DK:(hh + 1) * GDN_DK] for hh in range(GDN_HEADS)], axis=0))
        prep.append((u_all, w_all, qk[c], kdec_t, jnp.exp(glast)))
    fill()

    zeros_s = jnp.zeros((GDN_DK, GDN_DV), F32)
    o_chunks = []
    for c in range(nch):
        rows = slice(c * CHUNK, (c + 1) * CHUNK)
        u_all, w_all, qk, kdec_t, eg_last = prep[c]
        ws, qs = [], []
        for p in range(GDN_HEADS // 2):
            cols = slice(2 * p * LANES, (2 * p + 2) * LANES)
            sa = s_cat[:, (2 * p) * LANES:(2 * p + 1) * LANES]
            sb = s_cat[:, (2 * p + 1) * LANES:(2 * p + 2) * LANES]
            s_bd = jnp.concatenate([jnp.concatenate([sa, zeros_s], axis=1),
                                    jnp.concatenate([zeros_s, sb], axis=1)], axis=0)
            r = _mm(jnp.concatenate([w_all[:, cols], qd_all[rows, cols]], axis=0), s_bd)
            ws.append(r[0:CHUNK])
            qs.append(r[CHUNK:])
        v_new = u_all - jnp.concatenate(ws, axis=1)
        r2 = mmbd(jnp.concatenate([qk, kdec_t], axis=0), bd_w(v_new))
        o_chunks.append(jnp.concatenate(qs, axis=1) + r2[0:CHUNK])
        s_cat = s_cat * eg_last + r2[CHUNK:]
    while pending_heads:
        fill()
    os_ = [jnp.where(lane_lo, pv[p], pv[SWA_HEADS // 2 + p]) for p in range(SWA_HEADS // 2)]
    o_gdn = jnp.concatenate(o_chunks, axis=0)
    og = []
    for hh in range(GDN_HEADS):
        oh = _rms(o_gdn[:, hh * GDN_DV:(hh + 1) * GDN_DV], gnorm_ref[...])
        og.append(oh * _silu(proj_ref[0, :, P_Z + hh * GDN_DV:P_Z + (hh + 1) * GDN_DV]))

    o_ref[0] = jnp.concatenate(og + os_, axis=-1).astype(o_ref.dtype)

    s_ref[...] = s_cat
    kprev_ref[...] = k_own
    vprev_ref[...] = v_own
    xext_ref[0:SUBLANES, :] = conv_tail

    @pl.when(t == nt - 1)
    def _final():
        for hh in range(GDN_HEADS):
            sout_ref[0, hh] = s_ref[:, hh * GDN_DV:(hh + 1) * GDN_DV]
        convout_ref[0] = xext_ref[tb + SUBLANES - (GDN_CONV - 1):tb + SUBLANES, :]
        kout_ref[0] = k_own
        vout_ref[0] = v_own


def _mixer_prompt(proj3d, convw, alog_row, dt_row, gnorm, sinks, rc, rs1, rs2):
    bsz, seq, _ = proj3d.shape
    tb = WINDOW
    assert seq % tb == 0
    kern = functools.partial(_mixer_prompt_kernel, tb=tb)
    batch_blk = lambda shape: pl.BlockSpec((1,) + shape, lambda b, t: (b,) + (0,) * len(shape))
    const_blk = lambda shape: pl.BlockSpec(shape, lambda b, t: (0,) * len(shape))
    time_blk = lambda shape: pl.BlockSpec(shape, lambda b, t: (t, 0))
    return pl.pallas_call(
        kern,
        grid=(bsz, seq // tb),
        in_specs=[
            pl.BlockSpec(memory_space=pltpu.SMEM),
            pl.BlockSpec((1, tb, P_W), lambda b, t: (b, t, 0)),
            const_blk((GDN_CONV, GDN_CONV_W)),
            const_blk((1, LANES)),
            const_blk((1, LANES)),
            const_blk((1, GDN_DV)),
            time_blk((tb, LANES)),
            time_blk((tb, LANES)),
            time_blk((tb, LANES)),
        ],
        out_specs=[
            pl.BlockSpec((1, tb, D_MODEL), lambda b, t: (b, t, 0)),
            batch_blk((GDN_HEADS, GDN_DK, GDN_DV)),
            batch_blk((GDN_CONV - 1, GDN_CONV_W)),
            batch_blk((WINDOW, SWA_KV_W)),
            batch_blk((WINDOW, SWA_KV_W)),
        ],
        out_shape=[
            jax.ShapeDtypeStruct((bsz, seq, D_MODEL), BF16),
            jax.ShapeDtypeStruct((bsz, GDN_HEADS, GDN_DK, GDN_DV), F32),
            jax.ShapeDtypeStruct((bsz, GDN_CONV - 1, GDN_CONV_W), F32),
            jax.ShapeDtypeStruct((bsz, WINDOW, SWA_KV_W), F32),
            jax.ShapeDtypeStruct((bsz, WINDOW, SWA_KV_W), F32),
        ],
        scratch_shapes=[
            pltpu.VMEM((tb + SUBLANES, GDN_CONV_W), F32),
            pltpu.VMEM((GDN_DK, GDN_HEADS * GDN_DV), F32),
            pltpu.VMEM((WINDOW, SWA_KV_W), F32),
            pltpu.VMEM((WINDOW, SWA_KV_W), F32),
        ],
        compiler_params=pltpu.CompilerParams(
            dimension_semantics=("arbitrary", "arbitrary"), vmem_limit_bytes=VMEM_LIMIT_BYTES),
        name="mixer_prompt",
    )(sinks, proj3d, convw, alog_row, dt_row, gnorm, rc, rs1, rs2)


SAMPLE_T = SUBLANES
SAMPLE_NB = CHUNK // SAMPLE_T


def _mixer_sample_kernel(sinks_ref, proj_ref, s0_ref, conv0_ref, k0_ref, v0_ref, convw_ref, alog_ref, dt_ref,
                         gnorm_ref, rc_ref, rs1_ref, rs2_ref,
                         o_ref, sout_ref, convout_ref, kout_ref, vout_ref, xext_ref):
    nb, ts, rows = SAMPLE_NB, SAMPLE_T, CHUNK
    seqs = range(nb)
    grp = lambda x, b: x[b * ts:(b + 1) * ts]

    hist = SUBLANES - (GDN_CONV - 1)
    conv_parts = []
    for b in seqs:
        xext_ref[2 * ts * b + hist:2 * ts * b + ts, :] = conv0_ref[b]
        xext_ref[2 * ts * b + ts:2 * ts * (b + 1), :] = proj_ref[b, :, P_QKV:P_QKV + GDN_CONV_W]
    for b in seqs:
        base = 2 * ts * b + hist
        conv = xext_ref[base:base + ts, :] * convw_ref[0:1, :]
        for j in range(1, GDN_CONV):
            conv = conv + xext_ref[base + j:base + j + ts, :] * convw_ref[j:j + 1, :]
        conv_parts.append(conv)
        convout_ref[b] = xext_ref[2 * ts * (b + 1) - (GDN_CONV - 1):2 * ts * (b + 1), :]
    qkv = _silu(jnp.concatenate(conv_parts, axis=0))
    q_all = _head_l2(qkv[:, 0:GDN_QK_W]) * (GDN_DK ** -0.5)
    k_all = _head_l2(qkv[:, GDN_QK_W:2 * GDN_QK_W])
    v_all = qkv[:, 2 * GDN_QK_W:]

    def proj_cols(lo, width):
        return jnp.concatenate([proj_ref[b, :, lo:lo + width] for b in seqs], axis=0)

    g_all, beta_all = _gates(proj_cols(P_AB, LANES), alog_ref[...], dt_ref[...])

    r_t = jax.lax.broadcasted_iota(jnp.int32, (rows, LANES), 0)
    c_t = jax.lax.broadcasted_iota(jnp.int32, (rows, LANES), 1)
    r_p = jax.lax.broadcasted_iota(jnp.int32, (LANES, LANES), 0)
    c_p = jax.lax.broadcasted_iota(jnp.int32, (LANES, LANES), 1)
    lblk = ((c_t <= r_t) & ((r_t ^ c_t) < ts)).astype(BF16)
    ublk = ((r_p <= c_p) & ((r_p ^ c_p) < ts)).astype(BF16)
    g_pad = _pad_rows(g_all)
    g_hi, g_mid, g_lo = _split3(g_pad)
    gc_cols = (jnp.dot(lblk, g_hi, preferred_element_type=F32)
               + jnp.dot(lblk, g_mid, preferred_element_type=F32)
               + jnp.dot(lblk, g_lo, preferred_element_type=F32))
    gt = jnp.transpose(g_pad)[0:SUBLANES, :]
    t_hi, t_mid, t_lo = _split3(gt)
    gc_rows = (jnp.dot(t_hi, ublk, preferred_element_type=F32)
               + jnp.dot(t_mid, ublk, preferred_element_type=F32)
               + jnp.dot(t_lo, ublk, preferred_element_type=F32))
    gc_rows_sw = pltpu.roll(gc_rows, CHUNK, 1)

    lane_lo = c_t < CHUNK
    lane_lo_row = jax.lax.broadcasted_iota(jnp.int32, (1, LANES), 1) < CHUNK
    lane_lo_t = jax.lax.broadcasted_iota(jnp.int32, (ts, LANES), 1) < CHUNK
    gc_full = _lane_expand(gc_cols, 0, rows)
    beta_full = _lane_expand(beta_all, GDN_HEADS, rows)
    egc_full = jnp.exp(gc_full)
    kb_all = k_all * beta_full
    vb_all = v_all * beta_full
    kbg_all = kb_all * egc_full
    qd_all = q_all * egc_full

    ri = jax.lax.broadcasted_iota(jnp.int32, (rows, CAT_W), 0)
    ci = jax.lax.broadcasted_iota(jnp.int32, (rows, CAT_W), 1)
    cj = ci & (CHUNK - 1)
    same_seq = (ri ^ cj) < ts
    eye = (cj == ri).astype(F32)
    bd = functools.partial(_bd_cat, lane_lo_chunk=lane_lo)

    gcc = jnp.concatenate(
        [jnp.where(lane_lo, gc_full[:, (2 * p) * LANES:(2 * p + 1) * LANES],
                   gc_full[:, (2 * p + 1) * LANES:(2 * p + 2) * LANES]) for p in range(GDN_HEADS // 2)], axis=1)
    gcr = jnp.concatenate(
        [jnp.where(lane_lo_row, gc_rows[2 * p:2 * p + 1, :], gc_rows_sw[2 * p + 1:2 * p + 2, :])
         for p in range(GDN_HEADS // 2)], axis=1)
    decay = jnp.exp(jnp.where((cj <= ri) & same_seq, gcc - gcr, -jnp.inf))
    kq = jax.lax.dot_general(
        jnp.concatenate([kb_all, q_all], axis=0).astype(BF16), _bd_wide(k_all),
        (((1,), (1,)), ((), ())), preferred_element_type=F32)
    a_mat = jnp.where(cj < ri, kq[0:rows] * decay, 0.0)
    qk = kq[rows:] * decay
    xinv = eye - a_mat
    pw = _mm(a_mat, bd(a_mat))
    pw_bd = bd(pw)
    xinv = xinv + _mm(xinv, pw_bd)
    pw = _mm(pw, pw_bd)
    xinv = xinv + _mm(xinv, bd(pw))
    rhs = jnp.concatenate(
        [jnp.concatenate([vb_all[:, hh * GDN_DV:(hh + 1) * GDN_DV],
                          kbg_all[:, hh * GDN_DK:(hh + 1) * GDN_DK]], axis=1)
         for hh in range(GDN_HEADS)], axis=0)
    uw = _mm(bd(xinv), rhs)
    u_all = jnp.concatenate([uw[hh * CHUNK:(hh + 1) * CHUNK, 0:GDN_DV] for hh in range(GDN_HEADS)], axis=1)
    w_all = jnp.concatenate([uw[hh * CHUNK:(hh + 1) * CHUNK, GDN_DV:] for hh in range(GDN_HEADS)], axis=1)
    glast = [gc_full[(b + 1) * ts - 1:(b + 1) * ts, :] for b in seqs]
    glast_rows = jnp.concatenate([jnp.broadcast_to(glast[b], (ts, GDN_QK_W)) for b in seqs], axis=0)
    kdec = k_all * jnp.exp(glast_rows - gc_full)
    kdec_t = jnp.transpose(jnp.concatenate(
        [kdec[:, hh * GDN_DK:(hh + 1) * GDN_DK] for hh in range(GDN_HEADS)], axis=0))

    s_old, ws, qs = [], [], []
    for b in seqs:
        s_b = jnp.concatenate([s0_ref[b, hh] for hh in range(GDN_HEADS)], axis=1)
        s_old.append(s_b)
        ws_b, qs_b = [], []
        for p in range(GDN_HEADS // 2):
            cols = slice(2 * p * LANES, (2 * p + 2) * LANES)
            r = _mm(jnp.concatenate([grp(w_all, b)[:, cols], grp(qd_all, b)[:, cols]], axis=0),
                    _bd_pair(s_b[:, (2 * p) * LANES:(2 * p + 1) * LANES],
                             s_b[:, (2 * p + 1) * LANES:(2 * p + 2) * LANES]))
            ws_b.append(r[0:ts])
            qs_b.append(r[ts:])
        ws.append(jnp.concatenate(ws_b, axis=1))
        qs.append(jnp.concatenate(qs_b, axis=1))
    v_new = u_all - jnp.concatenate(ws, axis=0)
    v_bd = _bd_wide(v_new)
    o_gdn = jnp.concatenate(qs, axis=0) + _mm(qk, v_bd)
    cb = jax.lax.broadcasted_iota(jnp.int32, (GDN_DK, CAT_W), 1) & (CHUNK - 1)
    upd = _mm(jnp.concatenate([jnp.where((cb ^ (b * ts)) < ts, kdec_t, 0.0) for b in seqs], axis=0), v_bd)
    for b in seqs:
        s_new = s_old[b] * jnp.exp(glast[b]) + upd[b * GDN_DK:(b + 1) * GDN_DK]
        for hh in range(GDN_HEADS):
            sout_ref[b, hh] = s_new[:, hh * GDN_DV:(hh + 1) * GDN_DV]
    og = []
    for hh in range(GDN_HEADS):
        oh = _rms(o_gdn[:, hh * GDN_DV:(hh + 1) * GDN_DV], gnorm_ref[...])
        og.append(oh * _silu(proj_cols(P_Z + hh * GDN_DV, GDN_DV)))

    tile8 = lambda x: jnp.concatenate([x] * nb, axis=0)
    rc, rs1, rs2 = tile8(rc_ref[...]), tile8(rs1_ref[...]), tile8(rs2_ref[...])
    k_own = _rope(proj_cols(P_KS, SWA_KV_W), rc, rs1, rs2)
    v_own = proj_cols(P_VS, SWA_KV_W)
    q_rows = [None] * SWA_HEADS
    for p in range(SWA_HEADS // 2):
        qp = _rope(proj_cols(P_QS + p * LANES, LANES), rc, rs1, rs2)
        q_rows[p] = jnp.where(lane_lo, qp, 0.0)
        q_rows[SWA_HEADS // 2 + p] = jnp.where(lane_lo, 0.0, qp)
    scale = SWA_HD ** -0.5
    s_own_all = _mm_nt(jnp.concatenate(q_rows, axis=0), _pad_rows(k_own)) * scale
    tok = r_t & (ts - 1)
    prev_mask = c_t > tok
    sink_col = jnp.zeros((rows, 1), F32)
    for hh in range(SWA_HEADS):
        sink_col = jnp.where((r_t[:, 0:1] ^ (hh * ts)) < ts, sinks_ref[hh], sink_col)
    pp_all, po_all, den_all = [], [], []
    for b in seqs:
        q_b = jnp.concatenate([grp(q_rows[hh], b) for hh in range(SWA_HEADS)], axis=0)
        sp = jnp.where(prev_mask, _mm_nt(q_b, k0_ref[b]) * scale, -jnp.inf)
        so_raw = jnp.concatenate([s_own_all[hh * rows + b * ts:hh * rows + (b + 1) * ts]
                                  for hh in range(SWA_HEADS)], axis=0)
        so = jnp.where(((c_t ^ (b * ts)) < ts) & ((c_t & (ts - 1)) <= tok), so_raw, -jnp.inf)
        m = jnp.maximum(jnp.maximum(jnp.max(sp, axis=-1, keepdims=True),
                                    jnp.max(so, axis=-1, keepdims=True)), sink_col)
        pp = jnp.exp(sp - m)
        po = jnp.exp(so - m)
        den_all.append(jnp.sum(pp, axis=-1, keepdims=True) + jnp.sum(po, axis=-1, keepdims=True)
                       + jnp.exp(sink_col - m))
        pp_all.append(_mm(pp, v0_ref[b]))
        po_all.append(po)
    pv_own = _mm(jnp.concatenate(po_all, axis=0), _pad_rows(v_own))
    os_rows = []
    for b in seqs:
        pv_b = (pp_all[b] + pv_own[b * rows:(b + 1) * rows]) / den_all[b]
        os_rows.append(jnp.concatenate(
            [jnp.where(lane_lo_t, pv_b[p * ts:(p + 1) * ts],
                       pv_b[(SWA_HEADS // 2 + p) * ts:(SWA_HEADS // 2 + p + 1) * ts])
             for p in range(SWA_HEADS // 2)], axis=1))
        kout_ref[b] = jnp.concatenate([k0_ref[b, ts:, :], grp(k_own, b)], axis=0)
        vout_ref[b] = jnp.concatenate([v0_ref[b, ts:, :], grp(v_own, b)], axis=0)
    o_full = jnp.concatenate(og + [jnp.concatenate(os_rows, axis=0)], axis=-1)
    for b in seqs:
        o_ref[b] = grp(o_full, b).astype(o_ref.dtype)


def _mixer_sample(proj3d, s0, conv0, k0, v0, convw, alog_row, dt_row, gnorm, sinks, rc, rs1, rs2):
    bsz, seq, _ = proj3d.shape
    nb = SAMPLE_NB
    assert seq == SAMPLE_T and bsz % nb == 0
    seq_blk = lambda shape: pl.BlockSpec((nb,) + shape, lambda i: (i,) + (0,) * len(shape))
    const_blk = lambda shape: pl.BlockSpec(shape, lambda i: (0,) * len(shape))
    return pl.pallas_call(
        _mixer_sample_kernel,
        grid=(bsz // nb,),
        in_specs=[
            pl.BlockSpec(memory_space=pltpu.SMEM),
            seq_blk((seq, P_W)),
            seq_blk((GDN_HEADS, GDN_DK, GDN_DV)),
            seq_blk((GDN_CONV - 1, GDN_CONV_W)),
            seq_blk((WINDOW, SWA_KV_W)),
            seq_blk((WINDOW, SWA_KV_W)),
            const_blk((GDN_CONV, GDN_CONV_W)),
            const_blk((1, LANES)),
            const_blk((1, LANES)),
            const_blk((1, GDN_DV)),
            const_blk((seq, LANES)),
            const_blk((seq, LANES)),
            const_blk((seq, LANES)),
        ],
        out_specs=[
            seq_blk((seq, D_MODEL)),
            seq_blk((GDN_HEADS, GDN_DK, GDN_DV)),
            seq_blk((GDN_CONV - 1, GDN_CONV_W)),
            seq_blk((WINDOW, SWA_KV_W)),
            seq_blk((WINDOW, SWA_KV_W)),
        ],
        out_shape=[
            jax.ShapeDtypeStruct((bsz, seq, D_MODEL), BF16),
            jax.ShapeDtypeStruct((bsz, GDN_HEADS, GDN_DK, GDN_DV), F32),
            jax.ShapeDtypeStruct((bsz, GDN_CONV - 1, GDN_CONV_W), F32),
            jax.ShapeDtypeStruct((bsz, WINDOW, SWA_KV_W), F32),
            jax.ShapeDtypeStruct((bsz, WINDOW, SWA_KV_W), F32),
        ],
        scratch_shapes=[pltpu.VMEM((2 * SAMPLE_T * nb, GDN_CONV_W), F32)],
        compiler_params=pltpu.CompilerParams(
            dimension_semantics=("arbitrary",), vmem_limit_bytes=VMEM_LIMIT_BYTES),
        name="mixer_sample",
    )(sinks, proj3d, s0, conv0, k0, v0, convw, alog_row, dt_row, gnorm, rc, rs1, rs2)


def _rope_tables(pos):
    half = ROT_DIM // 2
    inv = ROPE_THETA ** (-jnp.arange(half, dtype=F32) * 2.0 / ROT_DIM)
    ang = pos.astype(F32)[:, None] * inv[None, :]
    cos, sin = jnp.cos(ang), jnp.sin(ang)
    n = pos.shape[0]
    rest = SWA_HD - ROT_DIM
    c = jnp.concatenate([cos, cos, jnp.ones((n, rest), F32)], axis=1)
    s1 = jnp.concatenate([-sin, jnp.zeros((n, half + rest), F32)], axis=1)
    s2 = jnp.concatenate([jnp.zeros((n, half), F32), sin, jnp.zeros((n, rest), F32)], axis=1)
    reps = LANES // SWA_HD
    return jnp.tile(c, (1, reps)), jnp.tile(s1, (1, reps)), jnp.tile(s2, (1, reps))


def _pair_order():
    order = []
    for p in range(SWA_HEADS // 2):
        order += [p, SWA_HEADS // 2 + p]
    return np.asarray(order)


def _layer_weights(norm_ffn1, ffn1_gu, ffn1_down, norm_mix, w_in, conv_w, a_log, dt_bias, gdn_norm, sinks,
                   w_out, norm_ffn2, ffn2_gu, ffn2_down, norm_ple, ple_proj, ple_gate, norm_final):
    order = _pair_order()
    head_cols = (order[:, None] * SWA_HD + np.arange(SWA_HD)[None, :]).reshape(-1)
    o_qkv, o_z = 0, GDN_CONV_W
    o_a = o_z + GDN_V_W
    o_b = o_a + GDN_HEADS
    o_qs = o_b + GDN_HEADS
    o_ks = o_qs + SWA_Q_W
    o_vs = o_ks + SWA_KV_W
    win_p = jnp.concatenate([
        w_in[:, o_qkv:o_z], w_in[:, o_z:o_a], w_in[:, o_qs:o_ks][:, head_cols], w_in[:, o_ks:o_vs],
        w_in[:, o_vs:o_vs + SWA_KV_W], w_in[:, o_a:o_qs],
        jnp.zeros((D_MODEL, LANES - 2 * GDN_HEADS), w_in.dtype)], axis=1).astype(BF16)
    wout_p = jnp.concatenate([w_out[:GDN_V_W], w_out[GDN_V_W:][head_cols]], axis=0).astype(BF16)
    pad_row = lambda v: jnp.concatenate([v.astype(F32), jnp.zeros((LANES - v.shape[0],), F32)])[None, :]
    return dict(
        g1=norm_ffn1[None, :], wgu1=ffn1_gu.astype(BF16), wd1=ffn1_down.astype(BF16),
        gm=norm_mix[None, :], win=win_p, convw=conv_w, alog=pad_row(a_log), dt=pad_row(dt_bias),
        gnorm=gdn_norm[None, :], sinks=sinks.astype(F32), wout=wout_p,
        g2=norm_ffn2[None, :], wgu2=ffn2_gu.astype(BF16), wd2=ffn2_down.astype(BF16),
        gp=norm_ple[None, :], wpp=ple_proj.astype(BF16), wpg=ple_gate.astype(BF16), gf=norm_final[None, :])


def _group(x, p, w, state, pos0):
    bsz, seq, _ = x.shape
    h2d, proj2d = _ffn_in(x.reshape(bsz * seq, D_MODEL), w["g1"], w["wgu1"], w["wd1"], w["gm"], w["win"])
    rc, rs1, rs2 = _rope_tables(pos0 + jnp.arange(seq))
    proj3d = proj2d.reshape(bsz, seq, P_W)
    if state is None:
        o, s_new, conv_new, k_new, v_new = _mixer_prompt(
            proj3d, w["convw"], w["alog"], w["dt"], w["gnorm"], w["sinks"], rc, rs1, rs2)
    else:
        s0, conv0, k0, v0 = state
        o, s_new, conv_new, k_new, v_new = _mixer_sample(
            proj3d, s0, conv0, k0, v0, w["convw"], w["alog"], w["dt"], w["gnorm"],
            w["sinks"], rc, rs1, rs2)
    y = _ffn_out(o.reshape(bsz * seq, D_MODEL), h2d, p.reshape(bsz * seq, PLE_DIM), w["wout"], w["g2"],
                 w["wgu2"], w["wd2"], w["gp"], w["wpg"], w["wpp"], w["gf"])
    kv_shape = (1, bsz, WINDOW, SWA_KV_HEADS, SWA_HD)
    return (y.reshape(bsz, seq, D_MODEL), s_new[None], conv_new[None], k_new.reshape(kv_shape),
            v_new.reshape(kv_shape))


def kernel(x_prompt, x_sample, state_gdn, state_conv, cache_swa_k, cache_swa_v, p_prompt, p_sample, norm_ffn1, ffn1_gu, ffn1_down, norm_mix, w_in, conv_w, a_log, dt_bias, gdn_norm, sinks, w_out, norm_ffn2, ffn2_gu, ffn2_down, norm_ple, ple_proj, ple_gate, norm_final):
    assert state_gdn.shape[0] == 1, "one layer"
    w = _layer_weights(norm_ffn1[0], ffn1_gu[0], ffn1_down[0], norm_mix[0], w_in[0], conv_w[0], a_log[0],
                       dt_bias[0], gdn_norm[0], sinks[0], w_out[0], norm_ffn2[0], ffn2_gu[0], ffn2_down[0],
                       norm_ple[0], ple_proj[0], ple_gate[0], norm_final)
    bs = x_sample.shape[0]
    yp, sg_p, sc_p, kk_p, vv_p = _group(x_prompt, p_prompt[0], w, None, 0)
    ys, sg_s, sc_s, kk_s, vv_s = _group(
        x_sample, p_sample[0], w,
        (state_gdn[0], state_conv[0], cache_swa_k[0].reshape(bs, WINDOW, SWA_KV_W),
         cache_swa_v[0].reshape(bs, WINDOW, SWA_KV_W)), PAST_LEN)
    return (yp, ys, sg_p, sc_p, kk_p, vv_p, sg_s, sc_s, kk_s, vv_s)
```

```python
import functools

import jax
import jax.numpy as jnp
import numpy as np
from jax.experimental import pallas as pl
from jax.experimental.pallas import tpu as pltpu

F32 = jnp.float32
BF16 = jnp.bfloat16

D_MODEL = 1024
D_FF = 2816
PLE_DIM = 256
NORM_EPS = 1e-6
GDN_HEADS = 4
GDN_DK = 128
GDN_DV = 128
GDN_CONV = 4
GDN_QK_W = GDN_HEADS * GDN_DK
GDN_V_W = GDN_HEADS * GDN_DV
GDN_CONV_W = 2 * GDN_QK_W + GDN_V_W
SWA_HEADS = 8
SWA_KV_HEADS = 2
SWA_HD = 64
SWA_Q_W = SWA_HEADS * SWA_HD
SWA_KV_W = SWA_KV_HEADS * SWA_HD
WINDOW = 128
ROT_DIM = SWA_HD // 4
ROPE_THETA = 500000.0
PAST_LEN = 16384

LANES = 128
SUBLANES = 8
VMEM_LIMIT_BYTES = 56 * 1024 * 1024

P_QKV = 0
P_Z = P_QKV + GDN_CONV_W
P_QS = P_Z + GDN_V_W
P_KS = P_QS + SWA_Q_W
P_VS = P_KS + SWA_KV_W
P_AB = P_VS + SWA_KV_W
P_W = P_AB + LANES

FF_CHUNK = 256
TOKEN_TILE = 512

CHUNK = 64
CAT_W = GDN_HEADS * CHUNK
SAMPLE_T = SUBLANES
SAMPLE_NB = CHUNK // SAMPLE_T


def _rms(x, g):
    return x * jax.lax.rsqrt(jnp.mean(x * x, axis=-1, keepdims=True) + NORM_EPS) * g


def _sigmoid(x):
    return 1.0 / (1.0 + jnp.exp(-x))


def _silu(x):
    return x * _sigmoid(x)


def _mm(a, b):
    return jnp.dot(a.astype(BF16), b.astype(BF16), preferred_element_type=F32)


def _mm_nt(a, b):
    return jax.lax.dot_general(a.astype(BF16), b.astype(BF16), (((1,), (1,)), ((), ())),
                               preferred_element_type=F32)


def _swiglu_acc(n_bf16, wgu_ref, wd_ref):
    rows = n_bf16.shape[0]
    acc = jnp.zeros((rows, D_MODEL), F32)
    for c in range(D_FF // FF_CHUNK):
        lo = c * FF_CHUNK
        gate = jnp.dot(n_bf16, wgu_ref[:, lo:lo + FF_CHUNK], preferred_element_type=F32)
        up = jnp.dot(n_bf16, wgu_ref[:, D_FF + lo:D_FF + lo + FF_CHUNK], preferred_element_type=F32)
        act = (_silu(gate) * up).astype(BF16)
        acc = acc + jnp.dot(act, wd_ref[lo:lo + FF_CHUNK, :], preferred_element_type=F32)
    return acc


def _ffn_in_kernel(x_ref, g1_ref, wgu_ref, wd_ref, gm_ref, win_ref, h_ref, proj_ref):
    x = x_ref[...]
    n1 = _rms(x, g1_ref[...]).astype(BF16)
    h = x + 0.5 * _swiglu_acc(n1, wgu_ref, wd_ref)
    h_ref[...] = h
    n = _rms(h, gm_ref[...]).astype(BF16)
    proj_ref[...] = jnp.dot(n, win_ref[...], preferred_element_type=F32)


def _const_spec(shape):
    return pl.BlockSpec(shape, lambda i: (0,) * len(shape), pipeline_mode=pl.Buffered(1))


def _ffn_in(x2d, g1, wgu, wd, gm, win):
    n_tok = x2d.shape[0]
    tm = min(TOKEN_TILE, n_tok)
    assert n_tok % tm == 0
    return pl.pallas_call(
        _ffn_in_kernel,
        grid=(n_tok // tm,),
        in_specs=[
            pl.BlockSpec((tm, D_MODEL), lambda i: (i, 0)),
            _const_spec((1, D_MODEL)),
            _const_spec((D_MODEL, 2 * D_FF)),
            _const_spec((D_FF, D_MODEL)),
            _const_spec((1, D_MODEL)),
            _const_spec((D_MODEL, P_W)),
        ],
        out_specs=[
            pl.BlockSpec((tm, D_MODEL), lambda i: (i, 0)),
            pl.BlockSpec((tm, P_W), lambda i: (i, 0)),
        ],
        out_shape=[
            jax.ShapeDtypeStruct((n_tok, D_MODEL), F32),
            jax.ShapeDtypeStruct((n_tok, P_W), F32),
        ],
        compiler_params=pltpu.CompilerParams(
            dimension_semantics=("arbitrary",), vmem_limit_bytes=VMEM_LIMIT_BYTES),
        name="ffn_in",
    )(x2d, g1, wgu, wd, gm, win)


def _ffn_out_kernel(o_ref, h_ref, p_ref, wout_ref, g2_ref, wgu_ref, wd_ref, gp_ref, wpg_ref, wpp_ref,
                    gf_ref, y_ref):
    h = h_ref[...] + jnp.dot(o_ref[...], wout_ref[...], preferred_element_type=F32)
    n2 = _rms(h, g2_ref[...]).astype(BF16)
    h = h + 0.5 * _swiglu_acc(n2, wgu_ref, wd_ref)
    npl = _rms(h, gp_ref[...]).astype(BF16)
    gate = _sigmoid(jnp.dot(npl, wpg_ref[...], preferred_element_type=F32))
    pe = jnp.dot(p_ref[...].astype(BF16), wpp_ref[...], preferred_element_type=F32)
    h = h + gate * pe
    y_ref[...] = _rms(h, gf_ref[...])


def _ffn_out(o2d, h2d, p2d, wout, g2, wgu, wd, gp, wpg, wpp, gf):
    n_tok = h2d.shape[0]
    tm = min(TOKEN_TILE, n_tok)
    assert n_tok % tm == 0
    return pl.pallas_call(
        _ffn_out_kernel,
        grid=(n_tok // tm,),
        in_specs=[
            pl.BlockSpec((tm, D_MODEL), lambda i: (i, 0)),
            pl.BlockSpec((tm, D_MODEL), lambda i: (i, 0)),
            pl.BlockSpec((tm, PLE_DIM), lambda i: (i, 0)),
            _const_spec((D_MODEL, D_MODEL)),
            _const_spec((1, D_MODEL)),
            _const_spec((D_MODEL, 2 * D_FF)),
            _const_spec((D_FF, D_MODEL)),
            _const_spec((1, D_MODEL)),
            _const_spec((D_MODEL, D_MODEL)),
            _const_spec((PLE_DIM, D_MODEL)),
            _const_spec((1, D_MODEL)),
        ],
        out_specs=pl.BlockSpec((tm, D_MODEL), lambda i: (i, 0)),
        out_shape=jax.ShapeDtypeStruct((n_tok, D_MODEL), F32),
        compiler_params=pltpu.CompilerParams(
            dimension_semantics=("arbitrary",), vmem_limit_bytes=VMEM_LIMIT_BYTES),
        name="ffn_out",
    )(o2d, h2d, p2d, wout, g2, wgu, wd, gp, wpg, wpp, gf)


def _pad_rows(x):
    rows = x.shape[0]
    if rows == LANES:
        return x
    return jnp.concatenate([x, jnp.zeros((LANES - rows, x.shape[1]), x.dtype)], axis=0)


def _split3(x):
    hi = x.astype(BF16)
    r = x - hi.astype(F32)
    mid = r.astype(BF16)
    lo = (r - mid.astype(F32)).astype(BF16)
    return hi, mid, lo


def _mm_exact_rhs(a_bf16, x):
    hi, mid, lo = _split3(x)
    return (jnp.dot(a_bf16, hi, preferred_element_type=F32) + jnp.dot(a_bf16, mid, preferred_element_type=F32)
            + jnp.dot(a_bf16, lo, preferred_element_type=F32))


def _mm_exact_lhs(x, a_bf16):
    hi, mid, lo = _split3(x)
    return (jnp.dot(hi, a_bf16, preferred_element_type=F32) + jnp.dot(mid, a_bf16, preferred_element_type=F32)
            + jnp.dot(lo, a_bf16, preferred_element_type=F32))


def _bd_cat(y, lane_lo_chunk):
    zero_tile = jnp.zeros((CHUNK, LANES), F32)
    blocks = []
    for hh in range(GDN_HEADS):
        y_tile = y[:, (hh // 2) * LANES:(hh // 2 + 1) * LANES]
        keep = jnp.where(lane_lo_chunk, y_tile, 0.0) if hh % 2 == 0 else jnp.where(lane_lo_chunk, 0.0, y_tile)
        blocks.append(jnp.concatenate([keep, zero_tile] if hh < 2 else [zero_tile, keep], axis=1))
    return jnp.concatenate(blocks, axis=0).astype(BF16)


def _bd_wide(y):
    zero_tile = jnp.zeros((CHUNK, LANES), F32)
    blocks = []
    for hh in range(GDN_HEADS):
        tiles = [zero_tile] * GDN_HEADS
        tiles[hh] = y[:, hh * LANES:(hh + 1) * LANES]
        blocks.append(jnp.concatenate(tiles, axis=1))
    return jnp.concatenate(blocks, axis=0).astype(BF16)


def _bd_pair(sa, sb):
    zeros = jnp.zeros(sa.shape, sa.dtype)
    return jnp.concatenate([jnp.concatenate([sa, zeros], axis=1), jnp.concatenate([zeros, sb], axis=1)], axis=0)


def _gates(ab, alog_row, dt_row):
    zab = ab + dt_row
    softplus = jnp.maximum(zab, 0.0) + jnp.log(1.0 + jnp.exp(-jnp.abs(zab)))
    return -jnp.exp(alog_row) * softplus, _sigmoid(ab)


def _head_l2(x):
    parts = []
    for hh in range(GDN_HEADS):
        xh = x[:, hh * GDN_DK:(hh + 1) * GDN_DK]
        parts.append(xh * jax.lax.rsqrt(jnp.sum(xh * xh, axis=-1, keepdims=True) + 1e-6))
    return jnp.concatenate(parts, axis=1)


def _lane_expand(cols, first_lane, rows):
    return jnp.concatenate(
        [jnp.broadcast_to(cols[:, first_lane + hh:first_lane + hh + 1], (rows, LANES))
         for hh in range(GDN_HEADS)], axis=1)


def _rope(x, rc, rs1, rs2):
    return x * rc + pltpu.roll(x, LANES - ROT_DIM // 2, 1) * rs1 + pltpu.roll(x, ROT_DIM // 2, 1) * rs2


def _after(v, anchor):
    z = pltpu.bitcast(anchor[0:v.shape[0], 0:v.shape[1]], jnp.uint32)
    z = jax.lax.shift_right_logical(jax.lax.shift_right_logical(z, jnp.uint32(16)), jnp.uint32(16))
    return pltpu.bitcast(pltpu.bitcast(v, jnp.uint32) | z, v.dtype)


def _uw_rhs(vb_all, kbg_all):
    return jnp.concatenate(
        [jnp.concatenate([vb_all[:, hh * GDN_DV:(hh + 1) * GDN_DV],
                          kbg_all[:, hh * GDN_DK:(hh + 1) * GDN_DK]], axis=1)
         for hh in range(GDN_HEADS)], axis=0)


def _split_uw(uw):
    u_all = jnp.concatenate([uw[hh * CHUNK:(hh + 1) * CHUNK, 0:GDN_DV] for hh in range(GDN_HEADS)], axis=1)
    w_all = jnp.concatenate([uw[hh * CHUNK:(hh + 1) * CHUNK, GDN_DV:] for hh in range(GDN_HEADS)], axis=1)
    return u_all, w_all


def _stack_heads_t(x):
    return jnp.transpose(jnp.concatenate(
        [x[:, hh * GDN_DK:(hh + 1) * GDN_DK] for hh in range(GDN_HEADS)], axis=0))


def _mixer_prompt_kernel(sinks_ref, proj_ref, convw_ref, alog_ref, dt_ref, gnorm_ref, rc_ref, rs1_ref,
                         rs2_ref, o_ref, sout_ref, convout_ref, kout_ref, vout_ref,
                         xext_ref, s_ref, kprev_ref, vprev_ref,
                         q_st, k_st, v_st, gcc_st, gcr_st, beta_st, zs_st, os_st, *, tb, n_blocks):
    t = pl.program_id(1)
    nch = tb // CHUNK
    stages = (q_st, k_st, v_st, gcc_st, gcr_st, beta_st, zs_st, os_st)

    @pl.when(t == 0)
    def _init():
        xext_ref[0:SUBLANES, :] = jnp.zeros((SUBLANES, GDN_CONV_W), F32)
        for ref in (s_ref, kprev_ref, vprev_ref) + stages:
            ref[...] = jnp.zeros(ref.shape, F32)

    s_cat = s_ref[...]
    k_prev = kprev_ref[...]
    v_prev = vprev_ref[...]
    q_all, k_all, v_all = q_st[...], k_st[...], v_st[...]
    gc_cols, gc_rows, beta_all = gcc_st[...], gcr_st[...], beta_st[...]
    zs_prev, os_prev = zs_st[...], os_st[...]

    r_t = jax.lax.broadcasted_iota(jnp.int32, (tb, LANES), 0)
    c_t = jax.lax.broadcasted_iota(jnp.int32, (tb, LANES), 1)
    lane_lo = c_t < CHUNK
    lane_lo_chunk = jax.lax.broadcasted_iota(jnp.int32, (CHUNK, LANES), 1) < CHUNK
    lane_lo_row = jax.lax.broadcasted_iota(jnp.int32, (1, LANES), 1) < CHUNK
    fr = {}

    def front():
        g_all, fr["beta"] = _gates(proj_ref[0, :, P_AB:P_AB + LANES], alog_ref[...], dt_ref[...])
        same_chunk = (r_t ^ c_t) < CHUNK
        fr["gcc"] = _mm_exact_rhs(((c_t <= r_t) & same_chunk).astype(BF16), g_all)
        fr["gcr"] = _mm_exact_lhs(jnp.transpose(g_all)[0:SUBLANES, :],
                                  ((r_t <= c_t) & same_chunk).astype(BF16))
        rc, rs1, rs2 = rc_ref[...], rs1_ref[...], rs2_ref[...]
        k_own = _rope(proj_ref[0, :, P_KS:P_KS + SWA_KV_W], rc, rs1, rs2)
        v_own = proj_ref[0, :, P_VS:P_VS + SWA_KV_W]
        fr["k_own"], fr["v_own"] = k_own, v_own
        k_cat = jnp.concatenate([k_prev, k_own], axis=0)
        q_rows = [None] * SWA_HEADS
        for p in range(SWA_HEADS // 2):
            qp = _rope(proj_ref[0, :, P_QS + p * LANES:P_QS + (p + 1) * LANES], rc, rs1, rs2)
            q_rows[p] = jnp.where(lane_lo, qp, 0.0)
            q_rows[SWA_HEADS // 2 + p] = jnp.where(lane_lo, 0.0, qp)
        s_all = _mm_nt(jnp.concatenate(q_rows, axis=0), k_cat) * (SWA_HD ** -0.5)
        yield
        xext_ref[SUBLANES:SUBLANES + tb, :] = proj_ref[0, :, P_QKV:P_QKV + GDN_CONV_W]
        base = SUBLANES - (GDN_CONV - 1)
        for name, lo, scale in (("q", 0, GDN_DK ** -0.5), ("k", GDN_QK_W, 1.0), ("v", 2 * GDN_QK_W, None)):
            heads = []
            for hh in range(GDN_HEADS):
                cols = slice(lo + hh * LANES, lo + (hh + 1) * LANES)
                taps = _after(convw_ref[:, cols], cur["anchor"])
                conv = xext_ref[base:base + tb, cols] * taps[0:1]
                for j in range(1, GDN_CONV):
                    conv = conv + xext_ref[base + j:base + j + tb, cols] * taps[j:j + 1]
                act = _silu(conv)
                if scale is not None:
                    act = act * (jax.lax.rsqrt(jnp.sum(act * act, axis=-1, keepdims=True) + 1e-6) * scale)
                heads.append(act)
                yield
            fr[name] = jnp.concatenate(heads, axis=1)
        fr["conv_tail"] = xext_ref[tb:tb + SUBLANES, :]
        rk = jax.lax.broadcasted_iota(jnp.int32, (tb, 2 * WINDOW), 0)
        ck = jax.lax.broadcasted_iota(jnp.int32, (tb, 2 * WINDOW), 1)
        first_off = jnp.where(t > 0, 0, WINDOW)
        dist = ck - rk
        kmask = ((dist > jnp.where(ck < WINDOW, first_off, -WINDOW))
                 & (dist <= jnp.where(ck < WINDOW, 2 * WINDOW, WINDOW)))
        probs, dens = [], []
        for hh in range(SWA_HEADS):
            masked = _after(jnp.full((SUBLANES, LANES), -jnp.inf, F32), cur["anchor"])
            masked = jnp.concatenate([jnp.concatenate([masked] * (tb // SUBLANES), axis=0)] * 2, axis=1)
            sh = jnp.where(kmask, s_all[hh * tb:(hh + 1) * tb], masked)
            sk = sinks_ref[hh]
            m = jnp.maximum(jnp.max(sh, axis=-1, keepdims=True), sk)
            ph = jnp.exp(sh - m)
            dens.append(jnp.sum(ph, axis=-1, keepdims=True) + jnp.exp(sk - m))
            probs.append(ph.astype(BF16))
            yield
        fr["zs"] = _silu(proj_ref[0, :, P_Z:P_Z + GDN_V_W])
        yield
        v_cat = jnp.concatenate([_after(v_prev[0:SUBLANES], cur["anchor"]), v_prev[SUBLANES:], v_own],
                                axis=0).astype(BF16)
        pv = [jnp.dot(probs[hh], v_cat, preferred_element_type=F32) / dens[hh] for hh in range(SWA_HEADS)]
        fr["os"] = jnp.concatenate(
            [jnp.where(lane_lo, pv[p], pv[SWA_HEADS // 2 + p]) for p in range(SWA_HEADS // 2)], axis=1)

    front_pieces = front()
    pieces_per_fill = [1] + [2] * 6 + [1] * 32

    cur = {"anchor": None}

    def fill(anchor):
        cur["anchor"] = anchor
        for _ in range(pieces_per_fill.pop(0)):
            next(front_pieces, None)

    gc_full = _lane_expand(gc_cols, 0, tb)
    beta_full = _lane_expand(beta_all, GDN_HEADS, tb)
    egc_full = jnp.exp(gc_full)
    kb_all = k_all * beta_full
    vb_all = v_all * beta_full
    kbg_all = kb_all * egc_full
    qd_all = q_all * egc_full
    gc_rows_sw = pltpu.roll(gc_rows, CHUNK, 1)

    ri = jax.lax.broadcasted_iota(jnp.int32, (CHUNK, CAT_W), 0)
    ci = jax.lax.broadcasted_iota(jnp.int32, (CHUNK, CAT_W), 1)
    cj = ci & (CHUNK - 1)
    tri = cj <= ri
    strict = cj < ri
    eye = (cj == ri).astype(F32)
    blk = ri ^ cj
    bd = functools.partial(_bd_cat, lane_lo_chunk=lane_lo_chunk)

    chunks = range(nch)
    row_sl = [slice(c * CHUNK, (c + 1) * CHUNK) for c in chunks]
    a_mat, qk, xinv, pw = [None] * nch, [None] * nch, [None] * nch, [None] * nch
    fill(None)
    for c in chunks:
        rows = row_sl[c]
        gcc_pairs, gcr_pairs = [], []
        for p in range(GDN_HEADS // 2):
            gcc_pairs.append(jnp.where(lane_lo_chunk, gc_full[rows, (2 * p) * LANES:(2 * p + 1) * LANES],
                                       gc_full[rows, (2 * p + 1) * LANES:(2 * p + 2) * LANES]))
            ra = (gc_rows if c == 0 else gc_rows_sw)[2 * p:2 * p + 1, :]
            rb_ = (gc_rows_sw if c == 0 else gc_rows)[2 * p + 1:2 * p + 2, :]
            gcr_pairs.append(jnp.where(lane_lo_row, ra, rb_))
        gcc = jnp.concatenate(gcc_pairs, axis=1)
        gcr = jnp.concatenate(gcr_pairs, axis=1)
        decay = jnp.exp(jnp.where(tri, gcc - gcr, -jnp.inf))
        kq = jax.lax.dot_general(
            jnp.concatenate([kb_all[rows], q_all[rows]], axis=0).astype(BF16), _bd_wide(k_all[rows]),
            (((1,), (1,)), ((), ())), preferred_element_type=F32)
        a_mat[c] = jnp.where(strict, kq[0:CHUNK] * decay, 0.0)
        qk[c] = kq[CHUNK:] * decay
    fill(kb_all)
    a0 = [jnp.where(blk < SUBLANES, a_mat[c], 0.0) for c in chunks]
    for c in chunks:
        xinv[c] = eye - a0[c]
        pw[c] = _mm(a0[c], bd(a0[c]))
    fill(a0[0])
    pw_in = pw[0]
    for c in chunks:
        pw_bd = bd(pw[c])
        xinv[c] = xinv[c] + _mm(xinv[c], pw_bd)
        pw[c] = _mm(pw[c], pw_bd)
    fill(pw_in)
    for c in chunks:
        xinv[c] = xinv[c] + _mm(xinv[c], bd(pw[c]))
    fill(pw[0])
    s = SUBLANES
    while s < CHUNK:
        y = [_mm(jnp.where((blk < 2 * s) & (blk >= s), a_mat[c], 0.0), bd(xinv[c])) for c in chunks]
        fill(xinv[0])
        for c in chunks:
            xinv[c] = xinv[c] - _mm(xinv[c], bd(y[c]))
        fill(y[0])
        s *= 2
    prep = []
    for c in chunks:
        rows = row_sl[c]
        u_all, w_all = _split_uw(_mm(bd(xinv[c]), _uw_rhs(vb_all[rows], kbg_all[rows])))
        glast = gc_full[c * CHUNK + CHUNK - 1:c * CHUNK + CHUNK, :]
        kdec_t = _stack_heads_t(k_all[rows] * jnp.exp(glast - gc_full[rows]))
        prep.append((u_all, w_all, kdec_t, jnp.exp(glast)))
    fill(xinv[0])

    o_chunks = []
    for c in chunks:
        rows = row_sl[c]
        u_all, w_all, kdec_t, eg_last = prep[c]
        ws, qs = [], []
        for p in range(GDN_HEADS // 2):
            cols = slice(2 * p * LANES, (2 * p + 2) * LANES)
            r = _mm(jnp.concatenate([w_all[:, cols], qd_all[rows, cols]], axis=0),
                    _bd_pair(s_cat[:, (2 * p) * LANES:(2 * p + 1) * LANES],
                             s_cat[:, (2 * p + 1) * LANES:(2 * p + 2) * LANES]))
            ws.append(r[0:CHUNK])
            qs.append(r[CHUNK:])
        fill(w_all)
        v_new = u_all - jnp.concatenate(ws, axis=1)
        r2 = _mm(jnp.concatenate([qk[c], kdec_t], axis=0), _bd_wide(v_new))
        o_chunks.append(jnp.concatenate(qs, axis=1) + r2[0:CHUNK])
        s_cat = s_cat * eg_last + r2[CHUNK:]
        fill(v_new)
    cur["anchor"] = s_cat
    for _ in front_pieces:
        pass
    o_gdn = jnp.concatenate(o_chunks, axis=0)
    og = [_rms(o_gdn[:, hh * GDN_DV:(hh + 1) * GDN_DV], gnorm_ref[...]) * zs_prev[:, hh * GDN_DV:(hh + 1) * GDN_DV]
          for hh in range(GDN_HEADS)]
    o_ref[0] = jnp.concatenate(og + [os_prev], axis=-1).astype(o_ref.dtype)

    s_ref[...] = s_cat
    kprev_ref[...] = fr["k_own"]
    vprev_ref[...] = fr["v_own"]
    xext_ref[0:SUBLANES, :] = fr["conv_tail"]
    for ref, name in zip(stages, ("q", "k", "v", "gcc", "gcr", "beta", "zs", "os")):
        ref[...] = fr[name]

    @pl.when(t == n_blocks - 1)
    def _window_out():
        convout_ref[0] = xext_ref[tb + SUBLANES - (GDN_CONV - 1):tb + SUBLANES, :]
        kout_ref[0] = fr["k_own"]
        vout_ref[0] = fr["v_own"]

    @pl.when(t == n_blocks)
    def _state_out():
        for hh in range(GDN_HEADS):
            sout_ref[0, hh] = s_ref[:, hh * GDN_DV:(hh + 1) * GDN_DV]


def _mixer_prompt(proj3d, convw, alog_row, dt_row, gnorm, sinks, rc, rs1, rs2):
    bsz, seq, _ = proj3d.shape
    tb = WINDOW
    assert seq % tb == 0
    n_blocks = seq // tb
    kern = functools.partial(_mixer_prompt_kernel, tb=tb, n_blocks=n_blocks)
    batch_blk = lambda shape: pl.BlockSpec((1,) + shape, lambda b, t: (b,) + (0,) * len(shape))
    const_blk = lambda shape: pl.BlockSpec(shape, lambda b, t: (0,) * len(shape))
    time_blk = lambda shape: pl.BlockSpec(shape, lambda b, t: (jnp.minimum(t, n_blocks - 1), 0))
    return pl.pallas_call(
        kern,
        grid=(bsz, n_blocks + 1),
        in_specs=[
            pl.BlockSpec(memory_space=pltpu.SMEM),
            pl.BlockSpec((1, tb, P_W), lambda b, t: (b, jnp.minimum(t, n_blocks - 1), 0)),
            const_blk((GDN_CONV, GDN_CONV_W)),
            const_blk((1, LANES)),
            const_blk((1, LANES)),
            const_blk((1, GDN_DV)),
            time_blk((tb, LANES)),
            time_blk((tb, LANES)),
            time_blk((tb, LANES)),
        ],
        out_specs=[
            pl.BlockSpec((1, tb, D_MODEL), lambda b, t: (b, jnp.maximum(t - 1, 0), 0)),
            batch_blk((GDN_HEADS, GDN_DK, GDN_DV)),
            batch_blk((GDN_CONV - 1, GDN_CONV_W)),
            batch_blk((WINDOW, SWA_KV_W)),
            batch_blk((WINDOW, SWA_KV_W)),
        ],
        out_shape=[
            jax.ShapeDtypeStruct((bsz, seq, D_MODEL), BF16),
            jax.ShapeDtypeStruct((bsz, GDN_HEADS, GDN_DK, GDN_DV), F32),
            jax.ShapeDtypeStruct((bsz, GDN_CONV - 1, GDN_CONV_W), F32),
            jax.ShapeDtypeStruct((bsz, WINDOW, SWA_KV_W), F32),
            jax.ShapeDtypeStruct((bsz, WINDOW, SWA_KV_W), F32),
        ],
        scratch_shapes=[
            pltpu.VMEM((tb + SUBLANES, GDN_CONV_W), F32),
            pltpu.VMEM((GDN_DK, GDN_HEADS * GDN_DV), F32),
            pltpu.VMEM((WINDOW, SWA_KV_W), F32),
            pltpu.VMEM((WINDOW, SWA_KV_W), F32),
            pltpu.VMEM((tb, GDN_QK_W), F32),
            pltpu.VMEM((tb, GDN_QK_W), F32),
            pltpu.VMEM((tb, GDN_V_W), F32),
            pltpu.VMEM((tb, LANES), F32),
            pltpu.VMEM((SUBLANES, LANES), F32),
            pltpu.VMEM((tb, LANES), F32),
            pltpu.VMEM((tb, GDN_V_W), F32),
            pltpu.VMEM((tb, SWA_Q_W), F32),
        ],
        compiler_params=pltpu.CompilerParams(
            dimension_semantics=("arbitrary", "arbitrary"), vmem_limit_bytes=VMEM_LIMIT_BYTES),
        name="mixer_prompt",
    )(sinks, proj3d, convw, alog_row, dt_row, gnorm, rc, rs1, rs2)


PROMPT_NS = 2


def _mixer_prompt2_kernel(sinks_ref, proj_ref, convw_ref, alog_ref, dt_ref, gnorm_ref, rc_ref, rs1_ref,
                          rs2_ref, o_ref, sout_ref, convout_ref, kout_ref, vout_ref,
                          xext_ref, s_ref, kprev_ref, vprev_ref,
                          q_st, k_st, v_st, gcc_st, gcr_st, beta_st, zs_st, os_st, *, tb, n_blocks):
    t = pl.program_id(1)
    ns = PROMPT_NS
    nch = tb // CHUNK
    stages = (q_st, k_st, v_st, gcc_st, gcr_st, beta_st, zs_st, os_st)
    stage_names = ("q", "k", "v", "gcc", "gcr", "beta", "zs", "os")

    @pl.when(t == 0)
    def _init():
        for sq in range(ns):
            xext_ref[sq, 0:SUBLANES, :] = jnp.zeros((SUBLANES, GDN_CONV_W), F32)
        for ref in (s_ref, kprev_ref, vprev_ref) + stages:
            ref[...] = jnp.zeros(ref.shape, F32)

    r_t = jax.lax.broadcasted_iota(jnp.int32, (tb, LANES), 0)
    c_t = jax.lax.broadcasted_iota(jnp.int32, (tb, LANES), 1)
    lane_lo = c_t < CHUNK
    lane_lo_chunk = jax.lax.broadcasted_iota(jnp.int32, (CHUNK, LANES), 1) < CHUNK
    lane_lo_row = jax.lax.broadcasted_iota(jnp.int32, (1, LANES), 1) < CHUNK
    same_chunk = (r_t ^ c_t) < CHUNK
    rc, rs1, rs2 = rc_ref[...], rs1_ref[...], rs2_ref[...]
    rk = jax.lax.broadcasted_iota(jnp.int32, (tb, 2 * WINDOW), 0)
    ck = jax.lax.broadcasted_iota(jnp.int32, (tb, 2 * WINDOW), 1)
    first_off = jnp.where(t > 0, 0, WINDOW)
    dist = ck - rk
    kmask = ((dist > jnp.where(ck < WINDOW, first_off, -WINDOW))
             & (dist <= jnp.where(ck < WINDOW, 2 * WINDOW, WINDOW)))

    s_cat = [s_ref[sq] for sq in range(ns)]
    k_prev = [kprev_ref[sq] for sq in range(ns)]
    v_prev = [vprev_ref[sq] for sq in range(ns)]
    staged = [{name: ref[sq] for ref, name in zip(stages, stage_names)} for sq in range(ns)]
    fr = [{} for _ in range(ns)]

    def front(sq):
        f = fr[sq]
        xext_ref[sq, SUBLANES:SUBLANES + tb, :] = proj_ref[sq, :, P_QKV:P_QKV + GDN_CONV_W]
        base = SUBLANES - (GDN_CONV - 1)
        for name, lo in (("q", 0), ("k", GDN_QK_W), ("v", 2 * GDN_QK_W)):
            cols = slice(lo, lo + GDN_QK_W)
            conv = xext_ref[sq, base:base + tb, cols] * convw_ref[0:1, cols]
            for j in range(1, GDN_CONV):
                conv = conv + xext_ref[sq, base + j:base + j + tb, cols] * convw_ref[j:j + 1, cols]
            act = _silu(conv)
            if name == "q":
                f["q"] = _head_l2(act) * (GDN_DK ** -0.5)
            elif name == "k":
                f["k"] = _head_l2(act)
            else:
                f["v"] = act
            yield
        f["conv_tail"] = xext_ref[sq, tb:tb + SUBLANES, :]
        g_all, f["beta"] = _gates(proj_ref[sq, :, P_AB:P_AB + LANES], alog_ref[...], dt_ref[...])
        f["gcc"] = _mm_exact_rhs(((c_t <= r_t) & same_chunk).astype(BF16), g_all)
        f["gcr"] = _mm_exact_lhs(jnp.transpose(g_all)[0:SUBLANES, :],
                                 ((r_t <= c_t) & same_chunk).astype(BF16))
        yield
        k_own = _rope(proj_ref[sq, :, P_KS:P_KS + SWA_KV_W], rc, rs1, rs2)
        v_own = proj_ref[sq, :, P_VS:P_VS + SWA_KV_W]
        f["k_own"], f["v_own"] = k_own, v_own
        k_cat = jnp.concatenate([k_prev[sq], k_own], axis=0)
        v_cat = jnp.concatenate([v_prev[sq], v_own], axis=0).astype(BF16)
        q_rows = [None] * SWA_HEADS
        for p in range(SWA_HEADS // 2):
            qp = _rope(proj_ref[sq, :, P_QS + p * LANES:P_QS + (p + 1) * LANES], rc, rs1, rs2)
            q_rows[p] = jnp.where(lane_lo, qp, 0.0)
            q_rows[SWA_HEADS // 2 + p] = jnp.where(lane_lo, 0.0, qp)
        s_all = _mm_nt(jnp.concatenate(q_rows, axis=0), k_cat) * (SWA_HD ** -0.5)
        yield
        pv = [None] * SWA_HEADS
        for hh in range(SWA_HEADS):
            sh = jnp.where(kmask, s_all[hh * tb:(hh + 1) * tb], -jnp.inf)
            sk = sinks_ref[hh]
            m = jnp.maximum(jnp.max(sh, axis=-1, keepdims=True), sk)
            ph = jnp.exp(sh - m)
            den = jnp.sum(ph, axis=-1, keepdims=True) + jnp.exp(sk - m)
            pv[hh] = jnp.dot(ph.astype(BF16), v_cat, preferred_element_type=F32) / den
            yield
        f["os"] = jnp.concatenate(
            [jnp.where(lane_lo, pv[p], pv[SWA_HEADS // 2 + p]) for p in range(SWA_HEADS // 2)], axis=1)
        f["zs"] = _silu(proj_ref[sq, :, P_Z:P_Z + GDN_V_W])
        yield

    fronts = [front(sq) for sq in range(ns)]

    def fill():
        for gen in fronts:
            next(gen, None)

    units = [(sq, c) for sq in range(ns) for c in range(nch)]
    rows_of = lambda c: slice(c * CHUNK, (c + 1) * CHUNK)
    ri = jax.lax.broadcasted_iota(jnp.int32, (CHUNK, CAT_W), 0)
    ci = jax.lax.broadcasted_iota(jnp.int32, (CHUNK, CAT_W), 1)
    cj = ci & (CHUNK - 1)
    tri = cj <= ri
    strict = cj < ri
    eye = (cj == ri).astype(F32)
    blk = ri ^ cj
    bd = functools.partial(_bd_cat, lane_lo_chunk=lane_lo_chunk)

    drv = []
    for sq in range(ns):
        st = staged[sq]
        gc_full = _lane_expand(st["gcc"], 0, tb)
        beta_full = _lane_expand(st["beta"], GDN_HEADS, tb)
        egc_full = jnp.exp(gc_full)
        kb_all = st["k"] * beta_full
        drv.append(dict(gc_full=gc_full, kb=kb_all, vb=st["v"] * beta_full, kbg=kb_all * egc_full,
                        qd=st["q"] * egc_full, gcr=st["gcr"], gcr_sw=pltpu.roll(st["gcr"], CHUNK, 1)))
    fill()
    a_mat, qk, xinv, pw = {}, {}, {}, {}
    for u in units:
        sq, c = u
        d, rows = drv[sq], rows_of(c)
        gcc_pairs, gcr_pairs = [], []
        for p in range(GDN_HEADS // 2):
            gcc_pairs.append(jnp.where(lane_lo_chunk, d["gc_full"][rows, (2 * p) * LANES:(2 * p + 1) * LANES],
                                       d["gc_full"][rows, (2 * p + 1) * LANES:(2 * p + 2) * LANES]))
            ra = (d["gcr"] if c == 0 else d["gcr_sw"])[2 * p:2 * p + 1, :]
            rb_ = (d["gcr_sw"] if c == 0 else d["gcr"])[2 * p + 1:2 * p + 2, :]
            gcr_pairs.append(jnp.where(lane_lo_row, ra, rb_))
        gcc = jnp.concatenate(gcc_pairs, axis=1)
        gcr = jnp.concatenate(gcr_pairs, axis=1)
        decay = jnp.exp(jnp.where(tri, gcc - gcr, -jnp.inf))
        kq = jax.lax.dot_general(
            jnp.concatenate([d["kb"][rows], staged[sq]["q"][rows]], axis=0).astype(BF16),
            _bd_wide(staged[sq]["k"][rows]), (((1,), (1,)), ((), ())), preferred_element_type=F32)
        a_mat[u] = jnp.where(strict, kq[0:CHUNK] * decay, 0.0)
        qk[u] = kq[CHUNK:] * decay
    fill()
    for u in units:
        a0 = jnp.where(blk < SUBLANES, a_mat[u], 0.0)
        xinv[u] = eye - a0
        pw[u] = _mm(a0, bd(a0))
    fill()
    for u in units:
        pw_bd = bd(pw[u])
        xinv[u] = xinv[u] + _mm(xinv[u], pw_bd)
        pw[u] = _mm(pw[u], pw_bd)
    fill()
    for u in units:
        xinv[u] = xinv[u] + _mm(xinv[u], bd(pw[u]))
    fill()
    s = SUBLANES
    while s < CHUNK:
        y = {u: _mm(jnp.where((blk < 2 * s) & (blk >= s), a_mat[u], 0.0), bd(xinv[u])) for u in units}
        fill()
        for u in units:
            xinv[u] = xinv[u] - _mm(xinv[u], bd(y[u]))
        fill()
        s *= 2
    prep = {}
    for u in units:
        sq, c = u
        d, rows = drv[sq], rows_of(c)
        u_all, w_all = _split_uw(_mm(bd(xinv[u]), _uw_rhs(d["vb"][rows], d["kbg"][rows])))
        glast = d["gc_full"][c * CHUNK + CHUNK - 1:c * CHUNK + CHUNK, :]
        kdec_t = _stack_heads_t(staged[sq]["k"][rows] * jnp.exp(glast - d["gc_full"][rows]))
        prep[u] = (u_all, w_all, kdec_t, jnp.exp(glast))
    fill()

    o_chunks = [[] for _ in range(ns)]
    for c in range(nch):
        rows = rows_of(c)
        ws, qs = [[] for _ in range(ns)], [[] for _ in range(ns)]
        for sq in range(ns):
            w_all = prep[(sq, c)][1]
            for p in range(GDN_HEADS // 2):
                cols = slice(2 * p * LANES, (2 * p + 2) * LANES)
                r = _mm(jnp.concatenate([w_all[:, cols], drv[sq]["qd"][rows, cols]], axis=0),
                        _bd_pair(s_cat[sq][:, (2 * p) * LANES:(2 * p + 1) * LANES],
                                 s_cat[sq][:, (2 * p + 1) * LANES:(2 * p + 2) * LANES]))
                ws[sq].append(r[0:CHUNK])
                qs[sq].append(r[CHUNK:])
        fill()
        for sq in range(ns):
            u_all, _, kdec_t, eg_last = prep[(sq, c)]
            v_new = u_all - jnp.concatenate(ws[sq], axis=1)
            r2 = _mm(jnp.concatenate([qk[(sq, c)], kdec_t], axis=0), _bd_wide(v_new))
            o_chunks[sq].append(jnp.concatenate(qs[sq], axis=1) + r2[0:CHUNK])
            s_cat[sq] = s_cat[sq] * eg_last + r2[CHUNK:]
        fill()
    for gen in fronts:
        for _ in gen:
            pass

    for sq in range(ns):
        o_gdn = jnp.concatenate(o_chunks[sq], axis=0)
        zs_prev = staged[sq]["zs"]
        og = [_rms(o_gdn[:, hh * GDN_DV:(hh + 1) * GDN_DV], gnorm_ref[...])
              * zs_prev[:, hh * GDN_DV:(hh + 1) * GDN_DV] for hh in range(GDN_HEADS)]
        o_ref[sq] = jnp.concatenate(og + [staged[sq]["os"]], axis=-1).astype(o_ref.dtype)
        s_ref[sq] = s_cat[sq]
        kprev_ref[sq] = fr[sq]["k_own"]
        vprev_ref[sq] = fr[sq]["v_own"]
        xext_ref[sq, 0:SUBLANES, :] = fr[sq]["conv_tail"]
        for ref, name in zip(stages, stage_names):
            ref[sq] = fr[sq][name]

    @pl.when(t == n_blocks - 1)
    def _window_out():
        for sq in range(ns):
            convout_ref[sq] = xext_ref[sq, tb + SUBLANES - (GDN_CONV - 1):tb + SUBLANES, :]
            kout_ref[sq] = fr[sq]["k_own"]
            vout_ref[sq] = fr[sq]["v_own"]

    @pl.when(t == n_blocks)
    def _state_out():
        for sq in range(ns):
            for hh in range(GDN_HEADS):
                sout_ref[sq, hh] = s_ref[sq, :, hh * GDN_DV:(hh + 1) * GDN_DV]


def _mixer_prompt2(proj3d, convw, alog_row, dt_row, gnorm, sinks, rc, rs1, rs2):
    bsz, seq, _ = proj3d.shape
    tb, ns = WINDOW, PROMPT_NS
    assert seq % tb == 0 and bsz % ns == 0
    n_blocks = seq // tb
    kern = functools.partial(_mixer_prompt2_kernel, tb=tb, n_blocks=n_blocks)
    seq_blk = lambda shape: pl.BlockSpec((ns,) + shape, lambda b, t: (b,) + (0,) * len(shape))
    const_blk = lambda shape: pl.BlockSpec(shape, lambda b, t: (0,) * len(shape))
    time_blk = lambda shape: pl.BlockSpec(shape, lambda b, t: (jnp.minimum(t, n_blocks - 1), 0))
    vmem = lambda *shape: pltpu.VMEM((ns,) + shape, F32)
    return pl.pallas_call(
        kern,
        grid=(bsz // ns, n_blocks + 1),
        in_specs=[
            pl.BlockSpec(memory_space=pltpu.SMEM),
            pl.BlockSpec((ns, tb, P_W), lambda b, t: (b, jnp.minimum(t, n_blocks - 1), 0)),
            const_blk((GDN_CONV, GDN_CONV_W)),
            const_blk((1, LANES)),
            const_blk((1, LANES)),
            const_blk((1, GDN_DV)),
            time_blk((tb, LANES)),
            time_blk((tb, LANES)),
            time_blk((tb, LANES)),
        ],
        out_specs=[
            pl.BlockSpec((ns, tb, D_MODEL), lambda b, t: (b, jnp.maximum(t - 1, 0), 0)),
            seq_blk((GDN_HEADS, GDN_DK, GDN_DV)),
            seq_blk((GDN_CONV - 1, GDN_CONV_W)),
            seq_blk((WINDOW, SWA_KV_W)),
            seq_blk((WINDOW, SWA_KV_W)),
        ],
        out_shape=[
            jax.ShapeDtypeStruct((bsz, seq, D_MODEL), BF16),
            jax.ShapeDtypeStruct((bsz, GDN_HEADS, GDN_DK, GDN_DV), F32),
            jax.ShapeDtypeStruct((bsz, GDN_CONV - 1, GDN_CONV_W), F32),
            jax.ShapeDtypeStruct((bsz, WINDOW, SWA_KV_W), F32),
            jax.ShapeDtypeStruct((bsz, WINDOW, SWA_KV_W), F32),
        ],
        scratch_shapes=[
            vmem(tb + SUBLANES, GDN_CONV_W),
            vmem(GDN_DK, GDN_HEADS * GDN_DV),
            vmem(WINDOW, SWA_KV_W),
            vmem(WINDOW, SWA_KV_W),
            vmem(tb, GDN_QK_W),
            vmem(tb, GDN_QK_W),
            vmem(tb, GDN_V_W),
            vmem(tb, LANES),
            vmem(SUBLANES, LANES),
            vmem(tb, LANES),
            vmem(tb, GDN_V_W),
            vmem(tb, SWA_Q_W),
        ],
        compiler_params=pltpu.CompilerParams(
            dimension_semantics=("arbitrary", "arbitrary"), vmem_limit_bytes=VMEM_LIMIT_BYTES),
        name="mixer_prompt",
    )(sinks, proj3d, convw, alog_row, dt_row, gnorm, rc, rs1, rs2)


def _mixer_sample_kernel(sinks_ref, proj_ref, s0_ref, conv0_ref, k0_ref, v0_ref, convw_ref, alog_ref, dt_ref,
                         gnorm_ref, rc_ref, rs1_ref, rs2_ref,
                         o_ref, sout_ref, convout_ref, kout_ref, vout_ref, xext_ref):
    nb, ts, rows = SAMPLE_NB, SAMPLE_T, CHUNK
    seqs = range(nb)
    grp = lambda x, b: x[b * ts:(b + 1) * ts]

    hist = SUBLANES - (GDN_CONV - 1)
    conv_parts = []
    for b in seqs:
        xext_ref[2 * ts * b + hist:2 * ts * b + ts, :] = conv0_ref[b]
        xext_ref[2 * ts * b + ts:2 * ts * (b + 1), :] = proj_ref[b, :, P_QKV:P_QKV + GDN_CONV_W]
    for b in seqs:
        base = 2 * ts * b + hist
        conv = xext_ref[base:base + ts, :] * convw_ref[0:1, :]
        for j in range(1, GDN_CONV):
            conv = conv + xext_ref[base + j:base + j + ts, :] * convw_ref[j:j + 1, :]
        conv_parts.append(conv)
        convout_ref[b] = xext_ref[2 * ts * (b + 1) - (GDN_CONV - 1):2 * ts * (b + 1), :]
    qkv = _silu(jnp.concatenate(conv_parts, axis=0))
    q_all = _head_l2(qkv[:, 0:GDN_QK_W]) * (GDN_DK ** -0.5)
    k_all = _head_l2(qkv[:, GDN_QK_W:2 * GDN_QK_W])
    v_all = qkv[:, 2 * GDN_QK_W:]

    def proj_cols(lo, width):
        return jnp.concatenate([proj_ref[b, :, lo:lo + width] for b in seqs], axis=0)

    g_all, beta_all = _gates(proj_cols(P_AB, LANES), alog_ref[...], dt_ref[...])

    r_t = jax.lax.broadcasted_iota(jnp.int32, (rows, LANES), 0)
    c_t = jax.lax.broadcasted_iota(jnp.int32, (rows, LANES), 1)
    r_p = jax.lax.broadcasted_iota(jnp.int32, (LANES, LANES), 0)
    c_p = jax.lax.broadcasted_iota(jnp.int32, (LANES, LANES), 1)
    g_pad = _pad_rows(g_all)
    gc_cols = _mm_exact_rhs(((c_t <= r_t) & ((r_t ^ c_t) < ts)).astype(BF16), g_pad)
    gc_rows = _mm_exact_lhs(jnp.transpose(g_pad)[0:SUBLANES, :],
                            ((r_p <= c_p) & ((r_p ^ c_p) < ts)).astype(BF16))
    gc_rows_sw = pltpu.roll(gc_rows, CHUNK, 1)

    lane_lo = c_t < CHUNK
    lane_lo_row = jax.lax.broadcasted_iota(jnp.int32, (1, LANES), 1) < CHUNK
    lane_lo_t = jax.lax.broadcasted_iota(jnp.int32, (ts, LANES), 1) < CHUNK
    gc_full = _lane_expand(gc_cols, 0, rows)
    beta_full = _lane_expand(beta_all, GDN_HEADS, rows)
    egc_full = jnp.exp(gc_full)
    kb_all = k_all * beta_full
    vb_all = v_all * beta_full
    kbg_all = kb_all * egc_full
    qd_all = q_all * egc_full

    ri = jax.lax.broadcasted_iota(jnp.int32, (rows, CAT_W), 0)
    ci = jax.lax.broadcasted_iota(jnp.int32, (rows, CAT_W), 1)
    cj = ci & (CHUNK - 1)
    same_seq = (ri ^ cj) < ts
    eye = (cj == ri).astype(F32)
    bd = functools.partial(_bd_cat, lane_lo_chunk=lane_lo)

    gcc = jnp.concatenate(
        [jnp.where(lane_lo, gc_full[:, (2 * p) * LANES:(2 * p + 1) * LANES],
                   gc_full[:, (2 * p + 1) * LANES:(2 * p + 2) * LANES]) for p in range(GDN_HEADS // 2)], axis=1)
    gcr = jnp.concatenate(
        [jnp.where(lane_lo_row, gc_rows[2 * p:2 * p + 1, :], gc_rows_sw[2 * p + 1:2 * p + 2, :])
         for p in range(GDN_HEADS // 2)], axis=1)
    decay = jnp.exp(jnp.where((cj <= ri) & same_seq, gcc - gcr, -jnp.inf))
    kq = jax.lax.dot_general(
        jnp.concatenate([kb_all, q_all], axis=0).astype(BF16), _bd_wide(k_all),
        (((1,), (1,)), ((), ())), preferred_element_type=F32)
    a_mat = jnp.where(cj < ri, kq[0:rows] * decay, 0.0)
    qk = kq[rows:] * decay
    xinv = eye - a_mat
    pw = _mm(a_mat, bd(a_mat))
    pw_bd = bd(pw)
    xinv = xinv + _mm(xinv, pw_bd)
    pw = _mm(pw, pw_bd)
    xinv = xinv + _mm(xinv, bd(pw))
    u_all, w_all = _split_uw(_mm(bd(xinv), _uw_rhs(vb_all, kbg_all)))
    glast = [gc_full[(b + 1) * ts - 1:(b + 1) * ts, :] for b in seqs]
    glast_rows = jnp.concatenate([jnp.broadcast_to(glast[b], (ts, GDN_QK_W)) for b in seqs], axis=0)
    kdec_t = _stack_heads_t(k_all * jnp.exp(glast_rows - gc_full))

    s_old, ws, qs = [], [], []
    for b in seqs:
        s_b = jnp.concatenate([s0_ref[b, hh] for hh in range(GDN_HEADS)], axis=1)
        s_old.append(s_b)
        ws_b, qs_b = [], []
        for p in range(GDN_HEADS // 2):
            cols = slice(2 * p * LANES, (2 * p + 2) * LANES)
            r = _mm(jnp.concatenate([grp(w_all, b)[:, cols], grp(qd_all, b)[:, cols]], axis=0),
                    _bd_pair(s_b[:, (2 * p) * LANES:(2 * p + 1) * LANES],
                             s_b[:, (2 * p + 1) * LANES:(2 * p + 2) * LANES]))
            ws_b.append(r[0:ts])
            qs_b.append(r[ts:])
        ws.append(jnp.concatenate(ws_b, axis=1))
        qs.append(jnp.concatenate(qs_b, axis=1))
    v_new = u_all - jnp.concatenate(ws, axis=0)
    v_bd = _bd_wide(v_new)
    o_gdn = jnp.concatenate(qs, axis=0) + _mm(qk, v_bd)
    cb = jax.lax.broadcasted_iota(jnp.int32, (GDN_DK, CAT_W), 1) & (CHUNK - 1)
    upd = _mm(jnp.concatenate([jnp.where((cb ^ (b * ts)) < ts, kdec_t, 0.0) for b in seqs], axis=0), v_bd)
    for b in seqs:
        s_new = s_old[b] * jnp.exp(glast[b]) + upd[b * GDN_DK:(b + 1) * GDN_DK]
        for hh in range(GDN_HEADS):
            sout_ref[b, hh] = s_new[:, hh * GDN_DV:(hh + 1) * GDN_DV]
    og = []
    for hh in range(GDN_HEADS):
        oh = _rms(o_gdn[:, hh * GDN_DV:(hh + 1) * GDN_DV], gnorm_ref[...])
        og.append(oh * _silu(proj_cols(P_Z + hh * GDN_DV, GDN_DV)))

    tile8 = lambda x: jnp.concatenate([x] * nb, axis=0)
    rc, rs1, rs2 = tile8(rc_ref[...]), tile8(rs1_ref[...]), tile8(rs2_ref[...])
    k_own = _rope(proj_cols(P_KS, SWA_KV_W), rc, rs1, rs2)
    v_own = proj_cols(P_VS, SWA_KV_W)
    q_rows = [None] * SWA_HEADS
    for p in range(SWA_HEADS // 2):
        qp = _rope(proj_cols(P_QS + p * LANES, LANES), rc, rs1, rs2)
        q_rows[p] = jnp.where(lane_lo, qp, 0.0)
        q_rows[SWA_HEADS // 2 + p] = jnp.where(lane_lo, 0.0, qp)
    scale = SWA_HD ** -0.5
    s_own_all = _mm_nt(jnp.concatenate(q_rows, axis=0), _pad_rows(k_own)) * scale
    tok = r_t & (ts - 1)
    prev_mask = c_t > tok
    sink_col = jnp.zeros((rows, 1), F32)
    for hh in range(SWA_HEADS):
        sink_col = jnp.where((r_t[:, 0:1] ^ (hh * ts)) < ts, sinks_ref[hh], sink_col)
    pp_all, po_all, den_all = [], [], []
    for b in seqs:
        q_b = jnp.concatenate([grp(q_rows[hh], b) for hh in range(SWA_HEADS)], axis=0)
        sp = jnp.where(prev_mask, _mm_nt(q_b, k0_ref[b]) * scale, -jnp.inf)
        so_raw = jnp.concatenate([s_own_all[hh * rows + b * ts:hh * rows + (b + 1) * ts]
                                  for hh in range(SWA_HEADS)], axis=0)
        so = jnp.where(((c_t ^ (b * ts)) < ts) & ((c_t & (ts - 1)) <= tok), so_raw, -jnp.inf)
        m = jnp.maximum(jnp.maximum(jnp.max(sp, axis=-1, keepdims=True),
                                    jnp.max(so, axis=-1, keepdims=True)), sink_col)
        pp = jnp.exp(sp - m)
        po = jnp.exp(so - m)
        den_all.append(jnp.sum(pp, axis=-1, keepdims=True) + jnp.sum(po, axis=-1, keepdims=True)
                       + jnp.exp(sink_col - m))
        pp_all.append(_mm(pp, v0_ref[b]))
        po_all.append(po)
    pv_own = _mm(jnp.concatenate(po_all, axis=0), _pad_rows(v_own))
    os_rows = []
    for b in seqs:
        pv_b = (pp_all[b] + pv_own[b * rows:(b + 1) * rows]) / den_all[b]
        os_rows.append(jnp.concatenate(
            [jnp.where(lane_lo_t, pv_b[p * ts:(p + 1) * ts],
                       pv_b[(SWA_HEADS // 2 + p) * ts:(SWA_HEADS // 2 + p + 1) * ts])
             for p in range(SWA_HEADS // 2)], axis=1))
        kout_ref[b] = jnp.concatenate([k0_ref[b, ts:, :], grp(k_own, b)], axis=0)
        vout_ref[b] = jnp.concatenate([v0_ref[b, ts:, :], grp(v_own, b)], axis=0)
    o_full = jnp.concatenate(og + [jnp.concatenate(os_rows, axis=0)], axis=-1)
    for b in seqs:
        o_ref[b] = grp(o_full, b).astype(o_ref.dtype)


def _mixer_sample(proj3d, s0, conv0, k0, v0, convw, alog_row, dt_row, gnorm, sinks, rc, rs1, rs2):
    bsz, seq, _ = proj3d.shape
    nb = SAMPLE_NB
    assert seq == SAMPLE_T and bsz % nb == 0
    seq_blk = lambda shape: pl.BlockSpec((nb,) + shape, lambda i: (i,) + (0,) * len(shape))
    const_blk = lambda shape: pl.BlockSpec(shape, lambda i: (0,) * len(shape))
    return pl.pallas_call(
        _mixer_sample_kernel,
        grid=(bsz // nb,),
        in_specs=[
            pl.BlockSpec(memory_space=pltpu.SMEM),
            seq_blk((seq, P_W)),
            seq_blk((GDN_HEADS, GDN_DK, GDN_DV)),
            seq_blk((GDN_CONV - 1, GDN_CONV_W)),
            seq_blk((WINDOW, SWA_KV_W)),
            seq_blk((WINDOW, SWA_KV_W)),
            const_blk((GDN_CONV, GDN_CONV_W)),
            const_blk((1, LANES)),
            const_blk((1, LANES)),
            const_blk((1, GDN_DV)),
            const_blk((seq, LANES)),
            const_blk((seq, LANES)),
            const_blk((seq, LANES)),
        ],
        out_specs=[
            seq_blk((seq, D_MODEL)),
            seq_blk((GDN_HEADS, GDN_DK, GDN_DV)),
            seq_blk((GDN_CONV - 1, GDN_CONV_W)),
            seq_blk((WINDOW, SWA_KV_W)),
            seq_blk((WINDOW, SWA_KV_W)),
        ],
        out_shape=[
            jax.ShapeDtypeStruct((bsz, seq, D_MODEL), BF16),
            jax.ShapeDtypeStruct((bsz, GDN_HEADS, GDN_DK, GDN_DV), F32),
            jax.ShapeDtypeStruct((bsz, GDN_CONV - 1, GDN_CONV_W), F32),
            jax.ShapeDtypeStruct((bsz, WINDOW, SWA_KV_W), F32),
            jax.ShapeDtypeStruct((bsz, WINDOW, SWA_KV_W), F32),
        ],
        scratch_shapes=[pltpu.VMEM((2 * SAMPLE_T * nb, GDN_CONV_W), F32)],
        compiler_params=pltpu.CompilerParams(
            dimension_semantics=("arbitrary",), vmem_limit_bytes=VMEM_LIMIT_BYTES),
        name="mixer_sample",
    )(sinks, proj3d, s0, conv0, k0, v0, convw, alog_row, dt_row, gnorm, rc, rs1, rs2)


def _rope_tables(pos):
    half = ROT_DIM // 2
    inv = ROPE_THETA ** (-jnp.arange(half, dtype=F32) * 2.0 / ROT_DIM)
    ang = pos.astype(F32)[:, None] * inv[None, :]
    cos, sin = jnp.cos(ang), jnp.sin(ang)
    n = pos.shape[0]
    rest = SWA_HD - ROT_DIM
    c = jnp.concatenate([cos, cos, jnp.ones((n, rest), F32)], axis=1)
    s1 = jnp.concatenate([-sin, jnp.zeros((n, half + rest), F32)], axis=1)
    s2 = jnp.concatenate([jnp.zeros((n, half), F32), sin, jnp.zeros((n, rest), F32)], axis=1)
    reps = LANES // SWA_HD
    return jnp.tile(c, (1, reps)), jnp.tile(s1, (1, reps)), jnp.tile(s2, (1, reps))


def _pair_order():
    order = []
    for p in range(SWA_HEADS // 2):
        order += [p, SWA_HEADS // 2 + p]
    return np.asarray(order)


def _layer_weights(norm_ffn1, ffn1_gu, ffn1_down, norm_mix, w_in, conv_w, a_log, dt_bias, gdn_norm, sinks,
                   w_out, norm_ffn2, ffn2_gu, ffn2_down, norm_ple, ple_proj, ple_gate, norm_final):
    order = _pair_order()
    head_cols = (order[:, None] * SWA_HD + np.arange(SWA_HD)[None, :]).reshape(-1)
    o_qkv, o_z = 0, GDN_CONV_W
    o_a = o_z + GDN_V_W
    o_b = o_a + GDN_HEADS
    o_qs = o_b + GDN_HEADS
    o_ks = o_qs + SWA_Q_W
    o_vs = o_ks + SWA_KV_W
    win_p = jnp.concatenate([
        w_in[:, o_qkv:o_z], w_in[:, o_z:o_a], w_in[:, o_qs:o_ks][:, head_cols], w_in[:, o_ks:o_vs],
        w_in[:, o_vs:o_vs + SWA_KV_W], w_in[:, o_a:o_qs],
        jnp.zeros((D_MODEL, LANES - 2 * GDN_HEADS), w_in.dtype)], axis=1).astype(BF16)
    wout_p = jnp.concatenate([w_out[:GDN_V_W], w_out[GDN_V_W:][head_cols]], axis=0).astype(BF16)
    pad_row = lambda v: jnp.concatenate([v.astype(F32), jnp.zeros((LANES - v.shape[0],), F32)])[None, :]
    return dict(
        g1=norm_ffn1[None, :], wgu1=ffn1_gu.astype(BF16), wd1=ffn1_down.astype(BF16),
        gm=norm_mix[None, :], win=win_p, convw=conv_w, alog=pad_row(a_log), dt=pad_row(dt_bias),
        gnorm=gdn_norm[None, :], sinks=sinks.astype(F32), wout=wout_p,
        g2=norm_ffn2[None, :], wgu2=ffn2_gu.astype(BF16), wd2=ffn2_down.astype(BF16),
        gp=norm_ple[None, :], wpp=ple_proj.astype(BF16), wpg=ple_gate.astype(BF16), gf=norm_final[None, :])


def _group(x, p, w, state, pos0):
    bsz, seq, _ = x.shape
    h2d, proj2d = _ffn_in(x.reshape(bsz * seq, D_MODEL), w["g1"], w["wgu1"], w["wd1"], w["gm"], w["win"])
    rc, rs1, rs2 = _rope_tables(pos0 + jnp.arange(seq))
    proj3d = proj2d.reshape(bsz, seq, P_W)
    if state is None:
        o, s_new, conv_new, k_new, v_new = _mixer_prompt2(
            proj3d, w["convw"], w["alog"], w["dt"], w["gnorm"], w["sinks"], rc, rs1, rs2)
    else:
        s0, conv0, k0, v0 = state
        o, s_new, conv_new, k_new, v_new = _mixer_sample(
            proj3d, s0, conv0, k0, v0, w["convw"], w["alog"], w["dt"], w["gnorm"],
            w["sinks"], rc, rs1, rs2)
    y = _ffn_out(o.reshape(bsz * seq, D_MODEL), h2d, p.reshape(bsz * seq, PLE_DIM), w["wout"], w["g2"],
                 w["wgu2"], w["wd2"], w["gp"], w["wpg"], w["wpp"], w["gf"])
    kv_shape = (1, bsz, WINDOW, SWA_KV_HEADS, SWA_HD)
    return (y.reshape(bsz, seq, D_MODEL), s_new[None], conv_new[None], k_new.reshape(kv_shape),
            v_new.reshape(kv_shape))


def kernel(x_prompt, x_sample, state_gdn, state_conv, cache_swa_k, cache_swa_v, p_prompt, p_sample, norm_ffn1, ffn1_gu, ffn1_down, norm_mix, w_in, conv_w, a_log, dt_bias, gdn_norm, sinks, w_out, norm_ffn2, ffn2_gu, ffn2_down, norm_ple, ple_proj, ple_gate, norm_final):
    assert state_gdn.shape[0] == 1, "one layer"
    w = _layer_weights(norm_ffn1[0], ffn1_gu[0], ffn1_down[0], norm_mix[0], w_in[0], conv_w[0], a_log[0],
                       dt_bias[0], gdn_norm[0], sinks[0], w_out[0], norm_ffn2[0], ffn2_gu[0], ffn2_down[0],
                       norm_ple[0], ple_proj[0], ple_gate[0], norm_final)
    bs = x_sample.shape[0]
    yp, sg_p, sc_p, kk_p, vv_p = _group(x_prompt, p_prompt[0], w, None, 0)
    ys, sg_s, sc_s, kk_s, vv_s = _group(
        x_sample, p_sample[0], w,
        (state_gdn[0], state_conv[0], cache_swa_k[0].reshape(bs, WINDOW, SWA_KV_W),
         cache_swa_v[0].reshape(bs, WINDOW, SWA_KV_W)), PAST_LEN)
    return (yp, ys, sg_p, sc_p, kk_p, vv_p, sg_s, sc_s, kk_s, vv_s)
```

```python
import functools

import jax
import jax.numpy as jnp
import numpy as np
from jax.experimental import pallas as pl
from jax.experimental.pallas import tpu as pltpu

F32 = jnp.float32
BF16 = jnp.bfloat16

D_MODEL = 1024
D_FF = 2816
PLE_DIM = 256
NORM_EPS = 1e-6
GDN_HEADS = 4
GDN_DK = 128
GDN_DV = 128
GDN_CONV = 4
GDN_QK_W = GDN_HEADS * GDN_DK
GDN_V_W = GDN_HEADS * GDN_DV
GDN_CONV_W = 2 * GDN_QK_W + GDN_V_W
SWA_HEADS = 8
SWA_KV_HEADS = 2
SWA_HD = 64
SWA_Q_W = SWA_HEADS * SWA_HD
SWA_KV_W = SWA_KV_HEADS * SWA_HD
WINDOW = 128
ROT_DIM = SWA_HD // 4
ROPE_THETA = 500000.0
PAST_LEN = 16384

LANES = 128
SUBLANES = 8
VMEM_LIMIT_BYTES = 56 * 1024 * 1024

P_QKV = 0
P_Z = P_QKV + GDN_CONV_W
P_QS = P_Z + GDN_V_W
P_KS = P_QS + SWA_Q_W
P_VS = P_KS + SWA_KV_W
P_AB = P_VS + SWA_KV_W
P_W = P_AB + LANES

FF_CHUNK = 256
TOKEN_TILE = 512

CHUNK = 64
CAT_W = GDN_HEADS * CHUNK
SAMPLE_T = SUBLANES
SAMPLE_NB = CHUNK // SAMPLE_T


def _rms(x, g):
    return x * jax.lax.rsqrt(jnp.mean(x * x, axis=-1, keepdims=True) + NORM_EPS) * g


def _sigmoid(x):
    return 1.0 / (1.0 + jnp.exp(-x))


def _silu(x):
    return x * _sigmoid(x)


def _mm(a, b):
    return jnp.dot(a.astype(BF16), b.astype(BF16), preferred_element_type=F32)


def _mm_nt(a, b):
    return jax.lax.dot_general(a.astype(BF16), b.astype(BF16), (((1,), (1,)), ((), ())),
                               preferred_element_type=F32)


def _swiglu_acc(n_bf16, wgu_ref, wd_ref):
    rows = n_bf16.shape[0]
    acc = jnp.zeros((rows, D_MODEL), F32)
    for c in range(D_FF // FF_CHUNK):
        lo = c * FF_CHUNK
        gate = jnp.dot(n_bf16, wgu_ref[:, lo:lo + FF_CHUNK], preferred_element_type=F32)
        up = jnp.dot(n_bf16, wgu_ref[:, D_FF + lo:D_FF + lo + FF_CHUNK], preferred_element_type=F32)
        act = (_silu(gate) * up).astype(BF16)
        acc = acc + jnp.dot(act, wd_ref[lo:lo + FF_CHUNK, :], preferred_element_type=F32)
    return acc


def _ffn_in_kernel(x_ref, g1_ref, wgu_ref, wd_ref, gm_ref, win_ref, h_ref, proj_ref):
    x = x_ref[...]
    n1 = _rms(x, g1_ref[...]).astype(BF16)
    h = x + 0.5 * _swiglu_acc(n1, wgu_ref, wd_ref)
    h_ref[...] = h
    n = _rms(h, gm_ref[...]).astype(BF16)
    proj_ref[...] = jnp.dot(n, win_ref[...], preferred_element_type=F32)


def _const_spec(shape):
    return pl.BlockSpec(shape, lambda i: (0,) * len(shape), pipeline_mode=pl.Buffered(1))


def _ffn_in(x2d, g1, wgu, wd, gm, win):
    n_tok = x2d.shape[0]
    tm = min(TOKEN_TILE, n_tok)
    assert n_tok % tm == 0
    return pl.pallas_call(
        _ffn_in_kernel,
        grid=(n_tok // tm,),
        in_specs=[
            pl.BlockSpec((tm, D_MODEL), lambda i: (i, 0)),
            _const_spec((1, D_MODEL)),
            _const_spec((D_MODEL, 2 * D_FF)),
            _const_spec((D_FF, D_MODEL)),
            _const_spec((1, D_MODEL)),
            _const_spec((D_MODEL, P_W)),
        ],
        out_specs=[
            pl.BlockSpec((tm, D_MODEL), lambda i: (i, 0)),
            pl.BlockSpec((tm, P_W), lambda i: (i, 0)),
        ],
        out_shape=[
            jax.ShapeDtypeStruct((n_tok, D_MODEL), F32),
            jax.ShapeDtypeStruct((n_tok, P_W), F32),
        ],
        compiler_params=pltpu.CompilerParams(
            dimension_semantics=("arbitrary",), vmem_limit_bytes=VMEM_LIMIT_BYTES),
        name="ffn_in",
    )(x2d, g1, wgu, wd, gm, win)


def _ffn_in_prep_kernel(x_ref, g1_ref, wgu_ref, wd_ref, gm_ref, win_ref, convw_ref, alog_ref, dt_ref,
                        rc_ref, rs1_ref, rs2_ref, h_ref, prep_ref, convout_ref, n_st, xext_ref, *, tiles_per_seq):
    i = pl.program_id(0)
    tm = x_ref.shape[0]
    piece = 2 * LANES

    @pl.when(i == 0)
    def _init():
        n_st[...] = jnp.zeros(n_st.shape, n_st.dtype)
        xext_ref[...] = jnp.zeros(xext_ref.shape, F32)

    n_prev = n_st[...]
    first_of_seq = ((i - 1) % tiles_per_seq) == 0
    conv_base = SUBLANES - (GDN_CONV - 1)

    def prep_tile(lo, val):
        cols = slice(lo, lo + LANES)
        if lo < P_Z:
            xext_ref[0:SUBLANES, cols] = jnp.where(first_of_seq, 0.0, xext_ref[tm:tm + SUBLANES, cols])
            xext_ref[SUBLANES:SUBLANES + tm, cols] = val
            convout_ref[0, :, cols] = val[tm - (GDN_CONV - 1):tm]
            conv = xext_ref[conv_base:conv_base + tm, cols] * convw_ref[0:1, cols]
            for j in range(1, GDN_CONV):
                conv = conv + xext_ref[conv_base + j:conv_base + j + tm, cols] * convw_ref[j:j + 1, cols]
            act = _silu(conv)
            if lo < 2 * GDN_QK_W:
                scale = GDN_DK ** -0.5 if lo < GDN_QK_W else 1.0
                act = act * (jax.lax.rsqrt(jnp.sum(act * act, axis=-1, keepdims=True) + 1e-6) * scale)
            prep_ref[:, cols] = act
        elif lo < P_QS:
            prep_ref[:, cols] = _silu(val)
        elif lo < P_VS:
            prep_ref[:, cols] = _rope(val, rc_ref[...], rs1_ref[...], rs2_ref[...])
        elif lo < P_AB:
            prep_ref[:, cols] = val
        else:
            g_all, beta_all = _gates(val, alog_ref[...], dt_ref[...])
            lane = jax.lax.broadcasted_iota(jnp.int32, (tm, LANES), 1)
            prep_ref[:, cols] = jnp.where(lane < GDN_HEADS, g_all, beta_all)

    def proj_piece(lo):
        width = min(piece, P_W - lo)
        proj = jnp.dot(n_prev, win_ref[:, lo:lo + width], preferred_element_type=F32)
        for off in range(0, width, LANES):
            prep_tile(lo + off, proj[:, off:off + LANES])

    x = x_ref[...]
    n1 = _rms(x, g1_ref[...]).astype(BF16)
    acc = jnp.zeros((tm, D_MODEL), F32)
    for c in range(D_FF // FF_CHUNK):
        lo = c * FF_CHUNK
        gate = jnp.dot(n1, wgu_ref[:, lo:lo + FF_CHUNK], preferred_element_type=F32)
        up = jnp.dot(n1, wgu_ref[:, D_FF + lo:D_FF + lo + FF_CHUNK], preferred_element_type=F32)
        act = (_silu(gate) * up).astype(BF16)
        acc = acc + jnp.dot(act, wd_ref[lo:lo + FF_CHUNK, :], preferred_element_type=F32)
        proj_piece(c * piece)
    for lo in range((D_FF // FF_CHUNK) * piece, P_W, piece):
        proj_piece(lo)
    h = x + 0.5 * acc
    h_ref[...] = h
    n_st[...] = _rms(h, gm_ref[...]).astype(BF16)


def _ffn_in_prep(x2d, seq_len, g1, wgu, wd, gm, win, convw, alog_row, dt_row, rc, rs1, rs2):
    n_tok = x2d.shape[0]
    tm = TOKEN_TILE
    assert seq_len % tm == 0 and n_tok % seq_len == 0
    tiles_per_seq = seq_len // tm
    n_tiles = n_tok // tm
    cur = lambda i: jnp.minimum(i, n_tiles - 1)
    prev = lambda i: jnp.maximum(i - 1, 0)
    rope_spec = pl.BlockSpec((tm, LANES), lambda i: (prev(i) % tiles_per_seq, 0))
    return pl.pallas_call(
        functools.partial(_ffn_in_prep_kernel, tiles_per_seq=tiles_per_seq),
        grid=(n_tiles + 1,),
        in_specs=[
            pl.BlockSpec((tm, D_MODEL), lambda i: (cur(i), 0)),
            _const_spec((1, D_MODEL)),
            _const_spec((D_MODEL, 2 * D_FF)),
            _const_spec((D_FF, D_MODEL)),
            _const_spec((1, D_MODEL)),
            _const_spec((D_MODEL, P_W)),
            _const_spec((GDN_CONV, GDN_CONV_W)),
            _const_spec((1, LANES)),
            _const_spec((1, LANES)),
            rope_spec, rope_spec, rope_spec,
        ],
        out_specs=[
            pl.BlockSpec((tm, D_MODEL), lambda i: (cur(i), 0)),
            pl.BlockSpec((tm, P_W), lambda i: (prev(i), 0)),
            pl.BlockSpec((1, GDN_CONV - 1, GDN_CONV_W), lambda i: (prev(i) // tiles_per_seq, 0, 0)),
        ],
        out_shape=[
            jax.ShapeDtypeStruct((n_tok, D_MODEL), F32),
            jax.ShapeDtypeStruct((n_tok, P_W), F32),
            jax.ShapeDtypeStruct((n_tok // seq_len, GDN_CONV - 1, GDN_CONV_W), F32),
        ],
        scratch_shapes=[pltpu.VMEM((tm, D_MODEL), BF16), pltpu.VMEM((SUBLANES + tm, GDN_CONV_W), F32)],
        compiler_params=pltpu.CompilerParams(
            dimension_semantics=("arbitrary",), vmem_limit_bytes=VMEM_LIMIT_BYTES),
        name="ffn_in_prep",
    )(x2d, g1, wgu, wd, gm, win, convw, alog_row, dt_row, rc, rs1, rs2)


def _ffn_out_kernel(o_ref, h_ref, p_ref, wout_ref, g2_ref, wgu_ref, wd_ref, gp_ref, wpg_ref, wpp_ref,
                    gf_ref, y_ref):
    h = h_ref[...] + jnp.dot(o_ref[...], wout_ref[...], preferred_element_type=F32)
    n2 = _rms(h, g2_ref[...]).astype(BF16)
    h = h + 0.5 * _swiglu_acc(n2, wgu_ref, wd_ref)
    npl = _rms(h, gp_ref[...]).astype(BF16)
    gate = _sigmoid(jnp.dot(npl, wpg_ref[...], preferred_element_type=F32))
    pe = jnp.dot(p_ref[...].astype(BF16), wpp_ref[...], preferred_element_type=F32)
    h = h + gate * pe
    y_ref[...] = _rms(h, gf_ref[...])


def _ffn_out(o2d, h2d, p2d, wout, g2, wgu, wd, gp, wpg, wpp, gf):
    n_tok = h2d.shape[0]
    tm = min(TOKEN_TILE, n_tok)
    assert n_tok % tm == 0
    return pl.pallas_call(
        _ffn_out_kernel,
        grid=(n_tok // tm,),
        in_specs=[
            pl.BlockSpec((tm, D_MODEL), lambda i: (i, 0)),
            pl.BlockSpec((tm, D_MODEL), lambda i: (i, 0)),
            pl.BlockSpec((tm, PLE_DIM), lambda i: (i, 0)),
            _const_spec((D_MODEL, D_MODEL)),
            _const_spec((1, D_MODEL)),
            _const_spec((D_MODEL, 2 * D_FF)),
            _const_spec((D_FF, D_MODEL)),
            _const_spec((1, D_MODEL)),
            _const_spec((D_MODEL, D_MODEL)),
            _const_spec((PLE_DIM, D_MODEL)),
            _const_spec((1, D_MODEL)),
        ],
        out_specs=pl.BlockSpec((tm, D_MODEL), lambda i: (i, 0)),
        out_shape=jax.ShapeDtypeStruct((n_tok, D_MODEL), F32),
        compiler_params=pltpu.CompilerParams(
            dimension_semantics=("arbitrary",), vmem_limit_bytes=VMEM_LIMIT_BYTES),
        name="ffn_out",
    )(o2d, h2d, p2d, wout, g2, wgu, wd, gp, wpg, wpp, gf)


def _pad_rows(x):
    rows = x.shape[0]
    if rows == LANES:
        return x
    return jnp.concatenate([x, jnp.zeros((LANES - rows, x.shape[1]), x.dtype)], axis=0)


def _split3(x):
    hi = x.astype(BF16)
    r = x - hi.astype(F32)
    mid = r.astype(BF16)
    lo = (r - mid.astype(F32)).astype(BF16)
    return hi, mid, lo


def _mm_exact_rhs(a_bf16, x):
    hi, mid, lo = _split3(x)
    return (jnp.dot(a_bf16, hi, preferred_element_type=F32) + jnp.dot(a_bf16, mid, preferred_element_type=F32)
            + jnp.dot(a_bf16, lo, preferred_element_type=F32))


def _mm_exact_lhs(x, a_bf16):
    hi, mid, lo = _split3(x)
    return (jnp.dot(hi, a_bf16, preferred_element_type=F32) + jnp.dot(mid, a_bf16, preferred_element_type=F32)
            + jnp.dot(lo, a_bf16, preferred_element_type=F32))


def _bd_cat(y, lane_lo_chunk):
    zero_tile = jnp.zeros((CHUNK, LANES), F32)
    blocks = []
    for hh in range(GDN_HEADS):
        y_tile = y[:, (hh // 2) * LANES:(hh // 2 + 1) * LANES]
        keep = jnp.where(lane_lo_chunk, y_tile, 0.0) if hh % 2 == 0 else jnp.where(lane_lo_chunk, 0.0, y_tile)
        blocks.append(jnp.concatenate([keep, zero_tile] if hh < 2 else [zero_tile, keep], axis=1))
    return jnp.concatenate(blocks, axis=0).astype(BF16)


def _bd_wide(y):
    zero_tile = jnp.zeros((CHUNK, LANES), F32)
    blocks = []
    for hh in range(GDN_HEADS):
        tiles = [zero_tile] * GDN_HEADS
        tiles[hh] = y[:, hh * LANES:(hh + 1) * LANES]
        blocks.append(jnp.concatenate(tiles, axis=1))
    return jnp.concatenate(blocks, axis=0).astype(BF16)


def _bd_pair(sa, sb):
    zeros = jnp.zeros(sa.shape, sa.dtype)
    return jnp.concatenate([jnp.concatenate([sa, zeros], axis=1), jnp.concatenate([zeros, sb], axis=1)], axis=0)


def _gates(ab, alog_row, dt_row):
    zab = ab + dt_row
    softplus = jnp.maximum(zab, 0.0) + jnp.log(1.0 + jnp.exp(-jnp.abs(zab)))
    return -jnp.exp(alog_row) * softplus, _sigmoid(ab)


def _head_l2(x):
    parts = []
    for hh in range(GDN_HEADS):
        xh = x[:, hh * GDN_DK:(hh + 1) * GDN_DK]
        parts.append(xh * jax.lax.rsqrt(jnp.sum(xh * xh, axis=-1, keepdims=True) + 1e-6))
    return jnp.concatenate(parts, axis=1)


def _lane_expand(cols, first_lane, rows):
    return jnp.concatenate(
        [jnp.broadcast_to(cols[:, first_lane + hh:first_lane + hh + 1], (rows, LANES))
         for hh in range(GDN_HEADS)], axis=1)


def _rope(x, rc, rs1, rs2):
    return x * rc + pltpu.roll(x, LANES - ROT_DIM // 2, 1) * rs1 + pltpu.roll(x, ROT_DIM // 2, 1) * rs2


def _uw_rhs(vb_all, kbg_all):
    return jnp.concatenate(
        [jnp.concatenate([vb_all[:, hh * GDN_DV:(hh + 1) * GDN_DV],
                          kbg_all[:, hh * GDN_DK:(hh + 1) * GDN_DK]], axis=1)
         for hh in range(GDN_HEADS)], axis=0)


def _split_uw(uw):
    u_all = jnp.concatenate([uw[hh * CHUNK:(hh + 1) * CHUNK, 0:GDN_DV] for hh in range(GDN_HEADS)], axis=1)
    w_all = jnp.concatenate([uw[hh * CHUNK:(hh + 1) * CHUNK, GDN_DV:] for hh in range(GDN_HEADS)], axis=1)
    return u_all, w_all


def _stack_heads_t(x):
    return jnp.transpose(jnp.concatenate(
        [x[:, hh * GDN_DK:(hh + 1) * GDN_DK] for hh in range(GDN_HEADS)], axis=0))


PROMPT_NS = 4


def _mixer_prompt_kernel(sinks_ref, cur_ref, prev_ref, gnorm_ref, o_ref, sout_ref, kout_ref, vout_ref,
                         s_ref, kprev_ref, vprev_ref, gcc_st, gcr_st, beta_st, os_st, *, tb, n_blocks):
    t = pl.program_id(1)
    ns = PROMPT_NS
    nch = tb // CHUNK
    stages = (gcc_st, gcr_st, beta_st, os_st)
    stage_names = ("gcc", "gcr", "beta", "os")

    @pl.when(t == 0)
    def _init():
        for ref in (s_ref, kprev_ref, vprev_ref) + stages:
            ref[...] = jnp.zeros(ref.shape, F32)

    r_t = jax.lax.broadcasted_iota(jnp.int32, (tb, LANES), 0)
    c_t = jax.lax.broadcasted_iota(jnp.int32, (tb, LANES), 1)
    lane_lo = c_t < CHUNK
    lane_lo_chunk = jax.lax.broadcasted_iota(jnp.int32, (CHUNK, LANES), 1) < CHUNK
    lane_lo_row = jax.lax.broadcasted_iota(jnp.int32, (1, LANES), 1) < CHUNK
    same_chunk = (r_t ^ c_t) < CHUNK
    rk = jax.lax.broadcasted_iota(jnp.int32, (tb, 2 * WINDOW), 0)
    ck = jax.lax.broadcasted_iota(jnp.int32, (tb, 2 * WINDOW), 1)
    first_off = jnp.where(t > 0, 0, WINDOW)
    dist = ck - rk
    kmask = ((dist > jnp.where(ck < WINDOW, first_off, -WINDOW))
             & (dist <= jnp.where(ck < WINDOW, 2 * WINDOW, WINDOW)))

    s_cat = [s_ref[sq] for sq in range(ns)]
    k_prev = [kprev_ref[sq] for sq in range(ns)]
    v_prev = [vprev_ref[sq] for sq in range(ns)]
    staged = [{name: ref[sq] for ref, name in zip(stages, stage_names)} for sq in range(ns)]
    for sq in range(ns):
        staged[sq].update(q=prev_ref[sq, :, 0:GDN_QK_W], k=prev_ref[sq, :, GDN_QK_W:2 * GDN_QK_W],
                          v=prev_ref[sq, :, 2 * GDN_QK_W:GDN_CONV_W], zs=prev_ref[sq, :, P_Z:P_Z + GDN_V_W])
    fr = [{} for _ in range(ns)]

    def front(sq):
        f = fr[sq]
        ab = cur_ref[sq, :, P_AB:P_AB + LANES]
        f["beta"] = ab
        f["gcc"] = _mm_exact_rhs(((c_t <= r_t) & same_chunk).astype(BF16), ab)
        f["gcr"] = _mm_exact_lhs(jnp.transpose(ab)[0:SUBLANES, :],
                                 ((r_t <= c_t) & same_chunk).astype(BF16))
        k_own = cur_ref[sq, :, P_KS:P_KS + SWA_KV_W]
        v_own = cur_ref[sq, :, P_VS:P_VS + SWA_KV_W]
        f["k_own"], f["v_own"] = k_own, v_own
        k_cat = jnp.concatenate([k_prev[sq], k_own], axis=0)
        v_cat = jnp.concatenate([v_prev[sq], v_own], axis=0).astype(BF16)
        q_rows = [None] * SWA_HEADS
        for p in range(SWA_HEADS // 2):
            qp = cur_ref[sq, :, P_QS + p * LANES:P_QS + (p + 1) * LANES]
            q_rows[p] = jnp.where(lane_lo, qp, 0.0)
            q_rows[SWA_HEADS // 2 + p] = jnp.where(lane_lo, 0.0, qp)
        s_all = _mm_nt(jnp.concatenate(q_rows, axis=0), k_cat) * (SWA_HD ** -0.5)
        yield
        pv = [None] * SWA_HEADS
        for hh in range(SWA_HEADS):
            sh = jnp.where(kmask, s_all[hh * tb:(hh + 1) * tb], -jnp.inf)
            sk = sinks_ref[hh]
            m = jnp.maximum(jnp.max(sh, axis=-1, keepdims=True), sk)
            ph = jnp.exp(sh - m)
            den = jnp.sum(ph, axis=-1, keepdims=True) + jnp.exp(sk - m)
            pv[hh] = jnp.dot(ph.astype(BF16), v_cat, preferred_element_type=F32) / den
            yield
        f["os"] = jnp.concatenate(
            [jnp.where(lane_lo, pv[p], pv[SWA_HEADS // 2 + p]) for p in range(SWA_HEADS // 2)], axis=1)

    fronts = [front(sq) for sq in range(ns)]

    def fill():
        for gen in fronts:
            next(gen, None)

    units = [(sq, c) for sq in range(ns) for c in range(nch)]
    rows_of = lambda c: slice(c * CHUNK, (c + 1) * CHUNK)
    ri = jax.lax.broadcasted_iota(jnp.int32, (CHUNK, CAT_W), 0)
    ci = jax.lax.broadcasted_iota(jnp.int32, (CHUNK, CAT_W), 1)
    cj = ci & (CHUNK - 1)
    tri = cj <= ri
    strict = cj < ri
    eye = (cj == ri).astype(F32)
    blk = ri ^ cj
    bd = functools.partial(_bd_cat, lane_lo_chunk=lane_lo_chunk)

    drv = []
    for sq in range(ns):
        st = staged[sq]
        gc_full = _lane_expand(st["gcc"], 0, tb)
        beta_full = _lane_expand(st["beta"], GDN_HEADS, tb)
        egc_full = jnp.exp(gc_full)
        kb_all = st["k"] * beta_full
        drv.append(dict(gc_full=gc_full, kb=kb_all, vb=st["v"] * beta_full, kbg=kb_all * egc_full,
                        qd=st["q"] * egc_full, gcr=st["gcr"], gcr_sw=pltpu.roll(st["gcr"], CHUNK, 1)))
    fill()
    a_mat, qk, xinv, pw = {}, {}, {}, {}
    for u in units:
        sq, c = u
        d, rows = drv[sq], rows_of(c)
        gcc_pairs, gcr_pairs = [], []
        for p in range(GDN_HEADS // 2):
            gcc_pairs.append(jnp.where(lane_lo_chunk, d["gc_full"][rows, (2 * p) * LANES:(2 * p + 1) * LANES],
                                       d["gc_full"][rows, (2 * p + 1) * LANES:(2 * p + 2) * LANES]))
            ra = (d["gcr"] if c == 0 else d["gcr_sw"])[2 * p:2 * p + 1, :]
            rb_ = (d["gcr_sw"] if c == 0 else d["gcr"])[2 * p + 1:2 * p + 2, :]
            gcr_pairs.append(jnp.where(lane_lo_row, ra, rb_))
        gcc = jnp.concatenate(gcc_pairs, axis=1)
        gcr = jnp.concatenate(gcr_pairs, axis=1)
        decay = jnp.exp(jnp.where(tri, gcc - gcr, -jnp.inf))
        kq = jax.lax.dot_general(
            jnp.concatenate([d["kb"][rows], staged[sq]["q"][rows]], axis=0).astype(BF16),
            _bd_wide(staged[sq]["k"][rows]), (((1,), (1,)), ((), ())), preferred_element_type=F32)
        a_mat[u] = jnp.where(strict, kq[0:CHUNK] * decay, 0.0)
        qk[u] = kq[CHUNK:] * decay
    fill()
    for u in units:
        a0 = jnp.where(blk < SUBLANES, a_mat[u], 0.0)
        xinv[u] = eye - a0
        pw[u] = _mm(a0, bd(a0))
    fill()
    for u in units:
        pw_bd = bd(pw[u])
        xinv[u] = xinv[u] + _mm(xinv[u], pw_bd)
        pw[u] = _mm(pw[u], pw_bd)
    fill()
    for u in units:
        xinv[u] = xinv[u] + _mm(xinv[u], bd(pw[u]))
    fill()
    s = SUBLANES
    while s < CHUNK:
        y = {u: _mm(jnp.where((blk < 2 * s) & (blk >= s), a_mat[u], 0.0), bd(xinv[u])) for u in units}
        fill()
        for u in units:
            xinv[u] = xinv[u] - _mm(xinv[u], bd(y[u]))
        fill()
        s *= 2
    prep = {}
    for u in units:
        sq, c = u
        d, rows = drv[sq], rows_of(c)
        u_all, w_all = _split_uw(_mm(bd(xinv[u]), _uw_rhs(d["vb"][rows], d["kbg"][rows])))
        glast = d["gc_full"][c * CHUNK + CHUNK - 1:c * CHUNK + CHUNK, :]
        kdec_t = _stack_heads_t(staged[sq]["k"][rows] * jnp.exp(glast - d["gc_full"][rows]))
        prep[u] = (u_all, w_all, kdec_t, jnp.exp(glast))
    fill()

    o_chunks = [[] for _ in range(ns)]
    for c in range(nch):
        rows = rows_of(c)
        ws, qs = [[] for _ in range(ns)], [[] for _ in range(ns)]
        for sq in range(ns):
            w_all = prep[(sq, c)][1]
            for p in range(GDN_HEADS // 2):
                cols = slice(2 * p * LANES, (2 * p + 2) * LANES)
                r = _mm(jnp.concatenate([w_all[:, cols], drv[sq]["qd"][rows, cols]], axis=0),
                        _bd_pair(s_cat[sq][:, (2 * p) * LANES:(2 * p + 1) * LANES],
                                 s_cat[sq][:, (2 * p + 1) * LANES:(2 * p + 2) * LANES]))
                ws[sq].append(r[0:CHUNK])
                qs[sq].append(r[CHUNK:])
        fill()
        for sq in range(ns):
            u_all, _, kdec_t, eg_last = prep[(sq, c)]
            v_new = u_all - jnp.concatenate(ws[sq], axis=1)
            r2 = _mm(jnp.concatenate([qk[(sq, c)], kdec_t], axis=0), _bd_wide(v_new))
            o_chunks[sq].append(jnp.concatenate(qs[sq], axis=1) + r2[0:CHUNK])
            s_cat[sq] = s_cat[sq] * eg_last + r2[CHUNK:]
        fill()
    for gen in fronts:
        for _ in gen:
            pass

    for sq in range(ns):
        o_gdn = jnp.concatenate(o_chunks[sq], axis=0)
        zs_prev = staged[sq]["zs"]
        og = [_rms(o_gdn[:, hh * GDN_DV:(hh + 1) * GDN_DV], gnorm_ref[...])
              * zs_prev[:, hh * GDN_DV:(hh + 1) * GDN_DV] for hh in range(GDN_HEADS)]
        o_ref[sq] = jnp.concatenate(og + [staged[sq]["os"]], axis=-1).astype(o_ref.dtype)
        s_ref[sq] = s_cat[sq]
        kprev_ref[sq] = fr[sq]["k_own"]
        vprev_ref[sq] = fr[sq]["v_own"]
        for ref, name in zip(stages, stage_names):
            ref[sq] = fr[sq][name]

    @pl.when(t == n_blocks - 1)
    def _window_out():
        for sq in range(ns):
            kout_ref[sq] = fr[sq]["k_own"]
            vout_ref[sq] = fr[sq]["v_own"]

    @pl.when(t == n_blocks)
    def _state_out():
        for sq in range(ns):
            for hh in range(GDN_HEADS):
                sout_ref[sq, hh] = s_ref[sq, :, hh * GDN_DV:(hh + 1) * GDN_DV]


def _mixer_prompt(prep3d, gnorm, sinks):
    bsz, seq, _ = prep3d.shape
    tb, ns = WINDOW, PROMPT_NS
    assert seq % tb == 0 and bsz % ns == 0
    n_blocks = seq // tb
    kern = functools.partial(_mixer_prompt_kernel, tb=tb, n_blocks=n_blocks)
    seq_blk = lambda shape: pl.BlockSpec((ns,) + shape, lambda b, t: (b,) + (0,) * len(shape))
    vmem = lambda *shape: pltpu.VMEM((ns,) + shape, F32)
    return pl.pallas_call(
        kern,
        grid=(bsz // ns, n_blocks + 1),
        in_specs=[
            pl.BlockSpec(memory_space=pltpu.SMEM),
            pl.BlockSpec((ns, tb, P_W), lambda b, t: (b, jnp.minimum(t, n_blocks - 1), 0)),
            pl.BlockSpec((ns, tb, P_W), lambda b, t: (b, jnp.maximum(t - 1, 0), 0)),
            pl.BlockSpec((1, GDN_DV), lambda b, t: (0, 0)),
        ],
        out_specs=[
            pl.BlockSpec((ns, tb, D_MODEL), lambda b, t: (b, jnp.maximum(t - 1, 0), 0)),
            seq_blk((GDN_HEADS, GDN_DK, GDN_DV)),
            seq_blk((WINDOW, SWA_KV_W)),
            seq_blk((WINDOW, SWA_KV_W)),
        ],
        out_shape=[
            jax.ShapeDtypeStruct((bsz, seq, D_MODEL), BF16),
            jax.ShapeDtypeStruct((bsz, GDN_HEADS, GDN_DK, GDN_DV), F32),
            jax.ShapeDtypeStruct((bsz, WINDOW, SWA_KV_W), F32),
            jax.ShapeDtypeStruct((bsz, WINDOW, SWA_KV_W), F32),
        ],
        scratch_shapes=[
            vmem(GDN_DK, GDN_HEADS * GDN_DV),
            vmem(WINDOW, SWA_KV_W),
            vmem(WINDOW, SWA_KV_W),
            vmem(tb, LANES),
            vmem(SUBLANES, LANES),
            vmem(tb, LANES),
            vmem(tb, SWA_Q_W),
        ],
        compiler_params=pltpu.CompilerParams(
            dimension_semantics=("arbitrary", "arbitrary"), vmem_limit_bytes=VMEM_LIMIT_BYTES),
        name="mixer_prompt",
    )(sinks, prep3d, prep3d, gnorm)


def _mixer_sample_kernel(sinks_ref, proj_ref, s0_ref, conv0_ref, k0_ref, v0_ref, convw_ref, alog_ref, dt_ref,
                         gnorm_ref, rc_ref, rs1_ref, rs2_ref,
                         o_ref, sout_ref, convout_ref, kout_ref, vout_ref, xext_ref):
    nb, ts, rows = SAMPLE_NB, SAMPLE_T, CHUNK
    seqs = range(nb)
    grp = lambda x, b: x[b * ts:(b + 1) * ts]

    hist = SUBLANES - (GDN_CONV - 1)
    conv_parts = []
    for b in seqs:
        xext_ref[2 * ts * b + hist:2 * ts * b + ts, :] = conv0_ref[b]
        xext_ref[2 * ts * b + ts:2 * ts * (b + 1), :] = proj_ref[b, :, P_QKV:P_QKV + GDN_CONV_W]
    for b in seqs:
        base = 2 * ts * b + hist
        conv = xext_ref[base:base + ts, :] * convw_ref[0:1, :]
        for j in range(1, GDN_CONV):
            conv = conv + xext_ref[base + j:base + j + ts, :] * convw_ref[j:j + 1, :]
        conv_parts.append(conv)
        convout_ref[b] = xext_ref[2 * ts * (b + 1) - (GDN_CONV - 1):2 * ts * (b + 1), :]
    qkv = _silu(jnp.concatenate(conv_parts, axis=0))
    q_all = _head_l2(qkv[:, 0:GDN_QK_W]) * (GDN_DK ** -0.5)
    k_all = _head_l2(qkv[:, GDN_QK_W:2 * GDN_QK_W])
    v_all = qkv[:, 2 * GDN_QK_W:]

    def proj_cols(lo, width):
        return jnp.concatenate([proj_ref[b, :, lo:lo + width] for b in seqs], axis=0)

    g_all, beta_all = _gates(proj_cols(P_AB, LANES), alog_ref[...], dt_ref[...])

    r_t = jax.lax.broadcasted_iota(jnp.int32, (rows, LANES), 0)
    c_t = jax.lax.broadcasted_iota(jnp.int32, (rows, LANES), 1)
    r_p = jax.lax.broadcasted_iota(jnp.int32, (LANES, LANES), 0)
    c_p = jax.lax.broadcasted_iota(jnp.int32, (LANES, LANES), 1)
    g_pad = _pad_rows(g_all)
    gc_cols = _mm_exact_rhs(((c_t <= r_t) & ((r_t ^ c_t) < ts)).astype(BF16), g_pad)
    gc_rows = _mm_exact_lhs(jnp.transpose(g_pad)[0:SUBLANES, :],
                            ((r_p <= c_p) & ((r_p ^ c_p) < ts)).astype(BF16))
    gc_rows_sw = pltpu.roll(gc_rows, CHUNK, 1)

    lane_lo = c_t < CHUNK
    lane_lo_row = jax.lax.broadcasted_iota(jnp.int32, (1, LANES), 1) < CHUNK
    lane_lo_t = jax.lax.broadcasted_iota(jnp.int32, (ts, LANES), 1) < CHUNK
    gc_full = _lane_expand(gc_cols, 0, rows)
    beta_full = _lane_expand(beta_all, GDN_HEADS, rows)
    egc_full = jnp.exp(gc_full)
    kb_all = k_all * beta_full
    vb_all = v_all * beta_full
    kbg_all = kb_all * egc_full
    qd_all = q_all * egc_full

    ri = jax.lax.broadcasted_iota(jnp.int32, (rows, CAT_W), 0)
    ci = jax.lax.broadcasted_iota(jnp.int32, (rows, CAT_W), 1)
    cj = ci & (CHUNK - 1)
    same_seq = (ri ^ cj) < ts
    eye = (cj == ri).astype(F32)
    bd = functools.partial(_bd_cat, lane_lo_chunk=lane_lo)

    gcc = jnp.concatenate(
        [jnp.where(lane_lo, gc_full[:, (2 * p) * LANES:(2 * p + 1) * LANES],
                   gc_full[:, (2 * p + 1) * LANES:(2 * p + 2) * LANES]) for p in range(GDN_HEADS // 2)], axis=1)
    gcr = jnp.concatenate(
        [jnp.where(lane_lo_row, gc_rows[2 * p:2 * p + 1, :], gc_rows_sw[2 * p + 1:2 * p + 2, :])
         for p in range(GDN_HEADS // 2)], axis=1)
    decay = jnp.exp(jnp.where((cj <= ri) & same_seq, gcc - gcr, -jnp.inf))
    kq = jax.lax.dot_general(
        jnp.concatenate([kb_all, q_all], axis=0).astype(BF16), _bd_wide(k_all),
        (((1,), (1,)), ((), ())), preferred_element_type=F32)
    a_mat = jnp.where(cj < ri, kq[0:rows] * decay, 0.0)
    qk = kq[rows:] * decay
    xinv = eye - a_mat
    pw = _mm(a_mat, bd(a_mat))
    pw_bd = bd(pw)
    xinv = xinv + _mm(xinv, pw_bd)
    pw = _mm(pw, pw_bd)
    xinv = xinv + _mm(xinv, bd(pw))
    u_all, w_all = _split_uw(_mm(bd(xinv), _uw_rhs(vb_all, kbg_all)))
    glast = [gc_full[(b + 1) * ts - 1:(b + 1) * ts, :] for b in seqs]
    glast_rows = jnp.concatenate([jnp.broadcast_to(glast[b], (ts, GDN_QK_W)) for b in seqs], axis=0)
    kdec_t = _stack_heads_t(k_all * jnp.exp(glast_rows - gc_full))

    s_old, ws, qs = [], [], []
    for b in seqs:
        s_b = jnp.concatenate([s0_ref[b, hh] for hh in range(GDN_HEADS)], axis=1)
        s_old.append(s_b)
        ws_b, qs_b = [], []
        for p in range(GDN_HEADS // 2):
            cols = slice(2 * p * LANES, (2 * p + 2) * LANES)
            r = _mm(jnp.concatenate([grp(w_all, b)[:, cols], grp(qd_all, b)[:, cols]], axis=0),
                    _bd_pair(s_b[:, (2 * p) * LANES:(2 * p + 1) * LANES],
                             s_b[:, (2 * p + 1) * LANES:(2 * p + 2) * LANES]))
            ws_b.append(r[0:ts])
            qs_b.append(r[ts:])
        ws.append(jnp.concatenate(ws_b, axis=1))
        qs.append(jnp.concatenate(qs_b, axis=1))
    v_new = u_all - jnp.concatenate(ws, axis=0)
    v_bd = _bd_wide(v_new)
    o_gdn = jnp.concatenate(qs, axis=0) + _mm(qk, v_bd)
    cb = jax.lax.broadcasted_iota(jnp.int32, (GDN_DK, CAT_W), 1) & (CHUNK - 1)
    upd = _mm(jnp.concatenate([jnp.where((cb ^ (b * ts)) < ts, kdec_t, 0.0) for b in seqs], axis=0), v_bd)
    for b in seqs:
        s_new = s_old[b] * jnp.exp(glast[b]) + upd[b * GDN_DK:(b + 1) * GDN_DK]
        for hh in range(GDN_HEADS):
            sout_ref[b, hh] = s_new[:, hh * GDN_DV:(hh + 1) * GDN_DV]
    og = []
    for hh in range(GDN_HEADS):
        oh = _rms(o_gdn[:, hh * GDN_DV:(hh + 1) * GDN_DV], gnorm_ref[...])
        og.append(oh * _silu(proj_cols(P_Z + hh * GDN_DV, GDN_DV)))

    tile8 = lambda x: jnp.concatenate([x] * nb, axis=0)
    rc, rs1, rs2 = tile8(rc_ref[...]), tile8(rs1_ref[...]), tile8(rs2_ref[...])
    k_own = _rope(proj_cols(P_KS, SWA_KV_W), rc, rs1, rs2)
    v_own = proj_cols(P_VS, SWA_KV_W)
    q_rows = [None] * SWA_HEADS
    for p in range(SWA_HEADS // 2):
        qp = _rope(proj_cols(P_QS + p * LANES, LANES), rc, rs1, rs2)
        q_rows[p] = jnp.where(lane_lo, qp, 0.0)
        q_rows[SWA_HEADS // 2 + p] = jnp.where(lane_lo, 0.0, qp)
    scale = SWA_HD ** -0.5
    s_own_all = _mm_nt(jnp.concatenate(q_rows, axis=0), _pad_rows(k_own)) * scale
    tok = r_t & (ts - 1)
    prev_mask = c_t > tok
    sink_col = jnp.zeros((rows, 1), F32)
    for hh in range(SWA_HEADS):
        sink_col = jnp.where((r_t[:, 0:1] ^ (hh * ts)) < ts, sinks_ref[hh], sink_col)
    pp_all, po_all, den_all = [], [], []
    for b in seqs:
        q_b = jnp.concatenate([grp(q_rows[hh], b) for hh in range(SWA_HEADS)], axis=0)
        sp = jnp.where(prev_mask, _mm_nt(q_b, k0_ref[b]) * scale, -jnp.inf)
        so_raw = jnp.concatenate([s_own_all[hh * rows + b * ts:hh * rows + (b + 1) * ts]
                                  for hh in range(SWA_HEADS)], axis=0)
        so = jnp.where(((c_t ^ (b * ts)) < ts) & ((c_t & (ts - 1)) <= tok), so_raw, -jnp.inf)
        m = jnp.maximum(jnp.maximum(jnp.max(sp, axis=-1, keepdims=True),
                                    jnp.max(so, axis=-1, keepdims=True)), sink_col)
        pp = jnp.exp(sp - m)
        po = jnp.exp(so - m)
        den_all.append(jnp.sum(pp, axis=-1, keepdims=True) + jnp.sum(po, axis=-1, keepdims=True)
                       + jnp.exp(sink_col - m))
        pp_all.append(_mm(pp, v0_ref[b]))
        po_all.append(po)
    pv_own = _mm(jnp.concatenate(po_all, axis=0), _pad_rows(v_own))
    os_rows = []
    for b in seqs:
        pv_b = (pp_all[b] + pv_own[b * rows:(b + 1) * rows]) / den_all[b]
        os_rows.append(jnp.concatenate(
            [jnp.where(lane_lo_t, pv_b[p * ts:(p + 1) * ts],
                       pv_b[(SWA_HEADS // 2 + p) * ts:(SWA_HEADS // 2 + p + 1) * ts])
             for p in range(SWA_HEADS // 2)], axis=1))
        kout_ref[b] = jnp.concatenate([k0_ref[b, ts:, :], grp(k_own, b)], axis=0)
        vout_ref[b] = jnp.concatenate([v0_ref[b, ts:, :], grp(v_own, b)], axis=0)
    o_full = jnp.concatenate(og + [jnp.concatenate(os_rows, axis=0)], axis=-1)
    for b in seqs:
        o_ref[b] = grp(o_full, b).astype(o_ref.dtype)


def _mixer_sample(proj3d, s0, conv0, k0, v0, convw, alog_row, dt_row, gnorm, sinks, rc, rs1, rs2):
    bsz, seq, _ = proj3d.shape
    nb = SAMPLE_NB
    assert seq == SAMPLE_T and bsz % nb == 0
    seq_blk = lambda shape: pl.BlockSpec((nb,) + shape, lambda i: (i,) + (0,) * len(shape))
    const_blk = lambda shape: pl.BlockSpec(shape, lambda i: (0,) * len(shape))
    return pl.pallas_call(
        _mixer_sample_kernel,
        grid=(bsz // nb,),
        in_specs=[
            pl.BlockSpec(memory_space=pltpu.SMEM),
            seq_blk((seq, P_W)),
            seq_blk((GDN_HEADS, GDN_DK, GDN_DV)),
            seq_blk((GDN_CONV - 1, GDN_CONV_W)),
            seq_blk((WINDOW, SWA_KV_W)),
            seq_blk((WINDOW, SWA_KV_W)),
            const_blk((GDN_CONV, GDN_CONV_W)),
            const_blk((1, LANES)),
            const_blk((1, LANES)),
            const_blk((1, GDN_DV)),
            const_blk((seq, LANES)),
            const_blk((seq, LANES)),
            const_blk((seq, LANES)),
        ],
        out_specs=[
            seq_blk((seq, D_MODEL)),
            seq_blk((GDN_HEADS, GDN_DK, GDN_DV)),
            seq_blk((GDN_CONV - 1, GDN_CONV_W)),
            seq_blk((WINDOW, SWA_KV_W)),
            seq_blk((WINDOW, SWA_KV_W)),
        ],
        out_shape=[
            jax.ShapeDtypeStruct((bsz, seq, D_MODEL), BF16),
            jax.ShapeDtypeStruct((bsz, GDN_HEADS, GDN_DK, GDN_DV), F32),
            jax.ShapeDtypeStruct((bsz, GDN_CONV - 1, GDN_CONV_W), F32),
            jax.ShapeDtypeStruct((bsz, WINDOW, SWA_KV_W), F32),
            jax.ShapeDtypeStruct((bsz, WINDOW, SWA_KV_W), F32),
        ],
        scratch_shapes=[pltpu.VMEM((2 * SAMPLE_T * nb, GDN_CONV_W), F32)],
        compiler_params=pltpu.CompilerParams(
            dimension_semantics=("arbitrary",), vmem_limit_bytes=VMEM_LIMIT_BYTES),
        name="mixer_sample",
    )(sinks, proj3d, s0, conv0, k0, v0, convw, alog_row, dt_row, gnorm, rc, rs1, rs2)


def _rope_tables(pos):
    half = ROT_DIM // 2
    inv = ROPE_THETA ** (-jnp.arange(half, dtype=F32) * 2.0 / ROT_DIM)
    ang = pos.astype(F32)[:, None] * inv[None, :]
    cos, sin = jnp.cos(ang), jnp.sin(ang)
    n = pos.shape[0]
    rest = SWA_HD - ROT_DIM
    c = jnp.concatenate([cos, cos, jnp.ones((n, rest), F32)], axis=1)
    s1 = jnp.concatenate([-sin, jnp.zeros((n, half + rest), F32)], axis=1)
    s2 = jnp.concatenate([jnp.zeros((n, half), F32), sin, jnp.zeros((n, rest), F32)], axis=1)
    reps = LANES // SWA_HD
    return jnp.tile(c, (1, reps)), jnp.tile(s1, (1, reps)), jnp.tile(s2, (1, reps))


def _pair_order():
    order = []
    for p in range(SWA_HEADS // 2):
        order += [p, SWA_HEADS // 2 + p]
    return np.asarray(order)


def _layer_weights(norm_ffn1, ffn1_gu, ffn1_down, norm_mix, w_in, conv_w, a_log, dt_bias, gdn_norm, sinks,
                   w_out, norm_ffn2, ffn2_gu, ffn2_down, norm_ple, ple_proj, ple_gate, norm_final):
    order = _pair_order()
    head_cols = (order[:, None] * SWA_HD + np.arange(SWA_HD)[None, :]).reshape(-1)
    o_qkv, o_z = 0, GDN_CONV_W
    o_a = o_z + GDN_V_W
    o_b = o_a + GDN_HEADS
    o_qs = o_b + GDN_HEADS
    o_ks = o_qs + SWA_Q_W
    o_vs = o_ks + SWA_KV_W
    win_p = jnp.concatenate([
        w_in[:, o_qkv:o_z], w_in[:, o_z:o_a], w_in[:, o_qs:o_ks][:, head_cols], w_in[:, o_ks:o_vs],
        w_in[:, o_vs:o_vs + SWA_KV_W], w_in[:, o_a:o_qs],
        jnp.zeros((D_MODEL, LANES - 2 * GDN_HEADS), w_in.dtype)], axis=1).astype(BF16)
    wout_p = jnp.concatenate([w_out[:GDN_V_W], w_out[GDN_V_W:][head_cols]], axis=0).astype(BF16)
    pad_row = lambda v: jnp.concatenate([v.astype(F32), jnp.zeros((LANES - v.shape[0],), F32)])[None, :]
    return dict(
        g1=norm_ffn1[None, :], wgu1=ffn1_gu.astype(BF16), wd1=ffn1_down.astype(BF16),
        gm=norm_mix[None, :], win=win_p, convw=conv_w, alog=pad_row(a_log), dt=pad_row(dt_bias),
        gnorm=gdn_norm[None, :], sinks=sinks.astype(F32), wout=wout_p,
        g2=norm_ffn2[None, :], wgu2=ffn2_gu.astype(BF16), wd2=ffn2_down.astype(BF16),
        gp=norm_ple[None, :], wpp=ple_proj.astype(BF16), wpg=ple_gate.astype(BF16), gf=norm_final[None, :])


def _group(x, p, w, state, pos0):
    bsz, seq, _ = x.shape
    rc, rs1, rs2 = _rope_tables(pos0 + jnp.arange(seq))
    x2d = x.reshape(bsz * seq, D_MODEL)
    if state is None:
        h2d, prep2d, conv_new = _ffn_in_prep(x2d, seq, w["g1"], w["wgu1"], w["wd1"], w["gm"], w["win"],
                                             w["convw"], w["alog"], w["dt"], rc, rs1, rs2)
        o, s_new, k_new, v_new = _mixer_prompt(prep2d.reshape(bsz, seq, P_W), w["gnorm"], w["sinks"])
    else:
        h2d, proj2d = _ffn_in(x2d, w["g1"], w["wgu1"], w["wd1"], w["gm"], w["win"])
        s0, conv0, k0, v0 = state
        o, s_new, conv_new, k_new, v_new = _mixer_sample(
            proj2d.reshape(bsz, seq, P_W), s0, conv0, k0, v0, w["convw"], w["alog"], w["dt"], w["gnorm"],
            w["sinks"], rc, rs1, rs2)
    y = _ffn_out(o.reshape(bsz * seq, D_MODEL), h2d, p.reshape(bsz * seq, PLE_DIM), w["wout"], w["g2"],
                 w["wgu2"], w["wd2"], w["gp"], w["wpg"], w["wpp"], w["gf"])
    kv_shape = (1, bsz, WINDOW, SWA_KV_HEADS, SWA_HD)
    return (y.reshape(bsz, seq, D_MODEL), s_new[None], conv_new[None], k_new.reshape(kv_shape),
            v_new.reshape(kv_shape))


def kernel(x_prompt, x_sample, state_gdn, state_conv, cache_swa_k, cache_swa_v, p_prompt, p_sample, norm_ffn1, ffn1_gu, ffn1_down, norm_mix, w_in, conv_w, a_log, dt_bias, gdn_norm, sinks, w_out, norm_ffn2, ffn2_gu, ffn2_down, norm_ple, ple_proj, ple_gate, norm_final):
    assert state_gdn.shape[0] == 1, "one layer"
    w = _layer_weights(norm_ffn1[0], ffn1_gu[0], ffn1_down[0], norm_mix[0], w_in[0], conv_w[0], a_log[0],
                       dt_bias[0], gdn_norm[0], sinks[0], w_out[0], norm_ffn2[0], ffn2_gu[0], ffn2_down[0],
                       norm_ple[0], ple_proj[0], ple_gate[0], norm_final)
    bs = x_sample.shape[0]
    yp, sg_p, sc_p, kk_p, vv_p = _group(x_prompt, p_prompt[0], w, None, 0)
    ys, sg_s, sc_s, kk_s, vv_s = _group(
        x_sample, p_sample[0], w,
        (state_gdn[0], state_conv[0], cache_swa_k[0].reshape(bs, WINDOW, SWA_KV_W),
         cache_swa_v[0].reshape(bs, WINDOW, SWA_KV_W)), PAST_LEN)
    return (yp, ys, sg_p, sc_p, kk_p, vv_p, sg_s, sc_s, kk_s, vv_s)
```

```python
import functools

import jax
import jax.numpy as jnp
import numpy as np
from jax.experimental import pallas as pl
from jax.experimental.pallas import tpu as pltpu

F32 = jnp.float32
BF16 = jnp.bfloat16

D_MODEL = 1024
D_FF = 2816
PLE_DIM = 256
NORM_EPS = 1e-6
GDN_HEADS = 4
GDN_DK = 128
GDN_DV = 128
GDN_CONV = 4
GDN_QK_W = GDN_HEADS * GDN_DK
GDN_V_W = GDN_HEADS * GDN_DV
GDN_CONV_W = 2 * GDN_QK_W + GDN_V_W
SWA_HEADS = 8
SWA_KV_HEADS = 2
SWA_HD = 64
SWA_Q_W = SWA_HEADS * SWA_HD
SWA_KV_W = SWA_KV_HEADS * SWA_HD
WINDOW = 128
ROT_DIM = SWA_HD // 4
ROPE_THETA = 500000.0
PAST_LEN = 16384

LANES = 128
SUBLANES = 8
VMEM_LIMIT_BYTES = 56 * 1024 * 1024

P_QKV = 0
P_Z = P_QKV + GDN_CONV_W
P_QS = P_Z + GDN_V_W
P_KS = P_QS + SWA_Q_W
P_VS = P_KS + SWA_KV_W
P_AB = P_VS + SWA_KV_W
P_W = P_AB + LANES

FF_CHUNK = 256
TOKEN_TILE = 512

CHUNK = 64
CAT_W = GDN_HEADS * CHUNK
PREP_TILE_ORDER = tuple(t for pair in zip(range(12), range(12, 23)) for t in pair) + (11,)
SAMPLE_T = SUBLANES
SAMPLE_NB = CHUNK // SAMPLE_T


def _rms(x, g):
    return x * jax.lax.rsqrt(jnp.mean(x * x, axis=-1, keepdims=True) + NORM_EPS) * g


def _sigmoid(x):
    return 1.0 / (1.0 + jnp.exp(-x))


def _silu(x):
    return x * _sigmoid(x)


def _mm(a, b):
    return jnp.dot(a.astype(BF16), b.astype(BF16), preferred_element_type=F32)


def _mm_nt(a, b):
    return jax.lax.dot_general(a.astype(BF16), b.astype(BF16), (((1,), (1,)), ((), ())),
                               preferred_element_type=F32)


def _swiglu_acc(n_bf16, wgu_ref, wd_ref):
    rows = n_bf16.shape[0]
    acc = jnp.zeros((rows, D_MODEL), F32)
    for c in range(D_FF // FF_CHUNK):
        lo = c * FF_CHUNK
        gate = jnp.dot(n_bf16, wgu_ref[:, lo:lo + FF_CHUNK], preferred_element_type=F32)
        up = jnp.dot(n_bf16, wgu_ref[:, D_FF + lo:D_FF + lo + FF_CHUNK], preferred_element_type=F32)
        act = (_silu(gate) * up).astype(BF16)
        acc = acc + jnp.dot(act, wd_ref[lo:lo + FF_CHUNK, :], preferred_element_type=F32)
    return acc


def _ffn_in_kernel(x_ref, g1_ref, wgu_ref, wd_ref, gm_ref, win_ref, h_ref, proj_ref):
    x = x_ref[...]
    n1 = _rms(x, g1_ref[...]).astype(BF16)
    h = x + 0.5 * _swiglu_acc(n1, wgu_ref, wd_ref)
    h_ref[...] = h
    n = _rms(h, gm_ref[...]).astype(BF16)
    proj_ref[...] = jnp.dot(n, win_ref[...], preferred_element_type=F32)


def _const_spec(shape):
    return pl.BlockSpec(shape, lambda i: (0,) * len(shape), pipeline_mode=pl.Buffered(1))


def _ffn_in(x2d, g1, wgu, wd, gm, win):
    n_tok = x2d.shape[0]
    tm = min(TOKEN_TILE, n_tok)
    assert n_tok % tm == 0
    return pl.pallas_call(
        _ffn_in_kernel,
        grid=(n_tok // tm,),
        in_specs=[
            pl.BlockSpec((tm, D_MODEL), lambda i: (i, 0)),
            _const_spec((1, D_MODEL)),
            _const_spec((D_MODEL, 2 * D_FF)),
            _const_spec((D_FF, D_MODEL)),
            _const_spec((1, D_MODEL)),
            _const_spec((D_MODEL, P_W)),
        ],
        out_specs=[
            pl.BlockSpec((tm, D_MODEL), lambda i: (i, 0)),
            pl.BlockSpec((tm, P_W), lambda i: (i, 0)),
        ],
        out_shape=[
            jax.ShapeDtypeStruct((n_tok, D_MODEL), F32),
            jax.ShapeDtypeStruct((n_tok, P_W), F32),
        ],
        compiler_params=pltpu.CompilerParams(
            dimension_semantics=("arbitrary",), vmem_limit_bytes=VMEM_LIMIT_BYTES),
        name="ffn_in",
    )(x2d, g1, wgu, wd, gm, win)


def _ffn_in_prep_kernel(x_ref, g1_ref, wgu_ref, wd_ref, gm_ref, win_ref, convw_ref, alog_ref, dt_ref,
                        rc_ref, rs1_ref, rs2_ref, h_ref, prep_ref, convout_ref, n_st, xext_ref, *, tiles_per_seq):
    i = pl.program_id(0)
    tm = x_ref.shape[0]
    piece = 2 * LANES

    @pl.when(i == 0)
    def _init():
        n_st[...] = jnp.zeros(n_st.shape, n_st.dtype)
        xext_ref[...] = jnp.zeros(xext_ref.shape, F32)

    n_prev = n_st[...]
    first_of_seq = ((i - 1) % tiles_per_seq) == 0

    def prep_tile(lo, val):
        cols = slice(lo, lo + LANES)
        if lo < P_Z:
            hist = jnp.where(first_of_seq, 0.0, xext_ref[:, cols])
            xext_ref[:, cols] = val[tm - SUBLANES:tm]
            convout_ref[0, :, cols] = val[tm - (GDN_CONV - 1):tm]
            ext = jnp.concatenate([hist, val], axis=0)
            conv = val * convw_ref[GDN_CONV - 1:GDN_CONV, cols]
            for j in range(1, GDN_CONV):
                conv = conv + pltpu.roll(ext, j, 0)[SUBLANES:] * convw_ref[GDN_CONV - 1 - j:GDN_CONV - j, cols]
            act = _silu(conv)
            if lo < 2 * GDN_QK_W:
                scale = GDN_DK ** -0.5 if lo < GDN_QK_W else 1.0
                act = act * (jax.lax.rsqrt(jnp.sum(act * act, axis=-1, keepdims=True) + 1e-6) * scale)
            prep_ref[:, cols] = act
        elif lo < P_QS:
            prep_ref[:, cols] = _silu(val)
        elif lo < P_VS:
            prep_ref[:, cols] = _rope(val, rc_ref[...], rs1_ref[...], rs2_ref[...])
        elif lo < P_AB:
            prep_ref[:, cols] = val
        else:
            g_all, beta_all = _gates(val, alog_ref[...], dt_ref[...])
            lane = jax.lax.broadcasted_iota(jnp.int32, (tm, LANES), 1)
            prep_ref[:, cols] = jnp.where(lane < GDN_HEADS, g_all, beta_all)

    def proj_piece(lo):
        width = min(piece, P_W - lo)
        proj = jnp.dot(n_prev, win_ref[:, lo:lo + width], preferred_element_type=F32)
        for off in range(0, width, LANES):
            prep_tile(PREP_TILE_ORDER[(lo + off) // LANES] * LANES, proj[:, off:off + LANES])

    x = x_ref[...]
    n1 = _rms(x, g1_ref[...]).astype(BF16)
    acc = jnp.zeros((tm, D_MODEL), F32)
    for c in range(D_FF // FF_CHUNK):
        lo = c * FF_CHUNK
        gate = jnp.dot(n1, wgu_ref[:, lo:lo + FF_CHUNK], preferred_element_type=F32)
        up = jnp.dot(n1, wgu_ref[:, D_FF + lo:D_FF + lo + FF_CHUNK], preferred_element_type=F32)
        act = (_silu(gate) * up).astype(BF16)
        acc = acc + jnp.dot(act, wd_ref[lo:lo + FF_CHUNK, :], preferred_element_type=F32)
        proj_piece(c * piece)
    for lo in range((D_FF // FF_CHUNK) * piece, P_W, piece):
        proj_piece(lo)
    h = x + 0.5 * acc
    h_ref[...] = h
    n_st[...] = _rms(h, gm_ref[...]).astype(BF16)


def _ffn_in_prep(x2d, seq_len, g1, wgu, wd, gm, win, convw, alog_row, dt_row, rc, rs1, rs2):
    n_tok = x2d.shape[0]
    tm = TOKEN_TILE
    assert seq_len % tm == 0 and n_tok % seq_len == 0
    tiles_per_seq = seq_len // tm
    n_tiles = n_tok // tm
    cur = lambda i: jnp.minimum(i, n_tiles - 1)
    prev = lambda i: jnp.maximum(i - 1, 0)
    rope_spec = pl.BlockSpec((tm, LANES), lambda i: (prev(i) % tiles_per_seq, 0))
    return pl.pallas_call(
        functools.partial(_ffn_in_prep_kernel, tiles_per_seq=tiles_per_seq),
        grid=(n_tiles + 1,),
        in_specs=[
            pl.BlockSpec((tm, D_MODEL), lambda i: (cur(i), 0)),
            _const_spec((1, D_MODEL)),
            _const_spec((D_MODEL, 2 * D_FF)),
            _const_spec((D_FF, D_MODEL)),
            _const_spec((1, D_MODEL)),
            _const_spec((D_MODEL, P_W)),
            _const_spec((GDN_CONV, GDN_CONV_W)),
            _const_spec((1, LANES)),
            _const_spec((1, LANES)),
            rope_spec, rope_spec, rope_spec,
        ],
        out_specs=[
            pl.BlockSpec((tm, D_MODEL), lambda i: (cur(i), 0)),
            pl.BlockSpec((tm, P_W), lambda i: (prev(i), 0)),
            pl.BlockSpec((1, GDN_CONV - 1, GDN_CONV_W), lambda i: (prev(i) // tiles_per_seq, 0, 0)),
        ],
        out_shape=[
            jax.ShapeDtypeStruct((n_tok, D_MODEL), F32),
            jax.ShapeDtypeStruct((n_tok, P_W), F32),
            jax.ShapeDtypeStruct((n_tok // seq_len, GDN_CONV - 1, GDN_CONV_W), F32),
        ],
        scratch_shapes=[pltpu.VMEM((tm, D_MODEL), BF16), pltpu.VMEM((SUBLANES, GDN_CONV_W), F32)],
        compiler_params=pltpu.CompilerParams(
            dimension_semantics=("arbitrary",), vmem_limit_bytes=VMEM_LIMIT_BYTES),
        name="ffn_in_prep",
    )(x2d, g1, wgu, wd, gm, win, convw, alog_row, dt_row, rc, rs1, rs2)


def _ffn_out_kernel(o_ref, h_ref, p_ref, wout_ref, g2_ref, wgu_ref, wd_ref, gp_ref, wpg_ref, wpp_ref,
                    gf_ref, y_ref):
    h = h_ref[...] + jnp.dot(o_ref[...], wout_ref[...], preferred_element_type=F32)
    n2 = _rms(h, g2_ref[...]).astype(BF16)
    h = h + 0.5 * _swiglu_acc(n2, wgu_ref, wd_ref)
    npl = _rms(h, gp_ref[...]).astype(BF16)
    gate = _sigmoid(jnp.dot(npl, wpg_ref[...], preferred_element_type=F32))
    pe = jnp.dot(p_ref[...].astype(BF16), wpp_ref[...], preferred_element_type=F32)
    h = h + gate * pe
    y_ref[...] = _rms(h, gf_ref[...])


def _ffn_out(o2d, h2d, p2d, wout, g2, wgu, wd, gp, wpg, wpp, gf):
    n_tok = h2d.shape[0]
    tm = min(TOKEN_TILE, n_tok)
    assert n_tok % tm == 0
    return pl.pallas_call(
        _ffn_out_kernel,
        grid=(n_tok // tm,),
        in_specs=[
            pl.BlockSpec((tm, D_MODEL), lambda i: (i, 0)),
            pl.BlockSpec((tm, D_MODEL), lambda i: (i, 0)),
            pl.BlockSpec((tm, PLE_DIM), lambda i: (i, 0)),
            _const_spec((D_MODEL, D_MODEL)),
            _const_spec((1, D_MODEL)),
            _const_spec((D_MODEL, 2 * D_FF)),
            _const_spec((D_FF, D_MODEL)),
            _const_spec((1, D_MODEL)),
            _const_spec((D_MODEL, D_MODEL)),
            _const_spec((PLE_DIM, D_MODEL)),
            _const_spec((1, D_MODEL)),
        ],
        out_specs=pl.BlockSpec((tm, D_MODEL), lambda i: (i, 0)),
        out_shape=jax.ShapeDtypeStruct((n_tok, D_MODEL), F32),
        compiler_params=pltpu.CompilerParams(
            dimension_semantics=("arbitrary",), vmem_limit_bytes=VMEM_LIMIT_BYTES),
        name="ffn_out",
    )(o2d, h2d, p2d, wout, g2, wgu, wd, gp, wpg, wpp, gf)


def _pad_rows(x):
    rows = x.shape[0]
    if rows == LANES:
        return x
    return jnp.concatenate([x, jnp.zeros((LANES - rows, x.shape[1]), x.dtype)], axis=0)


def _split3(x):
    hi = x.astype(BF16)
    r = x - hi.astype(F32)
    mid = r.astype(BF16)
    lo = (r - mid.astype(F32)).astype(BF16)
    return hi, mid, lo


def _mm_exact_rhs(a_bf16, x):
    hi, mid, lo = _split3(x)
    return (jnp.dot(a_bf16, hi, preferred_element_type=F32) + jnp.dot(a_bf16, mid, preferred_element_type=F32)
            + jnp.dot(a_bf16, lo, preferred_element_type=F32))


def _mm_exact_lhs(x, a_bf16):
    hi, mid, lo = _split3(x)
    return (jnp.dot(hi, a_bf16, preferred_element_type=F32) + jnp.dot(mid, a_bf16, preferred_element_type=F32)
            + jnp.dot(lo, a_bf16, preferred_element_type=F32))


def _bd_cat(y, lane_lo_chunk):
    zero_tile = jnp.zeros((CHUNK, LANES), F32)
    blocks = []
    for hh in range(GDN_HEADS):
        y_tile = y[:, (hh // 2) * LANES:(hh // 2 + 1) * LANES]
        keep = jnp.where(lane_lo_chunk, y_tile, 0.0) if hh % 2 == 0 else jnp.where(lane_lo_chunk, 0.0, y_tile)
        blocks.append(jnp.concatenate([keep, zero_tile] if hh < 2 else [zero_tile, keep], axis=1))
    return jnp.concatenate(blocks, axis=0).astype(BF16)


def _bd_wide(y):
    zero_tile = jnp.zeros((CHUNK, LANES), F32)
    blocks = []
    for hh in range(GDN_HEADS):
        tiles = [zero_tile] * GDN_HEADS
        tiles[hh] = y[:, hh * LANES:(hh + 1) * LANES]
        blocks.append(jnp.concatenate(tiles, axis=1))
    return jnp.concatenate(blocks, axis=0).astype(BF16)


def _bd_pair(sa, sb):
    zeros = jnp.zeros(sa.shape, sa.dtype)
    return jnp.concatenate([jnp.concatenate([sa, zeros], axis=1), jnp.concatenate([zeros, sb], axis=1)], axis=0)


def _gates(ab, alog_row, dt_row):
    zab = ab + dt_row
    softplus = jnp.maximum(zab, 0.0) + jnp.log(1.0 + jnp.exp(-jnp.abs(zab)))
    return -jnp.exp(alog_row) * softplus, _sigmoid(ab)


def _head_l2(x):
    parts = []
    for hh in range(GDN_HEADS):
        xh = x[:, hh * GDN_DK:(hh + 1) * GDN_DK]
        parts.append(xh * jax.lax.rsqrt(jnp.sum(xh * xh, axis=-1, keepdims=True) + 1e-6))
    return jnp.concatenate(parts, axis=1)


def _lane_expand(cols, first_lane, rows):
    return jnp.concatenate(
        [jnp.broadcast_to(cols[:, first_lane + hh:first_lane + hh + 1], (rows, LANES))
         for hh in range(GDN_HEADS)], axis=1)


def _rope(x, rc, rs1, rs2):
    return x * rc + pltpu.roll(x, LANES - ROT_DIM // 2, 1) * rs1 + pltpu.roll(x, ROT_DIM // 2, 1) * rs2


def _uw_rhs(vb_all, kbg_all):
    return jnp.concatenate(
        [jnp.concatenate([vb_all[:, hh * GDN_DV:(hh + 1) * GDN_DV],
                          kbg_all[:, hh * GDN_DK:(hh + 1) * GDN_DK]], axis=1)
         for hh in range(GDN_HEADS)], axis=0)


def _split_uw(uw):
    u_all = jnp.concatenate([uw[hh * CHUNK:(hh + 1) * CHUNK, 0:GDN_DV] for hh in range(GDN_HEADS)], axis=1)
    w_all = jnp.concatenate([uw[hh * CHUNK:(hh + 1) * CHUNK, GDN_DV:] for hh in range(GDN_HEADS)], axis=1)
    return u_all, w_all


def _stack_heads_t(x):
    return jnp.transpose(jnp.concatenate(
        [x[:, hh * GDN_DK:(hh + 1) * GDN_DK] for hh in range(GDN_HEADS)], axis=0))


PROMPT_NS = 4


def _mixer_prompt_kernel(sinks_ref, cur_ref, prev_ref, gnorm_ref, o_ref, sout_ref, kout_ref, vout_ref,
                         s_ref, kprev_ref, vprev_ref, gcc_st, gcr_st, beta_st, os_st, *, tb, n_blocks):
    t = pl.program_id(1)
    ns = PROMPT_NS
    nch = tb // CHUNK
    stages = (gcc_st, gcr_st, beta_st, os_st)
    stage_names = ("gcc", "gcr", "beta", "os")

    @pl.when(t == 0)
    def _init():
        for ref in (s_ref, kprev_ref, vprev_ref) + stages:
            ref[...] = jnp.zeros(ref.shape, F32)

    r_t = jax.lax.broadcasted_iota(jnp.int32, (tb, LANES), 0)
    c_t = jax.lax.broadcasted_iota(jnp.int32, (tb, LANES), 1)
    lane_lo = c_t < CHUNK
    lane_lo_chunk = jax.lax.broadcasted_iota(jnp.int32, (CHUNK, LANES), 1) < CHUNK
    lane_lo_row = jax.lax.broadcasted_iota(jnp.int32, (1, LANES), 1) < CHUNK
    same_chunk = (r_t ^ c_t) < CHUNK
    rk = jax.lax.broadcasted_iota(jnp.int32, (tb, 2 * WINDOW), 0)
    ck = jax.lax.broadcasted_iota(jnp.int32, (tb, 2 * WINDOW), 1)
    first_off = jnp.where(t > 0, 0, WINDOW)
    dist = ck - rk
    kmask = ((dist > jnp.where(ck < WINDOW, first_off, -WINDOW))
             & (dist <= jnp.where(ck < WINDOW, 2 * WINDOW, WINDOW)))

    s_cat = [s_ref[sq] for sq in range(ns)]
    k_prev = [kprev_ref[sq] for sq in range(ns)]
    v_prev = [vprev_ref[sq] for sq in range(ns)]
    staged = [{name: ref[sq] for ref, name in zip(stages, stage_names)} for sq in range(ns)]
    for sq in range(ns):
        staged[sq].update(q=prev_ref[sq, :, 0:GDN_QK_W], k=prev_ref[sq, :, GDN_QK_W:2 * GDN_QK_W],
                          v=prev_ref[sq, :, 2 * GDN_QK_W:GDN_CONV_W], zs=prev_ref[sq, :, P_Z:P_Z + GDN_V_W])
    fr = [{} for _ in range(ns)]

    def front(sq):
        f = fr[sq]
        ab = cur_ref[sq, :, P_AB:P_AB + LANES]
        f["beta"] = ab
        f["gcc"] = _mm_exact_rhs(((c_t <= r_t) & same_chunk).astype(BF16), ab)
        f["gcr"] = _mm_exact_lhs(jnp.transpose(ab)[0:SUBLANES, :],
                                 ((r_t <= c_t) & same_chunk).astype(BF16))
        k_own = cur_ref[sq, :, P_KS:P_KS + SWA_KV_W]
        v_own = cur_ref[sq, :, P_VS:P_VS + SWA_KV_W]
        f["k_own"], f["v_own"] = k_own, v_own
        k_cat = jnp.concatenate([k_prev[sq], k_own], axis=0)
        v_cat = jnp.concatenate([v_prev[sq], v_own], axis=0).astype(BF16)
        q_rows = [None] * SWA_HEADS
        for p in range(SWA_HEADS // 2):
            qp = cur_ref[sq, :, P_QS + p * LANES:P_QS + (p + 1) * LANES]
            q_rows[p] = jnp.where(lane_lo, qp, 0.0)
            q_rows[SWA_HEADS // 2 + p] = jnp.where(lane_lo, 0.0, qp)
        s_all = _mm_nt(jnp.concatenate(q_rows, axis=0), k_cat) * (SWA_HD ** -0.5)
        yield
        pv = [None] * SWA_HEADS
        for hh in range(SWA_HEADS):
            sh = jnp.where(kmask, s_all[hh * tb:(hh + 1) * tb], -jnp.inf)
            sk = sinks_ref[hh]
            m = jnp.maximum(jnp.max(sh, axis=-1, keepdims=True), sk)
            ph = jnp.exp(sh - m)
            den = jnp.sum(ph, axis=-1, keepdims=True) + jnp.exp(sk - m)
            pv[hh] = jnp.dot(ph.astype(BF16), v_cat, preferred_element_type=F32) / den
            yield
        f["os"] = jnp.concatenate(
            [jnp.where(lane_lo, pv[p], pv[SWA_HEADS // 2 + p]) for p in range(SWA_HEADS // 2)], axis=1)

    fronts = [front(sq) for sq in range(ns)]

    def fill():
        for gen in fronts:
            next(gen, None)

    units = [(sq, c) for sq in range(ns) for c in range(nch)]
    rows_of = lambda c: slice(c * CHUNK, (c + 1) * CHUNK)
    ri = jax.lax.broadcasted_iota(jnp.int32, (CHUNK, CAT_W), 0)
    ci = jax.lax.broadcasted_iota(jnp.int32, (CHUNK, CAT_W), 1)
    cj = ci & (CHUNK - 1)
    tri = cj <= ri
    strict = cj < ri
    eye = (cj == ri).astype(F32)
    blk = ri ^ cj
    bd = functools.partial(_bd_cat, lane_lo_chunk=lane_lo_chunk)

    drv = []
    for sq in range(ns):
        st = staged[sq]
        gc_full = _lane_expand(st["gcc"], 0, tb)
        beta_full = _lane_expand(st["beta"], GDN_HEADS, tb)
        egc_full = jnp.exp(gc_full)
        kb_all = st["k"] * beta_full
        drv.append(dict(gc_full=gc_full, kb=kb_all, vb=st["v"] * beta_full, kbg=kb_all * egc_full,
                        qd=st["q"] * egc_full, gcr=st["gcr"], gcr_sw=pltpu.roll(st["gcr"], CHUNK, 1)))
    fill()
    a_mat, qk, xinv, pw = {}, {}, {}, {}
    for u in units:
        sq, c = u
        d, rows = drv[sq], rows_of(c)
        gcc_pairs, gcr_pairs = [], []
        for p in range(GDN_HEADS // 2):
            gcc_pairs.append(jnp.where(lane_lo_chunk, d["gc_full"][rows, (2 * p) * LANES:(2 * p + 1) * LANES],
                                       d["gc_full"][rows, (2 * p + 1) * LANES:(2 * p + 2) * LANES]))
            ra = (d["gcr"] if c == 0 else d["gcr_sw"])[2 * p:2 * p + 1, :]
            rb_ = (d["gcr_sw"] if c == 0 else d["gcr"])[2 * p + 1:2 * p + 2, :]
            gcr_pairs.append(jnp.where(lane_lo_row, ra, rb_))
        gcc = jnp.concatenate(gcc_pairs, axis=1)
        gcr = jnp.concatenate(gcr_pairs, axis=1)
        decay = jnp.exp(jnp.where(tri, gcc - gcr, -jnp.inf))
        kq = jax.lax.dot_general(
            jnp.concatenate([d["kb"][rows], staged[sq]["q"][rows]], axis=0).astype(BF16),
            _bd_wide(staged[sq]["k"][rows]), (((1,), (1,)), ((), ())), preferred_element_type=F32)
        a_mat[u] = jnp.where(strict, kq[0:CHUNK] * decay, 0.0)
        qk[u] = kq[CHUNK:] * decay
    fill()
    for u in units:
        a0 = jnp.where(blk < SUBLANES, a_mat[u], 0.0)
        xinv[u] = eye - a0
        pw[u] = _mm(a0, bd(a0))
    fill()
    for u in units:
        pw_bd = bd(pw[u])
        xinv[u] = xinv[u] + _mm(xinv[u], pw_bd)
        pw[u] = _mm(pw[u], pw_bd)
    fill()
    for u in units:
        xinv[u] = xinv[u] + _mm(xinv[u], bd(pw[u]))
    fill()
    s = SUBLANES
    while s < CHUNK:
        y = {u: _mm(jnp.where((blk < 2 * s) & (blk >= s), a_mat[u], 0.0), bd(xinv[u])) for u in units}
        fill()
        for u in units:
            xinv[u] = xinv[u] - _mm(xinv[u], bd(y[u]))
        fill()
        s *= 2
    prep = {}
    for u in units:
        sq, c = u
        d, rows = drv[sq], rows_of(c)
        u_all, w_all = _split_uw(_mm(bd(xinv[u]), _uw_rhs(d["vb"][rows], d["kbg"][rows])))
        glast = d["gc_full"][c * CHUNK + CHUNK - 1:c * CHUNK + CHUNK, :]
        kdec_t = _stack_heads_t(staged[sq]["k"][rows] * jnp.exp(glast - d["gc_full"][rows]))
        prep[u] = (u_all, w_all, kdec_t, jnp.exp(glast))
    fill()

    o_chunks = [[] for _ in range(ns)]
    for c in range(nch):
        rows = rows_of(c)
        ws, qs = [[] for _ in range(ns)], [[] for _ in range(ns)]
        for sq in range(ns):
            w_all = prep[(sq, c)][1]
            for p in range(GDN_HEADS // 2):
                cols = slice(2 * p * LANES, (2 * p + 2) * LANES)
                r = _mm(jnp.concatenate([w_all[:, cols], drv[sq]["qd"][rows, cols]], axis=0),
                        _bd_pair(s_cat[sq][:, (2 * p) * LANES:(2 * p + 1) * LANES],
                                 s_cat[sq][:, (2 * p + 1) * LANES:(2 * p + 2) * LANES]))
                ws[sq].append(r[0:CHUNK])
                qs[sq].append(r[CHUNK:])
        fill()
        for sq in range(ns):
            u_all, _, kdec_t, eg_last = prep[(sq, c)]
            v_new = u_all - jnp.concatenate(ws[sq], axis=1)
            r2 = _mm(jnp.concatenate([qk[(sq, c)], kdec_t], axis=0), _bd_wide(v_new))
            o_chunks[sq].append(jnp.concatenate(qs[sq], axis=1) + r2[0:CHUNK])
            s_cat[sq] = s_cat[sq] * eg_last + r2[CHUNK:]
        fill()
    for gen in fronts:
        for _ in gen:
            pass

    for sq in range(ns):
        o_gdn = jnp.concatenate(o_chunks[sq], axis=0)
        zs_prev = staged[sq]["zs"]
        og = [_rms(o_gdn[:, hh * GDN_DV:(hh + 1) * GDN_DV], gnorm_ref[...])
              * zs_prev[:, hh * GDN_DV:(hh + 1) * GDN_DV] for hh in range(GDN_HEADS)]
        o_ref[sq] = jnp.concatenate(og + [staged[sq]["os"]], axis=-1).astype(o_ref.dtype)
        s_ref[sq] = s_cat[sq]
        kprev_ref[sq] = fr[sq]["k_own"]
        vprev_ref[sq] = fr[sq]["v_own"]
        for ref, name in zip(stages, stage_names):
            ref[sq] = fr[sq][name]

    @pl.when(t == n_blocks - 1)
    def _window_out():
        for sq in range(ns):
            kout_ref[sq] = fr[sq]["k_own"]
            vout_ref[sq] = fr[sq]["v_own"]

    @pl.when(t == n_blocks)
    def _state_out():
        for sq in range(ns):
            for hh in range(GDN_HEADS):
                sout_ref[sq, hh] = s_ref[sq, :, hh * GDN_DV:(hh + 1) * GDN_DV]


def _mixer_prompt(prep3d, gnorm, sinks):
    bsz, seq, _ = prep3d.shape
    tb, ns = WINDOW, PROMPT_NS
    assert seq % tb == 0 and bsz % ns == 0
    n_blocks = seq // tb
    kern = functools.partial(_mixer_prompt_kernel, tb=tb, n_blocks=n_blocks)
    seq_blk = lambda shape: pl.BlockSpec((ns,) + shape, lambda b, t: (b,) + (0,) * len(shape))
    vmem = lambda *shape: pltpu.VMEM((ns,) + shape, F32)
    return pl.pallas_call(
        kern,
        grid=(bsz // ns, n_blocks + 1),
        in_specs=[
            pl.BlockSpec(memory_space=pltpu.SMEM),
            pl.BlockSpec((ns, tb, P_W), lambda b, t: (b, jnp.minimum(t, n_blocks - 1), 0)),
            pl.BlockSpec((ns, tb, P_W), lambda b, t: (b, jnp.maximum(t - 1, 0), 0)),
            pl.BlockSpec((1, GDN_DV), lambda b, t: (0, 0)),
        ],
        out_specs=[
            pl.BlockSpec((ns, tb, D_MODEL), lambda b, t: (b, jnp.maximum(t - 1, 0), 0)),
            seq_blk((GDN_HEADS, GDN_DK, GDN_DV)),
            seq_blk((WINDOW, SWA_KV_W)),
            seq_blk((WINDOW, SWA_KV_W)),
        ],
        out_shape=[
            jax.ShapeDtypeStruct((bsz, seq, D_MODEL), BF16),
            jax.ShapeDtypeStruct((bsz, GDN_HEADS, GDN_DK, GDN_DV), F32),
            jax.ShapeDtypeStruct((bsz, WINDOW, SWA_KV_W), F32),
            jax.ShapeDtypeStruct((bsz, WINDOW, SWA_KV_W), F32),
        ],
        scratch_shapes=[
            vmem(GDN_DK, GDN_HEADS * GDN_DV),
            vmem(WINDOW, SWA_KV_W),
            vmem(WINDOW, SWA_KV_W),
            vmem(tb, LANES),
            vmem(SUBLANES, LANES),
            vmem(tb, LANES),
            vmem(tb, SWA_Q_W),
        ],
        compiler_params=pltpu.CompilerParams(
            dimension_semantics=("arbitrary", "arbitrary"), vmem_limit_bytes=VMEM_LIMIT_BYTES),
        name="mixer_prompt",
    )(sinks, prep3d, prep3d, gnorm)


def _mixer_sample_kernel(sinks_ref, proj_ref, s0_ref, conv0_ref, k0_ref, v0_ref, convw_ref, alog_ref, dt_ref,
                         gnorm_ref, rc_ref, rs1_ref, rs2_ref,
                         o_ref, sout_ref, convout_ref, kout_ref, vout_ref, xext_ref):
    nb, ts, rows = SAMPLE_NB, SAMPLE_T, CHUNK
    seqs = range(nb)
    grp = lambda x, b: x[b * ts:(b + 1) * ts]

    hist = SUBLANES - (GDN_CONV - 1)
    conv_parts = []
    for b in seqs:
        xext_ref[2 * ts * b + hist:2 * ts * b + ts, :] = conv0_ref[b]
        xext_ref[2 * ts * b + ts:2 * ts * (b + 1), :] = proj_ref[b, :, P_QKV:P_QKV + GDN_CONV_W]
    for b in seqs:
        base = 2 * ts * b + hist
        conv = xext_ref[base:base + ts, :] * convw_ref[0:1, :]
        for j in range(1, GDN_CONV):
            conv = conv + xext_ref[base + j:base + j + ts, :] * convw_ref[j:j + 1, :]
        conv_parts.append(conv)
        convout_ref[b] = xext_ref[2 * ts * (b + 1) - (GDN_CONV - 1):2 * ts * (b + 1), :]
    qkv = _silu(jnp.concatenate(conv_parts, axis=0))
    q_all = _head_l2(qkv[:, 0:GDN_QK_W]) * (GDN_DK ** -0.5)
    k_all = _head_l2(qkv[:, GDN_QK_W:2 * GDN_QK_W])
    v_all = qkv[:, 2 * GDN_QK_W:]

    def proj_cols(lo, width):
        return jnp.concatenate([proj_ref[b, :, lo:lo + width] for b in seqs], axis=0)

    g_all, beta_all = _gates(proj_cols(P_AB, LANES), alog_ref[...], dt_ref[...])

    r_t = jax.lax.broadcasted_iota(jnp.int32, (rows, LANES), 0)
    c_t = jax.lax.broadcasted_iota(jnp.int32, (rows, LANES), 1)
    r_p = jax.lax.broadcasted_iota(jnp.int32, (LANES, LANES), 0)
    c_p = jax.lax.broadcasted_iota(jnp.int32, (LANES, LANES), 1)
    g_pad = _pad_rows(g_all)
    gc_cols = _mm_exact_rhs(((c_t <= r_t) & ((r_t ^ c_t) < ts)).astype(BF16), g_pad)
    gc_rows = _mm_exact_lhs(jnp.transpose(g_pad)[0:SUBLANES, :],
                            ((r_p <= c_p) & ((r_p ^ c_p) < ts)).astype(BF16))
    gc_rows_sw = pltpu.roll(gc_rows, CHUNK, 1)

    lane_lo = c_t < CHUNK
    lane_lo_row = jax.lax.broadcasted_iota(jnp.int32, (1, LANES), 1) < CHUNK
    lane_lo_t = jax.lax.broadcasted_iota(jnp.int32, (ts, LANES), 1) < CHUNK
    gc_full = _lane_expand(gc_cols, 0, rows)
    beta_full = _lane_expand(beta_all, GDN_HEADS, rows)
    egc_full = jnp.exp(gc_full)
    kb_all = k_all * beta_full
    vb_all = v_all * beta_full
    kbg_all = kb_all * egc_full
    qd_all = q_all * egc_full

    ri = jax.lax.broadcasted_iota(jnp.int32, (rows, CAT_W), 0)
    ci = jax.lax.broadcasted_iota(jnp.int32, (rows, CAT_W), 1)
    cj = ci & (CHUNK - 1)
    same_seq = (ri ^ cj) < ts
    eye = (cj == ri).astype(F32)
    bd = functools.partial(_bd_cat, lane_lo_chunk=lane_lo)

    gcc = jnp.concatenate(
        [jnp.where(lane_lo, gc_full[:, (2 * p) * LANES:(2 * p + 1) * LANES],
                   gc_full[:, (2 * p + 1) * LANES:(2 * p + 2) * LANES]) for p in range(GDN_HEADS // 2)], axis=1)
    gcr = jnp.concatenate(
        [jnp.where(lane_lo_row, gc_rows[2 * p:2 * p + 1, :], gc_rows_sw[2 * p + 1:2 * p + 2, :])
         for p in range(GDN_HEADS // 2)], axis=1)
    decay = jnp.exp(jnp.where((cj <= ri) & same_seq, gcc - gcr, -jnp.inf))
    kq = jax.lax.dot_general(
        jnp.concatenate([kb_all, q_all], axis=0).astype(BF16), _bd_wide(k_all),
        (((1,), (1,)), ((), ())), preferred_element_type=F32)
    a_mat = jnp.where(cj < ri, kq[0:rows] * decay, 0.0)
    qk = kq[rows:] * decay
    xinv = eye - a_mat
    pw = _mm(a_mat, bd(a_mat))
    pw_bd = bd(pw)
    xinv = xinv + _mm(xinv, pw_bd)
    pw = _mm(pw, pw_bd)
    xinv = xinv + _mm(xinv, bd(pw))
    u_all, w_all = _split_uw(_mm(bd(xinv), _uw_rhs(vb_all, kbg_all)))
    glast = [gc_full[(b + 1) * ts - 1:(b + 1) * ts, :] for b in seqs]
    glast_rows = jnp.concatenate([jnp.broadcast_to(glast[b], (ts, GDN_QK_W)) for b in seqs], axis=0)
    kdec_t = _stack_heads_t(k_all * jnp.exp(glast_rows - gc_full))

    s_old, ws, qs = [], [], []
    for b in seqs:
        s_b = jnp.concatenate([s0_ref[b, hh] for hh in range(GDN_HEADS)], axis=1)
        s_old.append(s_b)
        ws_b, qs_b = [], []
        for p in range(GDN_HEADS // 2):
            cols = slice(2 * p * LANES, (2 * p + 2) * LANES)
            r = _mm(jnp.concatenate([grp(w_all, b)[:, cols], grp(qd_all, b)[:, cols]], axis=0),
                    _bd_pair(s_b[:, (2 * p) * LANES:(2 * p + 1) * LANES],
                             s_b[:, (2 * p + 1) * LANES:(2 * p + 2) * LANES]))
            ws_b.append(r[0:ts])
            qs_b.append(r[ts:])
        ws.append(jnp.concatenate(ws_b, axis=1))
        qs.append(jnp.concatenate(qs_b, axis=1))
    v_new = u_all - jnp.concatenate(ws, axis=0)
    v_bd = _bd_wide(v_new)
    o_gdn = jnp.concatenate(qs, axis=0) + _mm(qk, v_bd)
    cb = jax.lax.broadcasted_iota(jnp.int32, (GDN_DK, CAT_W), 1) & (CHUNK - 1)
    upd = _mm(jnp.concatenate([jnp.where((cb ^ (b * ts)) < ts, kdec_t, 0.0) for b in seqs], axis=0), v_bd)
    for b in seqs:
        s_new = s_old[b] * jnp.exp(glast[b]) + upd[b * GDN_DK:(b + 1) * GDN_DK]
        for hh in range(GDN_HEADS):
            sout_ref[b, hh] = s_new[:, hh * GDN_DV:(hh + 1) * GDN_DV]
    og = []
    for hh in range(GDN_HEADS):
        oh = _rms(o_gdn[:, hh * GDN_DV:(hh + 1) * GDN_DV], gnorm_ref[...])
        og.append(oh * _silu(proj_cols(P_Z + hh * GDN_DV, GDN_DV)))

    tile8 = lambda x: jnp.concatenate([x] * nb, axis=0)
    rc, rs1, rs2 = tile8(rc_ref[...]), tile8(rs1_ref[...]), tile8(rs2_ref[...])
    k_own = _rope(proj_cols(P_KS, SWA_KV_W), rc, rs1, rs2)
    v_own = proj_cols(P_VS, SWA_KV_W)
    q_rows = [None] * SWA_HEADS
    for p in range(SWA_HEADS // 2):
        qp = _rope(proj_cols(P_QS + p * LANES, LANES), rc, rs1, rs2)
        q_rows[p] = jnp.where(lane_lo, qp, 0.0)
        q_rows[SWA_HEADS // 2 + p] = jnp.where(lane_lo, 0.0, qp)
    scale = SWA_HD ** -0.5
    s_own_all = _mm_nt(jnp.concatenate(q_rows, axis=0), _pad_rows(k_own)) * scale
    tok = r_t & (ts - 1)
    prev_mask = c_t > tok
    sink_col = jnp.zeros((rows, 1), F32)
    for hh in range(SWA_HEADS):
        sink_col = jnp.where((r_t[:, 0:1] ^ (hh * ts)) < ts, sinks_ref[hh], sink_col)
    pp_all, po_all, den_all = [], [], []
    for b in seqs:
        q_b = jnp.concatenate([grp(q_rows[hh], b) for hh in range(SWA_HEADS)], axis=0)
        sp = jnp.where(prev_mask, _mm_nt(q_b, k0_ref[b]) * scale, -jnp.inf)
        so_raw = jnp.concatenate([s_own_all[hh * rows + b * ts:hh * rows + (b + 1) * ts]
                                  for hh in range(SWA_HEADS)], axis=0)
        so = jnp.where(((c_t ^ (b * ts)) < ts) & ((c_t & (ts - 1)) <= tok), so_raw, -jnp.inf)
        m = jnp.maximum(jnp.maximum(jnp.max(sp, axis=-1, keepdims=True),
                                    jnp.max(so, axis=-1, keepdims=True)), sink_col)
        pp = jnp.exp(sp - m)
        po = jnp.exp(so - m)
        den_all.append(jnp.sum(pp, axis=-1, keepdims=True) + jnp.sum(po, axis=-1, keepdims=True)
                       + jnp.exp(sink_col - m))
        pp_all.append(_mm(pp, v0_ref[b]))
        po_all.append(po)
    pv_own = _mm(jnp.concatenate(po_all, axis=0), _pad_rows(v_own))
    os_rows = []
    for b in seqs:
        pv_b = (pp_all[b] + pv_own[b * rows:(b + 1) * rows]) / den_all[b]
        os_rows.append(jnp.concatenate(
            [jnp.where(lane_lo_t, pv_b[p * ts:(p + 1) * ts],
                       pv_b[(SWA_HEADS // 2 + p) * ts:(SWA_HEADS // 2 + p + 1) * ts])
             for p in range(SWA_HEADS // 2)], axis=1))
        kout_ref[b] = jnp.concatenate([k0_ref[b, ts:, :], grp(k_own, b)], axis=0)
        vout_ref[b] = jnp.concatenate([v0_ref[b, ts:, :], grp(v_own, b)], axis=0)
    o_full = jnp.concatenate(og + [jnp.concatenate(os_rows, axis=0)], axis=-1)
    for b in seqs:
        o_ref[b] = grp(o_full, b).astype(o_ref.dtype)


def _mixer_sample(proj3d, s0, conv0, k0, v0, convw, alog_row, dt_row, gnorm, sinks, rc, rs1, rs2):
    bsz, seq, _ = proj3d.shape
    nb = SAMPLE_NB
    assert seq == SAMPLE_T and bsz % nb == 0
    seq_blk = lambda shape: pl.BlockSpec((nb,) + shape, lambda i: (i,) + (0,) * len(shape))
    const_blk = lambda shape: pl.BlockSpec(shape, lambda i: (0,) * len(shape))
    return pl.pallas_call(
        _mixer_sample_kernel,
        grid=(bsz // nb,),
        in_specs=[
            pl.BlockSpec(memory_space=pltpu.SMEM),
            seq_blk((seq, P_W)),
            seq_blk((GDN_HEADS, GDN_DK, GDN_DV)),
            seq_blk((GDN_CONV - 1, GDN_CONV_W)),
            seq_blk((WINDOW, SWA_KV_W)),
            seq_blk((WINDOW, SWA_KV_W)),
            const_blk((GDN_CONV, GDN_CONV_W)),
            const_blk((1, LANES)),
            const_blk((1, LANES)),
            const_blk((1, GDN_DV)),
            const_blk((seq, LANES)),
            const_blk((seq, LANES)),
            const_blk((seq, LANES)),
        ],
        out_specs=[
            seq_blk((seq, D_MODEL)),
            seq_blk((GDN_HEADS, GDN_DK, GDN_DV)),
            seq_blk((GDN_CONV - 1, GDN_CONV_W)),
            seq_blk((WINDOW, SWA_KV_W)),
            seq_blk((WINDOW, SWA_KV_W)),
        ],
        out_shape=[
            jax.ShapeDtypeStruct((bsz, seq, D_MODEL), BF16),
            jax.ShapeDtypeStruct((bsz, GDN_HEADS, GDN_DK, GDN_DV), F32),
            jax.ShapeDtypeStruct((bsz, GDN_CONV - 1, GDN_CONV_W), F32),
            jax.ShapeDtypeStruct((bsz, WINDOW, SWA_KV_W), F32),
            jax.ShapeDtypeStruct((bsz, WINDOW, SWA_KV_W), F32),
        ],
        scratch_shapes=[pltpu.VMEM((2 * SAMPLE_T * nb, GDN_CONV_W), F32)],
        compiler_params=pltpu.CompilerParams(
            dimension_semantics=("arbitrary",), vmem_limit_bytes=VMEM_LIMIT_BYTES),
        name="mixer_sample",
    )(sinks, proj3d, s0, conv0, k0, v0, convw, alog_row, dt_row, gnorm, rc, rs1, rs2)


def _rope_tables(pos):
    half = ROT_DIM // 2
    inv = ROPE_THETA ** (-jnp.arange(half, dtype=F32) * 2.0 / ROT_DIM)
    ang = pos.astype(F32)[:, None] * inv[None, :]
    cos, sin = jnp.cos(ang), jnp.sin(ang)
    n = pos.shape[0]
    rest = SWA_HD - ROT_DIM
    c = jnp.concatenate([cos, cos, jnp.ones((n, rest), F32)], axis=1)
    s1 = jnp.concatenate([-sin, jnp.zeros((n, half + rest), F32)], axis=1)
    s2 = jnp.concatenate([jnp.zeros((n, half), F32), sin, jnp.zeros((n, rest), F32)], axis=1)
    reps = LANES // SWA_HD
    return jnp.tile(c, (1, reps)), jnp.tile(s1, (1, reps)), jnp.tile(s2, (1, reps))


def _pair_order():
    order = []
    for p in range(SWA_HEADS // 2):
        order += [p, SWA_HEADS // 2 + p]
    return np.asarray(order)


def _layer_weights(norm_ffn1, ffn1_gu, ffn1_down, norm_mix, w_in, conv_w, a_log, dt_bias, gdn_norm, sinks,
                   w_out, norm_ffn2, ffn2_gu, ffn2_down, norm_ple, ple_proj, ple_gate, norm_final):
    order = _pair_order()
    head_cols = (order[:, None] * SWA_HD + np.arange(SWA_HD)[None, :]).reshape(-1)
    o_qkv, o_z = 0, GDN_CONV_W
    o_a = o_z + GDN_V_W
    o_b = o_a + GDN_HEADS
    o_qs = o_b + GDN_HEADS
    o_ks = o_qs + SWA_Q_W
    o_vs = o_ks + SWA_KV_W
    win_p = jnp.concatenate([
        w_in[:, o_qkv:o_z], w_in[:, o_z:o_a], w_in[:, o_qs:o_ks][:, head_cols], w_in[:, o_ks:o_vs],
        w_in[:, o_vs:o_vs + SWA_KV_W], w_in[:, o_a:o_qs],
        jnp.zeros((D_MODEL, LANES - 2 * GDN_HEADS), w_in.dtype)], axis=1).astype(BF16)
    wout_p = jnp.concatenate([w_out[:GDN_V_W], w_out[GDN_V_W:][head_cols]], axis=0).astype(BF16)
    pad_row = lambda v: jnp.concatenate([v.astype(F32), jnp.zeros((LANES - v.shape[0],), F32)])[None, :]
    prep_cols = (np.asarray(PREP_TILE_ORDER)[:, None] * LANES + np.arange(LANES)[None, :]).reshape(-1)
    return dict(
        g1=norm_ffn1[None, :], wgu1=ffn1_gu.astype(BF16), wd1=ffn1_down.astype(BF16),
        gm=norm_mix[None, :], win=win_p, win_prep=win_p[:, prep_cols], convw=conv_w, alog=pad_row(a_log), dt=pad_row(dt_bias),
        gnorm=gdn_norm[None, :], sinks=sinks.astype(F32), wout=wout_p,
        g2=norm_ffn2[None, :], wgu2=ffn2_gu.astype(BF16), wd2=ffn2_down.astype(BF16),
        gp=norm_ple[None, :], wpp=ple_proj.astype(BF16), wpg=ple_gate.astype(BF16), gf=norm_final[None, :])


def _group(x, p, w, state, pos0):
    bsz, seq, _ = x.shape
    rc, rs1, rs2 = _rope_tables(pos0 + jnp.arange(seq))
    x2d = x.reshape(bsz * seq, D_MODEL)
    if state is None:
        h2d, prep2d, conv_new = _ffn_in_prep(x2d, seq, w["g1"], w["wgu1"], w["wd1"], w["gm"], w["win_prep"],
                                             w["convw"], w["alog"], w["dt"], rc, rs1, rs2)
        o, s_new, k_new, v_new = _mixer_prompt(prep2d.reshape(bsz, seq, P_W), w["gnorm"], w["sinks"])
    else:
        h2d, proj2d = _ffn_in(x2d, w["g1"], w["wgu1"], w["wd1"], w["gm"], w["win"])
        s0, conv0, k0, v0 = state
        o, s_new, conv_new, k_new, v_new = _mixer_sample(
            proj2d.reshape(bsz, seq, P_W), s0, conv0, k0, v0, w["convw"], w["alog"], w["dt"], w["gnorm"],
            w["sinks"], rc, rs1, rs2)
    y = _ffn_out(o.reshape(bsz * seq, D_MODEL), h2d, p.reshape(bsz * seq, PLE_DIM), w["wout"], w["g2"],
                 w["wgu2"], w["wd2"], w["gp"], w["wpg"], w["wpp"], w["gf"])
    kv_shape = (1, bsz, WINDOW, SWA_KV_HEADS, SWA_HD)
    return (y.reshape(bsz, seq, D_MODEL), s_new[None], conv_new[None], k_new.reshape(kv_shape),
            v_new.reshape(kv_shape))


def kernel(x_prompt, x_sample, state_gdn, state_conv, cache_swa_k, cache_swa_v, p_prompt, p_sample, norm_ffn1, ffn1_gu, ffn1_down, norm_mix, w_in, conv_w, a_log, dt_bias, gdn_norm, sinks, w_out, norm_ffn2, ffn2_gu, ffn2_down, norm_ple, ple_proj, ple_gate, norm_final):
    assert state_gdn.shape[0] == 1, "one layer"
    w = _layer_weights(norm_ffn1[0], ffn1_gu[0], ffn1_down[0], norm_mix[0], w_in[0], conv_w[0], a_log[0],
                       dt_bias[0], gdn_norm[0], sinks[0], w_out[0], norm_ffn2[0], ffn2_gu[0], ffn2_down[0],
                       norm_ple[0], ple_proj[0], ple_gate[0], norm_final)
    bs = x_sample.shape[0]
    yp, sg_p, sc_p, kk_p, vv_p = _group(x_prompt, p_prompt[0], w, None, 0)
    ys, sg_s, sc_s, kk_s, vv_s = _group(
        x_sample, p_sample[0], w,
        (state_gdn[0], state_conv[0], cache_swa_k[0].reshape(bs, WINDOW, SWA_KV_W),
         cache_swa_v[0].reshape(bs, WINDOW, SWA_KV_W)), PAST_LEN)
    return (yp, ys, sg_p, sc_p, kk_p, vv_p, sg_s, sc_s, kk_s, vv_s)
```

```python
import functools

import jax
import jax.numpy as jnp
from jax.experimental import pallas as pl
from jax.experimental.pallas import tpu as pltpu

F32 = jnp.float32
BF16 = jnp.bfloat16

D_MODEL = 1024
D_FF = 2816
PLE_DIM = 256
NORM_EPS = 1e-6
GDN_HEADS = 4
GDN_DK = 128
GDN_DV = 128
GDN_CONV = 4
GDN_QK_W = GDN_HEADS * GDN_DK
GDN_V_W = GDN_HEADS * GDN_DV
GDN_CONV_W = 2 * GDN_QK_W + GDN_V_W
SWA_HEADS = 8
SWA_KV_HEADS = 2
SWA_HD = 64
SWA_Q_W = SWA_HEADS * SWA_HD
SWA_KV_W = SWA_KV_HEADS * SWA_HD
WINDOW = 128
ROT_DIM = SWA_HD // 4
ROPE_THETA = 500000.0
PAST_LEN = 16384

LANES = 128
SUBLANES = 8
VMEM_LIMIT_BYTES = 56 * 1024 * 1024

P_QKV = 0
P_Z = P_QKV + GDN_CONV_W
P_QS = P_Z + GDN_V_W
P_KS = P_QS + SWA_Q_W
P_VS = P_KS + SWA_KV_W
P_AB = P_VS + SWA_KV_W
P_W = P_AB + LANES

FF_CHUNK = 256
TOKEN_TILE = 512

CHUNK = 64
CAT_W = GDN_HEADS * CHUNK
PREP_TILE_ORDER = tuple(t for pair in zip(range(12), range(12, 23)) for t in pair) + (11,)
SAMPLE_T = SUBLANES
SAMPLE_NB = CHUNK // SAMPLE_T


def _rms(x, g):
    return x * jax.lax.rsqrt(jnp.mean(x * x, axis=-1, keepdims=True) + NORM_EPS) * g


def _sigmoid(x):
    return 1.0 / (1.0 + jnp.exp(-x))


def _silu(x):
    return x * _sigmoid(x)


def _mm(a, b):
    return jnp.dot(a.astype(BF16), b.astype(BF16), preferred_element_type=F32)


def _mm_nt(a, b):
    return jax.lax.dot_general(a.astype(BF16), b.astype(BF16), (((1,), (1,)), ((), ())),
                               preferred_element_type=F32)


def _swiglu_acc(n_bf16, wgu_ref, wd_ref):
    rows = n_bf16.shape[0]
    acc = jnp.zeros((rows, D_MODEL), F32)
    for c in range(D_FF // FF_CHUNK):
        lo = c * FF_CHUNK
        gate = jnp.dot(n_bf16, wgu_ref[:, lo:lo + FF_CHUNK], preferred_element_type=F32)
        up = jnp.dot(n_bf16, wgu_ref[:, D_FF + lo:D_FF + lo + FF_CHUNK], preferred_element_type=F32)
        act = (_silu(gate) * up).astype(BF16)
        acc = acc + jnp.dot(act, wd_ref[lo:lo + FF_CHUNK, :], preferred_element_type=F32)
    return acc


def _ffn_in_kernel(x_ref, g1_ref, wgu_ref, wd_ref, gm_ref, win_ref, h_ref, proj_ref):
    x = x_ref[...]
    n1 = _rms(x, g1_ref[...]).astype(BF16)
    h = x + 0.5 * _swiglu_acc(n1, wgu_ref, wd_ref)
    h_ref[...] = h
    n = _rms(h, gm_ref[...]).astype(BF16)
    proj_ref[...] = jnp.dot(n, win_ref[...], preferred_element_type=F32)


def _const_spec(shape):
    return pl.BlockSpec(shape, lambda i: (0,) * len(shape), pipeline_mode=pl.Buffered(1))


def _ffn_in(x2d, g1, wgu, wd, gm, win):
    n_tok = x2d.shape[0]
    tm = min(TOKEN_TILE, n_tok)
    assert n_tok % tm == 0
    return pl.pallas_call(
        _ffn_in_kernel,
        grid=(n_tok // tm,),
        in_specs=[
            pl.BlockSpec((tm, D_MODEL), lambda i: (i, 0)),
            _const_spec((1, D_MODEL)),
            _const_spec((D_MODEL, 2 * D_FF)),
            _const_spec((D_FF, D_MODEL)),
            _const_spec((1, D_MODEL)),
            _const_spec((D_MODEL, P_W)),
        ],
        out_specs=[
            pl.BlockSpec((tm, D_MODEL), lambda i: (i, 0)),
            pl.BlockSpec((tm, P_W), lambda i: (i, 0)),
        ],
        out_shape=[
            jax.ShapeDtypeStruct((n_tok, D_MODEL), F32),
            jax.ShapeDtypeStruct((n_tok, P_W), F32),
        ],
        compiler_params=pltpu.CompilerParams(
            dimension_semantics=("arbitrary",), vmem_limit_bytes=VMEM_LIMIT_BYTES),
        name="ffn_in",
    )(x2d, g1, wgu, wd, gm, win)


def _ffn_in_prep_kernel(x_ref, g1_ref, wgu_ref, wd_ref, gm_ref, win_ref, convw_ref, alog_ref, dt_ref,
                        rc_ref, rs1_ref, rs2_ref, h_ref, prep_ref, convout_ref, n_st, xext_ref, *, tiles_per_seq):
    i = pl.program_id(0)
    tm = x_ref.shape[0]
    piece = 2 * LANES

    @pl.when(i == 0)
    def _init():
        n_st[...] = jnp.zeros(n_st.shape, n_st.dtype)
        xext_ref[...] = jnp.zeros(xext_ref.shape, F32)

    n_prev = n_st[...]
    first_of_seq = ((i - 1) % tiles_per_seq) == 0

    def prep_tile(lo, val):
        cols = slice(lo, lo + LANES)
        if lo < P_Z:
            hist = jnp.where(first_of_seq, 0.0, xext_ref[:, cols])
            xext_ref[:, cols] = val[tm - SUBLANES:tm]
            convout_ref[0, :, cols] = val[tm - (GDN_CONV - 1):tm]
            ext = jnp.concatenate([hist, val], axis=0)
            conv = val * convw_ref[GDN_CONV - 1:GDN_CONV, cols]
            for j in range(1, GDN_CONV):
                conv = conv + pltpu.roll(ext, j, 0)[SUBLANES:] * convw_ref[GDN_CONV - 1 - j:GDN_CONV - j, cols]
            act = _silu(conv)
            if lo < 2 * GDN_QK_W:
                scale = GDN_DK ** -0.5 if lo < GDN_QK_W else 1.0
                act = act * (jax.lax.rsqrt(jnp.sum(act * act, axis=-1, keepdims=True) + 1e-6) * scale)
            prep_ref[:, cols] = act
        elif lo < P_QS:
            prep_ref[:, cols] = _silu(val)
        elif lo < P_VS:
            prep_ref[:, cols] = _rope(val, rc_ref[...], rs1_ref[...], rs2_ref[...])
        elif lo < P_AB:
            prep_ref[:, cols] = val
        else:
            g_all, beta_all = _gates(val, alog_ref[...], dt_ref[...])
            lane = jax.lax.broadcasted_iota(jnp.int32, (tm, LANES), 1)
            prep_ref[:, cols] = jnp.where(lane < GDN_HEADS, g_all, beta_all)

    def proj_piece(lo):
        width = min(piece, P_W - lo)
        proj = jnp.dot(n_prev, win_ref[:, lo:lo + width], preferred_element_type=F32)
        for off in range(0, width, LANES):
            prep_tile(PREP_TILE_ORDER[(lo + off) // LANES] * LANES, proj[:, off:off + LANES])

    x = x_ref[...]
    n1 = _rms(x, g1_ref[...]).astype(BF16)
    acc = jnp.zeros((tm, D_MODEL), F32)
    for c in range(D_FF // FF_CHUNK):
        lo = c * FF_CHUNK
        gate = jnp.dot(n1, wgu_ref[:, lo:lo + FF_CHUNK], preferred_element_type=F32)
        up = jnp.dot(n1, wgu_ref[:, D_FF + lo:D_FF + lo + FF_CHUNK], preferred_element_type=F32)
        act = (_silu(gate) * up).astype(BF16)
        acc = acc + jnp.dot(act, wd_ref[lo:lo + FF_CHUNK, :], preferred_element_type=F32)
        proj_piece(c * piece)
    for lo in range((D_FF // FF_CHUNK) * piece, P_W, piece):
        proj_piece(lo)
    h = x + 0.5 * acc
    h_ref[...] = h
    n_st[...] = _rms(h, gm_ref[...]).astype(BF16)


def _ffn_in_prep(x2d, seq_len, g1, wgu, wd, gm, win, convw, alog_row, dt_row, rc, rs1, rs2):
    n_tok = x2d.shape[0]
    tm = TOKEN_TILE
    assert seq_len % tm == 0 and n_tok % seq_len == 0
    tiles_per_seq = seq_len // tm
    n_tiles = n_tok // tm
    cur = lambda i: jnp.minimum(i, n_tiles - 1)
    prev = lambda i: jnp.maximum(i - 1, 0)
    rope_spec = pl.BlockSpec((tm, LANES), lambda i: (prev(i) % tiles_per_seq, 0))
    return pl.pallas_call(
        functools.partial(_ffn_in_prep_kernel, tiles_per_seq=tiles_per_seq),
        grid=(n_tiles + 1,),
        in_specs=[
            pl.BlockSpec((tm, D_MODEL), lambda i: (cur(i), 0)),
            _const_spec((1, D_MODEL)),
            _const_spec((D_MODEL, 2 * D_FF)),
            _const_spec((D_FF, D_MODEL)),
            _const_spec((1, D_MODEL)),
            _const_spec((D_MODEL, P_W)),
            _const_spec((GDN_CONV, GDN_CONV_W)),
            _const_spec((1, LANES)),
            _const_spec((1, LANES)),
            rope_spec, rope_spec, rope_spec,
        ],
        out_specs=[
            pl.BlockSpec((tm, D_MODEL), lambda i: (cur(i), 0)),
            pl.BlockSpec((tm, P_W), lambda i: (prev(i), 0)),
            pl.BlockSpec((1, GDN_CONV - 1, GDN_CONV_W), lambda i: (prev(i) // tiles_per_seq, 0, 0)),
        ],
        out_shape=[
            jax.ShapeDtypeStruct((n_tok, D_MODEL), F32),
            jax.ShapeDtypeStruct((n_tok, P_W), F32),
            jax.ShapeDtypeStruct((n_tok // seq_len, GDN_CONV - 1, GDN_CONV_W), F32),
        ],
        scratch_shapes=[pltpu.VMEM((tm, D_MODEL), BF16), pltpu.VMEM((SUBLANES, GDN_CONV_W), F32)],
        compiler_params=pltpu.CompilerParams(
            dimension_semantics=("arbitrary",), vmem_limit_bytes=VMEM_LIMIT_BYTES),
        name="ffn_in_prep",
    )(x2d, g1, wgu, wd, gm, win, convw, alog_row, dt_row, rc, rs1, rs2)


def _ffn_out_kernel(o_ref, h_ref, p_ref, wout_ref, g2_ref, wgu_ref, wd_ref, gp_ref, wpg_ref, wpp_ref,
                    gf_ref, y_ref):
    h = h_ref[...] + jnp.dot(o_ref[...], wout_ref[...], preferred_element_type=F32)
    n2 = _rms(h, g2_ref[...]).astype(BF16)
    h = h + 0.5 * _swiglu_acc(n2, wgu_ref, wd_ref)
    npl = _rms(h, gp_ref[...]).astype(BF16)
    gate = _sigmoid(jnp.dot(npl, wpg_ref[...], preferred_element_type=F32))
    pe = jnp.dot(p_ref[...].astype(BF16), wpp_ref[...], preferred_element_type=F32)
    h = h + gate * pe
    y_ref[...] = _rms(h, gf_ref[...])


def _ffn_out(o2d, h2d, p2d, wout, g2, wgu, wd, gp, wpg, wpp, gf):
    n_tok = h2d.shape[0]
    tm = min(TOKEN_TILE, n_tok)
    assert n_tok % tm == 0
    return pl.pallas_call(
        _ffn_out_kernel,
        grid=(n_tok // tm,),
        in_specs=[
            pl.BlockSpec((tm, D_MODEL), lambda i: (i, 0)),
            pl.BlockSpec((tm, D_MODEL), lambda i: (i, 0)),
            pl.BlockSpec((tm, PLE_DIM), lambda i: (i, 0)),
            _const_spec((D_MODEL, D_MODEL)),
            _const_spec((1, D_MODEL)),
            _const_spec((D_MODEL, 2 * D_FF)),
            _const_spec((D_FF, D_MODEL)),
            _const_spec((1, D_MODEL)),
            _const_spec((D_MODEL, D_MODEL)),
            _const_spec((PLE_DIM, D_MODEL)),
            _const_spec((1, D_MODEL)),
        ],
        out_specs=pl.BlockSpec((tm, D_MODEL), lambda i: (i, 0)),
        out_shape=jax.ShapeDtypeStruct((n_tok, D_MODEL), F32),
        compiler_params=pltpu.CompilerParams(
            dimension_semantics=("arbitrary",), vmem_limit_bytes=VMEM_LIMIT_BYTES),
        name="ffn_out",
    )(o2d, h2d, p2d, wout, g2, wgu, wd, gp, wpg, wpp, gf)


def _pad_rows(x):
    rows = x.shape[0]
    if rows == LANES:
        return x
    return jnp.concatenate([x, jnp.zeros((LANES - rows, x.shape[1]), x.dtype)], axis=0)


def _split3(x):
    hi = x.astype(BF16)
    r = x - hi.astype(F32)
    mid = r.astype(BF16)
    lo = (r - mid.astype(F32)).astype(BF16)
    return hi, mid, lo


def _mm_exact_rhs(a_bf16, x):
    hi, mid, lo = _split3(x)
    return (jnp.dot(a_bf16, hi, preferred_element_type=F32) + jnp.dot(a_bf16, mid, preferred_element_type=F32)
            + jnp.dot(a_bf16, lo, preferred_element_type=F32))


def _mm_exact_lhs(x, a_bf16):
    hi, mid, lo = _split3(x)
    return (jnp.dot(hi, a_bf16, preferred_element_type=F32) + jnp.dot(mid, a_bf16, preferred_element_type=F32)
            + jnp.dot(lo, a_bf16, preferred_element_type=F32))


def _bd_cat(y, lane_lo_chunk):
    zero_tile = jnp.zeros((CHUNK, LANES), F32)
    blocks = []
    for hh in range(GDN_HEADS):
        y_tile = y[:, (hh // 2) * LANES:(hh // 2 + 1) * LANES]
        keep = jnp.where(lane_lo_chunk, y_tile, 0.0) if hh % 2 == 0 else jnp.where(lane_lo_chunk, 0.0, y_tile)
        blocks.append(jnp.concatenate([keep, zero_tile] if hh < 2 else [zero_tile, keep], axis=1))
    return jnp.concatenate(blocks, axis=0).astype(BF16)


def _bd_wide(y):
    zero_tile = jnp.zeros((CHUNK, LANES), F32)
    blocks = []
    for hh in range(GDN_HEADS):
        tiles = [zero_tile] * GDN_HEADS
        tiles[hh] = y[:, hh * LANES:(hh + 1) * LANES]
        blocks.append(jnp.concatenate(tiles, axis=1))
    return jnp.concatenate(blocks, axis=0).astype(BF16)


def _bd_pair(sa, sb):
    zeros = jnp.zeros(sa.shape, sa.dtype)
    return jnp.concatenate([jnp.concatenate([sa, zeros], axis=1), jnp.concatenate([zeros, sb], axis=1)], axis=0)


def _gates(ab, alog_row, dt_row):
    zab = ab + dt_row
    softplus = jnp.maximum(zab, 0.0) + jnp.log(1.0 + jnp.exp(-jnp.abs(zab)))
    return -jnp.exp(alog_row) * softplus, _sigmoid(ab)


def _head_l2(x):
    parts = []
    for hh in range(GDN_HEADS):
        xh = x[:, hh * GDN_DK:(hh + 1) * GDN_DK]
        parts.append(xh * jax.lax.rsqrt(jnp.sum(xh * xh, axis=-1, keepdims=True) + 1e-6))
    return jnp.concatenate(parts, axis=1)


def _lane_expand(cols, first_lane, rows):
    return jnp.concatenate(
        [jnp.broadcast_to(cols[:, first_lane + hh:first_lane + hh + 1], (rows, LANES))
         for hh in range(GDN_HEADS)], axis=1)


def _rope(x, rc, rs1, rs2):
    return x * rc + pltpu.roll(x, LANES - ROT_DIM // 2, 1) * rs1 + pltpu.roll(x, ROT_DIM // 2, 1) * rs2


def _uw_rhs(vb_all, kbg_all):
    return jnp.concatenate(
        [jnp.concatenate([vb_all[:, hh * GDN_DV:(hh + 1) * GDN_DV],
                          kbg_all[:, hh * GDN_DK:(hh + 1) * GDN_DK]], axis=1)
         for hh in range(GDN_HEADS)], axis=0)


def _split_uw(uw):
    u_all = jnp.concatenate([uw[hh * CHUNK:(hh + 1) * CHUNK, 0:GDN_DV] for hh in range(GDN_HEADS)], axis=1)
    w_all = jnp.concatenate([uw[hh * CHUNK:(hh + 1) * CHUNK, GDN_DV:] for hh in range(GDN_HEADS)], axis=1)
    return u_all, w_all


def _stack_heads_t(x):
    return jnp.transpose(jnp.concatenate(
        [x[:, hh * GDN_DK:(hh + 1) * GDN_DK] for hh in range(GDN_HEADS)], axis=0))


PROMPT_NS = 4


def _mixer_prompt_kernel(sinks_ref, cur_ref, prev_ref, gnorm_ref, o_ref, sout_ref, kout_ref, vout_ref,
                         s_ref, kprev_ref, vprev_ref, gcc_st, gcr_st, beta_st, os_st, *, tb, n_blocks):
    t = pl.program_id(1)
    ns = PROMPT_NS
    nch = tb // CHUNK
    stages = (gcc_st, gcr_st, beta_st, os_st)
    stage_names = ("gcc", "gcr", "beta", "os")

    @pl.when(t == 0)
    def _init():
        for ref in (s_ref, kprev_ref, vprev_ref) + stages:
            ref[...] = jnp.zeros(ref.shape, F32)

    r_t = jax.lax.broadcasted_iota(jnp.int32, (tb, LANES), 0)
    c_t = jax.lax.broadcasted_iota(jnp.int32, (tb, LANES), 1)
    lane_lo = c_t < CHUNK
    lane_lo_chunk = jax.lax.broadcasted_iota(jnp.int32, (CHUNK, LANES), 1) < CHUNK
    lane_lo_row = jax.lax.broadcasted_iota(jnp.int32, (1, LANES), 1) < CHUNK
    same_chunk = (r_t ^ c_t) < CHUNK
    rk = jax.lax.broadcasted_iota(jnp.int32, (tb, 2 * WINDOW), 0)
    ck = jax.lax.broadcasted_iota(jnp.int32, (tb, 2 * WINDOW), 1)
    first_off = jnp.where(t > 0, 0, WINDOW)
    dist = ck - rk
    kmask = ((dist > jnp.where(ck < WINDOW, first_off, -WINDOW))
             & (dist <= jnp.where(ck < WINDOW, 2 * WINDOW, WINDOW)))

    s_cat = [s_ref[sq] for sq in range(ns)]
    k_prev = [kprev_ref[sq] for sq in range(ns)]
    v_prev = [vprev_ref[sq] for sq in range(ns)]
    staged = [{name: ref[sq] for ref, name in zip(stages, stage_names)} for sq in range(ns)]
    for sq in range(ns):
        staged[sq].update(q=prev_ref[sq, :, 0:GDN_QK_W], k=prev_ref[sq, :, GDN_QK_W:2 * GDN_QK_W],
                          v=prev_ref[sq, :, 2 * GDN_QK_W:GDN_CONV_W], zs=prev_ref[sq, :, P_Z:P_Z + GDN_V_W])
    fr = [{} for _ in range(ns)]

    def front(sq):
        f = fr[sq]
        ab = cur_ref[sq, :, P_AB:P_AB + LANES]
        f["beta"] = ab
        f["gcc"] = _mm_exact_rhs(((c_t <= r_t) & same_chunk).astype(BF16), ab)
        f["gcr"] = _mm_exact_lhs(jnp.transpose(ab)[0:SUBLANES, :],
                                 ((r_t <= c_t) & same_chunk).astype(BF16))
        k_own = cur_ref[sq, :, P_KS:P_KS + SWA_KV_W]
        v_own = cur_ref[sq, :, P_VS:P_VS + SWA_KV_W]
        f["k_own"], f["v_own"] = k_own, v_own
        k_cat = jnp.concatenate([k_prev[sq], k_own], axis=0)
        v_cat = jnp.concatenate([v_prev[sq], v_own], axis=0).astype(BF16)
        q_rows = [None] * SWA_HEADS
        for p in range(SWA_HEADS // 2):
            qp = cur_ref[sq, :, P_QS + p * LANES:P_QS + (p + 1) * LANES]
            q_rows[p] = jnp.where(lane_lo, qp, 0.0)
            q_rows[SWA_HEADS // 2 + p] = jnp.where(lane_lo, 0.0, qp)
        s_all = _mm_nt(jnp.concatenate(q_rows, axis=0), k_cat) * (SWA_HD ** -0.5)
        yield
        pv = [None] * SWA_HEADS
        for hh in range(SWA_HEADS):
            sh = jnp.where(kmask, s_all[hh * tb:(hh + 1) * tb], -jnp.inf)
            sk = sinks_ref[hh]
            m = jnp.maximum(jnp.max(sh, axis=-1, keepdims=True), sk)
            ph = jnp.exp(sh - m)
            den = jnp.sum(ph, axis=-1, keepdims=True) + jnp.exp(sk - m)
            pv[hh] = jnp.dot(ph.astype(BF16), v_cat, preferred_element_type=F32) / den
            yield
        f["os"] = jnp.concatenate(
            [jnp.where(lane_lo, pv[p], pv[SWA_HEADS // 2 + p]) for p in range(SWA_HEADS // 2)], axis=1)

    fronts = [front(sq) for sq in range(ns)]

    def fill():
        for gen in fronts:
            next(gen, None)

    units = [(sq, c) for sq in range(ns) for c in range(nch)]
    rows_of = lambda c: slice(c * CHUNK, (c + 1) * CHUNK)
    ri = jax.lax.broadcasted_iota(jnp.int32, (CHUNK, CAT_W), 0)
    ci = jax.lax.broadcasted_iota(jnp.int32, (CHUNK, CAT_W), 1)
    cj = ci & (CHUNK - 1)
    tri = cj <= ri
    strict = cj < ri
    eye = (cj == ri).astype(F32)
    blk = ri ^ cj
    bd = functools.partial(_bd_cat, lane_lo_chunk=lane_lo_chunk)

    drv = []
    for sq in range(ns):
        st = staged[sq]
        gc_full = _lane_expand(st["gcc"], 0, tb)
        beta_full = _lane_expand(st["beta"], GDN_HEADS, tb)
        egc_full = jnp.exp(gc_full)
        kb_all = st["k"] * beta_full
        drv.append(dict(gc_full=gc_full, kb=kb_all, vb=st["v"] * beta_full, kbg=kb_all * egc_full,
                        qd=st["q"] * egc_full, gcr=st["gcr"], gcr_sw=pltpu.roll(st["gcr"], CHUNK, 1)))
    fill()
    a_mat, qk, xinv, pw = {}, {}, {}, {}
    for u in units:
        sq, c = u
        d, rows = drv[sq], rows_of(c)
        gcc_pairs, gcr_pairs = [], []
        for p in range(GDN_HEADS // 2):
            gcc_pairs.append(jnp.where(lane_lo_chunk, d["gc_full"][rows, (2 * p) * LANES:(2 * p + 1) * LANES],
                                       d["gc_full"][rows, (2 * p + 1) * LANES:(2 * p + 2) * LANES]))
            ra = (d["gcr"] if c == 0 else d["gcr_sw"])[2 * p:2 * p + 1, :]
            rb_ = (d["gcr_sw"] if c == 0 else d["gcr"])[2 * p + 1:2 * p + 2, :]
            gcr_pairs.append(jnp.where(lane_lo_row, ra, rb_))
        gcc = jnp.concatenate(gcc_pairs, axis=1)
        gcr = jnp.concatenate(gcr_pairs, axis=1)
        decay = jnp.exp(jnp.where(tri, gcc - gcr, -jnp.inf))
        kq = jax.lax.dot_general(
            jnp.concatenate([d["kb"][rows], staged[sq]["q"][rows]], axis=0).astype(BF16),
            _bd_wide(staged[sq]["k"][rows]), (((1,), (1,)), ((), ())), preferred_element_type=F32)
        a_mat[u] = jnp.where(strict, kq[0:CHUNK] * decay, 0.0)
        qk[u] = kq[CHUNK:] * decay
    fill()
    for u in units:
        a0 = jnp.where(blk < SUBLANES, a_mat[u], 0.0)
        xinv[u] = eye - a0
        pw[u] = _mm(a0, bd(a0))
    fill()
    for u in units:
        pw_bd = bd(pw[u])
        xinv[u] = xinv[u] + _mm(xinv[u], pw_bd)
        pw[u] = _mm(pw[u], pw_bd)
    fill()
    for u in units:
        xinv[u] = xinv[u] + _mm(xinv[u], bd(pw[u]))
    fill()
    s = SUBLANES
    while s < CHUNK:
        y = {u: _mm(jnp.where((blk < 2 * s) & (blk >= s), a_mat[u], 0.0), bd(xinv[u])) for u in units}
        fill()
        for u in units:
            xinv[u] = xinv[u] - _mm(xinv[u], bd(y[u]))
        fill()
        s *= 2
    prep = {}
    for u in units:
        sq, c = u
        d, rows = drv[sq], rows_of(c)
        u_all, w_all = _split_uw(_mm(bd(xinv[u]), _uw_rhs(d["vb"][rows], d["kbg"][rows])))
        glast = d["gc_full"][c * CHUNK + CHUNK - 1:c * CHUNK + CHUNK, :]
        kdec_t = _stack_heads_t(staged[sq]["k"][rows] * jnp.exp(glast - d["gc_full"][rows]))
        prep[u] = (u_all, w_all, kdec_t, jnp.exp(glast))
    fill()

    o_chunks = [[] for _ in range(ns)]
    for c in range(nch):
        rows = rows_of(c)
        ws, qs = [[] for _ in range(ns)], [[] for _ in range(ns)]
        for sq in range(ns):
            w_all = prep[(sq, c)][1]
            for p in range(GDN_HEADS // 2):
                cols = slice(2 * p * LANES, (2 * p + 2) * LANES)
                r = _mm(jnp.concatenate([w_all[:, cols], drv[sq]["qd"][rows, cols]], axis=0),
                        _bd_pair(s_cat[sq][:, (2 * p) * LANES:(2 * p + 1) * LANES],
                                 s_cat[sq][:, (2 * p + 1) * LANES:(2 * p + 2) * LANES]))
                ws[sq].append(r[0:CHUNK])
                qs[sq].append(r[CHUNK:])
        fill()
        for sq in range(ns):
            u_all, _, kdec_t, eg_last = prep[(sq, c)]
            v_new = u_all - jnp.concatenate(ws[sq], axis=1)
            r2 = _mm(jnp.concatenate([qk[(sq, c)], kdec_t], axis=0), _bd_wide(v_new))
            o_chunks[sq].append(jnp.concatenate(qs[sq], axis=1) + r2[0:CHUNK])
            s_cat[sq] = s_cat[sq] * eg_last + r2[CHUNK:]
        fill()
    for gen in fronts:
        for _ in gen:
            pass

    for sq in range(ns):
        o_gdn = jnp.concatenate(o_chunks[sq], axis=0)
        zs_prev = staged[sq]["zs"]
        og = [_rms(o_gdn[:, hh * GDN_DV:(hh + 1) * GDN_DV], gnorm_ref[...])
              * zs_prev[:, hh * GDN_DV:(hh + 1) * GDN_DV] for hh in range(GDN_HEADS)]
        o_ref[sq] = jnp.concatenate(og + [staged[sq]["os"]], axis=-1).astype(o_ref.dtype)
        s_ref[sq] = s_cat[sq]
        kprev_ref[sq] = fr[sq]["k_own"]
        vprev_ref[sq] = fr[sq]["v_own"]
        for ref, name in zip(stages, stage_names):
            ref[sq] = fr[sq][name]

    @pl.when(t == n_blocks - 1)
    def _window_out():
        for sq in range(ns):
            kout_ref[sq] = fr[sq]["k_own"]
            vout_ref[sq] = fr[sq]["v_own"]

    @pl.when(t == n_blocks)
    def _state_out():
        for sq in range(ns):
            for hh in range(GDN_HEADS):
                sout_ref[sq, hh] = s_ref[sq, :, hh * GDN_DV:(hh + 1) * GDN_DV]


def _mixer_prompt(prep3d, gnorm, sinks):
    bsz, seq, _ = prep3d.shape
    tb, ns = WINDOW, PROMPT_NS
    assert seq % tb == 0 and bsz % ns == 0
    n_blocks = seq // tb
    kern = functools.partial(_mixer_prompt_kernel, tb=tb, n_blocks=n_blocks)
    seq_blk = lambda shape: pl.BlockSpec((ns,) + shape, lambda b, t: (b,) + (0,) * len(shape))
    vmem = lambda *shape: pltpu.VMEM((ns,) + shape, F32)
    return pl.pallas_call(
        kern,
        grid=(bsz // ns, n_blocks + 1),
        in_specs=[
            pl.BlockSpec(memory_space=pltpu.SMEM),
            pl.BlockSpec((ns, tb, P_W), lambda b, t: (b, jnp.minimum(t, n_blocks - 1), 0)),
            pl.BlockSpec((ns, tb, P_W), lambda b, t: (b, jnp.maximum(t - 1, 0), 0)),
            pl.BlockSpec((1, GDN_DV), lambda b, t: (0, 0)),
        ],
        out_specs=[
            pl.BlockSpec((ns, tb, D_MODEL), lambda b, t: (b, jnp.maximum(t - 1, 0), 0)),
            seq_blk((GDN_HEADS, GDN_DK, GDN_DV)),
            seq_blk((WINDOW, SWA_KV_W)),
            seq_blk((WINDOW, SWA_KV_W)),
        ],
        out_shape=[
            jax.ShapeDtypeStruct((bsz, seq, D_MODEL), BF16),
            jax.ShapeDtypeStruct((bsz, GDN_HEADS, GDN_DK, GDN_DV), F32),
            jax.ShapeDtypeStruct((bsz, WINDOW, SWA_KV_W), F32),
            jax.ShapeDtypeStruct((bsz, WINDOW, SWA_KV_W), F32),
        ],
        scratch_shapes=[
            vmem(GDN_DK, GDN_HEADS * GDN_DV),
            vmem(WINDOW, SWA_KV_W),
            vmem(WINDOW, SWA_KV_W),
            vmem(tb, LANES),
            vmem(SUBLANES, LANES),
            vmem(tb, LANES),
            vmem(tb, SWA_Q_W),
        ],
        compiler_params=pltpu.CompilerParams(
            dimension_semantics=("arbitrary", "arbitrary"), vmem_limit_bytes=VMEM_LIMIT_BYTES),
        name="mixer_prompt",
    )(sinks, prep3d, prep3d, gnorm)


def _mixer_sample_kernel(sinks_ref, proj_ref, s0_ref, conv0_ref, k0_ref, v0_ref, convw_ref, alog_ref, dt_ref,
                         gnorm_ref, rc_ref, rs1_ref, rs2_ref,
                         o_ref, sout_ref, convout_ref, kout_ref, vout_ref, xext_ref):
    nb, ts, rows = SAMPLE_NB, SAMPLE_T, CHUNK
    seqs = range(nb)
    grp = lambda x, b: x[b * ts:(b + 1) * ts]

    hist = SUBLANES - (GDN_CONV - 1)
    conv_parts = []
    for b in seqs:
        xext_ref[2 * ts * b + hist:2 * ts * b + ts, :] = conv0_ref[b]
        xext_ref[2 * ts * b + ts:2 * ts * (b + 1), :] = proj_ref[b, :, P_QKV:P_QKV + GDN_CONV_W]
    for b in seqs:
        base = 2 * ts * b + hist
        conv = xext_ref[base:base + ts, :] * convw_ref[0:1, :]
        for j in range(1, GDN_CONV):
            conv = conv + xext_ref[base + j:base + j + ts, :] * convw_ref[j:j + 1, :]
        conv_parts.append(conv)
        convout_ref[b] = xext_ref[2 * ts * (b + 1) - (GDN_CONV - 1):2 * ts * (b + 1), :]
    qkv = _silu(jnp.concatenate(conv_parts, axis=0))
    q_all = _head_l2(qkv[:, 0:GDN_QK_W]) * (GDN_DK ** -0.5)
    k_all = _head_l2(qkv[:, GDN_QK_W:2 * GDN_QK_W])
    v_all = qkv[:, 2 * GDN_QK_W:]

    def proj_cols(lo, width):
        return jnp.concatenate([proj_ref[b, :, lo:lo + width] for b in seqs], axis=0)

    g_all, beta_all = _gates(proj_cols(P_AB, LANES), alog_ref[...], dt_ref[...])

    r_t = jax.lax.broadcasted_iota(jnp.int32, (rows, LANES), 0)
    c_t = jax.lax.broadcasted_iota(jnp.int32, (rows, LANES), 1)
    r_p = jax.lax.broadcasted_iota(jnp.int32, (LANES, LANES), 0)
    c_p = jax.lax.broadcasted_iota(jnp.int32, (LANES, LANES), 1)
    g_pad = _pad_rows(g_all)
    gc_cols = _mm_exact_rhs(((c_t <= r_t) & ((r_t ^ c_t) < ts)).astype(BF16), g_pad)
    gc_rows = _mm_exact_lhs(jnp.transpose(g_pad)[0:SUBLANES, :],
                            ((r_p <= c_p) & ((r_p ^ c_p) < ts)).astype(BF16))
    gc_rows_sw = pltpu.roll(gc_rows, CHUNK, 1)

    lane_lo = c_t < CHUNK
    lane_lo_row = jax.lax.broadcasted_iota(jnp.int32, (1, LANES), 1) < CHUNK
    lane_lo_t = jax.lax.broadcasted_iota(jnp.int32, (ts, LANES), 1) < CHUNK
    gc_full = _lane_expand(gc_cols, 0, rows)
    beta_full = _lane_expand(beta_all, GDN_HEADS, rows)
    egc_full = jnp.exp(gc_full)
    kb_all = k_all * beta_full
    vb_all = v_all * beta_full
    kbg_all = kb_all * egc_full
    qd_all = q_all * egc_full

    ri = jax.lax.broadcasted_iota(jnp.int32, (rows, CAT_W), 0)
    ci = jax.lax.broadcasted_iota(jnp.int32, (rows, CAT_W), 1)
    cj = ci & (CHUNK - 1)
    same_seq = (ri ^ cj) < ts
    eye = (cj == ri).astype(F32)
    bd = functools.partial(_bd_cat, lane_lo_chunk=lane_lo)

    gcc = jnp.concatenate(
        [jnp.where(lane_lo, gc_full[:, (2 * p) * LANES:(2 * p + 1) * LANES],
                   gc_full[:, (2 * p + 1) * LANES:(2 * p + 2) * LANES]) for p in range(GDN_HEADS // 2)], axis=1)
    gcr = jnp.concatenate(
        [jnp.where(lane_lo_row, gc_rows[2 * p:2 * p + 1, :], gc_rows_sw[2 * p + 1:2 * p + 2, :])
         for p in range(GDN_HEADS // 2)], axis=1)
    decay = jnp.exp(jnp.where((cj <= ri) & same_seq, gcc - gcr, -jnp.inf))
    kq = jax.lax.dot_general(
        jnp.concatenate([kb_all, q_all], axis=0).astype(BF16), _bd_wide(k_all),
        (((1,), (1,)), ((), ())), preferred_element_type=F32)
    a_mat = jnp.where(cj < ri, kq[0:rows] * decay, 0.0)
    qk = kq[rows:] * decay
    xinv = eye - a_mat
    pw = _mm(a_mat, bd(a_mat))
    pw_bd = bd(pw)
    xinv = xinv + _mm(xinv, pw_bd)
    pw = _mm(pw, pw_bd)
    xinv = xinv + _mm(xinv, bd(pw))
    u_all, w_all = _split_uw(_mm(bd(xinv), _uw_rhs(vb_all, kbg_all)))
    glast = [gc_full[(b + 1) * ts - 1:(b + 1) * ts, :] for b in seqs]
    glast_rows = jnp.concatenate([jnp.broadcast_to(glast[b], (ts, GDN_QK_W)) for b in seqs], axis=0)
    kdec_t = _stack_heads_t(k_all * jnp.exp(glast_rows - gc_full))

    s_old, ws, qs = [], [], []
    for b in seqs:
        s_b = jnp.concatenate([s0_ref[b, hh] for hh in range(GDN_HEADS)], axis=1)
        s_old.append(s_b)
        ws_b, qs_b = [], []
        for p in range(GDN_HEADS // 2):
            cols = slice(2 * p * LANES, (2 * p + 2) * LANES)
            r = _mm(jnp.concatenate([grp(w_all, b)[:, cols], grp(qd_all, b)[:, cols]], axis=0),
                    _bd_pair(s_b[:, (2 * p) * LANES:(2 * p + 1) * LANES],
                             s_b[:, (2 * p + 1) * LANES:(2 * p + 2) * LANES]))
            ws_b.append(r[0:ts])
            qs_b.append(r[ts:])
        ws.append(jnp.concatenate(ws_b, axis=1))
        qs.append(jnp.concatenate(qs_b, axis=1))
    v_new = u_all - jnp.concatenate(ws, axis=0)
    v_bd = _bd_wide(v_new)
    o_gdn = jnp.concatenate(qs, axis=0) + _mm(qk, v_bd)
    cb = jax.lax.broadcasted_iota(jnp.int32, (GDN_DK, CAT_W), 1) & (CHUNK - 1)
    upd = _mm(jnp.concatenate([jnp.where((cb ^ (b * ts)) < ts, kdec_t, 0.0) for b in seqs], axis=0), v_bd)
    for b in seqs:
        s_new = s_old[b] * jnp.exp(glast[b]) + upd[b * GDN_DK:(b + 1) * GDN_DK]
        for hh in range(GDN_HEADS):
            sout_ref[b, hh] = s_new[:, hh * GDN_DV:(hh + 1) * GDN_DV]
    og = []
    for hh in range(GDN_HEADS):
        oh = _rms(o_gdn[:, hh * GDN_DV:(hh + 1) * GDN_DV], gnorm_ref[...])
        og.append(oh * _silu(proj_cols(P_Z + hh * GDN_DV, GDN_DV)))

    tile8 = lambda x: jnp.concatenate([x] * nb, axis=0)
    rc, rs1, rs2 = tile8(rc_ref[...]), tile8(rs1_ref[...]), tile8(rs2_ref[...])
    k_own = _rope(proj_cols(P_KS, SWA_KV_W), rc, rs1, rs2)
    v_own = proj_cols(P_VS, SWA_KV_W)
    q_rows = [None] * SWA_HEADS
    for p in range(SWA_HEADS // 2):
        qp = _rope(proj_cols(P_QS + p * LANES, LANES), rc, rs1, rs2)
        q_rows[p] = jnp.where(lane_lo, qp, 0.0)
        q_rows[SWA_HEADS // 2 + p] = jnp.where(lane_lo, 0.0, qp)
    scale = SWA_HD ** -0.5
    s_own_all = _mm_nt(jnp.concatenate(q_rows, axis=0), _pad_rows(k_own)) * scale
    tok = r_t & (ts - 1)
    prev_mask = c_t > tok
    sink_col = jnp.zeros((rows, 1), F32)
    for hh in range(SWA_HEADS):
        sink_col = jnp.where((r_t[:, 0:1] ^ (hh * ts)) < ts, sinks_ref[hh], sink_col)
    pp_all, po_all, den_all = [], [], []
    for b in seqs:
        q_b = jnp.concatenate([grp(q_rows[hh], b) for hh in range(SWA_HEADS)], axis=0)
        sp = jnp.where(prev_mask, _mm_nt(q_b, k0_ref[b]) * scale, -jnp.inf)
        so_raw = jnp.concatenate([s_own_all[hh * rows + b * ts:hh * rows + (b + 1) * ts]
                                  for hh in range(SWA_HEADS)], axis=0)
        so = jnp.where(((c_t ^ (b * ts)) < ts) & ((c_t & (ts - 1)) <= tok), so_raw, -jnp.inf)
        m = jnp.maximum(jnp.maximum(jnp.max(sp, axis=-1, keepdims=True),
                                    jnp.max(so, axis=-1, keepdims=True)), sink_col)
        pp = jnp.exp(sp - m)
        po = jnp.exp(so - m)
        den_all.append(jnp.sum(pp, axis=-1, keepdims=True) + jnp.sum(po, axis=-1, keepdims=True)
                       + jnp.exp(sink_col - m))
        pp_all.append(_mm(pp, v0_ref[b]))
        po_all.append(po)
    pv_own = _mm(jnp.concatenate(po_all, axis=0), _pad_rows(v_own))
    os_rows = []
    for b in seqs:
        pv_b = (pp_all[b] + pv_own[b * rows:(b + 1) * rows]) / den_all[b]
        os_rows.append(jnp.concatenate(
            [jnp.where(lane_lo_t, pv_b[p * ts:(p + 1) * ts],
                       pv_b[(SWA_HEADS // 2 + p) * ts:(SWA_HEADS // 2 + p + 1) * ts])
             for p in range(SWA_HEADS // 2)], axis=1))
        kout_ref[b] = jnp.concatenate([k0_ref[b, ts:, :], grp(k_own, b)], axis=0)
        vout_ref[b] = jnp.concatenate([v0_ref[b, ts:, :], grp(v_own, b)], axis=0)
    o_full = jnp.concatenate(og + [jnp.concatenate(os_rows, axis=0)], axis=-1)
    for b in seqs:
        o_ref[b] = grp(o_full, b).astype(o_ref.dtype)


def _mixer_sample(proj3d, s0, conv0, k0, v0, convw, alog_row, dt_row, gnorm, sinks, rc, rs1, rs2):
    bsz, seq, _ = proj3d.shape
    nb = SAMPLE_NB
    assert seq == SAMPLE_T and bsz % nb == 0
    seq_blk = lambda shape: pl.BlockSpec((nb,) + shape, lambda i: (i,) + (0,) * len(shape))
    const_blk = lambda shape: pl.BlockSpec(shape, lambda i: (0,) * len(shape))
    return pl.pallas_call(
        _mixer_sample_kernel,
        grid=(bsz // nb,),
        in_specs=[
            pl.BlockSpec(memory_space=pltpu.SMEM),
            seq_blk((seq, P_W)),
            seq_blk((GDN_HEADS, GDN_DK, GDN_DV)),
            seq_blk((GDN_CONV - 1, GDN_CONV_W)),
            seq_blk((WINDOW, SWA_KV_W)),
            seq_blk((WINDOW, SWA_KV_W)),
            const_blk((GDN_CONV, GDN_CONV_W)),
            const_blk((1, LANES)),
            const_blk((1, LANES)),
            const_blk((1, GDN_DV)),
            const_blk((seq, LANES)),
            const_blk((seq, LANES)),
            const_blk((seq, LANES)),
        ],
        out_specs=[
            seq_blk((seq, D_MODEL)),
            seq_blk((GDN_HEADS, GDN_DK, GDN_DV)),
            seq_blk((GDN_CONV - 1, GDN_CONV_W)),
            seq_blk((WINDOW, SWA_KV_W)),
            seq_blk((WINDOW, SWA_KV_W)),
        ],
        out_shape=[
            jax.ShapeDtypeStruct((bsz, seq, D_MODEL), BF16),
            jax.ShapeDtypeStruct((bsz, GDN_HEADS, GDN_DK, GDN_DV), F32),
            jax.ShapeDtypeStruct((bsz, GDN_CONV - 1, GDN_CONV_W), F32),
            jax.ShapeDtypeStruct((bsz, WINDOW, SWA_KV_W), F32),
            jax.ShapeDtypeStruct((bsz, WINDOW, SWA_KV_W), F32),
        ],
        scratch_shapes=[pltpu.VMEM((2 * SAMPLE_T * nb, GDN_CONV_W), F32)],
        compiler_params=pltpu.CompilerParams(
            dimension_semantics=("arbitrary",), vmem_limit_bytes=VMEM_LIMIT_BYTES),
        name="mixer_sample",
    )(sinks, proj3d, s0, conv0, k0, v0, convw, alog_row, dt_row, gnorm, rc, rs1, rs2)


def _rope_tables(pos):
    half = ROT_DIM // 2
    inv = ROPE_THETA ** (-jnp.arange(half, dtype=F32) * 2.0 / ROT_DIM)
    ang = pos.astype(F32)[:, None] * inv[None, :]
    cos, sin = jnp.cos(ang), jnp.sin(ang)
    n = pos.shape[0]
    rest = SWA_HD - ROT_DIM
    c = jnp.concatenate([cos, cos, jnp.ones((n, rest), F32)], axis=1)
    s1 = jnp.concatenate([-sin, jnp.zeros((n, half + rest), F32)], axis=1)
    s2 = jnp.concatenate([jnp.zeros((n, half), F32), sin, jnp.zeros((n, rest), F32)], axis=1)
    reps = LANES // SWA_HD
    return jnp.tile(c, (1, reps)), jnp.tile(s1, (1, reps)), jnp.tile(s2, (1, reps))


def _projection_tiles(w_in):
    o_z = GDN_CONV_W
    o_a = o_z + GDN_V_W
    o_qs = o_a + 2 * GDN_HEADS
    o_ks = o_qs + SWA_Q_W
    o_vs = o_ks + SWA_KV_W
    cols = lambda lo, width: w_in[:, lo:lo + width]
    tiles = [cols(t * LANES, LANES) for t in range((o_z + GDN_V_W) // LANES)]
    half = SWA_HEADS // 2
    tiles += [jnp.concatenate([cols(o_qs + p * SWA_HD, SWA_HD), cols(o_qs + (half + p) * SWA_HD, SWA_HD)], axis=1)
              for p in range(half)]
    tiles += [cols(o_ks, SWA_KV_W), cols(o_vs, SWA_KV_W)]
    tiles.append(jnp.concatenate([cols(o_a, 2 * GDN_HEADS),
                                  jnp.zeros((D_MODEL, LANES - 2 * GDN_HEADS), w_in.dtype)], axis=1))
    return tiles


def _layer_weights(norm_ffn1, ffn1_gu, ffn1_down, norm_mix, w_in, conv_w, a_log, dt_bias, gdn_norm, sinks,
                   w_out, norm_ffn2, ffn2_gu, ffn2_down, norm_ple, ple_proj, ple_gate, norm_final):
    tiles = _projection_tiles(w_in)
    win_p = jnp.concatenate(tiles, axis=1).astype(BF16)
    win_prep = jnp.concatenate([tiles[t] for t in PREP_TILE_ORDER], axis=1).astype(BF16)
    half = SWA_HEADS // 2
    head_rows = lambda hh: w_out[GDN_V_W + hh * SWA_HD:GDN_V_W + (hh + 1) * SWA_HD]
    wout_p = jnp.concatenate([w_out[:GDN_V_W]] + [head_rows(hh) for p in range(half) for hh in (p, half + p)],
                             axis=0).astype(BF16)
    pad_row = lambda v: jnp.concatenate([v.astype(F32), jnp.zeros((LANES - v.shape[0],), F32)])[None, :]
    return dict(
        g1=norm_ffn1[None, :], wgu1=ffn1_gu.astype(BF16), wd1=ffn1_down.astype(BF16),
        gm=norm_mix[None, :], win=win_p, win_prep=win_prep, convw=conv_w, alog=pad_row(a_log), dt=pad_row(dt_bias),
        gnorm=gdn_norm[None, :], sinks=sinks.astype(F32), wout=wout_p,
        g2=norm_ffn2[None, :], wgu2=ffn2_gu.astype(BF16), wd2=ffn2_down.astype(BF16),
        gp=norm_ple[None, :], wpp=ple_proj.astype(BF16), wpg=ple_gate.astype(BF16), gf=norm_final[None, :])


def _group(x, p, w, state, pos0):
    bsz, seq, _ = x.shape
    rc, rs1, rs2 = _rope_tables(pos0 + jnp.arange(seq))
    x2d = x.reshape(bsz * seq, D_MODEL)
    if state is None:
        h2d, prep2d, conv_new = _ffn_in_prep(x2d, seq, w["g1"], w["wgu1"], w["wd1"], w["gm"], w["win_prep"],
                                             w["convw"], w["alog"], w["dt"], rc, rs1, rs2)
        o, s_new, k_new, v_new = _mixer_prompt(prep2d.reshape(bsz, seq, P_W), w["gnorm"], w["sinks"])
    else:
        h2d, proj2d = _ffn_in(x2d, w["g1"], w["wgu1"], w["wd1"], w["gm"], w["win"])
        s0, conv0, k0, v0 = state
        o, s_new, conv_new, k_new, v_new = _mixer_sample(
            proj2d.reshape(bsz, seq, P_W), s0, conv0, k0, v0, w["convw"], w["alog"], w["dt"], w["gnorm"],
            w["sinks"], rc, rs1, rs2)
    y = _ffn_out(o.reshape(bsz * seq, D_MODEL), h2d, p.reshape(bsz * seq, PLE_DIM), w["wout"], w["g2"],
                 w["wgu2"], w["wd2"], w["gp"], w["wpg"], w["wpp"], w["gf"])
    kv_shape = (1, bsz, WINDOW, SWA_KV_HEADS, SWA_HD)
    return (y.reshape(bsz, seq, D_MODEL), s_new[None], conv_new[None], k_new.reshape(kv_shape),
            v_new.reshape(kv_shape))


def kernel(x_prompt, x_sample, state_gdn, state_conv, cache_swa_k, cache_swa_v, p_prompt, p_sample, norm_ffn1, ffn1_gu, ffn1_down, norm_mix, w_in, conv_w, a_log, dt_bias, gdn_norm, sinks, w_out, norm_ffn2, ffn2_gu, ffn2_down, norm_ple, ple_proj, ple_gate, norm_final):
    assert state_gdn.shape[0] == 1, "one layer"
    w = _layer_weights(norm_ffn1[0], ffn1_gu[0], ffn1_down[0], norm_mix[0], w_in[0], conv_w[0], a_log[0],
                       dt_bias[0], gdn_norm[0], sinks[0], w_out[0], norm_ffn2[0], ffn2_gu[0], ffn2_down[0],
                       norm_ple[0], ple_proj[0], ple_gate[0], norm_final)
    bs = x_sample.shape[0]
    yp, sg_p, sc_p, kk_p, vv_p = _group(x_prompt, p_prompt[0], w, None, 0)
    ys, sg_s, sc_s, kk_s, vv_s = _group(
        x_sample, p_sample[0], w,
        (state_gdn[0], state_conv[0], cache_swa_k[0].reshape(bs, WINDOW, SWA_KV_W),
         cache_swa_v[0].reshape(bs, WINDOW, SWA_KV_W)), PAST_LEN)
    return (yp, ys, sg_p, sc_p, kk_p, vv_p, sg_s, sc_s, kk_s, vv_s)
```

```python
import functools

import jax
import jax.numpy as jnp
from jax.experimental import pallas as pl
from jax.experimental.pallas import tpu as pltpu

F32 = jnp.float32
BF16 = jnp.bfloat16

D_MODEL = 1024
D_FF = 2816
PLE_DIM = 256
NORM_EPS = 1e-6
GDN_HEADS = 4
GDN_DK = 128
GDN_DV = 128
GDN_CONV = 4
GDN_QK_W = GDN_HEADS * GDN_DK
GDN_V_W = GDN_HEADS * GDN_DV
GDN_CONV_W = 2 * GDN_QK_W + GDN_V_W
SWA_HEADS = 8
SWA_KV_HEADS = 2
SWA_HD = 64
SWA_Q_W = SWA_HEADS * SWA_HD
SWA_KV_W = SWA_KV_HEADS * SWA_HD
WINDOW = 128
ROT_DIM = SWA_HD // 4
ROPE_THETA = 500000.0
PAST_LEN = 16384

LANES = 128
SUBLANES = 8
VMEM_LIMIT_BYTES = 56 * 1024 * 1024

P_QKV = 0
P_Z = P_QKV + GDN_CONV_W
P_QS = P_Z + GDN_V_W
P_KS = P_QS + SWA_Q_W
P_VS = P_KS + SWA_KV_W
P_AB = P_VS + SWA_KV_W
P_W = P_AB + LANES

FF_CHUNK = 256
TOKEN_TILE = 512

CHUNK = 64
CAT_W = GDN_HEADS * CHUNK
N_CONV_TILES = GDN_CONV_W // LANES
N_PROJ_TILES = P_W // LANES
assert N_CONV_TILES == N_PROJ_TILES - N_CONV_TILES + 1
PREP_TILE_ORDER = (tuple(t for pair in zip(range(N_CONV_TILES - 1), range(N_CONV_TILES, N_PROJ_TILES)) for t in pair)
                   + (N_CONV_TILES - 1,))
SAMPLE_T = SUBLANES
SAMPLE_NB = CHUNK // SAMPLE_T


def _rms(x, g):
    return x * jax.lax.rsqrt(jnp.mean(x * x, axis=-1, keepdims=True) + NORM_EPS) * g


def _sigmoid(x):
    return 1.0 / (1.0 + jnp.exp(-x))


def _silu(x):
    return x * _sigmoid(x)


def _mm(a, b):
    return jnp.dot(a.astype(BF16), b.astype(BF16), preferred_element_type=F32)


def _mm_nt(a, b):
    return jax.lax.dot_general(a.astype(BF16), b.astype(BF16), (((1,), (1,)), ((), ())),
                               preferred_element_type=F32)


def _swiglu_acc(n_bf16, wgu_ref, wd_ref):
    rows = n_bf16.shape[0]
    acc = jnp.zeros((rows, D_MODEL), F32)
    for c in range(D_FF // FF_CHUNK):
        lo = c * FF_CHUNK
        gate = jnp.dot(n_bf16, wgu_ref[:, lo:lo + FF_CHUNK], preferred_element_type=F32)
        up = jnp.dot(n_bf16, wgu_ref[:, D_FF + lo:D_FF + lo + FF_CHUNK], preferred_element_type=F32)
        act = (_silu(gate) * up).astype(BF16)
        acc = acc + jnp.dot(act, wd_ref[lo:lo + FF_CHUNK, :], preferred_element_type=F32)
    return acc


def _ffn_in_kernel(x_ref, g1_ref, wgu_ref, wd_ref, gm_ref, win_ref, h_ref, proj_ref):
    x = x_ref[...]
    n1 = _rms(x, g1_ref[...]).astype(BF16)
    h = x + 0.5 * _swiglu_acc(n1, wgu_ref, wd_ref)
    h_ref[...] = h
    n = _rms(h, gm_ref[...]).astype(BF16)
    proj_ref[...] = jnp.dot(n, win_ref[...], preferred_element_type=F32)


def _const_spec(shape):
    return pl.BlockSpec(shape, lambda i: (0,) * len(shape), pipeline_mode=pl.Buffered(1))


def _ffn_in(x2d, g1, wgu, wd, gm, win):
    n_tok = x2d.shape[0]
    tm = min(TOKEN_TILE, n_tok)
    assert n_tok % tm == 0
    return pl.pallas_call(
        _ffn_in_kernel,
        grid=(n_tok // tm,),
        in_specs=[
            pl.BlockSpec((tm, D_MODEL), lambda i: (i, 0)),
            _const_spec((1, D_MODEL)),
            _const_spec((D_MODEL, 2 * D_FF)),
            _const_spec((D_FF, D_MODEL)),
            _const_spec((1, D_MODEL)),
            _const_spec((D_MODEL, P_W)),
        ],
        out_specs=[
            pl.BlockSpec((tm, D_MODEL), lambda i: (i, 0)),
            pl.BlockSpec((tm, P_W), lambda i: (i, 0)),
        ],
        out_shape=[
            jax.ShapeDtypeStruct((n_tok, D_MODEL), F32),
            jax.ShapeDtypeStruct((n_tok, P_W), F32),
        ],
        compiler_params=pltpu.CompilerParams(
            dimension_semantics=("arbitrary",), vmem_limit_bytes=VMEM_LIMIT_BYTES),
        name="ffn_in",
    )(x2d, g1, wgu, wd, gm, win)


def _ffn_in_prep_kernel(x_ref, g1_ref, wgu_ref, wd_ref, gm_ref, win_ref, convw_ref, alog_ref, dt_ref,
                        rc_ref, rs1_ref, rs2_ref, h_ref, prep_ref, convout_ref, n_st, xext_ref, *, tiles_per_seq):
    i = pl.program_id(0)
    tm = x_ref.shape[0]
    piece = 2 * LANES

    @pl.when(i == 0)
    def _init():
        n_st[...] = jnp.zeros(n_st.shape, n_st.dtype)
        xext_ref[...] = jnp.zeros(xext_ref.shape, F32)

    n_prev = n_st[...]
    first_of_seq = ((i - 1) % tiles_per_seq) == 0

    def prep_tile(lo, val):
        cols = slice(lo, lo + LANES)
        if lo < P_Z:
            hist = jnp.where(first_of_seq, 0.0, xext_ref[:, cols])
            xext_ref[:, cols] = val[tm - SUBLANES:tm]
            convout_ref[0, :, cols] = val[tm - (GDN_CONV - 1):tm]
            ext = jnp.concatenate([hist, val], axis=0)
            conv = val * convw_ref[GDN_CONV - 1:GDN_CONV, cols]
            for j in range(1, GDN_CONV):
                conv = conv + pltpu.roll(ext, j, 0)[SUBLANES:] * convw_ref[GDN_CONV - 1 - j:GDN_CONV - j, cols]
            act = _silu(conv)
            if lo < 2 * GDN_QK_W:
                scale = GDN_DK ** -0.5 if lo < GDN_QK_W else 1.0
                act = act * (jax.lax.rsqrt(jnp.sum(act * act, axis=-1, keepdims=True) + 1e-6) * scale)
            prep_ref[:, cols] = act
        elif lo < P_QS:
            prep_ref[:, cols] = _silu(val)
        elif lo < P_VS:
            prep_ref[:, cols] = _rope(val, rc_ref[...], rs1_ref[...], rs2_ref[...])
        elif lo < P_AB:
            prep_ref[:, cols] = val
        else:
            g_all, beta_all = _gates(val, alog_ref[...], dt_ref[...])
            lane = jax.lax.broadcasted_iota(jnp.int32, (tm, LANES), 1)
            prep_ref[:, cols] = jnp.where(lane < GDN_HEADS, g_all, beta_all)

    def proj_piece(lo):
        width = min(piece, P_W - lo)
        proj = jnp.dot(n_prev, win_ref[:, lo:lo + width], preferred_element_type=F32)
        for off in range(0, width, LANES):
            prep_tile(PREP_TILE_ORDER[(lo + off) // LANES] * LANES, proj[:, off:off + LANES])

    x = x_ref[...]
    n1 = _rms(x, g1_ref[...]).astype(BF16)
    acc = jnp.zeros((tm, D_MODEL), F32)
    for c in range(D_FF // FF_CHUNK):
        lo = c * FF_CHUNK
        gate = jnp.dot(n1, wgu_ref[:, lo:lo + FF_CHUNK], preferred_element_type=F32)
        up = jnp.dot(n1, wgu_ref[:, D_FF + lo:D_FF + lo + FF_CHUNK], preferred_element_type=F32)
        act = (_silu(gate) * up).astype(BF16)
        acc = acc + jnp.dot(act, wd_ref[lo:lo + FF_CHUNK, :], preferred_element_type=F32)
        proj_piece(c * piece)
    for lo in range((D_FF // FF_CHUNK) * piece, P_W, piece):
        proj_piece(lo)
    h = x + 0.5 * acc
    h_ref[...] = h
    n_st[...] = _rms(h, gm_ref[...]).astype(BF16)


def _ffn_in_prep(x2d, seq_len, g1, wgu, wd, gm, win, convw, alog_row, dt_row, rc, rs1, rs2):
    n_tok = x2d.shape[0]
    tm = TOKEN_TILE
    assert seq_len % tm == 0 and n_tok % seq_len == 0
    tiles_per_seq = seq_len // tm
    n_tiles = n_tok // tm
    cur = lambda i: jnp.minimum(i, n_tiles - 1)
    prev = lambda i: jnp.maximum(i - 1, 0)
    rope_spec = pl.BlockSpec((tm, LANES), lambda i: (prev(i) % tiles_per_seq, 0))
    return pl.pallas_call(
        functools.partial(_ffn_in_prep_kernel, tiles_per_seq=tiles_per_seq),
        grid=(n_tiles + 1,),
        in_specs=[
            pl.BlockSpec((tm, D_MODEL), lambda i: (cur(i), 0)),
            _const_spec((1, D_MODEL)),
            _const_spec((D_MODEL, 2 * D_FF)),
            _const_spec((D_FF, D_MODEL)),
            _const_spec((1, D_MODEL)),
            _const_spec((D_MODEL, P_W)),
            _const_spec((GDN_CONV, GDN_CONV_W)),
            _const_spec((1, LANES)),
            _const_spec((1, LANES)),
            rope_spec, rope_spec, rope_spec,
        ],
        out_specs=[
            pl.BlockSpec((tm, D_MODEL), lambda i: (cur(i), 0)),
            pl.BlockSpec((tm, P_W), lambda i: (prev(i), 0)),
            pl.BlockSpec((1, GDN_CONV - 1, GDN_CONV_W), lambda i: (prev(i) // tiles_per_seq, 0, 0)),
        ],
        out_shape=[
            jax.ShapeDtypeStruct((n_tok, D_MODEL), F32),
            jax.ShapeDtypeStruct((n_tok, P_W), F32),
            jax.ShapeDtypeStruct((n_tok // seq_len, GDN_CONV - 1, GDN_CONV_W), F32),
        ],
        scratch_shapes=[pltpu.VMEM((tm, D_MODEL), BF16), pltpu.VMEM((SUBLANES, GDN_CONV_W), F32)],
        compiler_params=pltpu.CompilerParams(
            dimension_semantics=("arbitrary",), vmem_limit_bytes=VMEM_LIMIT_BYTES),
        name="ffn_in_prep",
    )(x2d, g1, wgu, wd, gm, win, convw, alog_row, dt_row, rc, rs1, rs2)


def _ffn_out_kernel(o_ref, h_ref, p_ref, wout_ref, g2_ref, wgu_ref, wd_ref, gp_ref, wpg_ref, wpp_ref,
                    gf_ref, y_ref):
    h = h_ref[...] + jnp.dot(o_ref[...], wout_ref[...], preferred_element_type=F32)
    n2 = _rms(h, g2_ref[...]).astype(BF16)
    h = h + 0.5 * _swiglu_acc(n2, wgu_ref, wd_ref)
    npl = _rms(h, gp_ref[...]).astype(BF16)
    gate = _sigmoid(jnp.dot(npl, wpg_ref[...], preferred_element_type=F32))
    pe = jnp.dot(p_ref[...].astype(BF16), wpp_ref[...], preferred_element_type=F32)
    h = h + gate * pe
    y_ref[...] = _rms(h, gf_ref[...])


def _ffn_out(o2d, h2d, p2d, wout, g2, wgu, wd, gp, wpg, wpp, gf):
    n_tok = h2d.shape[0]
    tm = min(TOKEN_TILE, n_tok)
    assert n_tok % tm == 0
    return pl.pallas_call(
        _ffn_out_kernel,
        grid=(n_tok // tm,),
        in_specs=[
            pl.BlockSpec((tm, D_MODEL), lambda i: (i, 0)),
            pl.BlockSpec((tm, D_MODEL), lambda i: (i, 0)),
            pl.BlockSpec((tm, PLE_DIM), lambda i: (i, 0)),
            _const_spec((D_MODEL, D_MODEL)),
            _const_spec((1, D_MODEL)),
            _const_spec((D_MODEL, 2 * D_FF)),
            _const_spec((D_FF, D_MODEL)),
            _const_spec((1, D_MODEL)),
            _const_spec((D_MODEL, D_MODEL)),
            _const_spec((PLE_DIM, D_MODEL)),
            _const_spec((1, D_MODEL)),
        ],
        out_specs=pl.BlockSpec((tm, D_MODEL), lambda i: (i, 0)),
        out_shape=jax.ShapeDtypeStruct((n_tok, D_MODEL), F32),
        compiler_params=pltpu.CompilerParams(
            dimension_semantics=("arbitrary",), vmem_limit_bytes=VMEM_LIMIT_BYTES),
        name="ffn_out",
    )(o2d, h2d, p2d, wout, g2, wgu, wd, gp, wpg, wpp, gf)


def _pad_rows(x):
    rows = x.shape[0]
    if rows == LANES:
        return x
    return jnp.concatenate([x, jnp.zeros((LANES - rows, x.shape[1]), x.dtype)], axis=0)


def _split3(x):
    hi = x.astype(BF16)
    r = x - hi.astype(F32)
    mid = r.astype(BF16)
    lo = (r - mid.astype(F32)).astype(BF16)
    return hi, mid, lo


def _mm_exact_rhs(a_bf16, x):
    hi, mid, lo = _split3(x)
    return (jnp.dot(a_bf16, hi, preferred_element_type=F32) + jnp.dot(a_bf16, mid, preferred_element_type=F32)
            + jnp.dot(a_bf16, lo, preferred_element_type=F32))


def _mm_exact_lhs(x, a_bf16):
    hi, mid, lo = _split3(x)
    return (jnp.dot(hi, a_bf16, preferred_element_type=F32) + jnp.dot(mid, a_bf16, preferred_element_type=F32)
            + jnp.dot(lo, a_bf16, preferred_element_type=F32))


def _bd_cat(y, lane_lo_chunk):
    zero_tile = jnp.zeros((CHUNK, LANES), F32)
    blocks = []
    for hh in range(GDN_HEADS):
        y_tile = y[:, (hh // 2) * LANES:(hh // 2 + 1) * LANES]
        keep = jnp.where(lane_lo_chunk, y_tile, 0.0) if hh % 2 == 0 else jnp.where(lane_lo_chunk, 0.0, y_tile)
        blocks.append(jnp.concatenate([keep, zero_tile] if hh < 2 else [zero_tile, keep], axis=1))
    return jnp.concatenate(blocks, axis=0).astype(BF16)


def _bd_wide(y):
    zero_tile = jnp.zeros((CHUNK, LANES), F32)
    blocks = []
    for hh in range(GDN_HEADS):
        tiles = [zero_tile] * GDN_HEADS
        tiles[hh] = y[:, hh * LANES:(hh + 1) * LANES]
        blocks.append(jnp.concatenate(tiles, axis=1))
    return jnp.concatenate(blocks, axis=0).astype(BF16)


def _bd_pair(sa, sb):
    zeros = jnp.zeros(sa.shape, sa.dtype)
    return jnp.concatenate([jnp.concatenate([sa, zeros], axis=1), jnp.concatenate([zeros, sb], axis=1)], axis=0)


def _gates(ab, alog_row, dt_row):
    zab = ab + dt_row
    softplus = jnp.maximum(zab, 0.0) + jnp.log(1.0 + jnp.exp(-jnp.abs(zab)))
    return -jnp.exp(alog_row) * softplus, _sigmoid(ab)


def _head_l2(x):
    parts = []
    for hh in range(GDN_HEADS):
        xh = x[:, hh * GDN_DK:(hh + 1) * GDN_DK]
        parts.append(xh * jax.lax.rsqrt(jnp.sum(xh * xh, axis=-1, keepdims=True) + 1e-6))
    return jnp.concatenate(parts, axis=1)


def _lane_expand(cols, first_lane, rows):
    return jnp.concatenate(
        [jnp.broadcast_to(cols[:, first_lane + hh:first_lane + hh + 1], (rows, LANES))
         for hh in range(GDN_HEADS)], axis=1)


def _rope(x, rc, rs1, rs2):
    return x * rc + pltpu.roll(x, LANES - ROT_DIM // 2, 1) * rs1 + pltpu.roll(x, ROT_DIM // 2, 1) * rs2


def _uw_rhs(vb_all, kbg_all):
    return jnp.concatenate(
        [jnp.concatenate([vb_all[:, hh * GDN_DV:(hh + 1) * GDN_DV],
                          kbg_all[:, hh * GDN_DK:(hh + 1) * GDN_DK]], axis=1)
         for hh in range(GDN_HEADS)], axis=0)


def _split_uw(uw):
    u_all = jnp.concatenate([uw[hh * CHUNK:(hh + 1) * CHUNK, 0:GDN_DV] for hh in range(GDN_HEADS)], axis=1)
    w_all = jnp.concatenate([uw[hh * CHUNK:(hh + 1) * CHUNK, GDN_DV:] for hh in range(GDN_HEADS)], axis=1)
    return u_all, w_all


def _stack_heads_t(x):
    return jnp.transpose(jnp.concatenate(
        [x[:, hh * GDN_DK:(hh + 1) * GDN_DK] for hh in range(GDN_HEADS)], axis=0))


PROMPT_NS = 4


def _mixer_prompt_kernel(sinks_ref, cur_ref, prev_ref, gnorm_ref, o_ref, sout_ref, kout_ref, vout_ref,
                         s_ref, kprev_ref, vprev_ref, gcc_st, gcr_st, beta_st, os_st, *, tb, n_blocks):
    t = pl.program_id(1)
    ns = PROMPT_NS
    nch = tb // CHUNK
    stages = (gcc_st, gcr_st, beta_st, os_st)
    stage_names = ("gcc", "gcr", "beta", "os")

    @pl.when(t == 0)
    def _init():
        for ref in (s_ref, kprev_ref, vprev_ref) + stages:
            ref[...] = jnp.zeros(ref.shape, F32)

    r_t = jax.lax.broadcasted_iota(jnp.int32, (tb, LANES), 0)
    c_t = jax.lax.broadcasted_iota(jnp.int32, (tb, LANES), 1)
    lane_lo = c_t < CHUNK
    lane_lo_chunk = jax.lax.broadcasted_iota(jnp.int32, (CHUNK, LANES), 1) < CHUNK
    lane_lo_row = jax.lax.broadcasted_iota(jnp.int32, (1, LANES), 1) < CHUNK
    same_chunk = (r_t ^ c_t) < CHUNK
    rk = jax.lax.broadcasted_iota(jnp.int32, (tb, 2 * WINDOW), 0)
    ck = jax.lax.broadcasted_iota(jnp.int32, (tb, 2 * WINDOW), 1)
    first_off = jnp.where(t > 0, 0, WINDOW)
    dist = ck - rk
    kmask = ((dist > jnp.where(ck < WINDOW, first_off, -WINDOW))
             & (dist <= jnp.where(ck < WINDOW, 2 * WINDOW, WINDOW)))

    s_cat = [s_ref[sq] for sq in range(ns)]
    k_prev = [kprev_ref[sq] for sq in range(ns)]
    v_prev = [vprev_ref[sq] for sq in range(ns)]
    staged = [{name: ref[sq] for ref, name in zip(stages, stage_names)} for sq in range(ns)]
    for sq in range(ns):
        staged[sq].update(q=prev_ref[sq, :, 0:GDN_QK_W], k=prev_ref[sq, :, GDN_QK_W:2 * GDN_QK_W],
                          v=prev_ref[sq, :, 2 * GDN_QK_W:GDN_CONV_W], zs=prev_ref[sq, :, P_Z:P_Z + GDN_V_W])
    fr = [{} for _ in range(ns)]

    def front(sq):
        f = fr[sq]
        ab = cur_ref[sq, :, P_AB:P_AB + LANES]
        f["beta"] = ab
        f["gcc"] = _mm_exact_rhs(((c_t <= r_t) & same_chunk).astype(BF16), ab)
        f["gcr"] = _mm_exact_lhs(jnp.transpose(ab)[0:SUBLANES, :],
                                 ((r_t <= c_t) & same_chunk).astype(BF16))
        k_own = cur_ref[sq, :, P_KS:P_KS + SWA_KV_W]
        v_own = cur_ref[sq, :, P_VS:P_VS + SWA_KV_W]
        f["k_own"], f["v_own"] = k_own, v_own
        k_cat = jnp.concatenate([k_prev[sq], k_own], axis=0)
        v_cat = jnp.concatenate([v_prev[sq], v_own], axis=0).astype(BF16)
        q_rows = [None] * SWA_HEADS
        for p in range(SWA_HEADS // 2):
            qp = cur_ref[sq, :, P_QS + p * LANES:P_QS + (p + 1) * LANES]
            q_rows[p] = jnp.where(lane_lo, qp, 0.0)
            q_rows[SWA_HEADS // 2 + p] = jnp.where(lane_lo, 0.0, qp)
        s_all = _mm_nt(jnp.concatenate(q_rows, axis=0), k_cat) * (SWA_HD ** -0.5)
        yield
        pv = [None] * SWA_HEADS
        for hh in range(SWA_HEADS):
            sh = jnp.where(kmask, s_all[hh * tb:(hh + 1) * tb], -jnp.inf)
            sk = sinks_ref[hh]
            m = jnp.maximum(jnp.max(sh, axis=-1, keepdims=True), sk)
            ph = jnp.exp(sh - m)
            den = jnp.sum(ph, axis=-1, keepdims=True) + jnp.exp(sk - m)
            pv[hh] = jnp.dot(ph.astype(BF16), v_cat, preferred_element_type=F32) / den
            yield
        f["os"] = jnp.concatenate(
            [jnp.where(lane_lo, pv[p], pv[SWA_HEADS // 2 + p]) for p in range(SWA_HEADS // 2)], axis=1)

    fronts = [front(sq) for sq in range(ns)]

    def fill():
        for gen in fronts:
            next(gen, None)

    units = [(sq, c) for sq in range(ns) for c in range(nch)]
    rows_of = lambda c: slice(c * CHUNK, (c + 1) * CHUNK)
    ri = jax.lax.broadcasted_iota(jnp.int32, (CHUNK, CAT_W), 0)
    ci = jax.lax.broadcasted_iota(jnp.int32, (CHUNK, CAT_W), 1)
    cj = ci & (CHUNK - 1)
    tri = cj <= ri
    strict = cj < ri
    eye = (cj == ri).astype(F32)
    blk = ri ^ cj
    bd = functools.partial(_bd_cat, lane_lo_chunk=lane_lo_chunk)

    drv = []
    for sq in range(ns):
        st = staged[sq]
        gc_full = _lane_expand(st["gcc"], 0, tb)
        beta_full = _lane_expand(st["beta"], GDN_HEADS, tb)
        egc_full = jnp.exp(gc_full)
        kb_all = st["k"] * beta_full
        drv.append(dict(gc_full=gc_full, kb=kb_all, vb=st["v"] * beta_full, kbg=kb_all * egc_full,
                        qd=st["q"] * egc_full, gcr=st["gcr"], gcr_sw=pltpu.roll(st["gcr"], CHUNK, 1)))
    fill()
    a_mat, qk, xinv, pw = {}, {}, {}, {}
    for u in units:
        sq, c = u
        d, rows = drv[sq], rows_of(c)
        gcc_pairs, gcr_pairs = [], []
        for p in range(GDN_HEADS // 2):
            gcc_pairs.append(jnp.where(lane_lo_chunk, d["gc_full"][rows, (2 * p) * LANES:(2 * p + 1) * LANES],
                                       d["gc_full"][rows, (2 * p + 1) * LANES:(2 * p + 2) * LANES]))
            ra = (d["gcr"] if c == 0 else d["gcr_sw"])[2 * p:2 * p + 1, :]
            rb_ = (d["gcr_sw"] if c == 0 else d["gcr"])[2 * p + 1:2 * p + 2, :]
            gcr_pairs.append(jnp.where(lane_lo_row, ra, rb_))
        gcc = jnp.concatenate(gcc_pairs, axis=1)
        gcr = jnp.concatenate(gcr_pairs, axis=1)
        decay = jnp.exp(jnp.where(tri, gcc - gcr, -jnp.inf))
        kq = jax.lax.dot_general(
            jnp.concatenate([d["kb"][rows], staged[sq]["q"][rows]], axis=0).astype(BF16),
            _bd_wide(staged[sq]["k"][rows]), (((1,), (1,)), ((), ())), preferred_element_type=F32)
        a_mat[u] = jnp.where(strict, kq[0:CHUNK] * decay, 0.0)
        qk[u] = kq[CHUNK:] * decay
    fill()
    for u in units:
        a0 = jnp.where(blk < SUBLANES, a_mat[u], 0.0)
        xinv[u] = eye - a0
        pw[u] = _mm(a0, bd(a0))
    fill()
    for u in units:
        pw_bd = bd(pw[u])
        xinv[u] = xinv[u] + _mm(xinv[u], pw_bd)
        pw[u] = _mm(pw[u], pw_bd)
    fill()
    for u in units:
        xinv[u] = xinv[u] + _mm(xinv[u], bd(pw[u]))
    fill()
    s = SUBLANES
    while s < CHUNK:
        y = {u: _mm(jnp.where((blk < 2 * s) & (blk >= s), a_mat[u], 0.0), bd(xinv[u])) for u in units}
        fill()
        for u in units:
            xinv[u] = xinv[u] - _mm(xinv[u], bd(y[u]))
        fill()
        s *= 2
    prep = {}
    for u in units:
        sq, c = u
        d, rows = drv[sq], rows_of(c)
        u_all, w_all = _split_uw(_mm(bd(xinv[u]), _uw_rhs(d["vb"][rows], d["kbg"][rows])))
        glast = d["gc_full"][c * CHUNK + CHUNK - 1:c * CHUNK + CHUNK, :]
        kdec_t = _stack_heads_t(staged[sq]["k"][rows] * jnp.exp(glast - d["gc_full"][rows]))
        prep[u] = (u_all, w_all, kdec_t, jnp.exp(glast))
    fill()

    o_chunks = [[] for _ in range(ns)]
    for c in range(nch):
        rows = rows_of(c)
        ws, qs = [[] for _ in range(ns)], [[] for _ in range(ns)]
        for sq in range(ns):
            w_all = prep[(sq, c)][1]
            for p in range(GDN_HEADS // 2):
                cols = slice(2 * p * LANES, (2 * p + 2) * LANES)
                r = _mm(jnp.concatenate([w_all[:, cols], drv[sq]["qd"][rows, cols]], axis=0),
                        _bd_pair(s_cat[sq][:, (2 * p) * LANES:(2 * p + 1) * LANES],
                                 s_cat[sq][:, (2 * p + 1) * LANES:(2 * p + 2) * LANES]))
                ws[sq].append(r[0:CHUNK])
                qs[sq].append(r[CHUNK:])
        fill()
        for sq in range(ns):
            u_all, _, kdec_t, eg_last = prep[(sq, c)]
            v_new = u_all - jnp.concatenate(ws[sq], axis=1)
            r2 = _mm(jnp.concatenate([qk[(sq, c)], kdec_t], axis=0), _bd_wide(v_new))
            o_chunks[sq].append(jnp.concatenate(qs[sq], axis=1) + r2[0:CHUNK])
            s_cat[sq] = s_cat[sq] * eg_last + r2[CHUNK:]
        fill()
    for gen in fronts:
        for _ in gen:
            pass

    for sq in range(ns):
        o_gdn = jnp.concatenate(o_chunks[sq], axis=0)
        zs_prev = staged[sq]["zs"]
        og = [_rms(o_gdn[:, hh * GDN_DV:(hh + 1) * GDN_DV], gnorm_ref[...])
              * zs_prev[:, hh * GDN_DV:(hh + 1) * GDN_DV] for hh in range(GDN_HEADS)]
        o_ref[sq] = jnp.concatenate(og + [staged[sq]["os"]], axis=-1).astype(o_ref.dtype)
        s_ref[sq] = s_cat[sq]
        kprev_ref[sq] = fr[sq]["k_own"]
        vprev_ref[sq] = fr[sq]["v_own"]
        for ref, name in zip(stages, stage_names):
            ref[sq] = fr[sq][name]

    @pl.when(t == n_blocks - 1)
    def _window_out():
        for sq in range(ns):
            kout_ref[sq] = fr[sq]["k_own"]
            vout_ref[sq] = fr[sq]["v_own"]

    @pl.when(t == n_blocks)
    def _state_out():
        for sq in range(ns):
            for hh in range(GDN_HEADS):
                sout_ref[sq, hh] = s_ref[sq, :, hh * GDN_DV:(hh + 1) * GDN_DV]


def _mixer_prompt(prep3d, gnorm, sinks):
    bsz, seq, _ = prep3d.shape
    tb, ns = WINDOW, PROMPT_NS
    assert seq % tb == 0 and bsz % ns == 0
    n_blocks = seq // tb
    kern = functools.partial(_mixer_prompt_kernel, tb=tb, n_blocks=n_blocks)
    seq_blk = lambda shape: pl.BlockSpec((ns,) + shape, lambda b, t: (b,) + (0,) * len(shape))
    vmem = lambda *shape: pltpu.VMEM((ns,) + shape, F32)
    return pl.pallas_call(
        kern,
        grid=(bsz // ns, n_blocks + 1),
        in_specs=[
            pl.BlockSpec(memory_space=pltpu.SMEM),
            pl.BlockSpec((ns, tb, P_W), lambda b, t: (b, jnp.minimum(t, n_blocks - 1), 0)),
            pl.BlockSpec((ns, tb, P_W), lambda b, t: (b, jnp.maximum(t - 1, 0), 0)),
            pl.BlockSpec((1, GDN_DV), lambda b, t: (0, 0)),
        ],
        out_specs=[
            pl.BlockSpec((ns, tb, D_MODEL), lambda b, t: (b, jnp.maximum(t - 1, 0), 0)),
            seq_blk((GDN_HEADS, GDN_DK, GDN_DV)),
            seq_blk((WINDOW, SWA_KV_W)),
            seq_blk((WINDOW, SWA_KV_W)),
        ],
        out_shape=[
            jax.ShapeDtypeStruct((bsz, seq, D_MODEL), BF16),
            jax.ShapeDtypeStruct((bsz, GDN_HEADS, GDN_DK, GDN_DV), F32),
            jax.ShapeDtypeStruct((bsz, WINDOW, SWA_KV_W), F32),
            jax.ShapeDtypeStruct((bsz, WINDOW, SWA_KV_W), F32),
        ],
        scratch_shapes=[
            vmem(GDN_DK, GDN_HEADS * GDN_DV),
            vmem(WINDOW, SWA_KV_W),
            vmem(WINDOW, SWA_KV_W),
            vmem(tb, LANES),
            vmem(SUBLANES, LANES),
            vmem(tb, LANES),
            vmem(tb, SWA_Q_W),
        ],
        compiler_params=pltpu.CompilerParams(
            dimension_semantics=("arbitrary", "arbitrary"), vmem_limit_bytes=VMEM_LIMIT_BYTES),
        name="mixer_prompt",
    )(sinks, prep3d, prep3d, gnorm)


def _mixer_sample_kernel(sinks_ref, proj_ref, s0_ref, conv0_ref, k0_ref, v0_ref, convw_ref, alog_ref, dt_ref,
                         gnorm_ref, rc_ref, rs1_ref, rs2_ref,
                         o_ref, sout_ref, convout_ref, kout_ref, vout_ref, xext_ref):
    nb, ts, rows = SAMPLE_NB, SAMPLE_T, CHUNK
    seqs = range(nb)
    grp = lambda x, b: x[b * ts:(b + 1) * ts]

    hist = SUBLANES - (GDN_CONV - 1)
    conv_parts = []
    for b in seqs:
        xext_ref[2 * ts * b + hist:2 * ts * b + ts, :] = conv0_ref[b]
        xext_ref[2 * ts * b + ts:2 * ts * (b + 1), :] = proj_ref[b, :, P_QKV:P_QKV + GDN_CONV_W]
    for b in seqs:
        base = 2 * ts * b + hist
        conv = xext_ref[base:base + ts, :] * convw_ref[0:1, :]
        for j in range(1, GDN_CONV):
            conv = conv + xext_ref[base + j:base + j + ts, :] * convw_ref[j:j + 1, :]
        conv_parts.append(conv)
        convout_ref[b] = xext_ref[2 * ts * (b + 1) - (GDN_CONV - 1):2 * ts * (b + 1), :]
    qkv = _silu(jnp.concatenate(conv_parts, axis=0))
    q_all = _head_l2(qkv[:, 0:GDN_QK_W]) * (GDN_DK ** -0.5)
    k_all = _head_l2(qkv[:, GDN_QK_W:2 * GDN_QK_W])
    v_all = qkv[:, 2 * GDN_QK_W:]

    def proj_cols(lo, width):
        return jnp.concatenate([proj_ref[b, :, lo:lo + width] for b in seqs], axis=0)

    g_all, beta_all = _gates(proj_cols(P_AB, LANES), alog_ref[...], dt_ref[...])

    r_t = jax.lax.broadcasted_iota(jnp.int32, (rows, LANES), 0)
    c_t = jax.lax.broadcasted_iota(jnp.int32, (rows, LANES), 1)
    r_p = jax.lax.broadcasted_iota(jnp.int32, (LANES, LANES), 0)
    c_p = jax.lax.broadcasted_iota(jnp.int32, (LANES, LANES), 1)
    g_pad = _pad_rows(g_all)
    gc_cols = _mm_exact_rhs(((c_t <= r_t) & ((r_t ^ c_t) < ts)).astype(BF16), g_pad)
    gc_rows = _mm_exact_lhs(jnp.transpose(g_pad)[0:SUBLANES, :],
                            ((r_p <= c_p) & ((r_p ^ c_p) < ts)).astype(BF16))
    gc_rows_sw = pltpu.roll(gc_rows, CHUNK, 1)

    lane_lo = c_t < CHUNK
    lane_lo_row = jax.lax.broadcasted_iota(jnp.int32, (1, LANES), 1) < CHUNK
    lane_lo_t = jax.lax.broadcasted_iota(jnp.int32, (ts, LANES), 1) < CHUNK
    gc_full = _lane_expand(gc_cols, 0, rows)
    beta_full = _lane_expand(beta_all, GDN_HEADS, rows)
    egc_full = jnp.exp(gc_full)
    kb_all = k_all * beta_full
    vb_all = v_all * beta_full
    kbg_all = kb_all * egc_full
    qd_all = q_all * egc_full

    ri = jax.lax.broadcasted_iota(jnp.int32, (rows, CAT_W), 0)
    ci = jax.lax.broadcasted_iota(jnp.int32, (rows, CAT_W), 1)
    cj = ci & (CHUNK - 1)
    same_seq = (ri ^ cj) < ts
    eye = (cj == ri).astype(F32)
    bd = functools.partial(_bd_cat, lane_lo_chunk=lane_lo)

    gcc = jnp.concatenate(
        [jnp.where(lane_lo, gc_full[:, (2 * p) * LANES:(2 * p + 1) * LANES],
                   gc_full[:, (2 * p + 1) * LANES:(2 * p + 2) * LANES]) for p in range(GDN_HEADS // 2)], axis=1)
    gcr = jnp.concatenate(
        [jnp.where(lane_lo_row, gc_rows[2 * p:2 * p + 1, :], gc_rows_sw[2 * p + 1:2 * p + 2, :])
         for p in range(GDN_HEADS // 2)], axis=1)
    decay = jnp.exp(jnp.where((cj <= ri) & same_seq, gcc - gcr, -jnp.inf))
    kq = jax.lax.dot_general(
        jnp.concatenate([kb_all, q_all], axis=0).astype(BF16), _bd_wide(k_all),
        (((1,), (1,)), ((), ())), preferred_element_type=F32)
    a_mat = jnp.where(cj < ri, kq[0:rows] * decay, 0.0)
    qk = kq[rows:] * decay
    xinv = eye - a_mat
    pw = _mm(a_mat, bd(a_mat))
    pw_bd = bd(pw)
    xinv = xinv + _mm(xinv, pw_bd)
    pw = _mm(pw, pw_bd)
    xinv = xinv + _mm(xinv, bd(pw))
    u_all, w_all = _split_uw(_mm(bd(xinv), _uw_rhs(vb_all, kbg_all)))
    glast = [gc_full[(b + 1) * ts - 1:(b + 1) * ts, :] for b in seqs]
    glast_rows = jnp.concatenate([jnp.broadcast_to(glast[b], (ts, GDN_QK_W)) for b in seqs], axis=0)
    kdec_t = _stack_heads_t(k_all * jnp.exp(glast_rows - gc_full))

    s_old, ws, qs = [], [], []
    for b in seqs:
        s_b = jnp.concatenate([s0_ref[b, hh] for hh in range(GDN_HEADS)], axis=1)
        s_old.append(s_b)
        ws_b, qs_b = [], []
        for p in range(GDN_HEADS // 2):
            cols = slice(2 * p * LANES, (2 * p + 2) * LANES)
            r = _mm(jnp.concatenate([grp(w_all, b)[:, cols], grp(qd_all, b)[:, cols]], axis=0),
                    _bd_pair(s_b[:, (2 * p) * LANES:(2 * p + 1) * LANES],
                             s_b[:, (2 * p + 1) * LANES:(2 * p + 2) * LANES]))
            ws_b.append(r[0:ts])
            qs_b.append(r[ts:])
        ws.append(jnp.concatenate(ws_b, axis=1))
        qs.append(jnp.concatenate(qs_b, axis=1))
    v_new = u_all - jnp.concatenate(ws, axis=0)
    v_bd = _bd_wide(v_new)
    o_gdn = jnp.concatenate(qs, axis=0) + _mm(qk, v_bd)
    cb = jax.lax.broadcasted_iota(jnp.int32, (GDN_DK, CAT_W), 1) & (CHUNK - 1)
    upd = _mm(jnp.concatenate([jnp.where((cb ^ (b * ts)) < ts, kdec_t, 0.0) for b in seqs], axis=0), v_bd)
    for b in seqs:
        s_new = s_old[b] * jnp.exp(glast[b]) + upd[b * GDN_DK:(b + 1) * GDN_DK]
        for hh in range(GDN_HEADS):
            sout_ref[b, hh] = s_new[:, hh * GDN_DV:(hh + 1) * GDN_DV]
    og = []
    for hh in range(GDN_HEADS):
        oh = _rms(o_gdn[:, hh * GDN_DV:(hh + 1) * GDN_DV], gnorm_ref[...])
        og.append(oh * _silu(proj_cols(P_Z + hh * GDN_DV, GDN_DV)))

    tile8 = lambda x: jnp.concatenate([x] * nb, axis=0)
    rc, rs1, rs2 = tile8(rc_ref[...]), tile8(rs1_ref[...]), tile8(rs2_ref[...])
    k_own = _rope(proj_cols(P_KS, SWA_KV_W), rc, rs1, rs2)
    v_own = proj_cols(P_VS, SWA_KV_W)
    q_rows = [None] * SWA_HEADS
    for p in range(SWA_HEADS // 2):
        qp = _rope(proj_cols(P_QS + p * LANES, LANES), rc, rs1, rs2)
        q_rows[p] = jnp.where(lane_lo, qp, 0.0)
        q_rows[SWA_HEADS // 2 + p] = jnp.where(lane_lo, 0.0, qp)
    scale = SWA_HD ** -0.5
    s_own_all = _mm_nt(jnp.concatenate(q_rows, axis=0), _pad_rows(k_own)) * scale
    tok = r_t & (ts - 1)
    prev_mask = c_t > tok
    sink_col = jnp.zeros((rows, 1), F32)
    for hh in range(SWA_HEADS):
        sink_col = jnp.where((r_t[:, 0:1] ^ (hh * ts)) < ts, sinks_ref[hh], sink_col)
    pp_all, po_all, den_all = [], [], []
    for b in seqs:
        q_b = jnp.concatenate([grp(q_rows[hh], b) for hh in range(SWA_HEADS)], axis=0)
        sp = jnp.where(prev_mask, _mm_nt(q_b, k0_ref[b]) * scale, -jnp.inf)
        so_raw = jnp.concatenate([s_own_all[hh * rows + b * ts:hh * rows + (b + 1) * ts]
                                  for hh in range(SWA_HEADS)], axis=0)
        so = jnp.where(((c_t ^ (b * ts)) < ts) & ((c_t & (ts - 1)) <= tok), so_raw, -jnp.inf)
        m = jnp.maximum(jnp.maximum(jnp.max(sp, axis=-1, keepdims=True),
                                    jnp.max(so, axis=-1, keepdims=True)), sink_col)
        pp = jnp.exp(sp - m)
        po = jnp.exp(so - m)
        den_all.append(jnp.sum(pp, axis=-1, keepdims=True) + jnp.sum(po, axis=-1, keepdims=True)
                       + jnp.exp(sink_col - m))
        pp_all.append(_mm(pp, v0_ref[b]))
        po_all.append(po)
    pv_own = _mm(jnp.concatenate(po_all, axis=0), _pad_rows(v_own))
    os_rows = []
    for b in seqs:
        pv_b = (pp_all[b] + pv_own[b * rows:(b + 1) * rows]) / den_all[b]
        os_rows.append(jnp.concatenate(
            [jnp.where(lane_lo_t, pv_b[p * ts:(p + 1) * ts],
                       pv_b[(SWA_HEADS // 2 + p) * ts:(SWA_HEADS // 2 + p + 1) * ts])
             for p in range(SWA_HEADS // 2)], axis=1))
        kout_ref[b] = jnp.concatenate([k0_ref[b, ts:, :], grp(k_own, b)], axis=0)
        vout_ref[b] = jnp.concatenate([v0_ref[b, ts:, :], grp(v_own, b)], axis=0)
    o_full = jnp.concatenate(og + [jnp.concatenate(os_rows, axis=0)], axis=-1)
    for b in seqs:
        o_ref[b] = grp(o_full, b).astype(o_ref.dtype)


def _mixer_sample(proj3d, s0, conv0, k0, v0, convw, alog_row, dt_row, gnorm, sinks, rc, rs1, rs2):
    bsz, seq, _ = proj3d.shape
    nb = SAMPLE_NB
    assert seq == SAMPLE_T and bsz % nb == 0
    seq_blk = lambda shape: pl.BlockSpec((nb,) + shape, lambda i: (i,) + (0,) * len(shape))
    const_blk = lambda shape: pl.BlockSpec(shape, lambda i: (0,) * len(shape))
    return pl.pallas_call(
        _mixer_sample_kernel,
        grid=(bsz // nb,),
        in_specs=[
            pl.BlockSpec(memory_space=pltpu.SMEM),
            seq_blk((seq, P_W)),
            seq_blk((GDN_HEADS, GDN_DK, GDN_DV)),
            seq_blk((GDN_CONV - 1, GDN_CONV_W)),
            seq_blk((WINDOW, SWA_KV_W)),
            seq_blk((WINDOW, SWA_KV_W)),
            const_blk((GDN_CONV, GDN_CONV_W)),
            const_blk((1, LANES)),
            const_blk((1, LANES)),
            const_blk((1, GDN_DV)),
            const_blk((seq, LANES)),
            const_blk((seq, LANES)),
            const_blk((seq, LANES)),
        ],
        out_specs=[
            seq_blk((seq, D_MODEL)),
            seq_blk((GDN_HEADS, GDN_DK, GDN_DV)),
            seq_blk((GDN_CONV - 1, GDN_CONV_W)),
            seq_blk((WINDOW, SWA_KV_W)),
            seq_blk((WINDOW, SWA_KV_W)),
        ],
        out_shape=[
            jax.ShapeDtypeStruct((bsz, seq, D_MODEL), BF16),
            jax.ShapeDtypeStruct((bsz, GDN_HEADS, GDN_DK, GDN_DV), F32),
            jax.ShapeDtypeStruct((bsz, GDN_CONV - 1, GDN_CONV_W), F32),
            jax.ShapeDtypeStruct((bsz, WINDOW, SWA_KV_W), F32),
            jax.ShapeDtypeStruct((bsz, WINDOW, SWA_KV_W), F32),
        ],
        scratch_shapes=[pltpu.VMEM((2 * SAMPLE_T * nb, GDN_CONV_W), F32)],
        compiler_params=pltpu.CompilerParams(
            dimension_semantics=("arbitrary",), vmem_limit_bytes=VMEM_LIMIT_BYTES),
        name="mixer_sample",
    )(sinks, proj3d, s0, conv0, k0, v0, convw, alog_row, dt_row, gnorm, rc, rs1, rs2)


def _rope_tables(pos):
    half = ROT_DIM // 2
    inv = ROPE_THETA ** (-jnp.arange(half, dtype=F32) * 2.0 / ROT_DIM)
    ang = pos.astype(F32)[:, None] * inv[None, :]
    cos, sin = jnp.cos(ang), jnp.sin(ang)
    n = pos.shape[0]
    rest = SWA_HD - ROT_DIM
    c = jnp.concatenate([cos, cos, jnp.ones((n, rest), F32)], axis=1)
    s1 = jnp.concatenate([-sin, jnp.zeros((n, half + rest), F32)], axis=1)
    s2 = jnp.concatenate([jnp.zeros((n, half), F32), sin, jnp.zeros((n, rest), F32)], axis=1)
    reps = LANES // SWA_HD
    return jnp.tile(c, (1, reps)), jnp.tile(s1, (1, reps)), jnp.tile(s2, (1, reps))


def _projection_tiles(w_in):
    o_z = GDN_CONV_W
    o_a = o_z + GDN_V_W
    o_qs = o_a + 2 * GDN_HEADS
    o_ks = o_qs + SWA_Q_W
    o_vs = o_ks + SWA_KV_W
    cols = lambda lo, width: w_in[:, lo:lo + width]
    tiles = [cols(t * LANES, LANES) for t in range((o_z + GDN_V_W) // LANES)]
    half = SWA_HEADS // 2
    tiles += [jnp.concatenate([cols(o_qs + p * SWA_HD, SWA_HD), cols(o_qs + (half + p) * SWA_HD, SWA_HD)], axis=1)
              for p in range(half)]
    tiles += [cols(o_ks, SWA_KV_W), cols(o_vs, SWA_KV_W)]
    tiles.append(jnp.concatenate([cols(o_a, 2 * GDN_HEADS),
                                  jnp.zeros((D_MODEL, LANES - 2 * GDN_HEADS), w_in.dtype)], axis=1))
    return tiles


def _layer_weights(norm_ffn1, ffn1_gu, ffn1_down, norm_mix, w_in, conv_w, a_log, dt_bias, gdn_norm, sinks,
                   w_out, norm_ffn2, ffn2_gu, ffn2_down, norm_ple, ple_proj, ple_gate, norm_final):
    tiles = _projection_tiles(w_in)
    win_p = jnp.concatenate(tiles, axis=1).astype(BF16)
    win_prep = jnp.concatenate([tiles[t] for t in PREP_TILE_ORDER], axis=1).astype(BF16)
    half = SWA_HEADS // 2
    head_rows = lambda hh: w_out[GDN_V_W + hh * SWA_HD:GDN_V_W + (hh + 1) * SWA_HD]
    wout_p = jnp.concatenate([w_out[:GDN_V_W]] + [head_rows(hh) for p in range(half) for hh in (p, half + p)],
                             axis=0).astype(BF16)
    pad_row = lambda v: jnp.concatenate([v.astype(F32), jnp.zeros((LANES - v.shape[0],), F32)])[None, :]
    return dict(
        g1=norm_ffn1[None, :], wgu1=ffn1_gu.astype(BF16), wd1=ffn1_down.astype(BF16),
        gm=norm_mix[None, :], win=win_p, win_prep=win_prep, convw=conv_w, alog=pad_row(a_log), dt=pad_row(dt_bias),
        gnorm=gdn_norm[None, :], sinks=sinks.astype(F32), wout=wout_p,
        g2=norm_ffn2[None, :], wgu2=ffn2_gu.astype(BF16), wd2=ffn2_down.astype(BF16),
        gp=norm_ple[None, :], wpp=ple_proj.astype(BF16), wpg=ple_gate.astype(BF16), gf=norm_final[None, :])


def _group(x, p, w, state, pos0):
    bsz, seq, _ = x.shape
    rc, rs1, rs2 = _rope_tables(pos0 + jnp.arange(seq))
    x2d = x.reshape(bsz * seq, D_MODEL)
    if state is None:
        h2d, prep2d, conv_new = _ffn_in_prep(x2d, seq, w["g1"], w["wgu1"], w["wd1"], w["gm"], w["win_prep"],
                                             w["convw"], w["alog"], w["dt"], rc, rs1, rs2)
        o, s_new, k_new, v_new = _mixer_prompt(prep2d.reshape(bsz, seq, P_W), w["gnorm"], w["sinks"])
    else:
        h2d, proj2d = _ffn_in(x2d, w["g1"], w["wgu1"], w["wd1"], w["gm"], w["win"])
        s0, conv0, k0, v0 = state
        o, s_new, conv_new, k_new, v_new = _mixer_sample(
            proj2d.reshape(bsz, seq, P_W), s0, conv0, k0, v0, w["convw"], w["alog"], w["dt"], w["gnorm"],
            w["sinks"], rc, rs1, rs2)
    y = _ffn_out(o.reshape(bsz * seq, D_MODEL), h2d, p.reshape(bsz * seq, PLE_DIM), w["wout"], w["g2"],
                 w["wgu2"], w["wd2"], w["gp"], w["wpg"], w["wpp"], w["gf"])
    kv_shape = (1, bsz, WINDOW, SWA_KV_HEADS, SWA_HD)
    return (y.reshape(bsz, seq, D_MODEL), s_new[None], conv_new[None], k_new.reshape(kv_shape),
            v_new.reshape(kv_shape))


def kernel(x_prompt, x_sample, state_gdn, state_conv, cache_swa_k, cache_swa_v, p_prompt, p_sample, norm_ffn1, ffn1_gu, ffn1_down, norm_mix, w_in, conv_w, a_log, dt_bias, gdn_norm, sinks, w_out, norm_ffn2, ffn2_gu, ffn2_down, norm_ple, ple_proj, ple_gate, norm_final):
    assert state_gdn.shape[0] == 1, "one layer"
    w = _layer_weights(norm_ffn1[0], ffn1_gu[0], ffn1_down[0], norm_mix[0], w_in[0], conv_w[0], a_log[0],
                       dt_bias[0], gdn_norm[0], sinks[0], w_out[0], norm_ffn2[0], ffn2_gu[0], ffn2_down[0],
                       norm_ple[0], ple_proj[0], ple_gate[0], norm_final)
    bs = x_sample.shape[0]
    yp, sg_p, sc_p, kk_p, vv_p = _group(x_prompt, p_prompt[0], w, None, 0)
    ys, sg_s, sc_s, kk_s, vv_s = _group(
        x_sample, p_sample[0], w,
        (state_gdn[0], state_conv[0], cache_swa_k[0].reshape(bs, WINDOW, SWA_KV_W),
         cache_swa_v[0].reshape(bs, WINDOW, SWA_KV_W)), PAST_LEN)
    return (yp, ys, sg_p, sc_p, kk_p, vv_p, sg_s, sc_s, kk_s, vv_s)
```

```python
import functools

import jax
import jax.numpy as jnp
from jax.experimental import pallas as pl
from jax.experimental.pallas import tpu as pltpu

F32 = jnp.float32
BF16 = jnp.bfloat16

D_MODEL = 1024
D_FF = 2816
PLE_DIM = 256
NORM_EPS = 1e-6
GDN_HEADS = 4
GDN_DK = 128
GDN_DV = 128
GDN_CONV = 4
GDN_QK_W = GDN_HEADS * GDN_DK
GDN_V_W = GDN_HEADS * GDN_DV
GDN_CONV_W = 2 * GDN_QK_W + GDN_V_W
SWA_HEADS = 8
SWA_KV_HEADS = 2
SWA_HD = 64
SWA_Q_W = SWA_HEADS * SWA_HD
SWA_KV_W = SWA_KV_HEADS * SWA_HD
WINDOW = 128
ROT_DIM = SWA_HD // 4
ROPE_THETA = 500000.0
PAST_LEN = 16384

LANES = 128
SUBLANES = 8
VMEM_LIMIT_BYTES = 56 * 1024 * 1024

P_QKV = 0
P_Z = P_QKV + GDN_CONV_W
P_QS = P_Z + GDN_V_W
P_KS = P_QS + SWA_Q_W
P_VS = P_KS + SWA_KV_W
P_AB = P_VS + SWA_KV_W
P_W = P_AB + LANES

FF_CHUNK = 256
TOKEN_TILE = 512

CHUNK = 64
CAT_W = GDN_HEADS * CHUNK
N_CONV_TILES = GDN_CONV_W // LANES
N_PROJ_TILES = P_W // LANES
assert N_CONV_TILES == N_PROJ_TILES - N_CONV_TILES + 1
PREP_TILE_ORDER = (tuple(t for pair in zip(range(N_CONV_TILES - 1), range(N_CONV_TILES, N_PROJ_TILES)) for t in pair)
                   + (N_CONV_TILES - 1,))
SAMPLE_T = SUBLANES
SAMPLE_NB = CHUNK // SAMPLE_T


def _rms(x, g):
    return x * jax.lax.rsqrt(jnp.mean(x * x, axis=-1, keepdims=True) + NORM_EPS) * g


def _sigmoid(x):
    return 1.0 / (1.0 + jnp.exp(-x))


def _silu(x):
    return x * _sigmoid(x)


def _mm(a, b):
    return jnp.dot(a.astype(BF16), b.astype(BF16), preferred_element_type=F32)


def _mm_nt(a, b):
    return jax.lax.dot_general(a.astype(BF16), b.astype(BF16), (((1,), (1,)), ((), ())),
                               preferred_element_type=F32)


def _swiglu_acc(n_bf16, wgu_ref, wd_ref):
    rows = n_bf16.shape[0]
    acc = jnp.zeros((rows, D_MODEL), F32)
    for c in range(D_FF // FF_CHUNK):
        lo = c * FF_CHUNK
        gate = jnp.dot(n_bf16, wgu_ref[:, lo:lo + FF_CHUNK], preferred_element_type=F32)
        up = jnp.dot(n_bf16, wgu_ref[:, D_FF + lo:D_FF + lo + FF_CHUNK], preferred_element_type=F32)
        act = (_silu(gate) * up).astype(BF16)
        acc = acc + jnp.dot(act, wd_ref[lo:lo + FF_CHUNK, :], preferred_element_type=F32)
    return acc


def _ffn_in_kernel(x_ref, g1_ref, wgu_ref, wd_ref, gm_ref, win_ref, h_ref, proj_ref):
    x = x_ref[...]
    n1 = _rms(x, g1_ref[...]).astype(BF16)
    h = x + 0.5 * _swiglu_acc(n1, wgu_ref, wd_ref)
    h_ref[...] = h
    n = _rms(h, gm_ref[...]).astype(BF16)
    proj_ref[...] = jnp.dot(n, win_ref[...], preferred_element_type=F32)


def _const_spec(shape):
    return pl.BlockSpec(shape, lambda i: (0,) * len(shape), pipeline_mode=pl.Buffered(1))


def _ffn_in(x2d, g1, wgu, wd, gm, win):
    n_tok = x2d.shape[0]
    tm = min(TOKEN_TILE, n_tok)
    assert n_tok % tm == 0
    return pl.pallas_call(
        _ffn_in_kernel,
        grid=(n_tok // tm,),
        in_specs=[
            pl.BlockSpec((tm, D_MODEL), lambda i: (i, 0)),
            _const_spec((1, D_MODEL)),
            _const_spec((D_MODEL, 2 * D_FF)),
            _const_spec((D_FF, D_MODEL)),
            _const_spec((1, D_MODEL)),
            _const_spec((D_MODEL, P_W)),
        ],
        out_specs=[
            pl.BlockSpec((tm, D_MODEL), lambda i: (i, 0)),
            pl.BlockSpec((tm, P_W), lambda i: (i, 0)),
        ],
        out_shape=[
            jax.ShapeDtypeStruct((n_tok, D_MODEL), F32),
            jax.ShapeDtypeStruct((n_tok, P_W), F32),
        ],
        compiler_params=pltpu.CompilerParams(
            dimension_semantics=("arbitrary",), vmem_limit_bytes=VMEM_LIMIT_BYTES),
        name="ffn_in",
    )(x2d, g1, wgu, wd, gm, win)


def _ffn_in_prep_kernel(x_ref, g1_ref, wgu_ref, wd_ref, gm_ref, win_ref, convw_ref, alog_ref, dt_ref,
                        rc_ref, rs1_ref, rs2_ref, h_ref, prep_ref, convout_ref, n_st, xext_ref, *, tiles_per_seq):
    i = pl.program_id(0)
    tm = x_ref.shape[0]
    piece = 2 * LANES

    @pl.when(i == 0)
    def _init():
        n_st[...] = jnp.zeros(n_st.shape, n_st.dtype)
        xext_ref[...] = jnp.zeros(xext_ref.shape, F32)

    n_prev = n_st[...]
    first_of_seq = ((i - 1) % tiles_per_seq) == 0

    def prep_tile(lo, val):
        cols = slice(lo, lo + LANES)
        if lo < P_Z:
            hist = jnp.where(first_of_seq, 0.0, xext_ref[:, cols])
            xext_ref[:, cols] = val[tm - SUBLANES:tm]
            convout_ref[0, :, cols] = val[tm - (GDN_CONV - 1):tm]
            ext = jnp.concatenate([hist, val], axis=0)
            conv = val * convw_ref[GDN_CONV - 1:GDN_CONV, cols]
            for j in range(1, GDN_CONV):
                conv = conv + pltpu.roll(ext, j, 0)[SUBLANES:] * convw_ref[GDN_CONV - 1 - j:GDN_CONV - j, cols]
            act = _silu(conv)
            if lo < 2 * GDN_QK_W:
                scale = GDN_DK ** -0.5 if lo < GDN_QK_W else 1.0
                act = act * (jax.lax.rsqrt(jnp.sum(act * act, axis=-1, keepdims=True) + 1e-6) * scale)
            prep_ref[:, cols] = act
        elif lo < P_QS:
            prep_ref[:, cols] = _silu(val)
        elif lo < P_VS:
            prep_ref[:, cols] = _rope(val, rc_ref[...], rs1_ref[...], rs2_ref[...])
        elif lo < P_AB:
            prep_ref[:, cols] = val
        else:
            g_all, beta_all = _gates(val, alog_ref[...], dt_ref[...])
            lane = jax.lax.broadcasted_iota(jnp.int32, (tm, LANES), 1)
            prep_ref[:, cols] = jnp.where(lane < GDN_HEADS, g_all, beta_all)

    def proj_piece(lo):
        width = min(piece, P_W - lo)
        proj = jnp.dot(n_prev, win_ref[:, lo:lo + width], preferred_element_type=F32)
        for off in range(0, width, LANES):
            prep_tile(PREP_TILE_ORDER[(lo + off) // LANES] * LANES, proj[:, off:off + LANES])

    x = x_ref[...]
    n1 = _rms(x, g1_ref[...]).astype(BF16)
    acc = jnp.zeros((tm, D_MODEL), F32)
    for c in range(D_FF // FF_CHUNK):
        lo = c * FF_CHUNK
        gate = jnp.dot(n1, wgu_ref[:, lo:lo + FF_CHUNK], preferred_element_type=F32)
        up = jnp.dot(n1, wgu_ref[:, D_FF + lo:D_FF + lo + FF_CHUNK], preferred_element_type=F32)
        act = (_silu(gate) * up).astype(BF16)
        acc = acc + jnp.dot(act, wd_ref[lo:lo + FF_CHUNK, :], preferred_element_type=F32)
        proj_piece(c * piece)
    for lo in range((D_FF // FF_CHUNK) * piece, P_W, piece):
        proj_piece(lo)
    h = x + 0.5 * acc
    h_ref[...] = h
    n_st[...] = _rms(h, gm_ref[...]).astype(BF16)


def _ffn_in_prep(x2d, seq_len, g1, wgu, wd, gm, win, convw, alog_row, dt_row, rc, rs1, rs2):
    n_tok = x2d.shape[0]
    tm = TOKEN_TILE
    assert seq_len % tm == 0 and n_tok % seq_len == 0
    tiles_per_seq = seq_len // tm
    n_tiles = n_tok // tm
    cur = lambda i: jnp.minimum(i, n_tiles - 1)
    prev = lambda i: jnp.maximum(i - 1, 0)
    rope_spec = pl.BlockSpec((tm, LANES), lambda i: (prev(i) % tiles_per_seq, 0))
    return pl.pallas_call(
        functools.partial(_ffn_in_prep_kernel, tiles_per_seq=tiles_per_seq),
        grid=(n_tiles + 1,),
        in_specs=[
            pl.BlockSpec((tm, D_MODEL), lambda i: (cur(i), 0)),
            _const_spec((1, D_MODEL)),
            _const_spec((D_MODEL, 2 * D_FF)),
            _const_spec((D_FF, D_MODEL)),
            _const_spec((1, D_MODEL)),
            _const_spec((D_MODEL, P_W)),
            _const_spec((GDN_CONV, GDN_CONV_W)),
            _const_spec((1, LANES)),
            _const_spec((1, LANES)),
            rope_spec, rope_spec, rope_spec,
        ],
        out_specs=[
            pl.BlockSpec((tm, D_MODEL), lambda i: (cur(i), 0)),
            pl.BlockSpec((tm, P_W), lambda i: (prev(i), 0)),
            pl.BlockSpec((1, GDN_CONV - 1, GDN_CONV_W), lambda i: (prev(i) // tiles_per_seq, 0, 0)),
        ],
        out_shape=[
            jax.ShapeDtypeStruct((n_tok, D_MODEL), F32),
            jax.ShapeDtypeStruct((n_tok, P_W), F32),
            jax.ShapeDtypeStruct((n_tok // seq_len, GDN_CONV - 1, GDN_CONV_W), F32),
        ],
        scratch_shapes=[pltpu.VMEM((tm, D_MODEL), BF16), pltpu.VMEM((SUBLANES, GDN_CONV_W), F32)],
        compiler_params=pltpu.CompilerParams(
            dimension_semantics=("arbitrary",), vmem_limit_bytes=VMEM_LIMIT_BYTES),
        name="ffn_in_prep",
    )(x2d, g1, wgu, wd, gm, win, convw, alog_row, dt_row, rc, rs1, rs2)


def _ffn_out_kernel(o_ref, h_ref, p_ref, wout_ref, g2_ref, wgu_ref, wd_ref, gp_ref, wpg_ref, wpp_ref,
                    gf_ref, y_ref):
    tm = h_ref.shape[0]
    halves = (slice(0, tm // 2), slice(tm // 2, tm))
    h_in, n2 = [], []
    for r in halves:
        h_r = h_ref[r, :] + jnp.dot(o_ref[r, :], wout_ref[...], preferred_element_type=F32)
        h_in.append(h_r)
        n2.append(_rms(h_r, g2_ref[...]).astype(BF16))
    h = jnp.concatenate(h_in, axis=0) + 0.5 * _swiglu_acc(jnp.concatenate(n2, axis=0), wgu_ref, wd_ref)
    for r in halves:
        h_r = h[r]
        npl = _rms(h_r, gp_ref[...]).astype(BF16)
        gate = _sigmoid(jnp.dot(npl, wpg_ref[...], preferred_element_type=F32))
        pe = jnp.dot(p_ref[r, :].astype(BF16), wpp_ref[...], preferred_element_type=F32)
        y_ref[r, :] = _rms(h_r + gate * pe, gf_ref[...])


def _ffn_out(o2d, h2d, p2d, wout, g2, wgu, wd, gp, wpg, wpp, gf):
    n_tok = h2d.shape[0]
    tm = min(TOKEN_TILE, n_tok)
    assert n_tok % tm == 0
    return pl.pallas_call(
        _ffn_out_kernel,
        grid=(n_tok // tm,),
        in_specs=[
            pl.BlockSpec((tm, D_MODEL), lambda i: (i, 0)),
            pl.BlockSpec((tm, D_MODEL), lambda i: (i, 0)),
            pl.BlockSpec((tm, PLE_DIM), lambda i: (i, 0)),
            _const_spec((D_MODEL, D_MODEL)),
            _const_spec((1, D_MODEL)),
            _const_spec((D_MODEL, 2 * D_FF)),
            _const_spec((D_FF, D_MODEL)),
            _const_spec((1, D_MODEL)),
            _const_spec((D_MODEL, D_MODEL)),
            _const_spec((PLE_DIM, D_MODEL)),
            _const_spec((1, D_MODEL)),
        ],
        out_specs=pl.BlockSpec((tm, D_MODEL), lambda i: (i, 0)),
        out_shape=jax.ShapeDtypeStruct((n_tok, D_MODEL), F32),
        compiler_params=pltpu.CompilerParams(
            dimension_semantics=("arbitrary",), vmem_limit_bytes=VMEM_LIMIT_BYTES),
        name="ffn_out",
    )(o2d, h2d, p2d, wout, g2, wgu, wd, gp, wpg, wpp, gf)


def _pad_rows(x):
    rows = x.shape[0]
    if rows == LANES:
        return x
    return jnp.concatenate([x, jnp.zeros((LANES - rows, x.shape[1]), x.dtype)], axis=0)


def _split3(x):
    hi = x.astype(BF16)
    r = x - hi.astype(F32)
    mid = r.astype(BF16)
    lo = (r - mid.astype(F32)).astype(BF16)
    return hi, mid, lo


def _mm_exact_rhs(a_bf16, x):
    hi, mid, lo = _split3(x)
    return (jnp.dot(a_bf16, hi, preferred_element_type=F32) + jnp.dot(a_bf16, mid, preferred_element_type=F32)
            + jnp.dot(a_bf16, lo, preferred_element_type=F32))


def _mm_exact_lhs(x, a_bf16):
    hi, mid, lo = _split3(x)
    return (jnp.dot(hi, a_bf16, preferred_element_type=F32) + jnp.dot(mid, a_bf16, preferred_element_type=F32)
            + jnp.dot(lo, a_bf16, preferred_element_type=F32))


def _bd_cat(y, lane_lo_chunk):
    zero_tile = jnp.zeros((CHUNK, LANES), F32)
    blocks = []
    for hh in range(GDN_HEADS):
        y_tile = y[:, (hh // 2) * LANES:(hh // 2 + 1) * LANES]
        keep = jnp.where(lane_lo_chunk, y_tile, 0.0) if hh % 2 == 0 else jnp.where(lane_lo_chunk, 0.0, y_tile)
        blocks.append(jnp.concatenate([keep, zero_tile] if hh < 2 else [zero_tile, keep], axis=1))
    return jnp.concatenate(blocks, axis=0).astype(BF16)


def _bd_wide(y):
    zero_tile = jnp.zeros((CHUNK, LANES), F32)
    blocks = []
    for hh in range(GDN_HEADS):
        tiles = [zero_tile] * GDN_HEADS
        tiles[hh] = y[:, hh * LANES:(hh + 1) * LANES]
        blocks.append(jnp.concatenate(tiles, axis=1))
    return jnp.concatenate(blocks, axis=0).astype(BF16)


def _bd_pair(sa, sb):
    zeros = jnp.zeros(sa.shape, sa.dtype)
    return jnp.concatenate([jnp.concatenate([sa, zeros], axis=1), jnp.concatenate([zeros, sb], axis=1)], axis=0)


def _gates(ab, alog_row, dt_row):
    zab = ab + dt_row
    softplus = jnp.maximum(zab, 0.0) + jnp.log(1.0 + jnp.exp(-jnp.abs(zab)))
    return -jnp.exp(alog_row) * softplus, _sigmoid(ab)


def _head_l2(x):
    parts = []
    for hh in range(GDN_HEADS):
        xh = x[:, hh * GDN_DK:(hh + 1) * GDN_DK]
        parts.append(xh * jax.lax.rsqrt(jnp.sum(xh * xh, axis=-1, keepdims=True) + 1e-6))
    return jnp.concatenate(parts, axis=1)


def _lane_expand(cols, first_lane, rows):
    return jnp.concatenate(
        [jnp.broadcast_to(cols[:, first_lane + hh:first_lane + hh + 1], (rows, LANES))
         for hh in range(GDN_HEADS)], axis=1)


def _rope(x, rc, rs1, rs2):
    return x * rc + pltpu.roll(x, LANES - ROT_DIM // 2, 1) * rs1 + pltpu.roll(x, ROT_DIM // 2, 1) * rs2


def _uw_rhs(vb_all, kbg_all):
    return jnp.concatenate(
        [jnp.concatenate([vb_all[:, hh * GDN_DV:(hh + 1) * GDN_DV],
                          kbg_all[:, hh * GDN_DK:(hh + 1) * GDN_DK]], axis=1)
         for hh in range(GDN_HEADS)], axis=0)


def _split_uw(uw):
    u_all = jnp.concatenate([uw[hh * CHUNK:(hh + 1) * CHUNK, 0:GDN_DV] for hh in range(GDN_HEADS)], axis=1)
    w_all = jnp.concatenate([uw[hh * CHUNK:(hh + 1) * CHUNK, GDN_DV:] for hh in range(GDN_HEADS)], axis=1)
    return u_all, w_all


def _stack_heads_t(x):
    return jnp.transpose(jnp.concatenate(
        [x[:, hh * GDN_DK:(hh + 1) * GDN_DK] for hh in range(GDN_HEADS)], axis=0))


PROMPT_NS = 4


def _mixer_prompt_kernel(sinks_ref, cur_ref, prev_ref, gnorm_ref, o_ref, sout_ref, kout_ref, vout_ref,
                         s_ref, kprev_ref, vprev_ref, gcc_st, gcr_st, beta_st, os_st, *, tb, n_blocks):
    t = pl.program_id(1)
    ns = PROMPT_NS
    nch = tb // CHUNK
    stages = (gcc_st, gcr_st, beta_st, os_st)
    stage_names = ("gcc", "gcr", "beta", "os")

    @pl.when(t == 0)
    def _init():
        for ref in (s_ref, kprev_ref, vprev_ref) + stages:
            ref[...] = jnp.zeros(ref.shape, F32)

    r_t = jax.lax.broadcasted_iota(jnp.int32, (tb, LANES), 0)
    c_t = jax.lax.broadcasted_iota(jnp.int32, (tb, LANES), 1)
    lane_lo = c_t < CHUNK
    lane_lo_chunk = jax.lax.broadcasted_iota(jnp.int32, (CHUNK, LANES), 1) < CHUNK
    lane_lo_row = jax.lax.broadcasted_iota(jnp.int32, (1, LANES), 1) < CHUNK
    same_chunk = (r_t ^ c_t) < CHUNK
    rk = jax.lax.broadcasted_iota(jnp.int32, (tb, 2 * WINDOW), 0)
    ck = jax.lax.broadcasted_iota(jnp.int32, (tb, 2 * WINDOW), 1)
    first_off = jnp.where(t > 0, 0, WINDOW)
    dist = ck - rk
    kmask = ((dist > jnp.where(ck < WINDOW, first_off, -WINDOW))
             & (dist <= jnp.where(ck < WINDOW, 2 * WINDOW, WINDOW)))

    s_cat = [s_ref[sq] for sq in range(ns)]
    k_prev = [kprev_ref[sq] for sq in range(ns)]
    v_prev = [vprev_ref[sq] for sq in range(ns)]
    staged = [{name: ref[sq] for ref, name in zip(stages, stage_names)} for sq in range(ns)]
    for sq in range(ns):
        staged[sq].update(q=prev_ref[sq, :, 0:GDN_QK_W], k=prev_ref[sq, :, GDN_QK_W:2 * GDN_QK_W],
                          v=prev_ref[sq, :, 2 * GDN_QK_W:GDN_CONV_W], zs=prev_ref[sq, :, P_Z:P_Z + GDN_V_W])
    fr = [{} for _ in range(ns)]

    def front(sq):
        f = fr[sq]
        ab = cur_ref[sq, :, P_AB:P_AB + LANES]
        f["beta"] = ab
        f["gcc"] = _mm_exact_rhs(((c_t <= r_t) & same_chunk).astype(BF16), ab)
        f["gcr"] = _mm_exact_lhs(jnp.transpose(ab)[0:SUBLANES, :],
                                 ((r_t <= c_t) & same_chunk).astype(BF16))
        k_own = cur_ref[sq, :, P_KS:P_KS + SWA_KV_W]
        v_own = cur_ref[sq, :, P_VS:P_VS + SWA_KV_W]
        f["k_own"], f["v_own"] = k_own, v_own
        k_cat = jnp.concatenate([k_prev[sq], k_own], axis=0)
        v_cat = jnp.concatenate([v_prev[sq], v_own], axis=0).astype(BF16)
        q_rows = [None] * SWA_HEADS
        for p in range(SWA_HEADS // 2):
            qp = cur_ref[sq, :, P_QS + p * LANES:P_QS + (p + 1) * LANES]
            q_rows[p] = jnp.where(lane_lo, qp, 0.0)
            q_rows[SWA_HEADS // 2 + p] = jnp.where(lane_lo, 0.0, qp)
        s_all = _mm_nt(jnp.concatenate(q_rows, axis=0), k_cat) * (SWA_HD ** -0.5)
        yield
        pv = [None] * SWA_HEADS
        for hh in range(SWA_HEADS):
            sh = jnp.where(kmask, s_all[hh * tb:(hh + 1) * tb], -jnp.inf)
            sk = sinks_ref[hh]
            m = jnp.maximum(jnp.max(sh, axis=-1, keepdims=True), sk)
            ph = jnp.exp(sh - m)
            den = jnp.sum(ph, axis=-1, keepdims=True) + jnp.exp(sk - m)
            pv[hh] = jnp.dot(ph.astype(BF16), v_cat, preferred_element_type=F32) / den
            yield
        f["os"] = jnp.concatenate(
            [jnp.where(lane_lo, pv[p], pv[SWA_HEADS // 2 + p]) for p in range(SWA_HEADS // 2)], axis=1)

    fronts = [front(sq) for sq in range(ns)]

    def fill():
        for gen in fronts:
            next(gen, None)

    units = [(sq, c) for sq in range(ns) for c in range(nch)]
    rows_of = lambda c: slice(c * CHUNK, (c + 1) * CHUNK)
    ri = jax.lax.broadcasted_iota(jnp.int32, (CHUNK, CAT_W), 0)
    ci = jax.lax.broadcasted_iota(jnp.int32, (CHUNK, CAT_W), 1)
    cj = ci & (CHUNK - 1)
    tri = cj <= ri
    strict = cj < ri
    eye = (cj == ri).astype(F32)
    blk = ri ^ cj
    bd = functools.partial(_bd_cat, lane_lo_chunk=lane_lo_chunk)

    drv = []
    for sq in range(ns):
        st = staged[sq]
        gc_full = _lane_expand(st["gcc"], 0, tb)
        beta_full = _lane_expand(st["beta"], GDN_HEADS, tb)
        egc_full = jnp.exp(gc_full)
        kb_all = st["k"] * beta_full
        drv.append(dict(gc_full=gc_full, kb=kb_all, vb=st["v"] * beta_full, kbg=kb_all * egc_full,
                        qd=st["q"] * egc_full, gcr=st["gcr"], gcr_sw=pltpu.roll(st["gcr"], CHUNK, 1)))
    fill()
    a_mat, qk, xinv, pw = {}, {}, {}, {}
    for u in units:
        sq, c = u
        d, rows = drv[sq], rows_of(c)
        gcc_pairs, gcr_pairs = [], []
        for p in range(GDN_HEADS // 2):
            gcc_pairs.append(jnp.where(lane_lo_chunk, d["gc_full"][rows, (2 * p) * LANES:(2 * p + 1) * LANES],
                                       d["gc_full"][rows, (2 * p + 1) * LANES:(2 * p + 2) * LANES]))
            ra = (d["gcr"] if c == 0 else d["gcr_sw"])[2 * p:2 * p + 1, :]
            rb_ = (d["gcr_sw"] if c == 0 else d["gcr"])[2 * p + 1:2 * p + 2, :]
            gcr_pairs.append(jnp.where(lane_lo_row, ra, rb_))
        gcc = jnp.concatenate(gcc_pairs, axis=1)
        gcr = jnp.concatenate(gcr_pairs, axis=1)
        decay = jnp.exp(jnp.where(tri, gcc - gcr, -jnp.inf))
        kq = jax.lax.dot_general(
            jnp.concatenate([d["kb"][rows], staged[sq]["q"][rows]], axis=0).astype(BF16),
            _bd_wide(staged[sq]["k"][rows]), (((1,), (1,)), ((), ())), preferred_element_type=F32)
        a_mat[u] = jnp.where(strict, kq[0:CHUNK] * decay, 0.0)
        qk[u] = kq[CHUNK:] * decay
    fill()
    for u in units:
        a0 = jnp.where(blk < SUBLANES, a_mat[u], 0.0)
        xinv[u] = eye - a0
        pw[u] = _mm(a0, bd(a0))
    fill()
    for u in units:
        pw_bd = bd(pw[u])
        xinv[u] = xinv[u] + _mm(xinv[u], pw_bd)
        pw[u] = _mm(pw[u], pw_bd)
    fill()
    for u in units:
        xinv[u] = xinv[u] + _mm(xinv[u], bd(pw[u]))
    fill()
    s = SUBLANES
    while s < CHUNK:
        y = {u: _mm(jnp.where((blk < 2 * s) & (blk >= s), a_mat[u], 0.0), bd(xinv[u])) for u in units}
        fill()
        for u in units:
            xinv[u] = xinv[u] - _mm(xinv[u], bd(y[u]))
        fill()
        s *= 2
    prep = {}
    for u in units:
        sq, c = u
        d, rows = drv[sq], rows_of(c)
        u_all, w_all = _split_uw(_mm(bd(xinv[u]), _uw_rhs(d["vb"][rows], d["kbg"][rows])))
        glast = d["gc_full"][c * CHUNK + CHUNK - 1:c * CHUNK + CHUNK, :]
        kdec_t = _stack_heads_t(staged[sq]["k"][rows] * jnp.exp(glast - d["gc_full"][rows]))
        prep[u] = (u_all, w_all, kdec_t, jnp.exp(glast))
    fill()

    o_chunks = [[] for _ in range(ns)]
    for c in range(nch):
        rows = rows_of(c)
        ws, qs = [[] for _ in range(ns)], [[] for _ in range(ns)]
        for sq in range(ns):
            w_all = prep[(sq, c)][1]
            for p in range(GDN_HEADS // 2):
                cols = slice(2 * p * LANES, (2 * p + 2) * LANES)
                r = _mm(jnp.concatenate([w_all[:, cols], drv[sq]["qd"][rows, cols]], axis=0),
                        _bd_pair(s_cat[sq][:, (2 * p) * LANES:(2 * p + 1) * LANES],
                                 s_cat[sq][:, (2 * p + 1) * LANES:(2 * p + 2) * LANES]))
                ws[sq].append(r[0:CHUNK])
                qs[sq].append(r[CHUNK:])
        fill()
        for sq in range(ns):
            u_all, _, kdec_t, eg_last = prep[(sq, c)]
            v_new = u_all - jnp.concatenate(ws[sq], axis=1)
            r2 = _mm(jnp.concatenate([qk[(sq, c)], kdec_t], axis=0), _bd_wide(v_new))
            o_chunks[sq].append(jnp.concatenate(qs[sq], axis=1) + r2[0:CHUNK])
            s_cat[sq] = s_cat[sq] * eg_last + r2[CHUNK:]
        fill()
    for gen in fronts:
        for _ in gen:
            pass

    for sq in range(ns):
        o_gdn = jnp.concatenate(o_chunks[sq], axis=0)
        zs_prev = staged[sq]["zs"]
        og = [_rms(o_gdn[:, hh * GDN_DV:(hh + 1) * GDN_DV], gnorm_ref[...])
              * zs_prev[:, hh * GDN_DV:(hh + 1) * GDN_DV] for hh in range(GDN_HEADS)]
        o_ref[sq] = jnp.concatenate(og + [staged[sq]["os"]], axis=-1).astype(o_ref.dtype)
        s_ref[sq] = s_cat[sq]
        kprev_ref[sq] = fr[sq]["k_own"]
        vprev_ref[sq] = fr[sq]["v_own"]
        for ref, name in zip(stages, stage_names):
            ref[sq] = fr[sq][name]

    @pl.when(t == n_blocks - 1)
    def _window_out():
        for sq in range(ns):
            kout_ref[sq] = fr[sq]["k_own"]
            vout_ref[sq] = fr[sq]["v_own"]

    @pl.when(t == n_blocks)
    def _state_out():
        for sq in range(ns):
            for hh in range(GDN_HEADS):
                sout_ref[sq, hh] = s_ref[sq, :, hh * GDN_DV:(hh + 1) * GDN_DV]


def _mixer_prompt(prep3d, gnorm, sinks):
    bsz, seq, _ = prep3d.shape
    tb, ns = WINDOW, PROMPT_NS
    assert seq % tb == 0 and bsz % ns == 0
    n_blocks = seq // tb
    kern = functools.partial(_mixer_prompt_kernel, tb=tb, n_blocks=n_blocks)
    seq_blk = lambda shape: pl.BlockSpec((ns,) + shape, lambda b, t: (b,) + (0,) * len(shape))
    vmem = lambda *shape: pltpu.VMEM((ns,) + shape, F32)
    return pl.pallas_call(
        kern,
        grid=(bsz // ns, n_blocks + 1),
        in_specs=[
            pl.BlockSpec(memory_space=pltpu.SMEM),
            pl.BlockSpec((ns, tb, P_W), lambda b, t: (b, jnp.minimum(t, n_blocks - 1), 0)),
            pl.BlockSpec((ns, tb, P_W), lambda b, t: (b, jnp.maximum(t - 1, 0), 0)),
            pl.BlockSpec((1, GDN_DV), lambda b, t: (0, 0)),
        ],
        out_specs=[
            pl.BlockSpec((ns, tb, D_MODEL), lambda b, t: (b, jnp.maximum(t - 1, 0), 0)),
            seq_blk((GDN_HEADS, GDN_DK, GDN_DV)),
            seq_blk((WINDOW, SWA_KV_W)),
            seq_blk((WINDOW, SWA_KV_W)),
        ],
        out_shape=[
            jax.ShapeDtypeStruct((bsz, seq, D_MODEL), BF16),
            jax.ShapeDtypeStruct((bsz, GDN_HEADS, GDN_DK, GDN_DV), F32),
            jax.ShapeDtypeStruct((bsz, WINDOW, SWA_KV_W), F32),
            jax.ShapeDtypeStruct((bsz, WINDOW, SWA_KV_W), F32),
        ],
        scratch_shapes=[
            vmem(GDN_DK, GDN_HEADS * GDN_DV),
            vmem(WINDOW, SWA_KV_W),
            vmem(WINDOW, SWA_KV_W),
            vmem(tb, LANES),
            vmem(SUBLANES, LANES),
            vmem(tb, LANES),
            vmem(tb, SWA_Q_W),
        ],
        compiler_params=pltpu.CompilerParams(
            dimension_semantics=("arbitrary", "arbitrary"), vmem_limit_bytes=VMEM_LIMIT_BYTES),
        name="mixer_prompt",
    )(sinks, prep3d, prep3d, gnorm)


def _mixer_sample_kernel(sinks_ref, proj_ref, s0_ref, conv0_ref, k0_ref, v0_ref, convw_ref, alog_ref, dt_ref,
                         gnorm_ref, rc_ref, rs1_ref, rs2_ref,
                         o_ref, sout_ref, convout_ref, kout_ref, vout_ref, xext_ref):
    nb, ts, rows = SAMPLE_NB, SAMPLE_T, CHUNK
    seqs = range(nb)
    grp = lambda x, b: x[b * ts:(b + 1) * ts]

    hist = SUBLANES - (GDN_CONV - 1)
    conv_parts = []
    for b in seqs:
        xext_ref[2 * ts * b + hist:2 * ts * b + ts, :] = conv0_ref[b]
        xext_ref[2 * ts * b + ts:2 * ts * (b + 1), :] = proj_ref[b, :, P_QKV:P_QKV + GDN_CONV_W]
    for b in seqs:
        base = 2 * ts * b + hist
        conv = xext_ref[base:base + ts, :] * convw_ref[0:1, :]
        for j in range(1, GDN_CONV):
            conv = conv + xext_ref[base + j:base + j + ts, :] * convw_ref[j:j + 1, :]
        conv_parts.append(conv)
        convout_ref[b] = xext_ref[2 * ts * (b + 1) - (GDN_CONV - 1):2 * ts * (b + 1), :]
    qkv = _silu(jnp.concatenate(conv_parts, axis=0))
    q_all = _head_l2(qkv[:, 0:GDN_QK_W]) * (GDN_DK ** -0.5)
    k_all = _head_l2(qkv[:, GDN_QK_W:2 * GDN_QK_W])
    v_all = qkv[:, 2 * GDN_QK_W:]

    def proj_cols(lo, width):
        return jnp.concatenate([proj_ref[b, :, lo:lo + width] for b in seqs], axis=0)

    g_all, beta_all = _gates(proj_cols(P_AB, LANES), alog_ref[...], dt_ref[...])

    r_t = jax.lax.broadcasted_iota(jnp.int32, (rows, LANES), 0)
    c_t = jax.lax.broadcasted_iota(jnp.int32, (rows, LANES), 1)
    r_p = jax.lax.broadcasted_iota(jnp.int32, (LANES, LANES), 0)
    c_p = jax.lax.broadcasted_iota(jnp.int32, (LANES, LANES), 1)
    g_pad = _pad_rows(g_all)
    gc_cols = _mm_exact_rhs(((c_t <= r_t) & ((r_t ^ c_t) < ts)).astype(BF16), g_pad)
    gc_rows = _mm_exact_lhs(jnp.transpose(g_pad)[0:SUBLANES, :],
                            ((r_p <= c_p) & ((r_p ^ c_p) < ts)).astype(BF16))
    gc_rows_sw = pltpu.roll(gc_rows, CHUNK, 1)

    lane_lo = c_t < CHUNK
    lane_lo_row = jax.lax.broadcasted_iota(jnp.int32, (1, LANES), 1) < CHUNK
    lane_lo_t = jax.lax.broadcasted_iota(jnp.int32, (ts, LANES), 1) < CHUNK
    gc_full = _lane_expand(gc_cols, 0, rows)
    beta_full = _lane_expand(beta_all, GDN_HEADS, rows)
    egc_full = jnp.exp(gc_full)
    kb_all = k_all * beta_full
    vb_all = v_all * beta_full
    kbg_all = kb_all * egc_full
    qd_all = q_all * egc_full

    ri = jax.lax.broadcasted_iota(jnp.int32, (rows, CAT_W), 0)
    ci = jax.lax.broadcasted_iota(jnp.int32, (rows, CAT_W), 1)
    cj = ci & (CHUNK - 1)
    same_seq = (ri ^ cj) < ts
    eye = (cj == ri).astype(F32)
    bd = functools.partial(_bd_cat, lane_lo_chunk=lane_lo)

    gcc = jnp.concatenate(
        [jnp.where(lane_lo, gc_full[:, (2 * p) * LANES:(2 * p + 1) * LANES],
                   gc_full[:, (2 * p + 1) * LANES:(2 * p + 2) * LANES]) for p in range(GDN_HEADS // 2)], axis=1)
    gcr = jnp.concatenate(
        [jnp.where(lane_lo_row, gc_rows[2 * p:2 * p + 1, :], gc_rows_sw[2 * p + 1:2 * p + 2, :])
         for p in range(GDN_HEADS // 2)], axis=1)
    decay = jnp.exp(jnp.where((cj <= ri) & same_seq, gcc - gcr, -jnp.inf))
    kq = jax.lax.dot_general(
        jnp.concatenate([kb_all, q_all], axis=0).astype(BF16), _bd_wide(k_all),
        (((1,), (1,)), ((), ())), preferred_element_type=F32)
    a_mat = jnp.where(cj < ri, kq[0:rows] * decay, 0.0)
    qk = kq[rows:] * decay
    xinv = eye - a_mat
    pw = _mm(a_mat, bd(a_mat))
    pw_bd = bd(pw)
    xinv = xinv + _mm(xinv, pw_bd)
    pw = _mm(pw, pw_bd)
    xinv = xinv + _mm(xinv, bd(pw))
    u_all, w_all = _split_uw(_mm(bd(xinv), _uw_rhs(vb_all, kbg_all)))
    glast = [gc_full[(b + 1) * ts - 1:(b + 1) * ts, :] for b in seqs]
    glast_rows = jnp.concatenate([jnp.broadcast_to(glast[b], (ts, GDN_QK_W)) for b in seqs], axis=0)
    kdec_t = _stack_heads_t(k_all * jnp.exp(glast_rows - gc_full))

    s_old, ws, qs = [], [], []
    for b in seqs:
        s_b = jnp.concatenate([s0_ref[b, hh] for hh in range(GDN_HEADS)], axis=1)
        s_old.append(s_b)
        ws_b, qs_b = [], []
        for p in range(GDN_HEADS // 2):
            cols = slice(2 * p * LANES, (2 * p + 2) * LANES)
            r = _mm(jnp.concatenate([grp(w_all, b)[:, cols], grp(qd_all, b)[:, cols]], axis=0),
                    _bd_pair(s_b[:, (2 * p) * LANES:(2 * p + 1) * LANES],
                             s_b[:, (2 * p + 1) * LANES:(2 * p + 2) * LANES]))
            ws_b.append(r[0:ts])
            qs_b.append(r[ts:])
        ws.append(jnp.concatenate(ws_b, axis=1))
        qs.append(jnp.concatenate(qs_b, axis=1))
    v_new = u_all - jnp.concatenate(ws, axis=0)
    v_bd = _bd_wide(v_new)
    o_gdn = jnp.concatenate(qs, axis=0) + _mm(qk, v_bd)
    cb = jax.lax.broadcasted_iota(jnp.int32, (GDN_DK, CAT_W), 1) & (CHUNK - 1)
    upd = _mm(jnp.concatenate([jnp.where((cb ^ (b * ts)) < ts, kdec_t, 0.0) for b in seqs], axis=0), v_bd)
    for b in seqs:
        s_new = s_old[b] * jnp.exp(glast[b]) + upd[b * GDN_DK:(b + 1) * GDN_DK]
        for hh in range(GDN_HEADS):
            sout_ref[b, hh] = s_new[:, hh * GDN_DV:(hh + 1) * GDN_DV]
    og = []
    for hh in range(GDN_HEADS):
        oh = _rms(o_gdn[:, hh * GDN_DV:(hh + 1) * GDN_DV], gnorm_ref[...])
        og.append(oh * _silu(proj_cols(P_Z + hh * GDN_DV, GDN_DV)))

    tile8 = lambda x: jnp.concatenate([x] * nb, axis=0)
    rc, rs1, rs2 = tile8(rc_ref[...]), tile8(rs1_ref[...]), tile8(rs2_ref[...])
    k_own = _rope(proj_cols(P_KS, SWA_KV_W), rc, rs1, rs2)
    v_own = proj_cols(P_VS, SWA_KV_W)
    q_rows = [None] * SWA_HEADS
    for p in range(SWA_HEADS // 2):
        qp = _rope(proj_cols(P_QS + p * LANES, LANES), rc, rs1, rs2)
        q_rows[p] = jnp.where(lane_lo, qp, 0.0)
        q_rows[SWA_HEADS // 2 + p] = jnp.where(lane_lo, 0.0, qp)
    scale = SWA_HD ** -0.5
    s_own_all = _mm_nt(jnp.concatenate(q_rows, axis=0), _pad_rows(k_own)) * scale
    tok = r_t & (ts - 1)
    prev_mask = c_t > tok
    sink_col = jnp.zeros((rows, 1), F32)
    for hh in range(SWA_HEADS):
        sink_col = jnp.where((r_t[:, 0:1] ^ (hh * ts)) < ts, sinks_ref[hh], sink_col)
    pp_all, po_all, den_all = [], [], []
    for b in seqs:
        q_b = jnp.concatenate([grp(q_rows[hh], b) for hh in range(SWA_HEADS)], axis=0)
        sp = jnp.where(prev_mask, _mm_nt(q_b, k0_ref[b]) * scale, -jnp.inf)
        so_raw = jnp.concatenate([s_own_all[hh * rows + b * ts:hh * rows + (b + 1) * ts]
                                  for hh in range(SWA_HEADS)], axis=0)
        so = jnp.where(((c_t ^ (b * ts)) < ts) & ((c_t & (ts - 1)) <= tok), so_raw, -jnp.inf)
        m = jnp.maximum(jnp.maximum(jnp.max(sp, axis=-1, keepdims=True),
                                    jnp.max(so, axis=-1, keepdims=True)), sink_col)
        pp = jnp.exp(sp - m)
        po = jnp.exp(so - m)
        den_all.append(jnp.sum(pp, axis=-1, keepdims=True) + jnp.sum(po, axis=-1, keepdims=True)
                       + jnp.exp(sink_col - m))
        pp_all.append(_mm(pp, v0_ref[b]))
        po_all.append(po)
    pv_own = _mm(jnp.concatenate(po_all, axis=0), _pad_rows(v_own))
    os_rows = []
    for b in seqs:
        pv_b = (pp_all[b] + pv_own[b * rows:(b + 1) * rows]) / den_all[b]
        os_rows.append(jnp.concatenate(
            [jnp.where(lane_lo_t, pv_b[p * ts:(p + 1) * ts],
                       pv_b[(SWA_HEADS // 2 + p) * ts:(SWA_HEADS // 2 + p + 1) * ts])
             for p in range(SWA_HEADS // 2)], axis=1))
        kout_ref[b] = jnp.concatenate([k0_ref[b, ts:, :], grp(k_own, b)], axis=0)
        vout_ref[b] = jnp.concatenate([v0_ref[b, ts:, :], grp(v_own, b)], axis=0)
    o_full = jnp.concatenate(og + [jnp.concatenate(os_rows, axis=0)], axis=-1)
    for b in seqs:
        o_ref[b] = grp(o_full, b).astype(o_ref.dtype)


def _mixer_sample(proj3d, s0, conv0, k0, v0, convw, alog_row, dt_row, gnorm, sinks, rc, rs1, rs2):
    bsz, seq, _ = proj3d.shape
    nb = SAMPLE_NB
    assert seq == SAMPLE_T and bsz % nb == 0
    seq_blk = lambda shape: pl.BlockSpec((nb,) + shape, lambda i: (i,) + (0,) * len(shape))
    const_blk = lambda shape: pl.BlockSpec(shape, lambda i: (0,) * len(shape))
    return pl.pallas_call(
        _mixer_sample_kernel,
        grid=(bsz // nb,),
        in_specs=[
            pl.BlockSpec(memory_space=pltpu.SMEM),
            seq_blk((seq, P_W)),
            seq_blk((GDN_HEADS, GDN_DK, GDN_DV)),
            seq_blk((GDN_CONV - 1, GDN_CONV_W)),
            seq_blk((WINDOW, SWA_KV_W)),
            seq_blk((WINDOW, SWA_KV_W)),
            const_blk((GDN_CONV, GDN_CONV_W)),
            const_blk((1, LANES)),
            const_blk((1, LANES)),
            const_blk((1, GDN_DV)),
            const_blk((seq, LANES)),
            const_blk((seq, LANES)),
            const_blk((seq, LANES)),
        ],
        out_specs=[
            seq_blk((seq, D_MODEL)),
            seq_blk((GDN_HEADS, GDN_DK, GDN_DV)),
            seq_blk((GDN_CONV - 1, GDN_CONV_W)),
            seq_blk((WINDOW, SWA_KV_W)),
            seq_blk((WINDOW, SWA_KV_W)),
        ],
        out_shape=[
            jax.ShapeDtypeStruct((bsz, seq, D_MODEL), BF16),
            jax.ShapeDtypeStruct((bsz, GDN_HEADS, GDN_DK, GDN_DV), F32),
            jax.ShapeDtypeStruct((bsz, GDN_CONV - 1, GDN_CONV_W), F32),
            jax.ShapeDtypeStruct((bsz, WINDOW, SWA_KV_W), F32),
            jax.ShapeDtypeStruct((bsz, WINDOW, SWA_KV_W), F32),
        ],
        scratch_shapes=[pltpu.VMEM((2 * SAMPLE_T * nb, GDN_CONV_W), F32)],
        compiler_params=pltpu.CompilerParams(
            dimension_semantics=("arbitrary",), vmem_limit_bytes=VMEM_LIMIT_BYTES),
        name="mixer_sample",
    )(sinks, proj3d, s0, conv0, k0, v0, convw, alog_row, dt_row, gnorm, rc, rs1, rs2)


def _rope_tables(pos):
    half = ROT_DIM // 2
    inv = ROPE_THETA ** (-jnp.arange(half, dtype=F32) * 2.0 / ROT_DIM)
    ang = pos.astype(F32)[:, None] * inv[None, :]
    cos, sin = jnp.cos(ang), jnp.sin(ang)
    n = pos.shape[0]
    rest = SWA_HD - ROT_DIM
    c = jnp.concatenate([cos, cos, jnp.ones((n, rest), F32)], axis=1)
    s1 = jnp.concatenate([-sin, jnp.zeros((n, half + rest), F32)], axis=1)
    s2 = jnp.concatenate([jnp.zeros((n, half), F32), sin, jnp.zeros((n, rest), F32)], axis=1)
    reps = LANES // SWA_HD
    return jnp.tile(c, (1, reps)), jnp.tile(s1, (1, reps)), jnp.tile(s2, (1, reps))


def _projection_tiles(w_in):
    o_z = GDN_CONV_W
    o_a = o_z + GDN_V_W
    o_qs = o_a + 2 * GDN_HEADS
    o_ks = o_qs + SWA_Q_W
    o_vs = o_ks + SWA_KV_W
    cols = lambda lo, width: w_in[:, lo:lo + width]
    tiles = [cols(t * LANES, LANES) for t in range((o_z + GDN_V_W) // LANES)]
    half = SWA_HEADS // 2
    tiles += [jnp.concatenate([cols(o_qs + p * SWA_HD, SWA_HD), cols(o_qs + (half + p) * SWA_HD, SWA_HD)], axis=1)
              for p in range(half)]
    tiles += [cols(o_ks, SWA_KV_W), cols(o_vs, SWA_KV_W)]
    tiles.append(jnp.concatenate([cols(o_a, 2 * GDN_HEADS),
                                  jnp.zeros((D_MODEL, LANES - 2 * GDN_HEADS), w_in.dtype)], axis=1))
    return tiles


def _layer_weights(norm_ffn1, ffn1_gu, ffn1_down, norm_mix, w_in, conv_w, a_log, dt_bias, gdn_norm, sinks,
                   w_out, norm_ffn2, ffn2_gu, ffn2_down, norm_ple, ple_proj, ple_gate, norm_final):
    tiles = _projection_tiles(w_in)
    win_p = jnp.concatenate(tiles, axis=1).astype(BF16)
    win_prep = jnp.concatenate([tiles[t] for t in PREP_TILE_ORDER], axis=1).astype(BF16)
    half = SWA_HEADS // 2
    head_rows = lambda hh: w_out[GDN_V_W + hh * SWA_HD:GDN_V_W + (hh + 1) * SWA_HD]
    wout_p = jnp.concatenate([w_out[:GDN_V_W]] + [head_rows(hh) for p in range(half) for hh in (p, half + p)],
                             axis=0).astype(BF16)
    pad_row = lambda v: jnp.concatenate([v.astype(F32), jnp.zeros((LANES - v.shape[0],), F32)])[None, :]
    return dict(
        g1=norm_ffn1[None, :], wgu1=ffn1_gu.astype(BF16), wd1=ffn1_down.astype(BF16),
        gm=norm_mix[None, :], win=win_p, win_prep=win_prep, convw=conv_w, alog=pad_row(a_log), dt=pad_row(dt_bias),
        gnorm=gdn_norm[None, :], sinks=sinks.astype(F32), wout=wout_p,
        g2=norm_ffn2[None, :], wgu2=ffn2_gu.astype(BF16), wd2=ffn2_down.astype(BF16),
        gp=norm_ple[None, :], wpp=ple_proj.astype(BF16), wpg=ple_gate.astype(BF16), gf=norm_final[None, :])


def _group(x, p, w, state, pos0):
    bsz, seq, _ = x.shape
    rc, rs1, rs2 = _rope_tables(pos0 + jnp.arange(seq))
    x2d = x.reshape(bsz * seq, D_MODEL)
    if state is None:
        h2d, prep2d, conv_new = _ffn_in_prep(x2d, seq, w["g1"], w["wgu1"], w["wd1"], w["gm"], w["win_prep"],
                                             w["convw"], w["alog"], w["dt"], rc, rs1, rs2)
        o, s_new, k_new, v_new = _mixer_prompt(prep2d.reshape(bsz, seq, P_W), w["gnorm"], w["sinks"])
    else:
        h2d, proj2d = _ffn_in(x2d, w["g1"], w["wgu1"], w["wd1"], w["gm"], w["win"])
        s0, conv0, k0, v0 = state
        o, s_new, conv_new, k_new, v_new = _mixer_sample(
            proj2d.reshape(bsz, seq, P_W), s0, conv0, k0, v0, w["convw"], w["alog"], w["dt"], w["gnorm"],
            w["sinks"], rc, rs1, rs2)
    y = _ffn_out(o.reshape(bsz * seq, D_MODEL), h2d, p.reshape(bsz * seq, PLE_DIM), w["wout"], w["g2"],
                 w["wgu2"], w["wd2"], w["gp"], w["wpg"], w["wpp"], w["gf"])
    kv_shape = (1, bsz, WINDOW, SWA_KV_HEADS, SWA_HD)
    return (y.reshape(bsz, seq, D_MODEL), s_new[None], conv_new[None], k_new.reshape(kv_shape),
            v_new.reshape(kv_shape))


def kernel(x_prompt, x_sample, state_gdn, state_conv, cache_swa_k, cache_swa_v, p_prompt, p_sample, norm_ffn1, ffn1_gu, ffn1_down, norm_mix, w_in, conv_w, a_log, dt_bias, gdn_norm, sinks, w_out, norm_ffn2, ffn2_gu, ffn2_down, norm_ple, ple_proj, ple_gate, norm_final):
    assert state_gdn.shape[0] == 1, "one layer"
    w = _layer_weights(norm_ffn1[0], ffn1_gu[0], ffn1_down[0], norm_mix[0], w_in[0], conv_w[0], a_log[0],
                       dt_bias[0], gdn_norm[0], sinks[0], w_out[0], norm_ffn2[0], ffn2_gu[0], ffn2_down[0],
                       norm_ple[0], ple_proj[0], ple_gate[0], norm_final)
    bs = x_sample.shape[0]
    yp, sg_p, sc_p, kk_p, vv_p = _group(x_prompt, p_prompt[0], w, None, 0)
    ys, sg_s, sc_s, kk_s, vv_s = _group(
        x_sample, p_sample[0], w,
        (state_gdn[0], state_conv[0], cache_swa_k[0].reshape(bs, WINDOW, SWA_KV_W),
         cache_swa_v[0].reshape(bs, WINDOW, SWA_KV_W)), PAST_LEN)
    return (yp, ys, sg_p, sc_p, kk_p, vv_p, sg_s, sc_s, kk_s, vv_s)
```

```python
import functools

import jax
import jax.numpy as jnp
from jax.experimental import pallas as pl
from jax.experimental.pallas import tpu as pltpu

F32 = jnp.float32
BF16 = jnp.bfloat16

D_MODEL = 1024
D_FF = 2816
PLE_DIM = 256
NORM_EPS = 1e-6
GDN_HEADS = 4
GDN_DK = 128
GDN_DV = 128
GDN_CONV = 4
GDN_QK_W = GDN_HEADS * GDN_DK
GDN_V_W = GDN_HEADS * GDN_DV
GDN_CONV_W = 2 * GDN_QK_W + GDN_V_W
SWA_HEADS = 8
SWA_KV_HEADS = 2
SWA_HD = 64
SWA_Q_W = SWA_HEADS * SWA_HD
SWA_KV_W = SWA_KV_HEADS * SWA_HD
WINDOW = 128
ROT_DIM = SWA_HD // 4
ROPE_THETA = 500000.0
PAST_LEN = 16384

LANES = 128
SUBLANES = 8
VMEM_LIMIT_BYTES = 56 * 1024 * 1024

P_QKV = 0
P_Z = P_QKV + GDN_CONV_W
P_QS = P_Z + GDN_V_W
P_KS = P_QS + SWA_Q_W
P_VS = P_KS + SWA_KV_W
P_AB = P_VS + SWA_KV_W
P_W = P_AB + LANES

FF_CHUNK = 256
TOKEN_TILE = 512

CHUNK = 64
CAT_W = GDN_HEADS * CHUNK
N_CONV_TILES = GDN_CONV_W // LANES
N_PROJ_TILES = P_W // LANES
assert N_CONV_TILES == N_PROJ_TILES - N_CONV_TILES + 1
PREP_TILE_ORDER = (tuple(t for pair in zip(range(N_CONV_TILES - 1), range(N_CONV_TILES, N_PROJ_TILES)) for t in pair)
                   + (N_CONV_TILES - 1,))
SAMPLE_T = SUBLANES
SAMPLE_NB = CHUNK // SAMPLE_T


def _rms(x, g):
    return x * jax.lax.rsqrt(jnp.mean(x * x, axis=-1, keepdims=True) + NORM_EPS) * g


def _sigmoid(x):
    return 1.0 / (1.0 + jnp.exp(-x))


def _silu(x):
    return x * _sigmoid(x)


def _mm(a, b):
    return jnp.dot(a.astype(BF16), b.astype(BF16), preferred_element_type=F32)


def _mm_nt(a, b):
    return jax.lax.dot_general(a.astype(BF16), b.astype(BF16), (((1,), (1,)), ((), ())),
                               preferred_element_type=F32)


def _swiglu_acc(n_bf16, wgu_ref, wd_ref):
    rows = n_bf16.shape[0]
    acc = jnp.zeros((rows, D_MODEL), F32)
    for c in range(D_FF // FF_CHUNK):
        lo = c * FF_CHUNK
        gate = jnp.dot(n_bf16, wgu_ref[:, lo:lo + FF_CHUNK], preferred_element_type=F32)
        up = jnp.dot(n_bf16, wgu_ref[:, D_FF + lo:D_FF + lo + FF_CHUNK], preferred_element_type=F32)
        act = (_silu(gate) * up).astype(BF16)
        acc = acc + jnp.dot(act, wd_ref[lo:lo + FF_CHUNK, :], preferred_element_type=F32)
    return acc


def _ffn_in_kernel(x_ref, g1_ref, wgu_ref, wd_ref, gm_ref, win_ref, h_ref, proj_ref):
    x = x_ref[...]
    n1 = _rms(x, g1_ref[...]).astype(BF16)
    h = x + 0.5 * _swiglu_acc(n1, wgu_ref, wd_ref)
    h_ref[...] = h
    n = _rms(h, gm_ref[...]).astype(BF16)
    proj_ref[...] = jnp.dot(n, win_ref[...], preferred_element_type=F32)


def _const_spec(shape):
    return pl.BlockSpec(shape, lambda i: (0,) * len(shape), pipeline_mode=pl.Buffered(1))


def _ffn_in(x2d, g1, wgu, wd, gm, win):
    n_tok = x2d.shape[0]
    tm = min(TOKEN_TILE, n_tok)
    assert n_tok % tm == 0
    return pl.pallas_call(
        _ffn_in_kernel,
        grid=(n_tok // tm,),
        in_specs=[
            pl.BlockSpec((tm, D_MODEL), lambda i: (i, 0)),
            _const_spec((1, D_MODEL)),
            _const_spec((D_MODEL, 2 * D_FF)),
            _const_spec((D_FF, D_MODEL)),
            _const_spec((1, D_MODEL)),
            _const_spec((D_MODEL, P_W)),
        ],
        out_specs=[
            pl.BlockSpec((tm, D_MODEL), lambda i: (i, 0)),
            pl.BlockSpec((tm, P_W), lambda i: (i, 0)),
        ],
        out_shape=[
            jax.ShapeDtypeStruct((n_tok, D_MODEL), F32),
            jax.ShapeDtypeStruct((n_tok, P_W), F32),
        ],
        compiler_params=pltpu.CompilerParams(
            dimension_semantics=("arbitrary",), vmem_limit_bytes=VMEM_LIMIT_BYTES),
        name="ffn_in",
    )(x2d, g1, wgu, wd, gm, win)


def _ffn_in_prep_kernel(x_ref, g1_ref, wgu_ref, wd_ref, gm_ref, win_ref, convw_ref, alog_ref, dt_ref,
                        rc_ref, rs1_ref, rs2_ref, h_ref, prep_ref, convout_ref, n_st, xext_ref, *, tiles_per_seq):
    i = pl.program_id(0)
    tm = x_ref.shape[0]
    piece = 2 * LANES

    @pl.when(i == 0)
    def _init():
        n_st[...] = jnp.zeros(n_st.shape, n_st.dtype)
        xext_ref[...] = jnp.zeros(xext_ref.shape, F32)

    n_prev = n_st[...]
    first_of_seq = ((i - 1) % tiles_per_seq) == 0

    def prep_tile(lo, val):
        cols = slice(lo, lo + LANES)
        if lo < P_Z:
            hist = jnp.where(first_of_seq, 0.0, xext_ref[:, cols])
            xext_ref[:, cols] = val[tm - SUBLANES:tm]
            convout_ref[0, :, cols] = val[tm - (GDN_CONV - 1):tm]
            ext = jnp.concatenate([hist, val], axis=0)
            conv = val * convw_ref[GDN_CONV - 1:GDN_CONV, cols]
            for j in range(1, GDN_CONV):
                conv = conv + pltpu.roll(ext, j, 0)[SUBLANES:] * convw_ref[GDN_CONV - 1 - j:GDN_CONV - j, cols]
            act = _silu(conv)
            if lo < 2 * GDN_QK_W:
                scale = GDN_DK ** -0.5 if lo < GDN_QK_W else 1.0
                act = act * (jax.lax.rsqrt(jnp.sum(act * act, axis=-1, keepdims=True) + 1e-6) * scale)
            prep_ref[:, cols] = act
        elif lo < P_QS:
            prep_ref[:, cols] = _silu(val)
        elif lo < P_VS:
            prep_ref[:, cols] = _rope(val, rc_ref[...], rs1_ref[...], rs2_ref[...])
        elif lo < P_AB:
            prep_ref[:, cols] = val
        else:
            g_all, beta_all = _gates(val, alog_ref[...], dt_ref[...])
            lane = jax.lax.broadcasted_iota(jnp.int32, (tm, LANES), 1)
            prep_ref[:, cols] = jnp.where(lane < GDN_HEADS, g_all, beta_all)

    def proj_piece(lo):
        width = min(piece, P_W - lo)
        proj = jnp.dot(n_prev, win_ref[:, lo:lo + width], preferred_element_type=F32)
        for off in range(0, width, LANES):
            prep_tile(PREP_TILE_ORDER[(lo + off) // LANES] * LANES, proj[:, off:off + LANES])

    x = x_ref[...]
    n1 = _rms(x, g1_ref[...]).astype(BF16)
    acc = jnp.zeros((tm, D_MODEL), F32)
    for c in range(D_FF // FF_CHUNK):
        lo = c * FF_CHUNK
        gate = jnp.dot(n1, wgu_ref[:, lo:lo + FF_CHUNK], preferred_element_type=F32)
        up = jnp.dot(n1, wgu_ref[:, D_FF + lo:D_FF + lo + FF_CHUNK], preferred_element_type=F32)
        act = (_silu(gate) * up).astype(BF16)
        acc = acc + jnp.dot(act, wd_ref[lo:lo + FF_CHUNK, :], preferred_element_type=F32)
        proj_piece(c * piece)
    for lo in range((D_FF // FF_CHUNK) * piece, P_W, piece):
        proj_piece(lo)
    h = x + 0.5 * acc
    h_ref[...] = h
    n_st[...] = _rms(h, gm_ref[...]).astype(BF16)


def _ffn_in_prep(x2d, seq_len, g1, wgu, wd, gm, win, convw, alog_row, dt_row, rc, rs1, rs2):
    n_tok = x2d.shape[0]
    tm = TOKEN_TILE
    assert seq_len % tm == 0 and n_tok % seq_len == 0
    tiles_per_seq = seq_len // tm
    n_tiles = n_tok // tm
    cur = lambda i: jnp.minimum(i, n_tiles - 1)
    prev = lambda i: jnp.maximum(i - 1, 0)
    rope_spec = pl.BlockSpec((tm, LANES), lambda i: (prev(i) % tiles_per_seq, 0))
    return pl.pallas_call(
        functools.partial(_ffn_in_prep_kernel, tiles_per_seq=tiles_per_seq),
        grid=(n_tiles + 1,),
        in_specs=[
            pl.BlockSpec((tm, D_MODEL), lambda i: (cur(i), 0)),
            _const_spec((1, D_MODEL)),
            _const_spec((D_MODEL, 2 * D_FF)),
            _const_spec((D_FF, D_MODEL)),
            _const_spec((1, D_MODEL)),
            _const_spec((D_MODEL, P_W)),
            _const_spec((GDN_CONV, GDN_CONV_W)),
            _const_spec((1, LANES)),
            _const_spec((1, LANES)),
            rope_spec, rope_spec, rope_spec,
        ],
        out_specs=[
            pl.BlockSpec((tm, D_MODEL), lambda i: (cur(i), 0)),
            pl.BlockSpec((tm, P_W), lambda i: (prev(i), 0)),
            pl.BlockSpec((1, GDN_CONV - 1, GDN_CONV_W), lambda i: (prev(i) // tiles_per_seq, 0, 0)),
        ],
        out_shape=[
            jax.ShapeDtypeStruct((n_tok, D_MODEL), F32),
            jax.ShapeDtypeStruct((n_tok, P_W), F32),
            jax.ShapeDtypeStruct((n_tok // seq_len, GDN_CONV - 1, GDN_CONV_W), F32),
        ],
        scratch_shapes=[pltpu.VMEM((tm, D_MODEL), BF16), pltpu.VMEM((SUBLANES, GDN_CONV_W), F32)],
        compiler_params=pltpu.CompilerParams(
            dimension_semantics=("arbitrary",), vmem_limit_bytes=VMEM_LIMIT_BYTES),
        name="ffn_in_prep",
    )(x2d, g1, wgu, wd, gm, win, convw, alog_row, dt_row, rc, rs1, rs2)


def _ffn_out_kernel(o_ref, h_ref, p_ref, wout_ref, g2_ref, wgu_ref, wd_ref, gp_ref, wpg_ref, wpp_ref,
                    gf_ref, y_ref):
    tm = h_ref.shape[0]
    halves = (slice(0, tm // 2), slice(tm // 2, tm))
    h_in, n2 = [], []
    for r in halves:
        h_r = h_ref[r, :] + jnp.dot(o_ref[r, :], wout_ref[...], preferred_element_type=F32)
        h_in.append(h_r)
        n2.append(_rms(h_r, g2_ref[...]).astype(BF16))
    h = jnp.concatenate(h_in, axis=0) + 0.5 * _swiglu_acc(jnp.concatenate(n2, axis=0), wgu_ref, wd_ref)
    for r in halves:
        h_r = h[r]
        npl = _rms(h_r, gp_ref[...]).astype(BF16)
        gate = _sigmoid(jnp.dot(npl, wpg_ref[...], preferred_element_type=F32))
        pe = jnp.dot(p_ref[r, :].astype(BF16), wpp_ref[...], preferred_element_type=F32)
        y_ref[r, :] = _rms(h_r + gate * pe, gf_ref[...])


def _ffn_out(o2d, h2d, p2d, wout, g2, wgu, wd, gp, wpg, wpp, gf):
    n_tok = h2d.shape[0]
    tm = min(TOKEN_TILE, n_tok)
    assert n_tok % tm == 0
    return pl.pallas_call(
        _ffn_out_kernel,
        grid=(n_tok // tm,),
        in_specs=[
            pl.BlockSpec((tm, D_MODEL), lambda i: (i, 0)),
            pl.BlockSpec((tm, D_MODEL), lambda i: (i, 0)),
            pl.BlockSpec((tm, PLE_DIM), lambda i: (i, 0)),
            _const_spec((D_MODEL, D_MODEL)),
            _const_spec((1, D_MODEL)),
            _const_spec((D_MODEL, 2 * D_FF)),
            _const_spec((D_FF, D_MODEL)),
            _const_spec((1, D_MODEL)),
            _const_spec((D_MODEL, D_MODEL)),
            _const_spec((PLE_DIM, D_MODEL)),
            _const_spec((1, D_MODEL)),
        ],
        out_specs=pl.BlockSpec((tm, D_MODEL), lambda i: (i, 0)),
        out_shape=jax.ShapeDtypeStruct((n_tok, D_MODEL), F32),
        compiler_params=pltpu.CompilerParams(
            dimension_semantics=("arbitrary",), vmem_limit_bytes=VMEM_LIMIT_BYTES),
        name="ffn_out",
    )(o2d, h2d, p2d, wout, g2, wgu, wd, gp, wpg, wpp, gf)


def _pad_rows(x):
    rows = x.shape[0]
    if rows == LANES:
        return x
    return jnp.concatenate([x, jnp.zeros((LANES - rows, x.shape[1]), x.dtype)], axis=0)


def _split3(x):
    hi = x.astype(BF16)
    r = x - hi.astype(F32)
    mid = r.astype(BF16)
    lo = (r - mid.astype(F32)).astype(BF16)
    return hi, mid, lo


def _mm_exact_rhs(a_bf16, x):
    hi, mid, lo = _split3(x)
    return (jnp.dot(a_bf16, hi, preferred_element_type=F32) + jnp.dot(a_bf16, mid, preferred_element_type=F32)
            + jnp.dot(a_bf16, lo, preferred_element_type=F32))


def _mm_exact_lhs(x, a_bf16):
    hi, mid, lo = _split3(x)
    return (jnp.dot(hi, a_bf16, preferred_element_type=F32) + jnp.dot(mid, a_bf16, preferred_element_type=F32)
            + jnp.dot(lo, a_bf16, preferred_element_type=F32))


def _bd_cat(y, lane_lo_chunk):
    zero_tile = jnp.zeros((CHUNK, LANES), F32)
    blocks = []
    for hh in range(GDN_HEADS):
        y_tile = y[:, (hh // 2) * LANES:(hh // 2 + 1) * LANES]
        keep = jnp.where(lane_lo_chunk, y_tile, 0.0) if hh % 2 == 0 else jnp.where(lane_lo_chunk, 0.0, y_tile)
        blocks.append(jnp.concatenate([keep, zero_tile] if hh < 2 else [zero_tile, keep], axis=1))
    return jnp.concatenate(blocks, axis=0).astype(BF16)


def _bd_wide(y):
    zero_tile = jnp.zeros((CHUNK, LANES), F32)
    blocks = []
    for hh in range(GDN_HEADS):
        tiles = [zero_tile] * GDN_HEADS
        tiles[hh] = y[:, hh * LANES:(hh + 1) * LANES]
        blocks.append(jnp.concatenate(tiles, axis=1))
    return jnp.concatenate(blocks, axis=0).astype(BF16)


def _bd_pair(sa, sb):
    zeros = jnp.zeros(sa.shape, sa.dtype)
    return jnp.concatenate([jnp.concatenate([sa, zeros], axis=1), jnp.concatenate([zeros, sb], axis=1)], axis=0)


def _gates(ab, alog_row, dt_row):
    zab = ab + dt_row
    softplus = jnp.maximum(zab, 0.0) + jnp.log(1.0 + jnp.exp(-jnp.abs(zab)))
    return -jnp.exp(alog_row) * softplus, _sigmoid(ab)


def _head_l2(x):
    parts = []
    for hh in range(GDN_HEADS):
        xh = x[:, hh * GDN_DK:(hh + 1) * GDN_DK]
        parts.append(xh * jax.lax.rsqrt(jnp.sum(xh * xh, axis=-1, keepdims=True) + 1e-6))
    return jnp.concatenate(parts, axis=1)


def _lane_expand(cols, first_lane, rows):
    return jnp.concatenate(
        [jnp.broadcast_to(cols[:, first_lane + hh:first_lane + hh + 1], (rows, LANES))
         for hh in range(GDN_HEADS)], axis=1)


def _rope(x, rc, rs1, rs2):
    return x * rc + pltpu.roll(x, LANES - ROT_DIM // 2, 1) * rs1 + pltpu.roll(x, ROT_DIM // 2, 1) * rs2


def _uw_rhs(vb_all, kbg_all):
    return jnp.concatenate(
        [jnp.concatenate([vb_all[:, hh * GDN_DV:(hh + 1) * GDN_DV],
                          kbg_all[:, hh * GDN_DK:(hh + 1) * GDN_DK]], axis=1)
         for hh in range(GDN_HEADS)], axis=0)


def _split_uw(uw):
    u_all = jnp.concatenate([uw[hh * CHUNK:(hh + 1) * CHUNK, 0:GDN_DV] for hh in range(GDN_HEADS)], axis=1)
    w_all = jnp.concatenate([uw[hh * CHUNK:(hh + 1) * CHUNK, GDN_DV:] for hh in range(GDN_HEADS)], axis=1)
    return u_all, w_all


def _stack_heads_t(x):
    return jnp.transpose(jnp.concatenate(
        [x[:, hh * GDN_DK:(hh + 1) * GDN_DK] for hh in range(GDN_HEADS)], axis=0))


PROMPT_NS = 4


def _mixer_prompt_kernel(sinks_ref, cur_ref, prev_ref, gnorm_ref, o_ref, sout_ref, kout_ref, vout_ref,
                         s_ref, kprev_ref, vprev_ref, gcc_st, gcr_st, beta_st, os_st, *, tb, n_blocks):
    t = pl.program_id(1)
    ns = PROMPT_NS
    nch = tb // CHUNK
    stages = (gcc_st, gcr_st, beta_st, os_st)
    stage_names = ("gcc", "gcr", "beta", "os")

    @pl.when(t == 0)
    def _init():
        for ref in (s_ref, kprev_ref, vprev_ref) + stages:
            ref[...] = jnp.zeros(ref.shape, F32)

    r_t = jax.lax.broadcasted_iota(jnp.int32, (tb, LANES), 0)
    c_t = jax.lax.broadcasted_iota(jnp.int32, (tb, LANES), 1)
    lane_lo = c_t < CHUNK
    lane_lo_chunk = jax.lax.broadcasted_iota(jnp.int32, (CHUNK, LANES), 1) < CHUNK
    lane_lo_row = jax.lax.broadcasted_iota(jnp.int32, (1, LANES), 1) < CHUNK
    same_chunk = (r_t ^ c_t) < CHUNK
    rk = jax.lax.broadcasted_iota(jnp.int32, (tb, 2 * WINDOW), 0)
    ck = jax.lax.broadcasted_iota(jnp.int32, (tb, 2 * WINDOW), 1)
    first_off = jnp.where(t > 0, 0, WINDOW)
    dist = ck - rk
    kmask = ((dist > jnp.where(ck < WINDOW, first_off, -WINDOW))
             & (dist <= jnp.where(ck < WINDOW, 2 * WINDOW, WINDOW)))

    s_cat = [s_ref[sq] for sq in range(ns)]
    k_prev = [kprev_ref[sq] for sq in range(ns)]
    v_prev = [vprev_ref[sq] for sq in range(ns)]
    staged = [{name: ref[sq] for ref, name in zip(stages, stage_names)} for sq in range(ns)]
    for sq in range(ns):
        staged[sq].update(q=prev_ref[sq, :, 0:GDN_QK_W], k=prev_ref[sq, :, GDN_QK_W:2 * GDN_QK_W],
                          v=prev_ref[sq, :, 2 * GDN_QK_W:GDN_CONV_W], zs=prev_ref[sq, :, P_Z:P_Z + GDN_V_W])
    fr = [{} for _ in range(ns)]

    def front(sq):
        f = fr[sq]
        ab = cur_ref[sq, :, P_AB:P_AB + LANES]
        f["beta"] = ab
        f["gcc"] = _mm_exact_rhs(((c_t <= r_t) & same_chunk).astype(BF16), ab)
        f["gcr"] = _mm_exact_lhs(jnp.transpose(ab)[0:SUBLANES, :],
                                 ((r_t <= c_t) & same_chunk).astype(BF16))
        k_own = cur_ref[sq, :, P_KS:P_KS + SWA_KV_W]
        v_own = cur_ref[sq, :, P_VS:P_VS + SWA_KV_W]
        f["k_own"], f["v_own"] = k_own, v_own
        k_cat = jnp.concatenate([k_prev[sq], k_own], axis=0)
        v_cat = jnp.concatenate([v_prev[sq], v_own], axis=0).astype(BF16)
        q_rows = [None] * SWA_HEADS
        for p in range(SWA_HEADS // 2):
            qp = cur_ref[sq, :, P_QS + p * LANES:P_QS + (p + 1) * LANES]
            q_rows[p] = jnp.where(lane_lo, qp, 0.0)
            q_rows[SWA_HEADS // 2 + p] = jnp.where(lane_lo, 0.0, qp)
        s_all = _mm_nt(jnp.concatenate(q_rows, axis=0), k_cat) * (SWA_HD ** -0.5)
        yield
        pv = [None] * SWA_HEADS
        for hh in range(SWA_HEADS):
            sh = jnp.where(kmask, s_all[hh * tb:(hh + 1) * tb], -jnp.inf)
            sk = sinks_ref[hh]
            m = jnp.maximum(jnp.max(sh, axis=-1, keepdims=True), sk)
            ph = jnp.exp(sh - m)
            den = jnp.sum(ph, axis=-1, keepdims=True) + jnp.exp(sk - m)
            pv[hh] = jnp.dot(ph.astype(BF16), v_cat, preferred_element_type=F32) / den
            yield
        f["os"] = jnp.concatenate(
            [jnp.where(lane_lo, pv[p], pv[SWA_HEADS // 2 + p]) for p in range(SWA_HEADS // 2)], axis=1)

    fronts = [front(sq) for sq in range(ns)]

    def fill():
        for gen in fronts:
            next(gen, None)

    units = [(sq, c) for sq in range(ns) for c in range(nch)]
    rows_of = lambda c: slice(c * CHUNK, (c + 1) * CHUNK)
    ri = jax.lax.broadcasted_iota(jnp.int32, (CHUNK, CAT_W), 0)
    ci = jax.lax.broadcasted_iota(jnp.int32, (CHUNK, CAT_W), 1)
    cj = ci & (CHUNK - 1)
    tri = cj <= ri
    strict = cj < ri
    eye = (cj == ri).astype(F32)
    blk = ri ^ cj
    bd = functools.partial(_bd_cat, lane_lo_chunk=lane_lo_chunk)

    drv = []
    for sq in range(ns):
        st = staged[sq]
        gc_full = _lane_expand(st["gcc"], 0, tb)
        beta_full = _lane_expand(st["beta"], GDN_HEADS, tb)
        egc_full = jnp.exp(gc_full)
        kb_all = st["k"] * beta_full
        drv.append(dict(gc_full=gc_full, kb=kb_all, vb=st["v"] * beta_full, kbg=kb_all * egc_full,
                        qd=st["q"] * egc_full, gcr=st["gcr"], gcr_sw=pltpu.roll(st["gcr"], CHUNK, 1)))
    fill()
    a_mat, qk, xinv, pw = {}, {}, {}, {}
    for u in units:
        sq, c = u
        d, rows = drv[sq], rows_of(c)
        gcc_pairs, gcr_pairs = [], []
        for p in range(GDN_HEADS // 2):
            gcc_pairs.append(jnp.where(lane_lo_chunk, d["gc_full"][rows, (2 * p) * LANES:(2 * p + 1) * LANES],
                                       d["gc_full"][rows, (2 * p + 1) * LANES:(2 * p + 2) * LANES]))
            ra = (d["gcr"] if c == 0 else d["gcr_sw"])[2 * p:2 * p + 1, :]
            rb_ = (d["gcr_sw"] if c == 0 else d["gcr"])[2 * p + 1:2 * p + 2, :]
            gcr_pairs.append(jnp.where(lane_lo_row, ra, rb_))
        gcc = jnp.concatenate(gcc_pairs, axis=1)
        gcr = jnp.concatenate(gcr_pairs, axis=1)
        decay = jnp.exp(jnp.where(tri, gcc - gcr, -jnp.inf))
        kq = jax.lax.dot_general(
            jnp.concatenate([d["kb"][rows], staged[sq]["q"][rows]], axis=0).astype(BF16),
            _bd_wide(staged[sq]["k"][rows]), (((1,), (1,)), ((), ())), preferred_element_type=F32)
        a_mat[u] = jnp.where(strict, kq[0:CHUNK] * decay, 0.0)
        qk[u] = kq[CHUNK:] * decay
    fill()
    for u in units:
        a0 = jnp.where(blk < SUBLANES, a_mat[u], 0.0)
        xinv[u] = eye - a0
        pw[u] = _mm(a0, bd(a0))
    fill()
    for u in units:
        pw_bd = bd(pw[u])
        xinv[u] = xinv[u] + _mm(xinv[u], pw_bd)
        pw[u] = _mm(pw[u], pw_bd)
    fill()
    for u in units:
        xinv[u] = xinv[u] + _mm(xinv[u], bd(pw[u]))
    fill()
    s = SUBLANES
    while s < CHUNK:
        y = {u: _mm(jnp.where((blk < 2 * s) & (blk >= s), a_mat[u], 0.0), bd(xinv[u])) for u in units}
        fill()
        for u in units:
            xinv[u] = xinv[u] - _mm(xinv[u], bd(y[u]))
        fill()
        s *= 2
    prep = {}
    for u in units:
        sq, c = u
        d, rows = drv[sq], rows_of(c)
        u_all, w_all = _split_uw(_mm(bd(xinv[u]), _uw_rhs(d["vb"][rows], d["kbg"][rows])))
        glast = d["gc_full"][c * CHUNK + CHUNK - 1:c * CHUNK + CHUNK, :]
        kdec_t = _stack_heads_t(staged[sq]["k"][rows] * jnp.exp(glast - d["gc_full"][rows]))
        prep[u] = (u_all, w_all, kdec_t, jnp.exp(glast))
    fill()

    o_chunks = [[] for _ in range(ns)]
    for c in range(nch):
        rows = rows_of(c)
        ws, qs = [[] for _ in range(ns)], [[] for _ in range(ns)]
        for sq in range(ns):
            w_all = prep[(sq, c)][1]
            for p in range(GDN_HEADS // 2):
                cols = slice(2 * p * LANES, (2 * p + 2) * LANES)
                r = _mm(jnp.concatenate([w_all[:, cols], drv[sq]["qd"][rows, cols]], axis=0),
                        _bd_pair(s_cat[sq][:, (2 * p) * LANES:(2 * p + 1) * LANES],
                                 s_cat[sq][:, (2 * p + 1) * LANES:(2 * p + 2) * LANES]))
                ws[sq].append(r[0:CHUNK])
                qs[sq].append(r[CHUNK:])
        fill()
        for sq in range(ns):
            u_all, _, kdec_t, eg_last = prep[(sq, c)]
            v_new = u_all - jnp.concatenate(ws[sq], axis=1)
            r2 = _mm(jnp.concatenate([qk[(sq, c)], kdec_t], axis=0), _bd_wide(v_new))
            o_chunks[sq].append(jnp.concatenate(qs[sq], axis=1) + r2[0:CHUNK])
            s_cat[sq] = s_cat[sq] * eg_last + r2[CHUNK:]
        fill()
    for gen in fronts:
        for _ in gen:
            pass

    for sq in range(ns):
        o_gdn = jnp.concatenate(o_chunks[sq], axis=0)
        zs_prev = staged[sq]["zs"]
        og = [_rms(o_gdn[:, hh * GDN_DV:(hh + 1) * GDN_DV], gnorm_ref[...])
              * zs_prev[:, hh * GDN_DV:(hh + 1) * GDN_DV] for hh in range(GDN_HEADS)]
        o_ref[sq] = jnp.concatenate(og + [staged[sq]["os"]], axis=-1).astype(o_ref.dtype)
        s_ref[sq] = s_cat[sq]
        kprev_ref[sq] = fr[sq]["k_own"]
        vprev_ref[sq] = fr[sq]["v_own"]
        for ref, name in zip(stages, stage_names):
            ref[sq] = fr[sq][name]

    @pl.when(t == n_blocks - 1)
    def _window_out():
        for sq in range(ns):
            kout_ref[sq] = fr[sq]["k_own"]
            vout_ref[sq] = fr[sq]["v_own"]

    @pl.when(t == n_blocks)
    def _state_out():
        for sq in range(ns):
            for hh in range(GDN_HEADS):
                sout_ref[sq, hh] = s_ref[sq, :, hh * GDN_DV:(hh + 1) * GDN_DV]


def _mixer_prompt(prep3d, gnorm, sinks):
    bsz, seq, _ = prep3d.shape
    tb, ns = WINDOW, PROMPT_NS
    assert seq % tb == 0 and bsz % ns == 0
    n_blocks = seq // tb
    kern = functools.partial(_mixer_prompt_kernel, tb=tb, n_blocks=n_blocks)
    seq_blk = lambda shape: pl.BlockSpec((ns,) + shape, lambda b, t: (b,) + (0,) * len(shape))
    vmem = lambda *shape: pltpu.VMEM((ns,) + shape, F32)
    return pl.pallas_call(
        kern,
        grid=(bsz // ns, n_blocks + 1),
        in_specs=[
            pl.BlockSpec(memory_space=pltpu.SMEM),
            pl.BlockSpec((ns, tb, P_W), lambda b, t: (b, jnp.minimum(t, n_blocks - 1), 0)),
            pl.BlockSpec((ns, tb, P_W), lambda b, t: (b, jnp.maximum(t - 1, 0), 0)),
            pl.BlockSpec((1, GDN_DV), lambda b, t: (0, 0)),
        ],
        out_specs=[
            pl.BlockSpec((ns, tb, D_MODEL), lambda b, t: (b, jnp.maximum(t - 1, 0), 0)),
            seq_blk((GDN_HEADS, GDN_DK, GDN_DV)),
            seq_blk((WINDOW, SWA_KV_W)),
            seq_blk((WINDOW, SWA_KV_W)),
        ],
        out_shape=[
            jax.ShapeDtypeStruct((bsz, seq, D_MODEL), BF16),
            jax.ShapeDtypeStruct((bsz, GDN_HEADS, GDN_DK, GDN_DV), F32),
            jax.ShapeDtypeStruct((bsz, WINDOW, SWA_KV_W), F32),
            jax.ShapeDtypeStruct((bsz, WINDOW, SWA_KV_W), F32),
        ],
        scratch_shapes=[
            vmem(GDN_DK, GDN_HEADS * GDN_DV),
            vmem(WINDOW, SWA_KV_W),
            vmem(WINDOW, SWA_KV_W),
            vmem(tb, LANES),
            vmem(SUBLANES, LANES),
            vmem(tb, LANES),
            vmem(tb, SWA_Q_W),
        ],
        compiler_params=pltpu.CompilerParams(
            dimension_semantics=("arbitrary", "arbitrary"), vmem_limit_bytes=VMEM_LIMIT_BYTES),
        name="mixer_prompt",
    )(sinks, prep3d, prep3d, gnorm)


def _mixer_sample_kernel(sinks_ref, proj_ref, s0_ref, conv0_ref, k0_ref, v0_ref, convw_ref, alog_ref, dt_ref,
                         gnorm_ref, rc_ref, rs1_ref, rs2_ref,
                         o_ref, sout_ref, convout_ref, kout_ref, vout_ref, xext_ref):
    nb, ts, rows = SAMPLE_NB, SAMPLE_T, CHUNK
    seqs = range(nb)
    grp = lambda x, b: x[b * ts:(b + 1) * ts]

    hist = SUBLANES - (GDN_CONV - 1)
    conv_parts = []
    for b in seqs:
        xext_ref[2 * ts * b + hist:2 * ts * b + ts, :] = conv0_ref[b]
        xext_ref[2 * ts * b + ts:2 * ts * (b + 1), :] = proj_ref[b, :, P_QKV:P_QKV + GDN_CONV_W]
    for b in seqs:
        base = 2 * ts * b + hist
        conv = xext_ref[base:base + ts, :] * convw_ref[0:1, :]
        for j in range(1, GDN_CONV):
            conv = conv + xext_ref[base + j:base + j + ts, :] * convw_ref[j:j + 1, :]
        conv_parts.append(conv)
        convout_ref[b] = xext_ref[2 * ts * (b + 1) - (GDN_CONV - 1):2 * ts * (b + 1), :]
    qkv = _silu(jnp.concatenate(conv_parts, axis=0))
    q_all = _head_l2(qkv[:, 0:GDN_QK_W]) * (GDN_DK ** -0.5)
    k_all = _head_l2(qkv[:, GDN_QK_W:2 * GDN_QK_W])
    v_all = qkv[:, 2 * GDN_QK_W:]

    def proj_cols(lo, width):
        return jnp.concatenate([proj_ref[b, :, lo:lo + width] for b in seqs], axis=0)

    g_all, beta_all = _gates(proj_cols(P_AB, LANES), alog_ref[...], dt_ref[...])

    r_t = jax.lax.broadcasted_iota(jnp.int32, (rows, LANES), 0)
    c_t = jax.lax.broadcasted_iota(jnp.int32, (rows, LANES), 1)
    r_p = jax.lax.broadcasted_iota(jnp.int32, (LANES, LANES), 0)
    c_p = jax.lax.broadcasted_iota(jnp.int32, (LANES, LANES), 1)
    g_pad = _pad_rows(g_all)
    gc_cols = _mm_exact_rhs(((c_t <= r_t) & ((r_t ^ c_t) < ts)).astype(BF16), g_pad)
    gc_rows = _mm_exact_lhs(jnp.transpose(g_pad)[0:SUBLANES, :],
                            ((r_p <= c_p) & ((r_p ^ c_p) < ts)).astype(BF16))
    gc_rows_sw = pltpu.roll(gc_rows, CHUNK, 1)

    lane_lo = c_t < CHUNK
    lane_lo_row = jax.lax.broadcasted_iota(jnp.int32, (1, LANES), 1) < CHUNK
    lane_lo_t = jax.lax.broadcasted_iota(jnp.int32, (ts, LANES), 1) < CHUNK
    gc_full = _lane_expand(gc_cols, 0, rows)
    beta_full = _lane_expand(beta_all, GDN_HEADS, rows)
    egc_full = jnp.exp(gc_full)
    kb_all = k_all * beta_full
    vb_all = v_all * beta_full
    kbg_all = kb_all * egc_full
    qd_all = q_all * egc_full

    ri = jax.lax.broadcasted_iota(jnp.int32, (rows, CAT_W), 0)
    ci = jax.lax.broadcasted_iota(jnp.int32, (rows, CAT_W), 1)
    cj = ci & (CHUNK - 1)
    same_seq = (ri ^ cj) < ts
    eye = (cj == ri).astype(F32)
    bd = functools.partial(_bd_cat, lane_lo_chunk=lane_lo)

    gcc = jnp.concatenate(
        [jnp.where(lane_lo, gc_full[:, (2 * p) * LANES:(2 * p + 1) * LANES],
                   gc_full[:, (2 * p + 1) * LANES:(2 * p + 2) * LANES]) for p in range(GDN_HEADS // 2)], axis=1)
    gcr = jnp.concatenate(
        [jnp.where(lane_lo_row, gc_rows[2 * p:2 * p + 1, :], gc_rows_sw[2 * p + 1:2 * p + 2, :])
         for p in range(GDN_HEADS // 2)], axis=1)
    decay = jnp.exp(jnp.where((cj <= ri) & same_seq, gcc - gcr, -jnp.inf))
    kq = jax.lax.dot_general(
        jnp.concatenate([kb_all, q_all], axis=0).astype(BF16), _bd_wide(k_all),
        (((1,), (1,)), ((), ())), preferred_element_type=F32)
    a_mat = jnp.where(cj < ri, kq[0:rows] * decay, 0.0)
    qk = kq[rows:] * decay
    xinv = eye - a_mat
    pw = _mm(a_mat, bd(a_mat))
    pw_bd = bd(pw)
    xinv = xinv + _mm(xinv, pw_bd)
    pw = _mm(pw, pw_bd)
    xinv = xinv + _mm(xinv, bd(pw))
    u_all, w_all = _split_uw(_mm(bd(xinv), _uw_rhs(vb_all, kbg_all)))
    glast = [gc_full[(b + 1) * ts - 1:(b + 1) * ts, :] for b in seqs]
    glast_rows = jnp.concatenate([jnp.broadcast_to(glast[b], (ts, GDN_QK_W)) for b in seqs], axis=0)
    kdec_t = _stack_heads_t(k_all * jnp.exp(glast_rows - gc_full))

    s_old, ws, qs = [], [], []
    for b in seqs:
        s_b = jnp.concatenate([s0_ref[b, hh] for hh in range(GDN_HEADS)], axis=1)
        s_old.append(s_b)
        ws_b, qs_b = [], []
        for p in range(GDN_HEADS // 2):
            cols = slice(2 * p * LANES, (2 * p + 2) * LANES)
            r = _mm(jnp.concatenate([grp(w_all, b)[:, cols], grp(qd_all, b)[:, cols]], axis=0),
                    _bd_pair(s_b[:, (2 * p) * LANES:(2 * p + 1) * LANES],
                             s_b[:, (2 * p + 1) * LANES:(2 * p + 2) * LANES]))
            ws_b.append(r[0:ts])
            qs_b.append(r[ts:])
        ws.append(jnp.concatenate(ws_b, axis=1))
        qs.append(jnp.concatenate(qs_b, axis=1))
    v_new = u_all - jnp.concatenate(ws, axis=0)
    v_bd = _bd_wide(v_new)
    o_gdn = jnp.concatenate(qs, axis=0) + _mm(qk, v_bd)
    cb = jax.lax.broadcasted_iota(jnp.int32, (GDN_DK, CAT_W), 1) & (CHUNK - 1)
    upd = _mm(jnp.concatenate([jnp.where((cb ^ (b * ts)) < ts, kdec_t, 0.0) for b in seqs], axis=0), v_bd)
    for b in seqs:
        s_new = s_old[b] * jnp.exp(glast[b]) + upd[b * GDN_DK:(b + 1) * GDN_DK]
        for hh in range(GDN_HEADS):
            sout_ref[b, hh] = s_new[:, hh * GDN_DV:(hh + 1) * GDN_DV]
    og = []
    for hh in range(GDN_HEADS):
        oh = _rms(o_gdn[:, hh * GDN_DV:(hh + 1) * GDN_DV], gnorm_ref[...])
        og.append(oh * _silu(proj_cols(P_Z + hh * GDN_DV, GDN_DV)))

    tile8 = lambda x: jnp.concatenate([x] * nb, axis=0)
    rc, rs1, rs2 = tile8(rc_ref[...]), tile8(rs1_ref[...]), tile8(rs2_ref[...])
    k_own = _rope(proj_cols(P_KS, SWA_KV_W), rc, rs1, rs2)
    v_own = proj_cols(P_VS, SWA_KV_W)
    q_rows = [None] * SWA_HEADS
    for p in range(SWA_HEADS // 2):
        qp = _rope(proj_cols(P_QS + p * LANES, LANES), rc, rs1, rs2)
        q_rows[p] = jnp.where(lane_lo, qp, 0.0)
        q_rows[SWA_HEADS // 2 + p] = jnp.where(lane_lo, 0.0, qp)
    scale = SWA_HD ** -0.5
    s_own_all = _mm_nt(jnp.concatenate(q_rows, axis=0), _pad_rows(k_own)) * scale
    tok = r_t & (ts - 1)
    prev_mask = c_t > tok
    sink_col = jnp.zeros((rows, 1), F32)
    for hh in range(SWA_HEADS):
        sink_col = jnp.where((r_t[:, 0:1] ^ (hh * ts)) < ts, sinks_ref[hh], sink_col)
    pp_all, po_all, den_all = [], [], []
    for b in seqs:
        q_b = jnp.concatenate([grp(q_rows[hh], b) for hh in range(SWA_HEADS)], axis=0)
        sp = jnp.where(prev_mask, _mm_nt(q_b, k0_ref[b]) * scale, -jnp.inf)
        so_raw = jnp.concatenate([s_own_all[hh * rows + b * ts:hh * rows + (b + 1) * ts]
                                  for hh in range(SWA_HEADS)], axis=0)
        so = jnp.where(((c_t ^ (b * ts)) < ts) & ((c_t & (ts - 1)) <= tok), so_raw, -jnp.inf)
        m = jnp.maximum(jnp.maximum(jnp.max(sp, axis=-1, keepdims=True),
                                    jnp.max(so, axis=-1, keepdims=True)), sink_col)
        pp = jnp.exp(sp - m)
        po = jnp.exp(so - m)
        den_all.append(jnp.sum(pp, axis=-1, keepdims=True) + jnp.sum(po, axis=-1, keepdims=True)
                       + jnp.exp(sink_col - m))
        pp_all.append(_mm(pp, v0_ref[b]))
        po_all.append(po)
    pv_own = _mm(jnp.concatenate(po_all, axis=0), _pad_rows(v_own))
    os_rows = []
    for b in seqs:
        pv_b = (pp_all[b] + pv_own[b * rows:(b + 1) * rows]) / den_all[b]
        os_rows.append(jnp.concatenate(
            [jnp.where(lane_lo_t, pv_b[p * ts:(p + 1) * ts],
                       pv_b[(SWA_HEADS // 2 + p) * ts:(SWA_HEADS // 2 + p + 1) * ts])
             for p in range(SWA_HEADS // 2)], axis=1))
        kout_ref[b] = jnp.concatenate([k0_ref[b, ts:, :], grp(k_own, b)], axis=0)
        vout_ref[b] = jnp.concatenate([v0_ref[b, ts:, :], grp(v_own, b)], axis=0)
    o_full = jnp.concatenate(og + [jnp.concatenate(os_rows, axis=0)], axis=-1)
    for b in seqs:
        o_ref[b] = grp(o_full, b).astype(o_ref.dtype)


def _mixer_sample(proj3d, s0, conv0, k0, v0, convw, alog_row, dt_row, gnorm, sinks, rc, rs1, rs2):
    bsz, seq, _ = proj3d.shape
    nb = SAMPLE_NB
    assert seq == SAMPLE_T and bsz % nb == 0
    seq_blk = lambda shape: pl.BlockSpec((nb,) + shape, lambda i: (i,) + (0,) * len(shape))
    const_blk = lambda shape: pl.BlockSpec(shape, lambda i: (0,) * len(shape))
    return pl.pallas_call(
        _mixer_sample_kernel,
        grid=(bsz // nb,),
        in_specs=[
            pl.BlockSpec(memory_space=pltpu.SMEM),
            seq_blk((seq, P_W)),
            seq_blk((GDN_HEADS, GDN_DK, GDN_DV)),
            seq_blk((GDN_CONV - 1, GDN_CONV_W)),
            seq_blk((WINDOW, SWA_KV_W)),
            seq_blk((WINDOW, SWA_KV_W)),
            const_blk((GDN_CONV, GDN_CONV_W)),
            const_blk((1, LANES)),
            const_blk((1, LANES)),
            const_blk((1, GDN_DV)),
            const_blk((seq, LANES)),
            const_blk((seq, LANES)),
            const_blk((seq, LANES)),
        ],
        out_specs=[
            seq_blk((seq, D_MODEL)),
            seq_blk((GDN_HEADS, GDN_DK, GDN_DV)),
            seq_blk((GDN_CONV - 1, GDN_CONV_W)),
            seq_blk((WINDOW, SWA_KV_W)),
            seq_blk((WINDOW, SWA_KV_W)),
        ],
        out_shape=[
            jax.ShapeDtypeStruct((bsz, seq, D_MODEL), BF16),
            jax.ShapeDtypeStruct((bsz, GDN_HEADS, GDN_DK, GDN_DV), F32),
            jax.ShapeDtypeStruct((bsz, GDN_CONV - 1, GDN_CONV_W), F32),
            jax.ShapeDtypeStruct((bsz, WINDOW, SWA_KV_W), F32),
            jax.ShapeDtypeStruct((bsz, WINDOW, SWA_KV_W), F32),
        ],
        scratch_shapes=[pltpu.VMEM((2 * SAMPLE_T * nb, GDN_CONV_W), F32)],
        compiler_params=pltpu.CompilerParams(
            dimension_semantics=("arbitrary",), vmem_limit_bytes=VMEM_LIMIT_BYTES),
        name="mixer_sample",
    )(sinks, proj3d, s0, conv0, k0, v0, convw, alog_row, dt_row, gnorm, rc, rs1, rs2)


def _rope_tables(pos):
    half = ROT_DIM // 2
    inv = ROPE_THETA ** (-jnp.arange(half, dtype=F32) * 2.0 / ROT_DIM)
    ang = pos.astype(F32)[:, None] * inv[None, :]
    cos, sin = jnp.cos(ang), jnp.sin(ang)
    n = pos.shape[0]
    rest = SWA_HD - ROT_DIM
    c = jnp.concatenate([cos, cos, jnp.ones((n, rest), F32)], axis=1)
    s1 = jnp.concatenate([-sin, jnp.zeros((n, half + rest), F32)], axis=1)
    s2 = jnp.concatenate([jnp.zeros((n, half), F32), sin, jnp.zeros((n, rest), F32)], axis=1)
    reps = LANES // SWA_HD
    return jnp.tile(c, (1, reps)), jnp.tile(s1, (1, reps)), jnp.tile(s2, (1, reps))


def _projection_tiles(w_in):
    o_z = GDN_CONV_W
    o_a = o_z + GDN_V_W
    o_qs = o_a + 2 * GDN_HEADS
    o_ks = o_qs + SWA_Q_W
    o_vs = o_ks + SWA_KV_W
    cols = lambda lo, width: w_in[:, lo:lo + width]
    tiles = [cols(t * LANES, LANES) for t in range((o_z + GDN_V_W) // LANES)]
    half = SWA_HEADS // 2
    tiles += [jnp.concatenate([cols(o_qs + p * SWA_HD, SWA_HD), cols(o_qs + (half + p) * SWA_HD, SWA_HD)], axis=1)
              for p in range(half)]
    tiles += [cols(o_ks, SWA_KV_W), cols(o_vs, SWA_KV_W)]
    tiles.append(jnp.concatenate([cols(o_a, 2 * GDN_HEADS),
                                  jnp.zeros((D_MODEL, LANES - 2 * GDN_HEADS), w_in.dtype)], axis=1))
    return tiles


def _layer_weights(norm_ffn1, ffn1_gu, ffn1_down, norm_mix, w_in, conv_w, a_log, dt_bias, gdn_norm, sinks,
                   w_out, norm_ffn2, ffn2_gu, ffn2_down, norm_ple, ple_proj, ple_gate, norm_final):
    tiles = _projection_tiles(w_in)
    win_p = jnp.concatenate(tiles, axis=1).astype(BF16)
    win_prep = jnp.concatenate([win_p[:, t * LANES:(t + 1) * LANES] for t in PREP_TILE_ORDER], axis=1)
    half = SWA_HEADS // 2
    head_rows = lambda hh: w_out[GDN_V_W + hh * SWA_HD:GDN_V_W + (hh + 1) * SWA_HD]
    wout_p = jnp.concatenate([w_out[:GDN_V_W]] + [head_rows(hh) for p in range(half) for hh in (p, half + p)],
                             axis=0).astype(BF16)
    pad_row = lambda v: jnp.concatenate([v.astype(F32), jnp.zeros((LANES - v.shape[0],), F32)])[None, :]
    return dict(
        g1=norm_ffn1[None, :], wgu1=ffn1_gu.astype(BF16), wd1=ffn1_down.astype(BF16),
        gm=norm_mix[None, :], win=win_p, win_prep=win_prep, convw=conv_w, alog=pad_row(a_log), dt=pad_row(dt_bias),
        gnorm=gdn_norm[None, :], sinks=sinks.astype(F32), wout=wout_p,
        g2=norm_ffn2[None, :], wgu2=ffn2_gu.astype(BF16), wd2=ffn2_down.astype(BF16),
        gp=norm_ple[None, :], wpp=ple_proj.astype(BF16), wpg=ple_gate.astype(BF16), gf=norm_final[None, :])


def _group(x, p, w, state, pos0):
    bsz, seq, _ = x.shape
    rc, rs1, rs2 = _rope_tables(pos0 + jnp.arange(seq))
    x2d = x.reshape(bsz * seq, D_MODEL)
    if state is None:
        h2d, prep2d, conv_new = _ffn_in_prep(x2d, seq, w["g1"], w["wgu1"], w["wd1"], w["gm"], w["win_prep"],
                                             w["convw"], w["alog"], w["dt"], rc, rs1, rs2)
        o, s_new, k_new, v_new = _mixer_prompt(prep2d.reshape(bsz, seq, P_W), w["gnorm"], w["sinks"])
    else:
        h2d, proj2d = _ffn_in(x2d, w["g1"], w["wgu1"], w["wd1"], w["gm"], w["win"])
        s0, conv0, k0, v0 = state
        o, s_new, conv_new, k_new, v_new = _mixer_sample(
            proj2d.reshape(bsz, seq, P_W), s0, conv0, k0, v0, w["convw"], w["alog"], w["dt"], w["gnorm"],
            w["sinks"], rc, rs1, rs2)
    y = _ffn_out(o.reshape(bsz * seq, D_MODEL), h2d, p.reshape(bsz * seq, PLE_DIM), w["wout"], w["g2"],
                 w["wgu2"], w["wd2"], w["gp"], w["wpg"], w["wpp"], w["gf"])
    kv_shape = (1, bsz, WINDOW, SWA_KV_HEADS, SWA_HD)
    return (y.reshape(bsz, seq, D_MODEL), s_new[None], conv_new[None], k_new.reshape(kv_shape),
            v_new.reshape(kv_shape))


def kernel(x_prompt, x_sample, state_gdn, state_conv, cache_swa_k, cache_swa_v, p_prompt, p_sample, norm_ffn1, ffn1_gu, ffn1_down, norm_mix, w_in, conv_w, a_log, dt_bias, gdn_norm, sinks, w_out, norm_ffn2, ffn2_gu, ffn2_down, norm_ple, ple_proj, ple_gate, norm_final):
    assert state_gdn.shape[0] == 1, "one layer"
    w = _layer_weights(norm_ffn1[0], ffn1_gu[0], ffn1_down[0], norm_mix[0], w_in[0], conv_w[0], a_log[0],
                       dt_bias[0], gdn_norm[0], sinks[0], w_out[0], norm_ffn2[0], ffn2_gu[0], ffn2_down[0],
                       norm_ple[0], ple_proj[0], ple_gate[0], norm_final)
    bs = x_sample.shape[0]
    yp, sg_p, sc_p, kk_p, vv_p = _group(x_prompt, p_prompt[0], w, None, 0)
    ys, sg_s, sc_s, kk_s, vv_s = _group(
        x_sample, p_sample[0], w,
        (state_gdn[0], state_conv[0], cache_swa_k[0].reshape(bs, WINDOW, SWA_KV_W),
         cache_swa_v[0].reshape(bs, WINDOW, SWA_KV_W)), PAST_LEN)
    return (yp, ys, sg_p, sc_p, kk_p, vv_p, sg_s, sc_s, kk_s, vv_s)
```

```python
import functools

import jax
import jax.numpy as jnp
from jax.experimental import pallas as pl
from jax.experimental.pallas import tpu as pltpu

F32 = jnp.float32
BF16 = jnp.bfloat16

D_MODEL = 1024
D_FF = 2816
PLE_DIM = 256
NORM_EPS = 1e-6
GDN_HEADS = 4
GDN_DK = 128
GDN_DV = 128
GDN_CONV = 4
GDN_QK_W = GDN_HEADS * GDN_DK
GDN_V_W = GDN_HEADS * GDN_DV
GDN_CONV_W = 2 * GDN_QK_W + GDN_V_W
SWA_HEADS = 8
SWA_KV_HEADS = 2
SWA_HD = 64
SWA_Q_W = SWA_HEADS * SWA_HD
SWA_KV_W = SWA_KV_HEADS * SWA_HD
WINDOW = 128
ROT_DIM = SWA_HD // 4
ROPE_THETA = 500000.0
PAST_LEN = 16384

LANES = 128
SUBLANES = 8
VMEM_LIMIT_BYTES = 56 * 1024 * 1024

P_QKV = 0
P_Z = P_QKV + GDN_CONV_W
P_QS = P_Z + GDN_V_W
P_KS = P_QS + SWA_Q_W
P_VS = P_KS + SWA_KV_W
P_AB = P_VS + SWA_KV_W
P_W = P_AB + LANES

FF_CHUNK = 256
TOKEN_TILE = 512

CHUNK = 64
CAT_W = GDN_HEADS * CHUNK
N_CONV_TILES = GDN_CONV_W // LANES
N_PROJ_TILES = P_W // LANES
assert N_CONV_TILES == N_PROJ_TILES - N_CONV_TILES + 1
PREP_TILE_ORDER = (tuple(t for pair in zip(range(N_CONV_TILES - 1), range(N_CONV_TILES, N_PROJ_TILES)) for t in pair)
                   + (N_CONV_TILES - 1,))
SAMPLE_T = SUBLANES
SAMPLE_NB = CHUNK // SAMPLE_T


def _rms(x, g):
    return x * jax.lax.rsqrt(jnp.mean(x * x, axis=-1, keepdims=True) + NORM_EPS) * g


def _sigmoid(x):
    return 1.0 / (1.0 + jnp.exp(-x))


def _silu(x):
    return x * _sigmoid(x)


def _mm(a, b):
    return jnp.dot(a.astype(BF16), b.astype(BF16), preferred_element_type=F32)


def _mm_nt(a, b):
    return jax.lax.dot_general(a.astype(BF16), b.astype(BF16), (((1,), (1,)), ((), ())),
                               preferred_element_type=F32)


def _swiglu_acc(n_bf16, wgu_ref, wd_ref, act_ref):
    for c in range(D_FF // FF_CHUNK):
        lo = c * FF_CHUNK
        gate = jnp.dot(n_bf16, wgu_ref[:, lo:lo + FF_CHUNK], preferred_element_type=F32)
        up = jnp.dot(n_bf16, wgu_ref[:, D_FF + lo:D_FF + lo + FF_CHUNK], preferred_element_type=F32)
        act_ref[:, lo:lo + FF_CHUNK] = (_silu(gate) * up).astype(BF16)
    return jnp.dot(act_ref[...], wd_ref[...], preferred_element_type=F32)


def _ffn_in_kernel(x_ref, g1_ref, wgu_ref, wd_ref, gm_ref, win_ref, h_ref, proj_ref, act_ref):
    x = x_ref[...]
    n1 = _rms(x, g1_ref[...]).astype(BF16)
    h = x + 0.5 * _swiglu_acc(n1, wgu_ref, wd_ref, act_ref)
    h_ref[...] = h
    n = _rms(h, gm_ref[...]).astype(BF16)
    proj_ref[...] = jnp.dot(n, win_ref[...], preferred_element_type=F32)


def _const_spec(shape):
    return pl.BlockSpec(shape, lambda i: (0,) * len(shape), pipeline_mode=pl.Buffered(1))


def _ffn_in(x2d, g1, wgu, wd, gm, win):
    n_tok = x2d.shape[0]
    tm = min(TOKEN_TILE, n_tok)
    assert n_tok % tm == 0
    return pl.pallas_call(
        _ffn_in_kernel,
        grid=(n_tok // tm,),
        in_specs=[
            pl.BlockSpec((tm, D_MODEL), lambda i: (i, 0)),
            _const_spec((1, D_MODEL)),
            _const_spec((D_MODEL, 2 * D_FF)),
            _const_spec((D_FF, D_MODEL)),
            _const_spec((1, D_MODEL)),
            _const_spec((D_MODEL, P_W)),
        ],
        out_specs=[
            pl.BlockSpec((tm, D_MODEL), lambda i: (i, 0)),
            pl.BlockSpec((tm, P_W), lambda i: (i, 0)),
        ],
        out_shape=[
            jax.ShapeDtypeStruct((n_tok, D_MODEL), F32),
            jax.ShapeDtypeStruct((n_tok, P_W), F32),
        ],
        scratch_shapes=[pltpu.VMEM((tm, D_FF), BF16)],
        compiler_params=pltpu.CompilerParams(
            dimension_semantics=("arbitrary",), vmem_limit_bytes=VMEM_LIMIT_BYTES),
        name="ffn_in",
    )(x2d, g1, wgu, wd, gm, win)


def _ffn_in_prep_kernel(x_ref, g1_ref, wgu_ref, wd_ref, gm_ref, win_ref, convw_ref, alog_ref, dt_ref,
                        rc_ref, rs1_ref, rs2_ref, h_ref, prep_ref, convout_ref, n_st, xext_ref, act_ref, *, tiles_per_seq):
    i = pl.program_id(0)
    tm = x_ref.shape[0]
    piece = 2 * LANES

    @pl.when(i == 0)
    def _init():
        n_st[...] = jnp.zeros(n_st.shape, n_st.dtype)
        xext_ref[...] = jnp.zeros(xext_ref.shape, F32)

    n_prev = n_st[...]
    first_of_seq = ((i - 1) % tiles_per_seq) == 0

    def prep_tile(lo, val):
        cols = slice(lo, lo + LANES)
        if lo < P_Z:
            hist = jnp.where(first_of_seq, 0.0, xext_ref[:, cols])
            xext_ref[:, cols] = val[tm - SUBLANES:tm]
            convout_ref[0, :, cols] = val[tm - (GDN_CONV - 1):tm]
            ext = jnp.concatenate([hist, val], axis=0)
            conv = val * convw_ref[GDN_CONV - 1:GDN_CONV, cols]
            for j in range(1, GDN_CONV):
                conv = conv + pltpu.roll(ext, j, 0)[SUBLANES:] * convw_ref[GDN_CONV - 1 - j:GDN_CONV - j, cols]
            act = _silu(conv)
            if lo < 2 * GDN_QK_W:
                scale = GDN_DK ** -0.5 if lo < GDN_QK_W else 1.0
                act = act * (jax.lax.rsqrt(jnp.sum(act * act, axis=-1, keepdims=True) + 1e-6) * scale)
            prep_ref[:, cols] = act
        elif lo < P_QS:
            prep_ref[:, cols] = _silu(val)
        elif lo < P_VS:
            prep_ref[:, cols] = _rope(val, rc_ref[...], rs1_ref[...], rs2_ref[...])
        elif lo < P_AB:
            prep_ref[:, cols] = val
        else:
            g_all, beta_all = _gates(val, alog_ref[...], dt_ref[...])
            lane = jax.lax.broadcasted_iota(jnp.int32, (tm, LANES), 1)
            prep_ref[:, cols] = jnp.where(lane < GDN_HEADS, g_all, beta_all)

    def proj_piece(lo):
        width = min(piece, P_W - lo)
        proj = jnp.dot(n_prev, win_ref[:, lo:lo + width], preferred_element_type=F32)
        for off in range(0, width, LANES):
            prep_tile(PREP_TILE_ORDER[(lo + off) // LANES] * LANES, proj[:, off:off + LANES])

    x = x_ref[...]
    n1 = _rms(x, g1_ref[...]).astype(BF16)
    for c in range(D_FF // FF_CHUNK):
        lo = c * FF_CHUNK
        gate = jnp.dot(n1, wgu_ref[:, lo:lo + FF_CHUNK], preferred_element_type=F32)
        up = jnp.dot(n1, wgu_ref[:, D_FF + lo:D_FF + lo + FF_CHUNK], preferred_element_type=F32)
        act_ref[:, lo:lo + FF_CHUNK] = (_silu(gate) * up).astype(BF16)
        proj_piece(c * piece)
    for lo in range((D_FF // FF_CHUNK) * piece, P_W, piece):
        proj_piece(lo)
    h = x + 0.5 * jnp.dot(act_ref[...], wd_ref[...], preferred_element_type=F32)
    h_ref[...] = h
    n_st[...] = _rms(h, gm_ref[...]).astype(BF16)


def _ffn_in_prep(x2d, seq_len, g1, wgu, wd, gm, win, convw, alog_row, dt_row, rc, rs1, rs2):
    n_tok = x2d.shape[0]
    tm = TOKEN_TILE
    assert seq_len % tm == 0 and n_tok % seq_len == 0
    tiles_per_seq = seq_len // tm
    n_tiles = n_tok // tm
    cur = lambda i: jnp.minimum(i, n_tiles - 1)
    prev = lambda i: jnp.maximum(i - 1, 0)
    rope_spec = pl.BlockSpec((tm, LANES), lambda i: (prev(i) % tiles_per_seq, 0))
    return pl.pallas_call(
        functools.partial(_ffn_in_prep_kernel, tiles_per_seq=tiles_per_seq),
        grid=(n_tiles + 1,),
        in_specs=[
            pl.BlockSpec((tm, D_MODEL), lambda i: (cur(i), 0)),
            _const_spec((1, D_MODEL)),
            _const_spec((D_MODEL, 2 * D_FF)),
            _const_spec((D_FF, D_MODEL)),
            _const_spec((1, D_MODEL)),
            _const_spec((D_MODEL, P_W)),
            _const_spec((GDN_CONV, GDN_CONV_W)),
            _const_spec((1, LANES)),
            _const_spec((1, LANES)),
            rope_spec, rope_spec, rope_spec,
        ],
        out_specs=[
            pl.BlockSpec((tm, D_MODEL), lambda i: (cur(i), 0)),
            pl.BlockSpec((tm, P_W), lambda i: (prev(i), 0)),
            pl.BlockSpec((1, GDN_CONV - 1, GDN_CONV_W), lambda i: (prev(i) // tiles_per_seq, 0, 0)),
        ],
        out_shape=[
            jax.ShapeDtypeStruct((n_tok, D_MODEL), F32),
            jax.ShapeDtypeStruct((n_tok, P_W), F32),
            jax.ShapeDtypeStruct((n_tok // seq_len, GDN_CONV - 1, GDN_CONV_W), F32),
        ],
        scratch_shapes=[pltpu.VMEM((tm, D_MODEL), BF16), pltpu.VMEM((SUBLANES, GDN_CONV_W), F32),
                        pltpu.VMEM((tm, D_FF), BF16)],
        compiler_params=pltpu.CompilerParams(
            dimension_semantics=("arbitrary",), vmem_limit_bytes=VMEM_LIMIT_BYTES),
        name="ffn_in_prep",
    )(x2d, g1, wgu, wd, gm, win, convw, alog_row, dt_row, rc, rs1, rs2)


def _ffn_out_kernel(o_ref, h_ref, p_ref, wout_ref, g2_ref, wgu_ref, wd_ref, gp_ref, wpg_ref, wpp_ref,
                    gf_ref, y_ref, act_ref):
    tm = h_ref.shape[0]
    halves = (slice(0, tm // 2), slice(tm // 2, tm))
    h_in, n2 = [], []
    for r in halves:
        h_r = h_ref[r, :] + jnp.dot(o_ref[r, :], wout_ref[...], preferred_element_type=F32)
        h_in.append(h_r)
        n2.append(_rms(h_r, g2_ref[...]).astype(BF16))
    h = jnp.concatenate(h_in, axis=0) + 0.5 * _swiglu_acc(jnp.concatenate(n2, axis=0), wgu_ref, wd_ref, act_ref)
    for r in halves:
        h_r = h[r]
        npl = _rms(h_r, gp_ref[...]).astype(BF16)
        gate = _sigmoid(jnp.dot(npl, wpg_ref[...], preferred_element_type=F32))
        pe = jnp.dot(p_ref[r, :].astype(BF16), wpp_ref[...], preferred_element_type=F32)
        y_ref[r, :] = _rms(h_r + gate * pe, gf_ref[...])


def _ffn_out(o2d, h2d, p2d, wout, g2, wgu, wd, gp, wpg, wpp, gf):
    n_tok = h2d.shape[0]
    tm = min(TOKEN_TILE, n_tok)
    assert n_tok % tm == 0
    return pl.pallas_call(
        _ffn_out_kernel,
        grid=(n_tok // tm,),
        in_specs=[
            pl.BlockSpec((tm, D_MODEL), lambda i: (i, 0)),
            pl.BlockSpec((tm, D_MODEL), lambda i: (i, 0)),
            pl.BlockSpec((tm, PLE_DIM), lambda i: (i, 0)),
            _const_spec((D_MODEL, D_MODEL)),
            _const_spec((1, D_MODEL)),
            _const_spec((D_MODEL, 2 * D_FF)),
            _const_spec((D_FF, D_MODEL)),
            _const_spec((1, D_MODEL)),
            _const_spec((D_MODEL, D_MODEL)),
            _const_spec((PLE_DIM, D_MODEL)),
            _const_spec((1, D_MODEL)),
        ],
        out_specs=pl.BlockSpec((tm, D_MODEL), lambda i: (i, 0)),
        out_shape=jax.ShapeDtypeStruct((n_tok, D_MODEL), F32),
        scratch_shapes=[pltpu.VMEM((tm, D_FF), BF16)],
        compiler_params=pltpu.CompilerParams(
            dimension_semantics=("arbitrary",), vmem_limit_bytes=VMEM_LIMIT_BYTES),
        name="ffn_out",
    )(o2d, h2d, p2d, wout, g2, wgu, wd, gp, wpg, wpp, gf)


def _pad_rows(x):
    rows = x.shape[0]
    if rows == LANES:
        return x
    return jnp.concatenate([x, jnp.zeros((LANES - rows, x.shape[1]), x.dtype)], axis=0)


def _split3(x):
    hi = x.astype(BF16)
    r = x - hi.astype(F32)
    mid = r.astype(BF16)
    lo = (r - mid.astype(F32)).astype(BF16)
    return hi, mid, lo


def _mm_exact_rhs(a_bf16, x):
    hi, mid, lo = _split3(x)
    return (jnp.dot(a_bf16, hi, preferred_element_type=F32) + jnp.dot(a_bf16, mid, preferred_element_type=F32)
            + jnp.dot(a_bf16, lo, preferred_element_type=F32))


def _mm_exact_lhs(x, a_bf16):
    hi, mid, lo = _split3(x)
    return (jnp.dot(hi, a_bf16, preferred_element_type=F32) + jnp.dot(mid, a_bf16, preferred_element_type=F32)
            + jnp.dot(lo, a_bf16, preferred_element_type=F32))


def _bd_cat(y, lane_lo_chunk):
    zero_tile = jnp.zeros((CHUNK, LANES), F32)
    blocks = []
    for hh in range(GDN_HEADS):
        y_tile = y[:, (hh // 2) * LANES:(hh // 2 + 1) * LANES]
        keep = jnp.where(lane_lo_chunk, y_tile, 0.0) if hh % 2 == 0 else jnp.where(lane_lo_chunk, 0.0, y_tile)
        blocks.append(jnp.concatenate([keep, zero_tile] if hh < 2 else [zero_tile, keep], axis=1))
    return jnp.concatenate(blocks, axis=0).astype(BF16)


def _bd_wide(y):
    zero_tile = jnp.zeros((CHUNK, LANES), F32)
    blocks = []
    for hh in range(GDN_HEADS):
        tiles = [zero_tile] * GDN_HEADS
        tiles[hh] = y[:, hh * LANES:(hh + 1) * LANES]
        blocks.append(jnp.concatenate(tiles, axis=1))
    return jnp.concatenate(blocks, axis=0).astype(BF16)


def _bd_pair(sa, sb):
    zeros = jnp.zeros(sa.shape, sa.dtype)
    return jnp.concatenate([jnp.concatenate([sa, zeros], axis=1), jnp.concatenate([zeros, sb], axis=1)], axis=0)


def _gates(ab, alog_row, dt_row):
    zab = ab + dt_row
    softplus = jnp.maximum(zab, 0.0) + jnp.log(1.0 + jnp.exp(-jnp.abs(zab)))
    return -jnp.exp(alog_row) * softplus, _sigmoid(ab)


def _head_l2(x):
    parts = []
    for hh in range(GDN_HEADS):
        xh = x[:, hh * GDN_DK:(hh + 1) * GDN_DK]
        parts.append(xh * jax.lax.rsqrt(jnp.sum(xh * xh, axis=-1, keepdims=True) + 1e-6))
    return jnp.concatenate(parts, axis=1)


def _lane_expand(cols, first_lane, rows):
    return jnp.concatenate(
        [jnp.broadcast_to(cols[:, first_lane + hh:first_lane + hh + 1], (rows, LANES))
         for hh in range(GDN_HEADS)], axis=1)


def _rope(x, rc, rs1, rs2):
    return x * rc + pltpu.roll(x, LANES - ROT_DIM // 2, 1) * rs1 + pltpu.roll(x, ROT_DIM // 2, 1) * rs2


def _uw_rhs(vb_all, kbg_all):
    return jnp.concatenate(
        [jnp.concatenate([vb_all[:, hh * GDN_DV:(hh + 1) * GDN_DV],
                          kbg_all[:, hh * GDN_DK:(hh + 1) * GDN_DK]], axis=1)
         for hh in range(GDN_HEADS)], axis=0)


def _split_uw(uw):
    u_all = jnp.concatenate([uw[hh * CHUNK:(hh + 1) * CHUNK, 0:GDN_DV] for hh in range(GDN_HEADS)], axis=1)
    w_all = jnp.concatenate([uw[hh * CHUNK:(hh + 1) * CHUNK, GDN_DV:] for hh in range(GDN_HEADS)], axis=1)
    return u_all, w_all


def _stack_heads_t(x):
    return jnp.transpose(jnp.concatenate(
        [x[:, hh * GDN_DK:(hh + 1) * GDN_DK] for hh in range(GDN_HEADS)], axis=0))


PROMPT_NS = 4


def _mixer_prompt_kernel(sinks_ref, cur_ref, prev_ref, gnorm_ref, o_ref, sout_ref, kout_ref, vout_ref,
                         s_ref, kprev_ref, vprev_ref, gcc_st, gcr_st, beta_st, os_st, *, tb, n_blocks):
    t = pl.program_id(1)
    ns = PROMPT_NS
    nch = tb // CHUNK
    stages = (gcc_st, gcr_st, beta_st, os_st)
    stage_names = ("gcc", "gcr", "beta", "os")

    @pl.when(t == 0)
    def _init():
        for ref in (s_ref, kprev_ref, vprev_ref) + stages:
            ref[...] = jnp.zeros(ref.shape, F32)

    r_t = jax.lax.broadcasted_iota(jnp.int32, (tb, LANES), 0)
    c_t = jax.lax.broadcasted_iota(jnp.int32, (tb, LANES), 1)
    lane_lo = c_t < CHUNK
    lane_lo_chunk = jax.lax.broadcasted_iota(jnp.int32, (CHUNK, LANES), 1) < CHUNK
    lane_lo_row = jax.lax.broadcasted_iota(jnp.int32, (1, LANES), 1) < CHUNK
    same_chunk = (r_t ^ c_t) < CHUNK
    rk = jax.lax.broadcasted_iota(jnp.int32, (tb, 2 * WINDOW), 0)
    ck = jax.lax.broadcasted_iota(jnp.int32, (tb, 2 * WINDOW), 1)
    first_off = jnp.where(t > 0, 0, WINDOW)
    dist = ck - rk
    kmask = ((dist > jnp.where(ck < WINDOW, first_off, -WINDOW))
             & (dist <= jnp.where(ck < WINDOW, 2 * WINDOW, WINDOW)))

    s_cat = [s_ref[sq] for sq in range(ns)]
    k_prev = [kprev_ref[sq] for sq in range(ns)]
    v_prev = [vprev_ref[sq] for sq in range(ns)]
    staged = [{name: ref[sq] for ref, name in zip(stages, stage_names)} for sq in range(ns)]
    for sq in range(ns):
        staged[sq].update(q=prev_ref[sq, :, 0:GDN_QK_W], k=prev_ref[sq, :, GDN_QK_W:2 * GDN_QK_W],
                          v=prev_ref[sq, :, 2 * GDN_QK_W:GDN_CONV_W], zs=prev_ref[sq, :, P_Z:P_Z + GDN_V_W])
    fr = [{} for _ in range(ns)]

    def front(sq):
        f = fr[sq]
        ab = cur_ref[sq, :, P_AB:P_AB + LANES]
        f["beta"] = ab
        f["gcc"] = _mm_exact_rhs(((c_t <= r_t) & same_chunk).astype(BF16), ab)
        f["gcr"] = _mm_exact_lhs(jnp.transpose(ab)[0:SUBLANES, :],
                                 ((r_t <= c_t) & same_chunk).astype(BF16))
        k_own = cur_ref[sq, :, P_KS:P_KS + SWA_KV_W]
        v_own = cur_ref[sq, :, P_VS:P_VS + SWA_KV_W]
        f["k_own"], f["v_own"] = k_own, v_own
        k_cat = jnp.concatenate([k_prev[sq], k_own], axis=0)
        v_cat = jnp.concatenate([v_prev[sq], v_own], axis=0).astype(BF16)
        q_rows = [None] * SWA_HEADS
        for p in range(SWA_HEADS // 2):
            qp = cur_ref[sq, :, P_QS + p * LANES:P_QS + (p + 1) * LANES]
            q_rows[p] = jnp.where(lane_lo, qp, 0.0)
            q_rows[SWA_HEADS // 2 + p] = jnp.where(lane_lo, 0.0, qp)
        s_all = _mm_nt(jnp.concatenate(q_rows, axis=0), k_cat) * (SWA_HD ** -0.5)
        yield
        pv = [None] * SWA_HEADS
        for hh in range(SWA_HEADS):
            sh = jnp.where(kmask, s_all[hh * tb:(hh + 1) * tb], -jnp.inf)
            sk = sinks_ref[hh]
            m = jnp.maximum(jnp.max(sh, axis=-1, keepdims=True), sk)
            ph = jnp.exp(sh - m)
            den = jnp.sum(ph, axis=-1, keepdims=True) + jnp.exp(sk - m)
            pv[hh] = jnp.dot(ph.astype(BF16), v_cat, preferred_element_type=F32) / den
            yield
        f["os"] = jnp.concatenate(
            [jnp.where(lane_lo, pv[p], pv[SWA_HEADS // 2 + p]) for p in range(SWA_HEADS // 2)], axis=1)

    fronts = [front(sq) for sq in range(ns)]

    def fill():
        for gen in fronts:
            next(gen, None)

    units = [(sq, c) for sq in range(ns) for c in range(nch)]
    rows_of = lambda c: slice(c * CHUNK, (c + 1) * CHUNK)
    ri = jax.lax.broadcasted_iota(jnp.int32, (CHUNK, CAT_W), 0)
    ci = jax.lax.broadcasted_iota(jnp.int32, (CHUNK, CAT_W), 1)
    cj = ci & (CHUNK - 1)
    tri = cj <= ri
    strict = cj < ri
    eye = (cj == ri).astype(F32)
    blk = ri ^ cj
    bd = functools.partial(_bd_cat, lane_lo_chunk=lane_lo_chunk)

    drv = []
    for sq in range(ns):
        st = staged[sq]
        gc_full = _lane_expand(st["gcc"], 0, tb)
        beta_full = _lane_expand(st["beta"], GDN_HEADS, tb)
        egc_full = jnp.exp(gc_full)
        kb_all = st["k"] * beta_full
        drv.append(dict(gc_full=gc_full, kb=kb_all, vb=st["v"] * beta_full, kbg=kb_all * egc_full,
                        qd=st["q"] * egc_full, gcr=st["gcr"], gcr_sw=pltpu.roll(st["gcr"], CHUNK, 1)))
    fill()
    a_mat, qk, xinv, pw = {}, {}, {}, {}
    for u in units:
        sq, c = u
        d, rows = drv[sq], rows_of(c)
        gcc_pairs, gcr_pairs = [], []
        for p in range(GDN_HEADS // 2):
            gcc_pairs.append(jnp.where(lane_lo_chunk, d["gc_full"][rows, (2 * p) * LANES:(2 * p + 1) * LANES],
                                       d["gc_full"][rows, (2 * p + 1) * LANES:(2 * p + 2) * LANES]))
            ra = (d["gcr"] if c == 0 else d["gcr_sw"])[2 * p:2 * p + 1, :]
            rb_ = (d["gcr_sw"] if c == 0 else d["gcr"])[2 * p + 1:2 * p + 2, :]
            gcr_pairs.append(jnp.where(lane_lo_row, ra, rb_))
        gcc = jnp.concatenate(gcc_pairs, axis=1)
        gcr = jnp.concatenate(gcr_pairs, axis=1)
        decay = jnp.exp(jnp.where(tri, gcc - gcr, -jnp.inf))
        kq = jax.lax.dot_general(
            jnp.concatenate([d["kb"][rows], staged[sq]["q"][rows]], axis=0).astype(BF16),
            _bd_wide(staged[sq]["k"][rows]), (((1,), (1,)), ((), ())), preferred_element_type=F32)
        a_mat[u] = jnp.where(strict, kq[0:CHUNK] * decay, 0.0)
        qk[u] = kq[CHUNK:] * decay
    fill()
    for u in units:
        a0 = jnp.where(blk < SUBLANES, a_mat[u], 0.0)
        xinv[u] = eye - a0
        pw[u] = _mm(a0, bd(a0))
    fill()
    for u in units:
        pw_bd = bd(pw[u])
        xinv[u] = xinv[u] + _mm(xinv[u], pw_bd)
        pw[u] = _mm(pw[u], pw_bd)
    fill()
    for u in units:
        xinv[u] = xinv[u] + _mm(xinv[u], bd(pw[u]))
    fill()
    s = SUBLANES
    while s < CHUNK:
        y = {u: _mm(jnp.where((blk < 2 * s) & (blk >= s), a_mat[u], 0.0), bd(xinv[u])) for u in units}
        fill()
        for u in units:
            xinv[u] = xinv[u] - _mm(xinv[u], bd(y[u]))
        fill()
        s *= 2
    prep = {}
    for u in units:
        sq, c = u
        d, rows = drv[sq], rows_of(c)
        u_all, w_all = _split_uw(_mm(bd(xinv[u]), _uw_rhs(d["vb"][rows], d["kbg"][rows])))
        glast = d["gc_full"][c * CHUNK + CHUNK - 1:c * CHUNK + CHUNK, :]
        kdec_t = _stack_heads_t(staged[sq]["k"][rows] * jnp.exp(glast - d["gc_full"][rows]))
        prep[u] = (u_all, w_all, kdec_t, jnp.exp(glast))
    fill()

    o_chunks = [[] for _ in range(ns)]
    for c in range(nch):
        rows = rows_of(c)
        ws, qs = [[] for _ in range(ns)], [[] for _ in range(ns)]
        for sq in range(ns):
            w_all = prep[(sq, c)][1]
            for p in range(GDN_HEADS // 2):
                cols = slice(2 * p * LANES, (2 * p + 2) * LANES)
                r = _mm(jnp.concatenate([w_all[:, cols], drv[sq]["qd"][rows, cols]], axis=0),
                        _bd_pair(s_cat[sq][:, (2 * p) * LANES:(2 * p + 1) * LANES],
                                 s_cat[sq][:, (2 * p + 1) * LANES:(2 * p + 2) * LANES]))
                ws[sq].append(r[0:CHUNK])
                qs[sq].append(r[CHUNK:])
        fill()
        for sq in range(ns):
            u_all, _, kdec_t, eg_last = prep[(sq, c)]
            v_new = u_all - jnp.concatenate(ws[sq], axis=1)
            r2 = _mm(jnp.concatenate([qk[(sq, c)], kdec_t], axis=0), _bd_wide(v_new))
            o_chunks[sq].append(jnp.concatenate(qs[sq], axis=1) + r2[0:CHUNK])
            s_cat[sq] = s_cat[sq] * eg_last + r2[CHUNK:]
        fill()
    for gen in fronts:
        for _ in gen:
            pass

    for sq in range(ns):
        o_gdn = jnp.concatenate(o_chunks[sq], axis=0)
        zs_prev = staged[sq]["zs"]
        og = [_rms(o_gdn[:, hh * GDN_DV:(hh + 1) * GDN_DV], gnorm_ref[...])
              * zs_prev[:, hh * GDN_DV:(hh + 1) * GDN_DV] for hh in range(GDN_HEADS)]
        o_ref[sq] = jnp.concatenate(og + [staged[sq]["os"]], axis=-1).astype(o_ref.dtype)
        s_ref[sq] = s_cat[sq]
        kprev_ref[sq] = fr[sq]["k_own"]
        vprev_ref[sq] = fr[sq]["v_own"]
        for ref, name in zip(stages, stage_names):
            ref[sq] = fr[sq][name]

    @pl.when(t == n_blocks - 1)
    def _window_out():
        for sq in range(ns):
            kout_ref[sq] = fr[sq]["k_own"]
            vout_ref[sq] = fr[sq]["v_own"]

    @pl.when(t == n_blocks)
    def _state_out():
        for sq in range(ns):
            for hh in range(GDN_HEADS):
                sout_ref[sq, hh] = s_ref[sq, :, hh * GDN_DV:(hh + 1) * GDN_DV]


def _mixer_prompt(prep3d, gnorm, sinks):
    bsz, seq, _ = prep3d.shape
    tb, ns = WINDOW, PROMPT_NS
    assert seq % tb == 0 and bsz % ns == 0
    n_blocks = seq // tb
    kern = functools.partial(_mixer_prompt_kernel, tb=tb, n_blocks=n_blocks)
    seq_blk = lambda shape: pl.BlockSpec((ns,) + shape, lambda b, t: (b,) + (0,) * len(shape))
    vmem = lambda *shape: pltpu.VMEM((ns,) + shape, F32)
    return pl.pallas_call(
        kern,
        grid=(bsz // ns, n_blocks + 1),
        in_specs=[
            pl.BlockSpec(memory_space=pltpu.SMEM),
            pl.BlockSpec((ns, tb, P_W), lambda b, t: (b, jnp.minimum(t, n_blocks - 1), 0)),
            pl.BlockSpec((ns, tb, P_W), lambda b, t: (b, jnp.maximum(t - 1, 0), 0)),
            pl.BlockSpec((1, GDN_DV), lambda b, t: (0, 0)),
        ],
        out_specs=[
            pl.BlockSpec((ns, tb, D_MODEL), lambda b, t: (b, jnp.maximum(t - 1, 0), 0)),
            seq_blk((GDN_HEADS, GDN_DK, GDN_DV)),
            seq_blk((WINDOW, SWA_KV_W)),
            seq_blk((WINDOW, SWA_KV_W)),
        ],
        out_shape=[
            jax.ShapeDtypeStruct((bsz, seq, D_MODEL), BF16),
            jax.ShapeDtypeStruct((bsz, GDN_HEADS, GDN_DK, GDN_DV), F32),
            jax.ShapeDtypeStruct((bsz, WINDOW, SWA_KV_W), F32),
            jax.ShapeDtypeStruct((bsz, WINDOW, SWA_KV_W), F32),
        ],
        scratch_shapes=[
            vmem(GDN_DK, GDN_HEADS * GDN_DV),
            vmem(WINDOW, SWA_KV_W),
            vmem(WINDOW, SWA_KV_W),
            vmem(tb, LANES),
            vmem(SUBLANES, LANES),
            vmem(tb, LANES),
            vmem(tb, SWA_Q_W),
        ],
        compiler_params=pltpu.CompilerParams(
            dimension_semantics=("arbitrary", "arbitrary"), vmem_limit_bytes=VMEM_LIMIT_BYTES),
        name="mixer_prompt",
    )(sinks, prep3d, prep3d, gnorm)


def _mixer_sample_kernel(sinks_ref, proj_ref, s0_ref, conv0_ref, k0_ref, v0_ref, convw_ref, alog_ref, dt_ref,
                         gnorm_ref, rc_ref, rs1_ref, rs2_ref,
                         o_ref, sout_ref, convout_ref, kout_ref, vout_ref, xext_ref):
    nb, ts, rows = SAMPLE_NB, SAMPLE_T, CHUNK
    seqs = range(nb)
    grp = lambda x, b: x[b * ts:(b + 1) * ts]

    hist = SUBLANES - (GDN_CONV - 1)
    conv_parts = []
    for b in seqs:
        xext_ref[2 * ts * b + hist:2 * ts * b + ts, :] = conv0_ref[b]
        xext_ref[2 * ts * b + ts:2 * ts * (b + 1), :] = proj_ref[b, :, P_QKV:P_QKV + GDN_CONV_W]
    for b in seqs:
        base = 2 * ts * b + hist
        conv = xext_ref[base:base + ts, :] * convw_ref[0:1, :]
        for j in range(1, GDN_CONV):
            conv = conv + xext_ref[base + j:base + j + ts, :] * convw_ref[j:j + 1, :]
        conv_parts.append(conv)
        convout_ref[b] = xext_ref[2 * ts * (b + 1) - (GDN_CONV - 1):2 * ts * (b + 1), :]
    qkv = _silu(jnp.concatenate(conv_parts, axis=0))
    q_all = _head_l2(qkv[:, 0:GDN_QK_W]) * (GDN_DK ** -0.5)
    k_all = _head_l2(qkv[:, GDN_QK_W:2 * GDN_QK_W])
    v_all = qkv[:, 2 * GDN_QK_W:]

    def proj_cols(lo, width):
        return jnp.concatenate([proj_ref[b, :, lo:lo + width] for b in seqs], axis=0)

    g_all, beta_all = _gates(proj_cols(P_AB, LANES), alog_ref[...], dt_ref[...])

    r_t = jax.lax.broadcasted_iota(jnp.int32, (rows, LANES), 0)
    c_t = jax.lax.broadcasted_iota(jnp.int32, (rows, LANES), 1)
    r_p = jax.lax.broadcasted_iota(jnp.int32, (LANES, LANES), 0)
    c_p = jax.lax.broadcasted_iota(jnp.int32, (LANES, LANES), 1)
    g_pad = _pad_rows(g_all)
    gc_cols = _mm_exact_rhs(((c_t <= r_t) & ((r_t ^ c_t) < ts)).astype(BF16), g_pad)
    gc_rows = _mm_exact_lhs(jnp.transpose(g_pad)[0:SUBLANES, :],
                            ((r_p <= c_p) & ((r_p ^ c_p) < ts)).astype(BF16))
    gc_rows_sw = pltpu.roll(gc_rows, CHUNK, 1)

    lane_lo = c_t < CHUNK
    lane_lo_row = jax.lax.broadcasted_iota(jnp.int32, (1, LANES), 1) < CHUNK
    lane_lo_t = jax.lax.broadcasted_iota(jnp.int32, (ts, LANES), 1) < CHUNK
    gc_full = _lane_expand(gc_cols, 0, rows)
    beta_full = _lane_expand(beta_all, GDN_HEADS, rows)
    egc_full = jnp.exp(gc_full)
    kb_all = k_all * beta_full
    vb_all = v_all * beta_full
    kbg_all = kb_all * egc_full
    qd_all = q_all * egc_full

    ri = jax.lax.broadcasted_iota(jnp.int32, (rows, CAT_W), 0)
    ci = jax.lax.broadcasted_iota(jnp.int32, (rows, CAT_W), 1)
    cj = ci & (CHUNK - 1)
    same_seq = (ri ^ cj) < ts
    eye = (cj == ri).astype(F32)
    bd = functools.partial(_bd_cat, lane_lo_chunk=lane_lo)

    gcc = jnp.concatenate(
        [jnp.where(lane_lo, gc_full[:, (2 * p) * LANES:(2 * p + 1) * LANES],
                   gc_full[:, (2 * p + 1) * LANES:(2 * p + 2) * LANES]) for p in range(GDN_HEADS // 2)], axis=1)
    gcr = jnp.concatenate(
        [jnp.where(lane_lo_row, gc_rows[2 * p:2 * p + 1, :], gc_rows_sw[2 * p + 1:2 * p + 2, :])
         for p in range(GDN_HEADS // 2)], axis=1)
    decay = jnp.exp(jnp.where((cj <= ri) & same_seq, gcc - gcr, -jnp.inf))
    kq = jax.lax.dot_general(
        jnp.concatenate([kb_all, q_all], axis=0).astype(BF16), _bd_wide(k_all),
        (((1,), (1,)), ((), ())), preferred_element_type=F32)
    a_mat = jnp.where(cj < ri, kq[0:rows] * decay, 0.0)
    qk = kq[rows:] * decay
    xinv = eye - a_mat
    pw = _mm(a_mat, bd(a_mat))
    pw_bd = bd(pw)
    xinv = xinv + _mm(xinv, pw_bd)
    pw = _mm(pw, pw_bd)
    xinv = xinv + _mm(xinv, bd(pw))
    u_all, w_all = _split_uw(_mm(bd(xinv), _uw_rhs(vb_all, kbg_all)))
    glast = [gc_full[(b + 1) * ts - 1:(b + 1) * ts, :] for b in seqs]
    glast_rows = jnp.concatenate([jnp.broadcast_to(glast[b], (ts, GDN_QK_W)) for b in seqs], axis=0)
    kdec_t = _stack_heads_t(k_all * jnp.exp(glast_rows - gc_full))

    s_old, ws, qs = [], [], []
    for b in seqs:
        s_b = jnp.concatenate([s0_ref[b, hh] for hh in range(GDN_HEADS)], axis=1)
        s_old.append(s_b)
        ws_b, qs_b = [], []
        for p in range(GDN_HEADS // 2):
            cols = slice(2 * p * LANES, (2 * p + 2) * LANES)
            r = _mm(jnp.concatenate([grp(w_all, b)[:, cols], grp(qd_all, b)[:, cols]], axis=0),
                    _bd_pair(s_b[:, (2 * p) * LANES:(2 * p + 1) * LANES],
                             s_b[:, (2 * p + 1) * LANES:(2 * p + 2) * LANES]))
            ws_b.append(r[0:ts])
            qs_b.append(r[ts:])
        ws.append(jnp.concatenate(ws_b, axis=1))
        qs.append(jnp.concatenate(qs_b, axis=1))
    v_new = u_all - jnp.concatenate(ws, axis=0)
    v_bd = _bd_wide(v_new)
    o_gdn = jnp.concatenate(qs, axis=0) + _mm(qk, v_bd)
    cb = jax.lax.broadcasted_iota(jnp.int32, (GDN_DK, CAT_W), 1) & (CHUNK - 1)
    upd = _mm(jnp.concatenate([jnp.where((cb ^ (b * ts)) < ts, kdec_t, 0.0) for b in seqs], axis=0), v_bd)
    for b in seqs:
        s_new = s_old[b] * jnp.exp(glast[b]) + upd[b * GDN_DK:(b + 1) * GDN_DK]
        for hh in range(GDN_HEADS):
            sout_ref[b, hh] = s_new[:, hh * GDN_DV:(hh + 1) * GDN_DV]
    og = []
    for hh in range(GDN_HEADS):
        oh = _rms(o_gdn[:, hh * GDN_DV:(hh + 1) * GDN_DV], gnorm_ref[...])
        og.append(oh * _silu(proj_cols(P_Z + hh * GDN_DV, GDN_DV)))

    tile8 = lambda x: jnp.concatenate([x] * nb, axis=0)
    rc, rs1, rs2 = tile8(rc_ref[...]), tile8(rs1_ref[...]), tile8(rs2_ref[...])
    k_own = _rope(proj_cols(P_KS, SWA_KV_W), rc, rs1, rs2)
    v_own = proj_cols(P_VS, SWA_KV_W)
    q_rows = [None] * SWA_HEADS
    for p in range(SWA_HEADS // 2):
        qp = _rope(proj_cols(P_QS + p * LANES, LANES), rc, rs1, rs2)
        q_rows[p] = jnp.where(lane_lo, qp, 0.0)
        q_rows[SWA_HEADS // 2 + p] = jnp.where(lane_lo, 0.0, qp)
    scale = SWA_HD ** -0.5
    s_own_all = _mm_nt(jnp.concatenate(q_rows, axis=0), _pad_rows(k_own)) * scale
    tok = r_t & (ts - 1)
    prev_mask = c_t > tok
    sink_col = jnp.zeros((rows, 1), F32)
    for hh in range(SWA_HEADS):
        sink_col = jnp.where((r_t[:, 0:1] ^ (hh * ts)) < ts, sinks_ref[hh], sink_col)
    pp_all, po_all, den_all = [], [], []
    for b in seqs:
        q_b = jnp.concatenate([grp(q_rows[hh], b) for hh in range(SWA_HEADS)], axis=0)
        sp = jnp.where(prev_mask, _mm_nt(q_b, k0_ref[b]) * scale, -jnp.inf)
        so_raw = jnp.concatenate([s_own_all[hh * rows + b * ts:hh * rows + (b + 1) * ts]
                                  for hh in range(SWA_HEADS)], axis=0)
        so = jnp.where(((c_t ^ (b * ts)) < ts) & ((c_t & (ts - 1)) <= tok), so_raw, -jnp.inf)
        m = jnp.maximum(jnp.maximum(jnp.max(sp, axis=-1, keepdims=True),
                                    jnp.max(so, axis=-1, keepdims=True)), sink_col)
        pp = jnp.exp(sp - m)
        po = jnp.exp(so - m)
        den_all.append(jnp.sum(pp, axis=-1, keepdims=True) + jnp.sum(po, axis=-1, keepdims=True)
                       + jnp.exp(sink_col - m))
        pp_all.append(_mm(pp, v0_ref[b]))
        po_all.append(po)
    pv_own = _mm(jnp.concatenate(po_all, axis=0), _pad_rows(v_own))
    os_rows = []
    for b in seqs:
        pv_b = (pp_all[b] + pv_own[b * rows:(b + 1) * rows]) / den_all[b]
        os_rows.append(jnp.concatenate(
            [jnp.where(lane_lo_t, pv_b[p * ts:(p + 1) * ts],
                       pv_b[(SWA_HEADS // 2 + p) * ts:(SWA_HEADS // 2 + p + 1) * ts])
             for p in range(SWA_HEADS // 2)], axis=1))
        kout_ref[b] = jnp.concatenate([k0_ref[b, ts:, :], grp(k_own, b)], axis=0)
        vout_ref[b] = jnp.concatenate([v0_ref[b, ts:, :], grp(v_own, b)], axis=0)
    o_full = jnp.concatenate(og + [jnp.concatenate(os_rows, axis=0)], axis=-1)
    for b in seqs:
        o_ref[b] = grp(o_full, b).astype(o_ref.dtype)


def _mixer_sample(proj3d, s0, conv0, k0, v0, convw, alog_row, dt_row, gnorm, sinks, rc, rs1, rs2):
    bsz, seq, _ = proj3d.shape
    nb = SAMPLE_NB
    assert seq == SAMPLE_T and bsz % nb == 0
    seq_blk = lambda shape: pl.BlockSpec((nb,) + shape, lambda i: (i,) + (0,) * len(shape))
    const_blk = lambda shape: pl.BlockSpec(shape, lambda i: (0,) * len(shape))
    return pl.pallas_call(
        _mixer_sample_kernel,
        grid=(bsz // nb,),
        in_specs=[
            pl.BlockSpec(memory_space=pltpu.SMEM),
            seq_blk((seq, P_W)),
            seq_blk((GDN_HEADS, GDN_DK, GDN_DV)),
            seq_blk((GDN_CONV - 1, GDN_CONV_W)),
            seq_blk((WINDOW, SWA_KV_W)),
            seq_blk((WINDOW, SWA_KV_W)),
            const_blk((GDN_CONV, GDN_CONV_W)),
            const_blk((1, LANES)),
            const_blk((1, LANES)),
            const_blk((1, GDN_DV)),
            const_blk((seq, LANES)),
            const_blk((seq, LANES)),
            const_blk((seq, LANES)),
        ],
        out_specs=[
            seq_blk((seq, D_MODEL)),
            seq_blk((GDN_HEADS, GDN_DK, GDN_DV)),
            seq_blk((GDN_CONV - 1, GDN_CONV_W)),
            seq_blk((WINDOW, SWA_KV_W)),
            seq_blk((WINDOW, SWA_KV_W)),
        ],
        out_shape=[
            jax.ShapeDtypeStruct((bsz, seq, D_MODEL), BF16),
            jax.ShapeDtypeStruct((bsz, GDN_HEADS, GDN_DK, GDN_DV), F32),
            jax.ShapeDtypeStruct((bsz, GDN_CONV - 1, GDN_CONV_W), F32),
            jax.ShapeDtypeStruct((bsz, WINDOW, SWA_KV_W), F32),
            jax.ShapeDtypeStruct((bsz, WINDOW, SWA_KV_W), F32),
        ],
        scratch_shapes=[pltpu.VMEM((2 * SAMPLE_T * nb, GDN_CONV_W), F32)],
        compiler_params=pltpu.CompilerParams(
            dimension_semantics=("arbitrary",), vmem_limit_bytes=VMEM_LIMIT_BYTES),
        name="mixer_sample",
    )(sinks, proj3d, s0, conv0, k0, v0, convw, alog_row, dt_row, gnorm, rc, rs1, rs2)


def _rope_tables(pos):
    half = ROT_DIM // 2
    inv = ROPE_THETA ** (-jnp.arange(half, dtype=F32) * 2.0 / ROT_DIM)
    ang = pos.astype(F32)[:, None] * inv[None, :]
    cos, sin = jnp.cos(ang), jnp.sin(ang)
    n = pos.shape[0]
    rest = SWA_HD - ROT_DIM
    c = jnp.concatenate([cos, cos, jnp.ones((n, rest), F32)], axis=1)
    s1 = jnp.concatenate([-sin, jnp.zeros((n, half + rest), F32)], axis=1)
    s2 = jnp.concatenate([jnp.zeros((n, half), F32), sin, jnp.zeros((n, rest), F32)], axis=1)
    reps = LANES // SWA_HD
    return jnp.tile(c, (1, reps)), jnp.tile(s1, (1, reps)), jnp.tile(s2, (1, reps))


def _projection_tiles(w_in):
    o_z = GDN_CONV_W
    o_a = o_z + GDN_V_W
    o_qs = o_a + 2 * GDN_HEADS
    o_ks = o_qs + SWA_Q_W
    o_vs = o_ks + SWA_KV_W
    cols = lambda lo, width: w_in[:, lo:lo + width]
    tiles = [cols(t * LANES, LANES) for t in range((o_z + GDN_V_W) // LANES)]
    half = SWA_HEADS // 2
    tiles += [jnp.concatenate([cols(o_qs + p * SWA_HD, SWA_HD), cols(o_qs + (half + p) * SWA_HD, SWA_HD)], axis=1)
              for p in range(half)]
    tiles += [cols(o_ks, SWA_KV_W), cols(o_vs, SWA_KV_W)]
    tiles.append(jnp.concatenate([cols(o_a, 2 * GDN_HEADS),
                                  jnp.zeros((D_MODEL, LANES - 2 * GDN_HEADS), w_in.dtype)], axis=1))
    return tiles


def _layer_weights(norm_ffn1, ffn1_gu, ffn1_down, norm_mix, w_in, conv_w, a_log, dt_bias, gdn_norm, sinks,
                   w_out, norm_ffn2, ffn2_gu, ffn2_down, norm_ple, ple_proj, ple_gate, norm_final):
    tiles = _projection_tiles(w_in)
    win_p = jnp.concatenate(tiles, axis=1).astype(BF16)
    win_prep = jnp.concatenate([win_p[:, t * LANES:(t + 1) * LANES] for t in PREP_TILE_ORDER], axis=1)
    half = SWA_HEADS // 2
    head_rows = lambda hh: w_out[GDN_V_W + hh * SWA_HD:GDN_V_W + (hh + 1) * SWA_HD]
    wout_p = jnp.concatenate([w_out[:GDN_V_W]] + [head_rows(hh) for p in range(half) for hh in (p, half + p)],
                             axis=0).astype(BF16)
    pad_row = lambda v: jnp.concatenate([v.astype(F32), jnp.zeros((LANES - v.shape[0],), F32)])[None, :]
    return dict(
        g1=norm_ffn1[None, :], wgu1=ffn1_gu.astype(BF16), wd1=ffn1_down.astype(BF16),
        gm=norm_mix[None, :], win=win_p, win_prep=win_prep, convw=conv_w, alog=pad_row(a_log), dt=pad_row(dt_bias),
        gnorm=gdn_norm[None, :], sinks=sinks.astype(F32), wout=wout_p,
        g2=norm_ffn2[None, :], wgu2=ffn2_gu.astype(BF16), wd2=ffn2_down.astype(BF16),
        gp=norm_ple[None, :], wpp=ple_proj.astype(BF16), wpg=ple_gate.astype(BF16), gf=norm_final[None, :])


def _group(x, p, w, state, pos0):
    bsz, seq, _ = x.shape
    rc, rs1, rs2 = _rope_tables(pos0 + jnp.arange(seq))
    x2d = x.reshape(bsz * seq, D_MODEL)
    if state is None:
        h2d, prep2d, conv_new = _ffn_in_prep(x2d, seq, w["g1"], w["wgu1"], w["wd1"], w["gm"], w["win_prep"],
                                             w["convw"], w["alog"], w["dt"], rc, rs1, rs2)
        o, s_new, k_new, v_new = _mixer_prompt(prep2d.reshape(bsz, seq, P_W), w["gnorm"], w["sinks"])
    else:
        h2d, proj2d = _ffn_in(x2d, w["g1"], w["wgu1"], w["wd1"], w["gm"], w["win"])
        s0, conv0, k0, v0 = state
        o, s_new, conv_new, k_new, v_new = _mixer_sample(
            proj2d.reshape(bsz, seq, P_W), s0, conv0, k0, v0, w["convw"], w["alog"], w["dt"], w["gnorm"],
            w["sinks"], rc, rs1, rs2)
    y = _ffn_out(o.reshape(bsz * seq, D_MODEL), h2d, p.reshape(bsz * seq, PLE_DIM), w["wout"], w["g2"],
                 w["wgu2"], w["wd2"], w["gp"], w["wpg"], w["wpp"], w["gf"])
    kv_shape = (1, bsz, WINDOW, SWA_KV_HEADS, SWA_HD)
    return (y.reshape(bsz, seq, D_MODEL), s_new[None], conv_new[None], k_new.reshape(kv_shape),
            v_new.reshape(kv_shape))


def kernel(x_prompt, x_sample, state_gdn, state_conv, cache_swa_k, cache_swa_v, p_prompt, p_sample, norm_ffn1, ffn1_gu, ffn1_down, norm_mix, w_in, conv_w, a_log, dt_bias, gdn_norm, sinks, w_out, norm_ffn2, ffn2_gu, ffn2_down, norm_ple, ple_proj, ple_gate, norm_final):
    assert state_gdn.shape[0] == 1, "one layer"
    w = _layer_weights(norm_ffn1[0], ffn1_gu[0], ffn1_down[0], norm_mix[0], w_in[0], conv_w[0], a_log[0],
                       dt_bias[0], gdn_norm[0], sinks[0], w_out[0], norm_ffn2[0], ffn2_gu[0], ffn2_down[0],
                       norm_ple[0], ple_proj[0], ple_gate[0], norm_final)
    bs = x_sample.shape[0]
    yp, sg_p, sc_p, kk_p, vv_p = _group(x_prompt, p_prompt[0], w, None, 0)
    ys, sg_s, sc_s, kk_s, vv_s = _group(
        x_sample, p_sample[0], w,
        (state_gdn[0], state_conv[0], cache_swa_k[0].reshape(bs, WINDOW, SWA_KV_W),
         cache_swa_v[0].reshape(bs, WINDOW, SWA_KV_W)), PAST_LEN)
    return (yp, ys, sg_p, sc_p, kk_p, vv_p, sg_s, sc_s, kk_s, vv_s)
```

```python
import functools

import jax
import jax.numpy as jnp
from jax.experimental import pallas as pl
from jax.experimental.pallas import tpu as pltpu

F32 = jnp.float32
BF16 = jnp.bfloat16

D_MODEL = 1024
D_FF = 2816
PLE_DIM = 256
NORM_EPS = 1e-6
GDN_HEADS = 4
GDN_DK = 128
GDN_DV = 128
GDN_CONV = 4
GDN_QK_W = GDN_HEADS * GDN_DK
GDN_V_W = GDN_HEADS * GDN_DV
GDN_CONV_W = 2 * GDN_QK_W + GDN_V_W
SWA_HEADS = 8
SWA_KV_HEADS = 2
SWA_HD = 64
SWA_Q_W = SWA_HEADS * SWA_HD
SWA_KV_W = SWA_KV_HEADS * SWA_HD
WINDOW = 128
ROT_DIM = SWA_HD // 4
ROPE_THETA = 500000.0
PAST_LEN = 16384

LANES = 128
SUBLANES = 8
VMEM_LIMIT_BYTES = 56 * 1024 * 1024

P_QKV = 0
P_Z = P_QKV + GDN_CONV_W
P_QS = P_Z + GDN_V_W
P_KS = P_QS + SWA_Q_W
P_VS = P_KS + SWA_KV_W
P_AB = P_VS + SWA_KV_W
P_W = P_AB + LANES

FF_CHUNK = 256
TOKEN_TILE = 512

CHUNK = 64
CAT_W = GDN_HEADS * CHUNK
N_CONV_TILES = GDN_CONV_W // LANES
N_PROJ_TILES = P_W // LANES
assert N_CONV_TILES == N_PROJ_TILES - N_CONV_TILES + 1
PREP_TILE_ORDER = (tuple(t for pair in zip(range(N_CONV_TILES - 1), range(N_CONV_TILES, N_PROJ_TILES)) for t in pair)
                   + (N_CONV_TILES - 1,))
SAMPLE_T = SUBLANES
SAMPLE_NB = CHUNK // SAMPLE_T


def _rms(x, g):
    return x * jax.lax.rsqrt(jnp.mean(x * x, axis=-1, keepdims=True) + NORM_EPS) * g


def _sigmoid(x):
    return 1.0 / (1.0 + jnp.exp(-x))


def _silu(x):
    return x * _sigmoid(x)


def _mm(a, b):
    return jnp.dot(a.astype(BF16), b.astype(BF16), preferred_element_type=F32)


def _mm_nt(a, b):
    return jax.lax.dot_general(a.astype(BF16), b.astype(BF16), (((1,), (1,)), ((), ())),
                               preferred_element_type=F32)


def _swiglu_acc(n_bf16, wgu_ref, wd_ref, act_ref):
    for c in range(D_FF // FF_CHUNK):
        lo = c * FF_CHUNK
        gate = jnp.dot(n_bf16, wgu_ref[:, lo:lo + FF_CHUNK], preferred_element_type=F32)
        up = jnp.dot(n_bf16, wgu_ref[:, D_FF + lo:D_FF + lo + FF_CHUNK], preferred_element_type=F32)
        act_ref[:, lo:lo + FF_CHUNK] = (_silu(gate) * up).astype(BF16)
    return jnp.dot(act_ref[...], wd_ref[...], preferred_element_type=F32)


def _ffn_in_kernel(x_ref, g1_ref, wgu_ref, wd_ref, gm_ref, win_ref, h_ref, proj_ref, act_ref):
    x = x_ref[...]
    n1 = _rms(x, g1_ref[...]).astype(BF16)
    h = x + 0.5 * _swiglu_acc(n1, wgu_ref, wd_ref, act_ref)
    h_ref[...] = h
    n = _rms(h, gm_ref[...]).astype(BF16)
    proj_ref[...] = jnp.dot(n, win_ref[...], preferred_element_type=F32)


def _const_spec(shape):
    return pl.BlockSpec(shape, lambda i: (0,) * len(shape), pipeline_mode=pl.Buffered(1))


def _ffn_in(x2d, g1, wgu, wd, gm, win):
    n_tok = x2d.shape[0]
    tm = min(TOKEN_TILE, n_tok)
    assert n_tok % tm == 0
    return pl.pallas_call(
        _ffn_in_kernel,
        grid=(n_tok // tm,),
        in_specs=[
            pl.BlockSpec((tm, D_MODEL), lambda i: (i, 0)),
            _const_spec((1, D_MODEL)),
            _const_spec((D_MODEL, 2 * D_FF)),
            _const_spec((D_FF, D_MODEL)),
            _const_spec((1, D_MODEL)),
            _const_spec((D_MODEL, P_W)),
        ],
        out_specs=[
            pl.BlockSpec((tm, D_MODEL), lambda i: (i, 0)),
            pl.BlockSpec((tm, P_W), lambda i: (i, 0)),
        ],
        out_shape=[
            jax.ShapeDtypeStruct((n_tok, D_MODEL), F32),
            jax.ShapeDtypeStruct((n_tok, P_W), F32),
        ],
        scratch_shapes=[pltpu.VMEM((tm, D_FF), BF16)],
        compiler_params=pltpu.CompilerParams(
            dimension_semantics=("arbitrary",), vmem_limit_bytes=VMEM_LIMIT_BYTES),
        name="ffn_in",
    )(x2d, g1, wgu, wd, gm, win)


def _ffn_in_prep_kernel(x_ref, g1_ref, wgu_ref, wd_ref, gm_ref, win_ref, convw_ref, alog_ref, dt_ref,
                        rc_ref, rs1_ref, rs2_ref, h_ref, prep_ref, convout_ref, n_st, xext_ref, *, tiles_per_seq):
    i = pl.program_id(0)
    tm = x_ref.shape[0]
    piece = 2 * LANES

    @pl.when(i == 0)
    def _init():
        n_st[...] = jnp.zeros(n_st.shape, n_st.dtype)
        xext_ref[...] = jnp.zeros(xext_ref.shape, F32)

    n_prev = n_st[...]
    first_of_seq = ((i - 1) % tiles_per_seq) == 0

    def prep_tile(lo, val):
        cols = slice(lo, lo + LANES)
        if lo < P_Z:
            hist = jnp.where(first_of_seq, 0.0, xext_ref[:, cols])
            xext_ref[:, cols] = val[tm - SUBLANES:tm]
            convout_ref[0, :, cols] = val[tm - (GDN_CONV - 1):tm]
            ext = jnp.concatenate([hist, val], axis=0)
            conv = val * convw_ref[GDN_CONV - 1:GDN_CONV, cols]
            for j in range(1, GDN_CONV):
                conv = conv + pltpu.roll(ext, j, 0)[SUBLANES:] * convw_ref[GDN_CONV - 1 - j:GDN_CONV - j, cols]
            act = _silu(conv)
            if lo < 2 * GDN_QK_W:
                scale = GDN_DK ** -0.5 if lo < GDN_QK_W else 1.0
                act = act * (jax.lax.rsqrt(jnp.sum(act * act, axis=-1, keepdims=True) + 1e-6) * scale)
            prep_ref[:, cols] = act
        elif lo < P_QS:
            prep_ref[:, cols] = _silu(val)
        elif lo < P_VS:
            prep_ref[:, cols] = _rope(val, rc_ref[...], rs1_ref[...], rs2_ref[...])
        elif lo < P_AB:
            prep_ref[:, cols] = val
        else:
            g_all, beta_all = _gates(val, alog_ref[...], dt_ref[...])
            lane = jax.lax.broadcasted_iota(jnp.int32, (tm, LANES), 1)
            prep_ref[:, cols] = jnp.where(lane < GDN_HEADS, g_all, beta_all)

    def proj_piece(lo):
        width = min(piece, P_W - lo)
        proj = jnp.dot(n_prev, win_ref[:, lo:lo + width], preferred_element_type=F32)
        for off in range(0, width, LANES):
            prep_tile(PREP_TILE_ORDER[(lo + off) // LANES] * LANES, proj[:, off:off + LANES])

    x = x_ref[...]
    n1 = _rms(x, g1_ref[...]).astype(BF16)
    acc = jnp.zeros((tm, D_MODEL), F32)
    for c in range(D_FF // FF_CHUNK):
        lo = c * FF_CHUNK
        gate = jnp.dot(n1, wgu_ref[:, lo:lo + FF_CHUNK], preferred_element_type=F32)
        up = jnp.dot(n1, wgu_ref[:, D_FF + lo:D_FF + lo + FF_CHUNK], preferred_element_type=F32)
        act = (_silu(gate) * up).astype(BF16)
        acc = acc + jnp.dot(act, wd_ref[lo:lo + FF_CHUNK, :], preferred_element_type=F32)
        proj_piece(c * piece)
    for lo in range((D_FF // FF_CHUNK) * piece, P_W, piece):
        proj_piece(lo)
    h = x + 0.5 * acc
    h_ref[...] = h
    n_st[...] = _rms(h, gm_ref[...]).astype(BF16)


def _ffn_in_prep(x2d, seq_len, g1, wgu, wd, gm, win, convw, alog_row, dt_row, rc, rs1, rs2):
    n_tok = x2d.shape[0]
    tm = TOKEN_TILE
    assert seq_len % tm == 0 and n_tok % seq_len == 0
    tiles_per_seq = seq_len // tm
    n_tiles = n_tok // tm
    cur = lambda i: jnp.minimum(i, n_tiles - 1)
    prev = lambda i: jnp.maximum(i - 1, 0)
    rope_spec = pl.BlockSpec((tm, LANES), lambda i: (prev(i) % tiles_per_seq, 0))
    return pl.pallas_call(
        functools.partial(_ffn_in_prep_kernel, tiles_per_seq=tiles_per_seq),
        grid=(n_tiles + 1,),
        in_specs=[
            pl.BlockSpec((tm, D_MODEL), lambda i: (cur(i), 0)),
            _const_spec((1, D_MODEL)),
            _const_spec((D_MODEL, 2 * D_FF)),
            _const_spec((D_FF, D_MODEL)),
            _const_spec((1, D_MODEL)),
            _const_spec((D_MODEL, P_W)),
            _const_spec((GDN_CONV, GDN_CONV_W)),
            _const_spec((1, LANES)),
            _const_spec((1, LANES)),
            rope_spec, rope_spec, rope_spec,
        ],
        out_specs=[
            pl.BlockSpec((tm, D_MODEL), lambda i: (cur(i), 0)),
            pl.BlockSpec((tm, P_W), lambda i: (prev(i), 0)),
            pl.BlockSpec((1, GDN_CONV - 1, GDN_CONV_W), lambda i: (prev(i) // tiles_per_seq, 0, 0)),
        ],
        out_shape=[
            jax.ShapeDtypeStruct((n_tok, D_MODEL), F32),
            jax.ShapeDtypeStruct((n_tok, P_W), F32),
            jax.ShapeDtypeStruct((n_tok // seq_len, GDN_CONV - 1, GDN_CONV_W), F32),
        ],
        scratch_shapes=[pltpu.VMEM((tm, D_MODEL), BF16), pltpu.VMEM((SUBLANES, GDN_CONV_W), F32)],
        compiler_params=pltpu.CompilerParams(
            dimension_semantics=("arbitrary",), vmem_limit_bytes=VMEM_LIMIT_BYTES),
        name="ffn_in_prep",
    )(x2d, g1, wgu, wd, gm, win, convw, alog_row, dt_row, rc, rs1, rs2)


def _ffn_out_kernel(o_ref, h_ref, p_ref, wout_ref, g2_ref, wgu_ref, wd_ref, gp_ref, wpg_ref, wpp_ref,
                    gf_ref, y_ref, act_ref):
    tm = h_ref.shape[0]
    halves = (slice(0, tm // 2), slice(tm // 2, tm))
    h_in, n2 = [], []
    for r in halves:
        h_r = h_ref[r, :] + jnp.dot(o_ref[r, :], wout_ref[...], preferred_element_type=F32)
        h_in.append(h_r)
        n2.append(_rms(h_r, g2_ref[...]).astype(BF16))
    h = jnp.concatenate(h_in, axis=0) + 0.5 * _swiglu_acc(jnp.concatenate(n2, axis=0), wgu_ref, wd_ref, act_ref)
    for r in halves:
        h_r = h[r]
        npl = _rms(h_r, gp_ref[...]).astype(BF16)
        gate = _sigmoid(jnp.dot(npl, wpg_ref[...], preferred_element_type=F32))
        pe = jnp.dot(p_ref[r, :].astype(BF16), wpp_ref[...], preferred_element_type=F32)
        y_ref[r, :] = _rms(h_r + gate * pe, gf_ref[...])


def _ffn_out(o2d, h2d, p2d, wout, g2, wgu, wd, gp, wpg, wpp, gf):
    n_tok = h2d.shape[0]
    tm = min(TOKEN_TILE, n_tok)
    assert n_tok % tm == 0
    return pl.pallas_call(
        _ffn_out_kernel,
        grid=(n_tok // tm,),
        in_specs=[
            pl.BlockSpec((tm, D_MODEL), lambda i: (i, 0)),
            pl.BlockSpec((tm, D_MODEL), lambda i: (i, 0)),
            pl.BlockSpec((tm, PLE_DIM), lambda i: (i, 0)),
            _const_spec((D_MODEL, D_MODEL)),
            _const_spec((1, D_MODEL)),
            _const_spec((D_MODEL, 2 * D_FF)),
            _const_spec((D_FF, D_MODEL)),
            _const_spec((1, D_MODEL)),
            _const_spec((D_MODEL, D_MODEL)),
            _const_spec((PLE_DIM, D_MODEL)),
            _const_spec((1, D_MODEL)),
        ],
        out_specs=pl.BlockSpec((tm, D_MODEL), lambda i: (i, 0)),
        out_shape=jax.ShapeDtypeStruct((n_tok, D_MODEL), F32),
        scratch_shapes=[pltpu.VMEM((tm, D_FF), BF16)],
        compiler_params=pltpu.CompilerParams(
            dimension_semantics=("arbitrary",), vmem_limit_bytes=VMEM_LIMIT_BYTES),
        name="ffn_out",
    )(o2d, h2d, p2d, wout, g2, wgu, wd, gp, wpg, wpp, gf)


def _pad_rows(x):
    rows = x.shape[0]
    if rows == LANES:
        return x
    return jnp.concatenate([x, jnp.zeros((LANES - rows, x.shape[1]), x.dtype)], axis=0)


def _split3(x):
    hi = x.astype(BF16)
    r = x - hi.astype(F32)
    mid = r.astype(BF16)
    lo = (r - mid.astype(F32)).astype(BF16)
    return hi, mid, lo


def _mm_exact_rhs(a_bf16, x):
    hi, mid, lo = _split3(x)
    return (jnp.dot(a_bf16, hi, preferred_element_type=F32) + jnp.dot(a_bf16, mid, preferred_element_type=F32)
            + jnp.dot(a_bf16, lo, preferred_element_type=F32))


def _mm_exact_lhs(x, a_bf16):
    hi, mid, lo = _split3(x)
    return (jnp.dot(hi, a_bf16, preferred_element_type=F32) + jnp.dot(mid, a_bf16, preferred_element_type=F32)
            + jnp.dot(lo, a_bf16, preferred_element_type=F32))


def _bd_cat(y, lane_lo_chunk):
    zero_tile = jnp.zeros((CHUNK, LANES), F32)
    blocks = []
    for hh in range(GDN_HEADS):
        y_tile = y[:, (hh // 2) * LANES:(hh // 2 + 1) * LANES]
        keep = jnp.where(lane_lo_chunk, y_tile, 0.0) if hh % 2 == 0 else jnp.where(lane_lo_chunk, 0.0, y_tile)
        blocks.append(jnp.concatenate([keep, zero_tile] if hh < 2 else [zero_tile, keep], axis=1))
    return jnp.concatenate(blocks, axis=0).astype(BF16)


def _bd_wide(y):
    zero_tile = jnp.zeros((CHUNK, LANES), F32)
    blocks = []
    for hh in range(GDN_HEADS):
        tiles = [zero_tile] * GDN_HEADS
        tiles[hh] = y[:, hh * LANES:(hh + 1) * LANES]
        blocks.append(jnp.concatenate(tiles, axis=1))
    return jnp.concatenate(blocks, axis=0).astype(BF16)


def _bd_pair(sa, sb):
    zeros = jnp.zeros(sa.shape, sa.dtype)
    return jnp.concatenate([jnp.concatenate([sa, zeros], axis=1), jnp.concatenate([zeros, sb], axis=1)], axis=0)


def _gates(ab, alog_row, dt_row):
    zab = ab + dt_row
    softplus = jnp.maximum(zab, 0.0) + jnp.log(1.0 + jnp.exp(-jnp.abs(zab)))
    return -jnp.exp(alog_row) * softplus, _sigmoid(ab)


def _head_l2(x):
    parts = []
    for hh in range(GDN_HEADS):
        xh = x[:, hh * GDN_DK:(hh + 1) * GDN_DK]
        parts.append(xh * jax.lax.rsqrt(jnp.sum(xh * xh, axis=-1, keepdims=True) + 1e-6))
    return jnp.concatenate(parts, axis=1)


def _lane_expand(cols, first_lane, rows):
    return jnp.concatenate(
        [jnp.broadcast_to(cols[:, first_lane + hh:first_lane + hh + 1], (rows, LANES))
         for hh in range(GDN_HEADS)], axis=1)


def _rope(x, rc, rs1, rs2):
    return x * rc + pltpu.roll(x, LANES - ROT_DIM // 2, 1) * rs1 + pltpu.roll(x, ROT_DIM // 2, 1) * rs2


def _uw_rhs(vb_all, kbg_all):
    return jnp.concatenate(
        [jnp.concatenate([vb_all[:, hh * GDN_DV:(hh + 1) * GDN_DV],
                          kbg_all[:, hh * GDN_DK:(hh + 1) * GDN_DK]], axis=1)
         for hh in range(GDN_HEADS)], axis=0)


def _split_uw(uw):
    u_all = jnp.concatenate([uw[hh * CHUNK:(hh + 1) * CHUNK, 0:GDN_DV] for hh in range(GDN_HEADS)], axis=1)
    w_all = jnp.concatenate([uw[hh * CHUNK:(hh + 1) * CHUNK, GDN_DV:] for hh in range(GDN_HEADS)], axis=1)
    return u_all, w_all


def _stack_heads_t(x):
    return jnp.transpose(jnp.concatenate(
        [x[:, hh * GDN_DK:(hh + 1) * GDN_DK] for hh in range(GDN_HEADS)], axis=0))


PROMPT_NS = 4


def _mixer_prompt_kernel(sinks_ref, cur_ref, prev_ref, gnorm_ref, o_ref, sout_ref, kout_ref, vout_ref,
                         s_ref, kprev_ref, vprev_ref, gcc_st, gcr_st, beta_st, os_st, *, tb, n_blocks):
    t = pl.program_id(1)
    ns = PROMPT_NS
    nch = tb // CHUNK
    stages = (gcc_st, gcr_st, beta_st, os_st)
    stage_names = ("gcc", "gcr", "beta", "os")

    @pl.when(t == 0)
    def _init():
        for ref in (s_ref, kprev_ref, vprev_ref) + stages:
            ref[...] = jnp.zeros(ref.shape, F32)

    r_t = jax.lax.broadcasted_iota(jnp.int32, (tb, LANES), 0)
    c_t = jax.lax.broadcasted_iota(jnp.int32, (tb, LANES), 1)
    lane_lo = c_t < CHUNK
    lane_lo_chunk = jax.lax.broadcasted_iota(jnp.int32, (CHUNK, LANES), 1) < CHUNK
    lane_lo_row = jax.lax.broadcasted_iota(jnp.int32, (1, LANES), 1) < CHUNK
    same_chunk = (r_t ^ c_t) < CHUNK
    rk = jax.lax.broadcasted_iota(jnp.int32, (tb, 2 * WINDOW), 0)
    ck = jax.lax.broadcasted_iota(jnp.int32, (tb, 2 * WINDOW), 1)
    first_off = jnp.where(t > 0, 0, WINDOW)
    dist = ck - rk
    kmask = ((dist > jnp.where(ck < WINDOW, first_off, -WINDOW))
             & (dist <= jnp.where(ck < WINDOW, 2 * WINDOW, WINDOW)))

    s_cat = [s_ref[sq] for sq in range(ns)]
    k_prev = [kprev_ref[sq] for sq in range(ns)]
    v_prev = [vprev_ref[sq] for sq in range(ns)]
    staged = [{name: ref[sq] for ref, name in zip(stages, stage_names)} for sq in range(ns)]
    for sq in range(ns):
        staged[sq].update(q=prev_ref[sq, :, 0:GDN_QK_W], k=prev_ref[sq, :, GDN_QK_W:2 * GDN_QK_W],
                          v=prev_ref[sq, :, 2 * GDN_QK_W:GDN_CONV_W], zs=prev_ref[sq, :, P_Z:P_Z + GDN_V_W])
    fr = [{} for _ in range(ns)]

    def front(sq):
        f = fr[sq]
        ab = cur_ref[sq, :, P_AB:P_AB + LANES]
        f["beta"] = ab
        f["gcc"] = _mm_exact_rhs(((c_t <= r_t) & same_chunk).astype(BF16), ab)
        f["gcr"] = _mm_exact_lhs(jnp.transpose(ab)[0:SUBLANES, :],
                                 ((r_t <= c_t) & same_chunk).astype(BF16))
        k_own = cur_ref[sq, :, P_KS:P_KS + SWA_KV_W]
        v_own = cur_ref[sq, :, P_VS:P_VS + SWA_KV_W]
        f["k_own"], f["v_own"] = k_own, v_own
        k_cat = jnp.concatenate([k_prev[sq], k_own], axis=0)
        v_cat = jnp.concatenate([v_prev[sq], v_own], axis=0).astype(BF16)
        q_rows = [None] * SWA_HEADS
        for p in range(SWA_HEADS // 2):
            qp = cur_ref[sq, :, P_QS + p * LANES:P_QS + (p + 1) * LANES]
            q_rows[p] = jnp.where(lane_lo, qp, 0.0)
            q_rows[SWA_HEADS // 2 + p] = jnp.where(lane_lo, 0.0, qp)
        s_all = _mm_nt(jnp.concatenate(q_rows, axis=0), k_cat) * (SWA_HD ** -0.5)
        yield
        pv = [None] * SWA_HEADS
        for hh in range(SWA_HEADS):
            sh = jnp.where(kmask, s_all[hh * tb:(hh + 1) * tb], -jnp.inf)
            sk = sinks_ref[hh]
            m = jnp.maximum(jnp.max(sh, axis=-1, keepdims=True), sk)
            ph = jnp.exp(sh - m)
            den = jnp.sum(ph, axis=-1, keepdims=True) + jnp.exp(sk - m)
            pv[hh] = jnp.dot(ph.astype(BF16), v_cat, preferred_element_type=F32) / den
            yield
        f["os"] = jnp.concatenate(
            [jnp.where(lane_lo, pv[p], pv[SWA_HEADS // 2 + p]) for p in range(SWA_HEADS // 2)], axis=1)

    fronts = [front(sq) for sq in range(ns)]

    def fill():
        for gen in fronts:
            next(gen, None)

    units = [(sq, c) for sq in range(ns) for c in range(nch)]
    rows_of = lambda c: slice(c * CHUNK, (c + 1) * CHUNK)
    ri = jax.lax.broadcasted_iota(jnp.int32, (CHUNK, CAT_W), 0)
    ci = jax.lax.broadcasted_iota(jnp.int32, (CHUNK, CAT_W), 1)
    cj = ci & (CHUNK - 1)
    tri = cj <= ri
    strict = cj < ri
    eye = (cj == ri).astype(F32)
    blk = ri ^ cj
    bd = functools.partial(_bd_cat, lane_lo_chunk=lane_lo_chunk)

    drv = []
    for sq in range(ns):
        st = staged[sq]
        gc_full = _lane_expand(st["gcc"], 0, tb)
        beta_full = _lane_expand(st["beta"], GDN_HEADS, tb)
        egc_full = jnp.exp(gc_full)
        kb_all = st["k"] * beta_full
        drv.append(dict(gc_full=gc_full, kb=kb_all, vb=st["v"] * beta_full, kbg=kb_all * egc_full,
                        qd=st["q"] * egc_full, gcr=st["gcr"], gcr_sw=pltpu.roll(st["gcr"], CHUNK, 1)))
    fill()
    a_mat, qk, xinv, pw = {}, {}, {}, {}
    for u in units:
        sq, c = u
        d, rows = drv[sq], rows_of(c)
        gcc_pairs, gcr_pairs = [], []
        for p in range(GDN_HEADS // 2):
            gcc_pairs.append(jnp.where(lane_lo_chunk, d["gc_full"][rows, (2 * p) * LANES:(2 * p + 1) * LANES],
                                       d["gc_full"][rows, (2 * p + 1) * LANES:(2 * p + 2) * LANES]))
            ra = (d["gcr"] if c == 0 else d["gcr_sw"])[2 * p:2 * p + 1, :]
            rb_ = (d["gcr_sw"] if c == 0 else d["gcr"])[2 * p + 1:2 * p + 2, :]
            gcr_pairs.append(jnp.where(lane_lo_row, ra, rb_))
        gcc = jnp.concatenate(gcc_pairs, axis=1)
        gcr = jnp.concatenate(gcr_pairs, axis=1)
        decay = jnp.exp(jnp.where(tri, gcc - gcr, -jnp.inf))
        kq = jax.lax.dot_general(
            jnp.concatenate([d["kb"][rows], staged[sq]["q"][rows]], axis=0).astype(BF16),
            _bd_wide(staged[sq]["k"][rows]), (((1,), (1,)), ((), ())), preferred_element_type=F32)
        a_mat[u] = jnp.where(strict, kq[0:CHUNK] * decay, 0.0)
        qk[u] = kq[CHUNK:] * decay
    fill()
    for u in units:
        a0 = jnp.where(blk < SUBLANES, a_mat[u], 0.0)
        xinv[u] = eye - a0
        pw[u] = _mm(a0, bd(a0))
    fill()
    for u in units:
        pw_bd = bd(pw[u])
        xinv[u] = xinv[u] + _mm(xinv[u], pw_bd)
        pw[u] = _mm(pw[u], pw_bd)
    fill()
    for u in units:
        xinv[u] = xinv[u] + _mm(xinv[u], bd(pw[u]))
    fill()
    s = SUBLANES
    while s < CHUNK:
        y = {u: _mm(jnp.where((blk < 2 * s) & (blk >= s), a_mat[u], 0.0), bd(xinv[u])) for u in units}
        fill()
        for u in units:
            xinv[u] = xinv[u] - _mm(xinv[u], bd(y[u]))
        fill()
        s *= 2
    prep = {}
    for u in units:
        sq, c = u
        d, rows = drv[sq], rows_of(c)
        u_all, w_all = _split_uw(_mm(bd(xinv[u]), _uw_rhs(d["vb"][rows], d["kbg"][rows])))
        glast = d["gc_full"][c * CHUNK + CHUNK - 1:c * CHUNK + CHUNK, :]
        kdec_t = _stack_heads_t(staged[sq]["k"][rows] * jnp.exp(glast - d["gc_full"][rows]))
        prep[u] = (u_all, w_all, kdec_t, jnp.exp(glast))
    fill()

    o_chunks = [[] for _ in range(ns)]
    for c in range(nch):
        rows = rows_of(c)
        ws, qs = [[] for _ in range(ns)], [[] for _ in range(ns)]
        for sq in range(ns):
            w_all = prep[(sq, c)][1]
            for p in range(GDN_HEADS // 2):
                cols = slice(2 * p * LANES, (2 * p + 2) * LANES)
                r = _mm(jnp.concatenate([w_all[:, cols], drv[sq]["qd"][rows, cols]], axis=0),
                        _bd_pair(s_cat[sq][:, (2 * p) * LANES:(2 * p + 1) * LANES],
                                 s_cat[sq][:, (2 * p + 1) * LANES:(2 * p + 2) * LANES]))
                ws[sq].append(r[0:CHUNK])
                qs[sq].append(r[CHUNK:])
        fill()
        for sq in range(ns):
            u_all, _, kdec_t, eg_last = prep[(sq, c)]
            v_new = u_all - jnp.concatenate(ws[sq], axis=1)
            r2 = _mm(jnp.concatenate([qk[(sq, c)], kdec_t], axis=0), _bd_wide(v_new))
            o_chunks[sq].append(jnp.concatenate(qs[sq], axis=1) + r2[0:CHUNK])
            s_cat[sq] = s_cat[sq] * eg_last + r2[CHUNK:]
        fill()
    for gen in fronts:
        for _ in gen:
            pass

    for sq in range(ns):
        o_gdn = jnp.concatenate(o_chunks[sq], axis=0)
        zs_prev = staged[sq]["zs"]
        og = [_rms(o_gdn[:, hh * GDN_DV:(hh + 1) * GDN_DV], gnorm_ref[...])
              * zs_prev[:, hh * GDN_DV:(hh + 1) * GDN_DV] for hh in range(GDN_HEADS)]
        o_ref[sq] = jnp.concatenate(og + [staged[sq]["os"]], axis=-1).astype(o_ref.dtype)
        s_ref[sq] = s_cat[sq]
        kprev_ref[sq] = fr[sq]["k_own"]
        vprev_ref[sq] = fr[sq]["v_own"]
        for ref, name in zip(stages, stage_names):
            ref[sq] = fr[sq][name]

    @pl.when(t == n_blocks - 1)
    def _window_out():
        for sq in range(ns):
            kout_ref[sq] = fr[sq]["k_own"]
            vout_ref[sq] = fr[sq]["v_own"]

    @pl.when(t == n_blocks)
    def _state_out():
        for sq in range(ns):
            for hh in range(GDN_HEADS):
                sout_ref[sq, hh] = s_ref[sq, :, hh * GDN_DV:(hh + 1) * GDN_DV]


def _mixer_prompt(prep3d, gnorm, sinks):
    bsz, seq, _ = prep3d.shape
    tb, ns = WINDOW, PROMPT_NS
    assert seq % tb == 0 and bsz % ns == 0
    n_blocks = seq // tb
    kern = functools.partial(_mixer_prompt_kernel, tb=tb, n_blocks=n_blocks)
    seq_blk = lambda shape: pl.BlockSpec((ns,) + shape, lambda b, t: (b,) + (0,) * len(shape))
    vmem = lambda *shape: pltpu.VMEM((ns,) + shape, F32)
    return pl.pallas_call(
        kern,
        grid=(bsz // ns, n_blocks + 1),
        in_specs=[
            pl.BlockSpec(memory_space=pltpu.SMEM),
            pl.BlockSpec((ns, tb, P_W), lambda b, t: (b, jnp.minimum(t, n_blocks - 1), 0)),
            pl.BlockSpec((ns, tb, P_W), lambda b, t: (b, jnp.maximum(t - 1, 0), 0)),
            pl.BlockSpec((1, GDN_DV), lambda b, t: (0, 0)),
        ],
        out_specs=[
            pl.BlockSpec((ns, tb, D_MODEL), lambda b, t: (b, jnp.maximum(t - 1, 0), 0)),
            seq_blk((GDN_HEADS, GDN_DK, GDN_DV)),
            seq_blk((WINDOW, SWA_KV_W)),
            seq_blk((WINDOW, SWA_KV_W)),
        ],
        out_shape=[
            jax.ShapeDtypeStruct((bsz, seq, D_MODEL), BF16),
            jax.ShapeDtypeStruct((bsz, GDN_HEADS, GDN_DK, GDN_DV), F32),
            jax.ShapeDtypeStruct((bsz, WINDOW, SWA_KV_W), F32),
            jax.ShapeDtypeStruct((bsz, WINDOW, SWA_KV_W), F32),
        ],
        scratch_shapes=[
            vmem(GDN_DK, GDN_HEADS * GDN_DV),
            vmem(WINDOW, SWA_KV_W),
            vmem(WINDOW, SWA_KV_W),
            vmem(tb, LANES),
            vmem(SUBLANES, LANES),
            vmem(tb, LANES),
            vmem(tb, SWA_Q_W),
        ],
        compiler_params=pltpu.CompilerParams(
            dimension_semantics=("arbitrary", "arbitrary"), vmem_limit_bytes=VMEM_LIMIT_BYTES),
        name="mixer_prompt",
    )(sinks, prep3d, prep3d, gnorm)


def _mixer_sample_kernel(sinks_ref, proj_ref, s0_ref, conv0_ref, k0_ref, v0_ref, convw_ref, alog_ref, dt_ref,
                         gnorm_ref, rc_ref, rs1_ref, rs2_ref,
                         o_ref, sout_ref, convout_ref, kout_ref, vout_ref, xext_ref):
    nb, ts, rows = SAMPLE_NB, SAMPLE_T, CHUNK
    seqs = range(nb)
    grp = lambda x, b: x[b * ts:(b + 1) * ts]

    hist = SUBLANES - (GDN_CONV - 1)
    conv_parts = []
    for b in seqs:
        xext_ref[2 * ts * b + hist:2 * ts * b + ts, :] = conv0_ref[b]
        xext_ref[2 * ts * b + ts:2 * ts * (b + 1), :] = proj_ref[b, :, P_QKV:P_QKV + GDN_CONV_W]
    for b in seqs:
        base = 2 * ts * b + hist
        conv = xext_ref[base:base + ts, :] * convw_ref[0:1, :]
        for j in range(1, GDN_CONV):
            conv = conv + xext_ref[base + j:base + j + ts, :] * convw_ref[j:j + 1, :]
        conv_parts.append(conv)
        convout_ref[b] = xext_ref[2 * ts * (b + 1) - (GDN_CONV - 1):2 * ts * (b + 1), :]
    qkv = _silu(jnp.concatenate(conv_parts, axis=0))
    q_all = _head_l2(qkv[:, 0:GDN_QK_W]) * (GDN_DK ** -0.5)
    k_all = _head_l2(qkv[:, GDN_QK_W:2 * GDN_QK_W])
    v_all = qkv[:, 2 * GDN_QK_W:]

    def proj_cols(lo, width):
        return jnp.concatenate([proj_ref[b, :, lo:lo + width] for b in seqs], axis=0)

    g_all, beta_all = _gates(proj_cols(P_AB, LANES), alog_ref[...], dt_ref[...])

    r_t = jax.lax.broadcasted_iota(jnp.int32, (rows, LANES), 0)
    c_t = jax.lax.broadcasted_iota(jnp.int32, (rows, LANES), 1)
    r_p = jax.lax.broadcasted_iota(jnp.int32, (LANES, LANES), 0)
    c_p = jax.lax.broadcasted_iota(jnp.int32, (LANES, LANES), 1)
    g_pad = _pad_rows(g_all)
    gc_cols = _mm_exact_rhs(((c_t <= r_t) & ((r_t ^ c_t) < ts)).astype(BF16), g_pad)
    gc_rows = _mm_exact_lhs(jnp.transpose(g_pad)[0:SUBLANES, :],
                            ((r_p <= c_p) & ((r_p ^ c_p) < ts)).astype(BF16))
    gc_rows_sw = pltpu.roll(gc_rows, CHUNK, 1)

    lane_lo = c_t < CHUNK
    lane_lo_row = jax.lax.broadcasted_iota(jnp.int32, (1, LANES), 1) < CHUNK
    lane_lo_t = jax.lax.broadcasted_iota(jnp.int32, (ts, LANES), 1) < CHUNK
    gc_full = _lane_expand(gc_cols, 0, rows)
    beta_full = _lane_expand(beta_all, GDN_HEADS, rows)
    egc_full = jnp.exp(gc_full)
    kb_all = k_all * beta_full
    vb_all = v_all * beta_full
    kbg_all = kb_all * egc_full
    qd_all = q_all * egc_full

    ri = jax.lax.broadcasted_iota(jnp.int32, (rows, CAT_W), 0)
    ci = jax.lax.broadcasted_iota(jnp.int32, (rows, CAT_W), 1)
    cj = ci & (CHUNK - 1)
    same_seq = (ri ^ cj) < ts
    eye = (cj == ri).astype(F32)
    bd = functools.partial(_bd_cat, lane_lo_chunk=lane_lo)

    gcc = jnp.concatenate(
        [jnp.where(lane_lo, gc_full[:, (2 * p) * LANES:(2 * p + 1) * LANES],
                   gc_full[:, (2 * p + 1) * LANES:(2 * p + 2) * LANES]) for p in range(GDN_HEADS // 2)], axis=1)
    gcr = jnp.concatenate(
        [jnp.where(lane_lo_row, gc_rows[2 * p:2 * p + 1, :], gc_rows_sw[2 * p + 1:2 * p + 2, :])
         for p in range(GDN_HEADS // 2)], axis=1)
    decay = jnp.exp(jnp.where((cj <= ri) & same_seq, gcc - gcr, -jnp.inf))
    kq = jax.lax.dot_general(
        jnp.concatenate([kb_all, q_all], axis=0).astype(BF16), _bd_wide(k_all),
        (((1,), (1,)), ((), ())), preferred_element_type=F32)
    a_mat = jnp.where(cj < ri, kq[0:rows] * decay, 0.0)
    qk = kq[rows:] * decay
    xinv = eye - a_mat
    pw = _mm(a_mat, bd(a_mat))
    pw_bd = bd(pw)
    xinv = xinv + _mm(xinv, pw_bd)
    pw = _mm(pw, pw_bd)
    xinv = xinv + _mm(xinv, bd(pw))
    u_all, w_all = _split_uw(_mm(bd(xinv), _uw_rhs(vb_all, kbg_all)))
    glast = [gc_full[(b + 1) * ts - 1:(b + 1) * ts, :] for b in seqs]
    glast_rows = jnp.concatenate([jnp.broadcast_to(glast[b], (ts, GDN_QK_W)) for b in seqs], axis=0)
    kdec_t = _stack_heads_t(k_all * jnp.exp(glast_rows - gc_full))

    s_old, ws, qs = [], [], []
    for b in seqs:
        s_b = jnp.concatenate([s0_ref[b, hh] for hh in range(GDN_HEADS)], axis=1)
        s_old.append(s_b)
        ws_b, qs_b = [], []
        for p in range(GDN_HEADS // 2):
            cols = slice(2 * p * LANES, (2 * p + 2) * LANES)
            r = _mm(jnp.concatenate([grp(w_all, b)[:, cols], grp(qd_all, b)[:, cols]], axis=0),
                    _bd_pair(s_b[:, (2 * p) * LANES:(2 * p + 1) * LANES],
                             s_b[:, (2 * p + 1) * LANES:(2 * p + 2) * LANES]))
            ws_b.append(r[0:ts])
            qs_b.append(r[ts:])
        ws.append(jnp.concatenate(ws_b, axis=1))
        qs.append(jnp.concatenate(qs_b, axis=1))
    v_new = u_all - jnp.concatenate(ws, axis=0)
    v_bd = _bd_wide(v_new)
    o_gdn = jnp.concatenate(qs, axis=0) + _mm(qk, v_bd)
    cb = jax.lax.broadcasted_iota(jnp.int32, (GDN_DK, CAT_W), 1) & (CHUNK - 1)
    upd = _mm(jnp.concatenate([jnp.where((cb ^ (b * ts)) < ts, kdec_t, 0.0) for b in seqs], axis=0), v_bd)
    for b in seqs:
        s_new = s_old[b] * jnp.exp(glast[b]) + upd[b * GDN_DK:(b + 1) * GDN_DK]
        for hh in range(GDN_HEADS):
            sout_ref[b, hh] = s_new[:, hh * GDN_DV:(hh + 1) * GDN_DV]
    og = []
    for hh in range(GDN_HEADS):
        oh = _rms(o_gdn[:, hh * GDN_DV:(hh + 1) * GDN_DV], gnorm_ref[...])
        og.append(oh * _silu(proj_cols(P_Z + hh * GDN_DV, GDN_DV)))

    tile8 = lambda x: jnp.concatenate([x] * nb, axis=0)
    rc, rs1, rs2 = tile8(rc_ref[...]), tile8(rs1_ref[...]), tile8(rs2_ref[...])
    k_own = _rope(proj_cols(P_KS, SWA_KV_W), rc, rs1, rs2)
    v_own = proj_cols(P_VS, SWA_KV_W)
    q_rows = [None] * SWA_HEADS
    for p in range(SWA_HEADS // 2):
        qp = _rope(proj_cols(P_QS + p * LANES, LANES), rc, rs1, rs2)
        q_rows[p] = jnp.where(lane_lo, qp, 0.0)
        q_rows[SWA_HEADS // 2 + p] = jnp.where(lane_lo, 0.0, qp)
    scale = SWA_HD ** -0.5
    s_own_all = _mm_nt(jnp.concatenate(q_rows, axis=0), _pad_rows(k_own)) * scale
    tok = r_t & (ts - 1)
    prev_mask = c_t > tok
    sink_col = jnp.zeros((rows, 1), F32)
    for hh in range(SWA_HEADS):
        sink_col = jnp.where((r_t[:, 0:1] ^ (hh * ts)) < ts, sinks_ref[hh], sink_col)
    pp_all, po_all, den_all = [], [], []
    for b in seqs:
        q_b = jnp.concatenate([grp(q_rows[hh], b) for hh in range(SWA_HEADS)], axis=0)
        sp = jnp.where(prev_mask, _mm_nt(q_b, k0_ref[b]) * scale, -jnp.inf)
        so_raw = jnp.concatenate([s_own_all[hh * rows + b * ts:hh * rows + (b + 1) * ts]
                                  for hh in range(SWA_HEADS)], axis=0)
        so = jnp.where(((c_t ^ (b * ts)) < ts) & ((c_t & (ts - 1)) <= tok), so_raw, -jnp.inf)
        m = jnp.maximum(jnp.maximum(jnp.max(sp, axis=-1, keepdims=True),
                                    jnp.max(so, axis=-1, keepdims=True)), sink_col)
        pp = jnp.exp(sp - m)
        po = jnp.exp(so - m)
        den_all.append(jnp.sum(pp, axis=-1, keepdims=True) + jnp.sum(po, axis=-1, keepdims=True)
                       + jnp.exp(sink_col - m))
        pp_all.append(_mm(pp, v0_ref[b]))
        po_all.append(po)
    pv_own = _mm(jnp.concatenate(po_all, axis=0), _pad_rows(v_own))
    os_rows = []
    for b in seqs:
        pv_b = (pp_all[b] + pv_own[b * rows:(b + 1) * rows]) / den_all[b]
        os_rows.append(jnp.concatenate(
            [jnp.where(lane_lo_t, pv_b[p * ts:(p + 1) * ts],
                       pv_b[(SWA_HEADS // 2 + p) * ts:(SWA_HEADS // 2 + p + 1) * ts])
             for p in range(SWA_HEADS // 2)], axis=1))
        kout_ref[b] = jnp.concatenate([k0_ref[b, ts:, :], grp(k_own, b)], axis=0)
        vout_ref[b] = jnp.concatenate([v0_ref[b, ts:, :], grp(v_own, b)], axis=0)
    o_full = jnp.concatenate(og + [jnp.concatenate(os_rows, axis=0)], axis=-1)
    for b in seqs:
        o_ref[b] = grp(o_full, b).astype(o_ref.dtype)


def _mixer_sample(proj3d, s0, conv0, k0, v0, convw, alog_row, dt_row, gnorm, sinks, rc, rs1, rs2):
    bsz, seq, _ = proj3d.shape
    nb = SAMPLE_NB
    assert seq == SAMPLE_T and bsz % nb == 0
    seq_blk = lambda shape: pl.BlockSpec((nb,) + shape, lambda i: (i,) + (0,) * len(shape))
    const_blk = lambda shape: pl.BlockSpec(shape, lambda i: (0,) * len(shape))
    return pl.pallas_call(
        _mixer_sample_kernel,
        grid=(bsz // nb,),
        in_specs=[
            pl.BlockSpec(memory_space=pltpu.SMEM),
            seq_blk((seq, P_W)),
            seq_blk((GDN_HEADS, GDN_DK, GDN_DV)),
            seq_blk((GDN_CONV - 1, GDN_CONV_W)),
            seq_blk((WINDOW, SWA_KV_W)),
            seq_blk((WINDOW, SWA_KV_W)),
            const_blk((GDN_CONV, GDN_CONV_W)),
            const_blk((1, LANES)),
            const_blk((1, LANES)),
            const_blk((1, GDN_DV)),
            const_blk((seq, LANES)),
            const_blk((seq, LANES)),
            const_blk((seq, LANES)),
        ],
        out_specs=[
            seq_blk((seq, D_MODEL)),
            seq_blk((GDN_HEADS, GDN_DK, GDN_DV)),
            seq_blk((GDN_CONV - 1, GDN_CONV_W)),
            seq_blk((WINDOW, SWA_KV_W)),
            seq_blk((WINDOW, SWA_KV_W)),
        ],
        out_shape=[
            jax.ShapeDtypeStruct((bsz, seq, D_MODEL), BF16),
            jax.ShapeDtypeStruct((bsz, GDN_HEADS, GDN_DK, GDN_DV), F32),
            jax.ShapeDtypeStruct((bsz, GDN_CONV - 1, GDN_CONV_W), F32),
            jax.ShapeDtypeStruct((bsz, WINDOW, SWA_KV_W), F32),
            jax.ShapeDtypeStruct((bsz, WINDOW, SWA_KV_W), F32),
        ],
        scratch_shapes=[pltpu.VMEM((2 * SAMPLE_T * nb, GDN_CONV_W), F32)],
        compiler_params=pltpu.CompilerParams(
            dimension_semantics=("arbitrary",), vmem_limit_bytes=VMEM_LIMIT_BYTES),
        name="mixer_sample",
    )(sinks, proj3d, s0, conv0, k0, v0, convw, alog_row, dt_row, gnorm, rc, rs1, rs2)


def _rope_tables(pos):
    half = ROT_DIM // 2
    inv = ROPE_THETA ** (-jnp.arange(half, dtype=F32) * 2.0 / ROT_DIM)
    ang = pos.astype(F32)[:, None] * inv[None, :]
    cos, sin = jnp.cos(ang), jnp.sin(ang)
    n = pos.shape[0]
    rest = SWA_HD - ROT_DIM
    c = jnp.concatenate([cos, cos, jnp.ones((n, rest), F32)], axis=1)
    s1 = jnp.concatenate([-sin, jnp.zeros((n, half + rest), F32)], axis=1)
    s2 = jnp.concatenate([jnp.zeros((n, half), F32), sin, jnp.zeros((n, rest), F32)], axis=1)
    reps = LANES // SWA_HD
    return jnp.tile(c, (1, reps)), jnp.tile(s1, (1, reps)), jnp.tile(s2, (1, reps))


def _projection_tiles(w_in):
    o_z = GDN_CONV_W
    o_a = o_z + GDN_V_W
    o_qs = o_a + 2 * GDN_HEADS
    o_ks = o_qs + SWA_Q_W
    o_vs = o_ks + SWA_KV_W
    cols = lambda lo, width: w_in[:, lo:lo + width]
    tiles = [cols(t * LANES, LANES) for t in range((o_z + GDN_V_W) // LANES)]
    half = SWA_HEADS // 2
    tiles += [jnp.concatenate([cols(o_qs + p * SWA_HD, SWA_HD), cols(o_qs + (half + p) * SWA_HD, SWA_HD)], axis=1)
              for p in range(half)]
    tiles += [cols(o_ks, SWA_KV_W), cols(o_vs, SWA_KV_W)]
    tiles.append(jnp.concatenate([cols(o_a, 2 * GDN_HEADS),
                                  jnp.zeros((D_MODEL, LANES - 2 * GDN_HEADS), w_in.dtype)], axis=1))
    return tiles


def _layer_weights(norm_ffn1, ffn1_gu, ffn1_down, norm_mix, w_in, conv_w, a_log, dt_bias, gdn_norm, sinks,
                   w_out, norm_ffn2, ffn2_gu, ffn2_down, norm_ple, ple_proj, ple_gate, norm_final):
    tiles = _projection_tiles(w_in)
    win_p = jnp.concatenate(tiles, axis=1).astype(BF16)
    win_prep = jnp.concatenate([win_p[:, t * LANES:(t + 1) * LANES] for t in PREP_TILE_ORDER], axis=1)
    half = SWA_HEADS // 2
    head_rows = lambda hh: w_out[GDN_V_W + hh * SWA_HD:GDN_V_W + (hh + 1) * SWA_HD]
    wout_p = jnp.concatenate([w_out[:GDN_V_W]] + [head_rows(hh) for p in range(half) for hh in (p, half + p)],
                             axis=0).astype(BF16)
    pad_row = lambda v: jnp.concatenate([v.astype(F32), jnp.zeros((LANES - v.shape[0],), F32)])[None, :]
    return dict(
        g1=norm_ffn1[None, :], wgu1=ffn1_gu.astype(BF16), wd1=ffn1_down.astype(BF16),
        gm=norm_mix[None, :], win=win_p, win_prep=win_prep, convw=conv_w, alog=pad_row(a_log), dt=pad_row(dt_bias),
        gnorm=gdn_norm[None, :], sinks=sinks.astype(F32), wout=wout_p,
        g2=norm_ffn2[None, :], wgu2=ffn2_gu.astype(BF16), wd2=ffn2_down.astype(BF16),
        gp=norm_ple[None, :], wpp=ple_proj.astype(BF16), wpg=ple_gate.astype(BF16), gf=norm_final[None, :])


def _group(x, p, w, state, pos0):
    bsz, seq, _ = x.shape
    rc, rs1, rs2 = _rope_tables(pos0 + jnp.arange(seq))
    x2d = x.reshape(bsz * seq, D_MODEL)
    if state is None:
        h2d, prep2d, conv_new = _ffn_in_prep(x2d, seq, w["g1"], w["wgu1"], w["wd1"], w["gm"], w["win_prep"],
                                             w["convw"], w["alog"], w["dt"], rc, rs1, rs2)
        o, s_new, k_new, v_new = _mixer_prompt(prep2d.reshape(bsz, seq, P_W), w["gnorm"], w["sinks"])
    else:
        h2d, proj2d = _ffn_in(x2d, w["g1"], w["wgu1"], w["wd1"], w["gm"], w["win"])
        s0, conv0, k0, v0 = state
        o, s_new, conv_new, k_new, v_new = _mixer_sample(
            proj2d.reshape(bsz, seq, P_W), s0, conv0, k0, v0, w["convw"], w["alog"], w["dt"], w["gnorm"],
            w["sinks"], rc, rs1, rs2)
    y = _ffn_out(o.reshape(bsz * seq, D_MODEL), h2d, p.reshape(bsz * seq, PLE_DIM), w["wout"], w["g2"],
                 w["wgu2"], w["wd2"], w["gp"], w["wpg"], w["wpp"], w["gf"])
    kv_shape = (1, bsz, WINDOW, SWA_KV_HEADS, SWA_HD)
    return (y.reshape(bsz, seq, D_MODEL), s_new[None], conv_new[None], k_new.reshape(kv_shape),
            v_new.reshape(kv_shape))


def kernel(x_prompt, x_sample, state_gdn, state_conv, cache_swa_k, cache_swa_v, p_prompt, p_sample, norm_ffn1, ffn1_gu, ffn1_down, norm_mix, w_in, conv_w, a_log, dt_bias, gdn_norm, sinks, w_out, norm_ffn2, ffn2_gu, ffn2_down, norm_ple, ple_proj, ple_gate, norm_final):
    assert state_gdn.shape[0] == 1, "one layer"
    w = _layer_weights(norm_ffn1[0], ffn1_gu[0], ffn1_down[0], norm_mix[0], w_in[0], conv_w[0], a_log[0],
                       dt_bias[0], gdn_norm[0], sinks[0], w_out[0], norm_ffn2[0], ffn2_gu[0], ffn2_down[0],
                       norm_ple[0], ple_proj[0], ple_gate[0], norm_final)
    bs = x_sample.shape[0]
    yp, sg_p, sc_p, kk_p, vv_p = _group(x_prompt, p_prompt[0], w, None, 0)
    ys, sg_s, sc_s, kk_s, vv_s = _group(
        x_sample, p_sample[0], w,
        (state_gdn[0], state_conv[0], cache_swa_k[0].reshape(bs, WINDOW, SWA_KV_W),
         cache_swa_v[0].reshape(bs, WINDOW, SWA_KV_W)), PAST_LEN)
    return (yp, ys, sg_p, sc_p, kk_p, vv_p, sg_s, sc_s, kk_s, vv_s)
```
